```python
import jax, jax.numpy as jnp
from jax import lax
import numpy as np

D_MODEL = 2048
BATCH = 1
SEQ = 16384
DEPTH = 4
DEC_BATCH = 8
DEC_SEQ = 64
PAST_LEN = 2048

CHUNK = 64
Q_BLOCK = 128
HEAD_DIM = 128
FOX_HEADS = 6
DSA_HEADS = 6
DSA_KV_HEADS = 2
DSA_GROUP = DSA_HEADS // DSA_KV_HEADS
IDX_HEADS = 16
IDX_DIM = 64
IDX_TOPK_MAX = 256
MLA_HEADS = 4
MLA_Q_LORA = 512
MLA_KV_LORA = 256
MLA_NOPE = 128
MLA_ROPE = 64
MLA_V = 128
MIX_WIDTH = FOX_HEADS * HEAD_DIM + DSA_HEADS * HEAD_DIM + MLA_HEADS * MLA_V
D_FF = ((8 * D_MODEL // 3 + 255) // 256) * 256
PARTIAL_ROT = HEAD_DIM // 4
IDX_ROT = IDX_DIM // 4
ROPE_THETA = 500000.0
EPS = 1e-6
NEG_INF = -1e30
FOX_BIAS_INIT = 2.0
FOX_SCALE = HEAD_DIM ** -0.5
DSA_SCALE = HEAD_DIM ** -0.5
MLA_SCALE = (MLA_NOPE + MLA_ROPE) ** -0.5
IDX_W_SCALE = (IDX_HEADS * IDX_DIM) ** -0.5

IN_SIZES = (
    FOX_HEADS * HEAD_DIM, FOX_HEADS * HEAD_DIM, FOX_HEADS * HEAD_DIM, FOX_HEADS,
    DSA_HEADS * HEAD_DIM, DSA_KV_HEADS * HEAD_DIM, DSA_KV_HEADS * HEAD_DIM,
    IDX_HEADS * IDX_DIM, IDX_DIM, IDX_HEADS,
    MLA_Q_LORA, MLA_KV_LORA, MLA_ROPE,
)
IN_COLS = sum(IN_SIZES)
IN_SPLITS = tuple(int(v) for v in np.cumsum(IN_SIZES)[:-1])

kernel_name = 'hybrid_fox_dsa_mla_streaming_step'


def rms_norm(x, g):
    xf = x.astype(jnp.float32)
    y = xf * lax.rsqrt(jnp.mean(xf * xf, axis=-1, keepdims=True) + EPS)
    return (y * g.astype(jnp.float32)).astype(x.dtype)


def rope(x, pos, rot_dim):
    half = rot_dim // 2
    inv_freq = ROPE_THETA ** (-jnp.arange(half, dtype=jnp.float32) / half)
    ang = pos.astype(jnp.float32)[:, None] * inv_freq[None, :]
    cos, sin = jnp.cos(ang)[:, None, :], jnp.sin(ang)[:, None, :]
    xr = x[..., :rot_dim].astype(jnp.float32)
    x1, x2 = xr[..., :half], xr[..., half:]
    rot = jnp.concatenate([x1 * cos - x2 * sin, x1 * sin + x2 * cos], axis=-1).astype(x.dtype)
    return jnp.concatenate([rot, x[..., rot_dim:]], axis=-1)


def project(n, pos, w_in, fox_bias, mla_q_norm, mla_w_uq, mla_kv_norm):
    b, l, _ = n.shape
    (q_a, k_a, v_a, f_a, q_b, k_b, v_b, q_i, k_i, w_i, c_q, c_kv, k_r) = jnp.split(n @ w_in, IN_SPLITS, axis=-1)
    heads = lambda t, h: t.reshape(b, l, h, -1)
    fox_q, fox_k, fox_v = heads(q_a, FOX_HEADS), heads(k_a, FOX_HEADS), heads(v_a, FOX_HEADS)
    fox_logf = jax.nn.log_sigmoid(f_a.astype(jnp.float32) + fox_bias.astype(jnp.float32))
    dsa_q = rope(heads(q_b, DSA_HEADS), pos, PARTIAL_ROT)
    dsa_k = rope(heads(k_b, DSA_KV_HEADS), pos, PARTIAL_ROT)
    dsa_v = heads(v_b, DSA_KV_HEADS)
    idx_q = rope(heads(q_i, IDX_HEADS), pos, IDX_ROT)
    idx_k = rope(k_i[:, :, None, :], pos, IDX_ROT)[:, :, 0]
    idx_w = w_i * IDX_W_SCALE
    mq = heads(rms_norm(c_q, mla_q_norm) @ mla_w_uq, MLA_HEADS)
    mla_qn = mq[..., :MLA_NOPE]
    mla_qr = rope(mq[..., MLA_NOPE:], pos, MLA_ROPE)
    mla_ckv = rms_norm(c_kv, mla_kv_norm)
    mla_kr = rope(k_r[:, :, None, :], pos, MLA_ROPE)[:, :, 0]
    queries = (fox_q, dsa_q, idx_q, idx_w, mla_qn, mla_qr)
    rows = (fox_k, fox_v, fox_logf, dsa_k, dsa_v, idx_k, mla_ckv, mla_kr)
    return queries, rows


def key_side(rows, mla_w_ukv):
    fox_k, fox_v, fox_logf, dsa_k, dsa_v, idx_k, mla_ckv, mla_kr = rows
    b, s, _ = mla_ckv.shape
    fox_c = jnp.cumsum(fox_logf.astype(jnp.float32), axis=1)
    kv = (mla_ckv @ mla_w_ukv).reshape(b, s, MLA_HEADS, MLA_NOPE + MLA_V)
    return (fox_k, fox_v, fox_c, dsa_k, dsa_v, idx_k, kv[..., :MLA_NOPE], kv[..., MLA_NOPE:], mla_kr)


def attend(block, keys, kpos):
    fox_q, fox_cq, dsa_q, idx_q, idx_w, mla_qn, mla_qr, qpos = block
    fox_k, fox_v, fox_c, dsa_k, dsa_v, idx_k, mla_kn, mla_v, mla_kr = keys
    b, q = fox_q.shape[:2]
    causal = kpos[None, :] <= qpos[:, None]
    chunk_causal = (kpos // CHUNK)[None, :] <= (qpos // CHUNK)[:, None]

    s = jnp.einsum('bqhd,bshd->bhqs', fox_q, fox_k).astype(jnp.float32) * FOX_SCALE
    s = s + jnp.moveaxis(fox_cq, 2, 1)[..., None] - jnp.moveaxis(fox_c, 2, 1)[:, :, None, :]
    p = jax.nn.softmax(jnp.where(causal, s, NEG_INF), axis=-1)
    out_a = jnp.einsum('bhqs,bshd->bqhd', p.astype(fox_v.dtype), fox_v).reshape(b, q, -1)

    topk = min(IDX_TOPK_MAX, fox_k.shape[1] // 4)
    score = jax.nn.relu(jnp.einsum('bqhe,bse->bqhs', idx_q, idx_k).astype(jnp.float32))
    score = jnp.einsum('bqh,bqhs->bqs', idx_w.astype(jnp.float32), score)
    score = jnp.where(chunk_causal, score, -jnp.inf)
    top_val, top_idx = lax.top_k(score, topk)
    take = jax.vmap(lambda r, i: r[i])
    sel_k = take(dsa_k, top_idx)
    sel_v = take(dsa_v, top_idx)
    qg = dsa_q.reshape(b, q, DSA_KV_HEADS, DSA_GROUP, HEAD_DIM)
    s = jnp.einsum('bqhgd,bqnhd->bqhgn', qg, sel_k).astype(jnp.float32) * DSA_SCALE
    s = jnp.where(jnp.isfinite(top_val)[:, :, None, None, :], s, NEG_INF)
    p = jax.nn.softmax(s, axis=-1)
    out_b = jnp.einsum('bqhgn,bqnhd->bqhgd', p.astype(sel_v.dtype), sel_v).reshape(b, q, -1)

    s = (jnp.einsum('bqhd,bshd->bhqs', mla_qn, mla_kn)
         + jnp.einsum('bqhr,bsr->bhqs', mla_qr, mla_kr)).astype(jnp.float32) * MLA_SCALE
    p = jax.nn.softmax(jnp.where(chunk_causal, s, NEG_INF), axis=-1)
    out_c = jnp.einsum('bhqs,bshd->bqhd', p.astype(mla_v.dtype), mla_v).reshape(b, q, -1)

    return jnp.concatenate([out_a, out_b, out_c], axis=-1)


def to_blocks(a):
    b, l = a.shape[:2]
    return jnp.moveaxis(a.reshape(b, l // Q_BLOCK, Q_BLOCK, *a.shape[2:]), 1, 0)


def from_blocks(a):
    a = jnp.moveaxis(a, 0, 1)
    return a.reshape(a.shape[0], -1, *a.shape[3:])


def swiglu(h, w_gate, w_up, w_down):
    return (jax.nn.silu(h @ w_gate) * (h @ w_up)) @ w_down


def stack_rows(rows_per_layer):
    return tuple(jnp.stack(t) for t in zip(*rows_per_layer))


def setup_inputs(seed: int = 0) -> dict:
    key = jax.random.key(seed)
    ks = iter(jax.random.split(key, 32))
    nrm = lambda shape, scale=1.0: jax.random.normal(next(ks), shape, jnp.float32) * scale
    gain = lambda shape: 1.0 + nrm(shape, 0.05)
    cshape = (DEPTH, DEC_BATCH, PAST_LEN)
    return {
        'x_prompt': nrm((BATCH, SEQ, D_MODEL)),
        'x_sample': nrm((DEC_BATCH, DEC_SEQ, D_MODEL)),
        'cache_fox_k': nrm(cshape + (FOX_HEADS, HEAD_DIM)),
        'cache_fox_v': nrm(cshape + (FOX_HEADS, HEAD_DIM)),
        'cache_fox_logf': jax.nn.log_sigmoid(FOX_BIAS_INIT + nrm(cshape + (FOX_HEADS,))),
        'cache_dsa_k': nrm(cshape + (DSA_KV_HEADS, HEAD_DIM)),
        'cache_dsa_v': nrm(cshape + (DSA_KV_HEADS, HEAD_DIM)),
        'cache_idx_k': nrm(cshape + (IDX_DIM,)),
        'cache_mla_ckv': nrm(cshape + (MLA_KV_LORA,)),
        'cache_mla_krope': nrm(cshape + (MLA_ROPE,)),
        'w_in': nrm((DEPTH, D_MODEL, IN_COLS), D_MODEL ** -0.5),
        'fox_bias': FOX_BIAS_INIT + nrm((DEPTH, FOX_HEADS), 0.5),
        'mla_q_norm': gain((DEPTH, MLA_Q_LORA)),
        'mla_w_uq': nrm((DEPTH, MLA_Q_LORA, MLA_HEADS * (MLA_NOPE + MLA_ROPE)), MLA_Q_LORA ** -0.5),
        'mla_kv_norm': gain((DEPTH, MLA_KV_LORA)),
        'mla_w_ukv': nrm((DEPTH, MLA_KV_LORA, MLA_HEADS * (MLA_NOPE + MLA_V)), MLA_KV_LORA ** -0.5),
        'w_o': nrm((DEPTH, MIX_WIDTH, D_MODEL), MIX_WIDTH ** -0.5),
        'attn_norm': gain((DEPTH, D_MODEL)),
        'ffn_norm': gain((DEPTH, D_MODEL)),
        'w_gate': nrm((DEPTH, D_MODEL, D_FF), D_MODEL ** -0.5),
        'w_up': nrm((DEPTH, D_MODEL, D_FF), D_MODEL ** -0.5),
        'w_down': nrm((DEPTH, D_FF, D_MODEL), D_FF ** -0.5),
        'final_norm': gain((D_MODEL,)),
    }


def reference(x_prompt, x_sample, cache_fox_k, cache_fox_v, cache_fox_logf, cache_dsa_k, cache_dsa_v,
              cache_idx_k, cache_mla_ckv, cache_mla_krope, w_in, fox_bias, mla_q_norm, mla_w_uq,
              mla_kv_norm, mla_w_ukv, w_o, attn_norm, ffn_norm, w_gate, w_up, w_down, final_norm):
    caches = (cache_fox_k, cache_fox_v, cache_fox_logf, cache_dsa_k, cache_dsa_v,
              cache_idx_k, cache_mla_ckv, cache_mla_krope)
    seq = x_prompt.shape[1]
    dec = x_sample.shape[1]
    past = cache_fox_k.shape[2]
    p_pos = jnp.arange(seq, dtype=jnp.int32)
    s_qpos = past + jnp.arange(dec, dtype=jnp.int32)
    s_kpos = jnp.arange(past + dec, dtype=jnp.int32)

    xp, xs = x_prompt, x_sample
    p_rows, s_rows = [], []
    for l in range(DEPTH):
        proj_w = (w_in[l], fox_bias[l], mla_q_norm[l], mla_w_uq[l], mla_kv_norm[l])

        q, rows = project(rms_norm(xp, attn_norm[l]), p_pos, *proj_w)
        keys = key_side(rows, mla_w_ukv[l])
        fox_q, dsa_q, idx_q, idx_w, mla_qn, mla_qr = q
        blocks = tuple(to_blocks(t) for t in (fox_q, keys[2], dsa_q, idx_q, idx_w, mla_qn, mla_qr))
        blocks = blocks + (p_pos.reshape(-1, Q_BLOCK),)
        mixed = from_blocks(lax.map(lambda blk: attend(blk, keys, p_pos), blocks))
        xp = xp + mixed @ w_o[l]
        xp = xp + swiglu(rms_norm(xp, ffn_norm[l]), w_gate[l], w_up[l], w_down[l])
        p_rows.append(rows)

        q, rows = project(rms_norm(xs, attn_norm[l]), s_qpos, *proj_w)
        full = tuple(jnp.concatenate([c[l], r], axis=1) for c, r in zip(caches, rows))
        keys = key_side(full, mla_w_ukv[l])
        fox_q, dsa_q, idx_q, idx_w, mla_qn, mla_qr = q
        blk = (fox_q, keys[2][:, past:], dsa_q, idx_q, idx_w, mla_qn, mla_qr, s_qpos)
        xs = xs + attend(blk, keys, s_kpos) @ w_o[l]
        xs = xs + swiglu(rms_norm(xs, ffn_norm[l]), w_gate[l], w_up[l], w_down[l])
        s_rows.append(rows)

    y_prompt = rms_norm(xp, final_norm)
    y_sample = rms_norm(xs, final_norm)
    (p_fox_k, p_fox_v, p_fox_logf, p_dsa_k, p_dsa_v, p_idx_k, p_mla_ckv, p_mla_krope) = stack_rows(p_rows)
    (s_fox_k, s_fox_v, s_fox_logf, s_dsa_k, s_dsa_v, s_idx_k, s_mla_ckv, s_mla_krope) = stack_rows(s_rows)
    return (y_prompt, y_sample,
            p_fox_k, p_fox_v, p_fox_logf, p_dsa_k, p_dsa_v, p_idx_k, p_mla_ckv, p_mla_krope,
            s_fox_k, s_fox_v, s_fox_logf, s_dsa_k, s_dsa_v, s_idx_k, s_mla_ckv, s_mla_krope)
```

```python
import functools
import math

import jax
import jax.numpy as jnp
import numpy as np
from jax import lax
from jax.experimental import pallas as pl
from jax.experimental.pallas import tpu as pltpu

CHUNK = 64
HEAD_DIM = 128
FOX_HEADS = 6
DSA_HEADS = 6
DSA_KV_HEADS = 2
DSA_GROUP = DSA_HEADS // DSA_KV_HEADS
IDX_HEADS = 16
IDX_DIM = 64
IDX_TOPK_MAX = 256
MLA_HEADS = 4
MLA_Q_LORA = 512
MLA_KV_LORA = 256
MLA_NOPE = 128
MLA_ROPE = 64
MLA_V = 128
PARTIAL_ROT = HEAD_DIM // 4
IDX_ROT = IDX_DIM // 4
ROPE_THETA = 500000.0
EPS = 1e-6
NEG_INF = -1e30
FOX_SCALE = HEAD_DIM ** -0.5
DSA_SCALE = HEAD_DIM ** -0.5
MLA_SCALE = (MLA_NOPE + MLA_ROPE) ** -0.5
IDX_W_SCALE = (IDX_HEADS * IDX_DIM) ** -0.5

LANE = 128
VMEM_LIMIT = 56 * 1024 * 1024

FOX_W = FOX_HEADS * HEAD_DIM
DSA_W = DSA_HEADS * HEAD_DIM
DSA_KV_W = DSA_KV_HEADS * HEAD_DIM
IDX_W = IDX_HEADS * IDX_DIM
MLA_QP = 2 * LANE
MLA_QW = MLA_HEADS * MLA_QP
MLA_VW = MLA_HEADS * MLA_V

_SEGS = (("fq", FOX_W), ("fk", FOX_W), ("fv", FOX_W), ("dq", DSA_W), ("dk", DSA_KV_W), ("dv", DSA_KV_W),
         ("iq", IDX_W), ("cq", MLA_Q_LORA), ("ckv", MLA_KV_LORA),
         ("fa", LANE), ("ik", LANE), ("iw", LANE), ("kr", LANE))
_OFF = {}
_o = 0
for _n, _w in _SEGS:
    _OFF[_n] = _o
    _o += _w
PROJ_W = ((_o + 511) // 512) * 512

_NT = (((1,), (1,)), ((), ()))


def _dot(a, b):
    return jnp.dot(a, b, preferred_element_type=jnp.float32)


def _dot_nt(a, b):
    return lax.dot_general(a, b, _NT, preferred_element_type=jnp.float32)


def _cparams(sem):
    return pltpu.CompilerParams(dimension_semantics=sem, vmem_limit_bytes=VMEM_LIMIT)


def _resident(block_shape, index_map):
    return pl.BlockSpec(block_shape, index_map, pipeline_mode=pl.Buffered(1))


def _rms(x, g):
    return x * lax.rsqrt(jnp.mean(x * x, axis=-1, keepdims=True) + EPS) * g


def _norm_matmul_kernel(x_ref, g_ref, w_ref, o_ref, xn_ref):
    @pl.when(pl.program_id(1) == 0)
    def _():
        xn_ref[...] = _rms(x_ref[...], g_ref[...]).astype(jnp.bfloat16)

    o_ref[...] = _dot(xn_ref[...], w_ref[...])


def norm_matmul(x, g, w, *, bm, bn):
    m, d = x.shape
    n = w.shape[1]
    return pl.pallas_call(
        _norm_matmul_kernel,
        grid=(m // bm, n // bn),
        in_specs=[pl.BlockSpec((bm, d), lambda i, j: (i, 0)),
                  pl.BlockSpec((1, d), lambda i, j: (0, 0)),
                  pl.BlockSpec((d, bn), lambda i, j: (0, j))],
        out_specs=pl.BlockSpec((bm, bn), lambda i, j: (i, j)),
        out_shape=jax.ShapeDtypeStruct((m, n), jnp.float32),
        scratch_shapes=[pltpu.VMEM((bm, d), jnp.bfloat16)],
        compiler_params=_cparams(("parallel", "arbitrary")),
        name="norm_matmul",
    )(x, g.reshape(1, d), w)


def _rope(x, tab, half):
    c, s1, s2 = tab[:, 0:LANE], tab[:, LANE:2 * LANE], tab[:, 2 * LANE:3 * LANE]
    return x * c + pltpu.roll(x, half, 1) * s1 + pltpu.roll(x, LANE - half, 1) * s2


def _post_kernel(p_ref, tab_ref, fb_ref, qn_ref, kvn_ref, wuq_ref,
                 fq_o, fk_o, fv_o, fkb_o, fvb_o, lf_o,
                 dq_o, dk_o, dv_o, dkb_o, dvb_o,
                 iq_o, iqs_o, ik_o, ikb_o, iw_o,
                 mq_o, ckv_o, kr_o):
    bf = jnp.bfloat16
    seg = lambda name, w: p_ref[:, _OFF[name]:_OFF[name] + w]
    tab_d = tab_ref[:, 0:3 * LANE]
    tab_i = tab_ref[:, 3 * LANE:6 * LANE]
    tab_m = tab_ref[:, 6 * LANE:9 * LANE]

    fq_o[...] = (seg("fq", FOX_W) * FOX_SCALE).astype(bf)
    fk = seg("fk", FOX_W)
    fk_o[...] = fk
    fkb_o[...] = fk.astype(bf)
    fv = seg("fv", FOX_W)
    fv_o[...] = fv
    fvb_o[...] = fv.astype(bf)
    z = seg("fa", LANE) + fb_ref[...]
    lf_o[...] = jnp.minimum(z, 0.0) - jnp.log1p(jnp.exp(-jnp.abs(z)))

    for h in range(DSA_HEADS):
        x = p_ref[:, _OFF["dq"] + h * LANE:_OFF["dq"] + (h + 1) * LANE]
        dq_o[:, h * LANE:(h + 1) * LANE] = (_rope(x, tab_d, PARTIAL_ROT // 2) * DSA_SCALE).astype(bf)
    for h in range(DSA_KV_HEADS):
        x = p_ref[:, _OFF["dk"] + h * LANE:_OFF["dk"] + (h + 1) * LANE]
        y = _rope(x, tab_d, PARTIAL_ROT // 2)
        dk_o[:, h * LANE:(h + 1) * LANE] = y
        dkb_o[:, h * LANE:(h + 1) * LANE] = y.astype(bf)
    dv = seg("dv", DSA_KV_W)
    dv_o[...] = dv
    dvb_o[...] = dv.astype(bf)
    for j in range(IDX_W // LANE):
        x = p_ref[:, _OFF["iq"] + j * LANE:_OFF["iq"] + (j + 1) * LANE]
        y = _rope(x, tab_i, IDX_ROT // 2)
        iq_o[:, j * LANE:(j + 1) * LANE] = y.astype(bf)
        iqs_o[:, j * LANE:(j + 1) * LANE] = pltpu.roll(y, IDX_DIM, 1).astype(bf)
    y = _rope(seg("ik", LANE), tab_i, IDX_ROT // 2)
    ik_o[...] = y
    ikb_o[...] = y.astype(bf)
    iw_o[...] = seg("iw", LANE) * IDX_W_SCALE

    cqn = _rms(seg("cq", MLA_Q_LORA), qn_ref[...]).astype(bf)
    mq = _dot(cqn, wuq_ref[...])
    for h in range(MLA_HEADS):
        a = h * MLA_QP
        mq_o[:, a:a + LANE] = (mq[:, a:a + LANE] * MLA_SCALE).astype(bf)
        r = _rope(mq[:, a + LANE:a + 2 * LANE], tab_m, MLA_ROPE // 2)
        mq_o[:, a + LANE:a + 2 * LANE] = (r * MLA_SCALE).astype(bf)
    ckv_o[...] = _rms(seg("ckv", MLA_KV_LORA), kvn_ref[...])
    kr_o[...] = _rope(seg("kr", LANE), tab_m, MLA_ROPE // 2)


def post_proj(proj, tab, fox_bias_p, q_norm, kv_norm, wuq_p, *, bm):
    m = proj.shape[0]
    f32, bf = jnp.float32, jnp.bfloat16
    outs = [(FOX_W, bf), (FOX_W, f32), (FOX_W, f32), (FOX_W, bf), (FOX_W, bf), (LANE, f32),
            (DSA_W, bf), (DSA_KV_W, f32), (DSA_KV_W, f32), (DSA_KV_W, bf), (DSA_KV_W, bf),
            (IDX_W, bf), (IDX_W, bf), (LANE, f32), (LANE, bf), (LANE, f32),
            (MLA_QW, bf), (MLA_KV_LORA, f32), (LANE, f32)]
    row = lambda w: pl.BlockSpec((bm, w), lambda i: (i, 0))
    full = lambda a: pl.BlockSpec(a.shape, lambda i: (0,) * a.ndim)
    return pl.pallas_call(
        _post_kernel,
        grid=(m // bm,),
        in_specs=[row(PROJ_W), row(9 * LANE), full(fox_bias_p), full(q_norm), full(kv_norm), full(wuq_p)],
        out_specs=[row(w) for w, _ in outs],
        out_shape=[jax.ShapeDtypeStruct((m, w), dt) for w, dt in outs],
        compiler_params=_cparams(("parallel",)),
        name="post_proj",
    )(proj, tab, fox_bias_p, q_norm, kv_norm, wuq_p)


def _kv_up_kernel(ckv_ref, kr_ref, w_ref, k_o, v_o):
    bf = jnp.bfloat16
    kv = _dot(ckv_ref[...].astype(bf), w_ref[...])
    kr = kr_ref[...].astype(bf)
    for h in range(MLA_HEADS):
        k_o[:, h * MLA_QP:h * MLA_QP + LANE] = kv[:, h * LANE:(h + 1) * LANE].astype(bf)
        k_o[:, h * MLA_QP + LANE:(h + 1) * MLA_QP] = kr
    v_o[...] = kv[:, MLA_HEADS * LANE:].astype(bf)


def kv_up(ckv, kr, w, *, bm):
    r = ckv.shape[0]
    row = lambda w_: pl.BlockSpec((bm, w_), lambda i: (i, 0))
    return pl.pallas_call(
        _kv_up_kernel,
        grid=(r // bm,),
        in_specs=[row(MLA_KV_LORA), row(LANE), pl.BlockSpec(w.shape, lambda i: (0, 0))],
        out_specs=[row(MLA_QW), row(MLA_VW)],
        out_shape=[jax.ShapeDtypeStruct((r, MLA_QW), jnp.bfloat16),
                   jax.ShapeDtypeStruct((r, MLA_VW), jnp.bfloat16)],
        compiler_params=_cparams(("parallel",)),
        name="mla_kv_up",
    )(ckv, kr, w)


def _cumsum_kernel(x_ref, c_o, ct_o, carry_ref, carry_t_ref):
    t = x_ref.shape[1]

    @pl.when(pl.program_id(1) == 0)
    def _():
        carry_ref[...] = jnp.zeros_like(carry_ref)
        carry_t_ref[...] = jnp.zeros_like(carry_t_ref)

    x = x_ref[0]
    r = lax.broadcasted_iota(jnp.int32, (t, t), 0)
    c = lax.broadcasted_iota(jnp.int32, (t, t), 1)
    lower = jnp.where(c <= r, 1.0, 0.0).astype(jnp.float32)
    upper = jnp.where(r <= c, 1.0, 0.0).astype(jnp.float32)
    cs = jnp.dot(lower, x, preferred_element_type=jnp.float32, precision=lax.Precision.HIGHEST)
    cs = cs + carry_ref[...]
    c_o[0] = cs
    carry_ref[...] = cs[t - 1:t, :]
    cst = jnp.dot(x.T, upper, preferred_element_type=jnp.float32, precision=lax.Precision.HIGHEST)
    cst = cst + carry_t_ref[...]
    ct_o[0] = cst[0:8, :]
    carry_t_ref[...] = cst[:, t - 1:t]


def cumsum_rows(x, *, bt):
    b, s, _ = x.shape
    return pl.pallas_call(
        _cumsum_kernel,
        grid=(b, s // bt),
        in_specs=[pl.BlockSpec((1, bt, LANE), lambda i, j: (i, j, 0))],
        out_specs=[pl.BlockSpec((1, bt, LANE), lambda i, j: (i, j, 0)),
                   pl.BlockSpec((1, 8, bt), lambda i, j: (i, 0, j))],
        out_shape=[jax.ShapeDtypeStruct((b, s, LANE), jnp.float32),
                   jax.ShapeDtypeStruct((b, 8, s), jnp.float32)],
        scratch_shapes=[pltpu.VMEM((1, LANE), jnp.float32), pltpu.VMEM((LANE, 1), jnp.float32)],
        compiler_params=_cparams(("parallel", "arbitrary")),
        name="logf_cumsum",
    )(x)


def _num_key_tiles(i, bq, bk, q_off, nk_max):
    last = q_off + (i + 1) * bq
    return jnp.minimum((last + bk - 1) // bk, nk_max)


def _flash_kernel(*refs, bq, bk, q_off, nk_max, fox):
    if fox:
        q_ref, k_ref, v_ref, cq_ref, ck_ref, o_ref = refs
    else:
        q_ref, k_ref, v_ref, o_ref = refs
    h = pl.program_id(1)
    i = pl.program_id(2)
    q = q_ref[0]
    qpos = q_off + i * bq + lax.broadcasted_iota(jnp.int32, (bq, 1), 0)
    if fox:
        lane = lax.broadcasted_iota(jnp.int32, (bq, LANE), 1)
        cq = jnp.sum(jnp.where(lane == h, cq_ref[0], 0.0), axis=1, keepdims=True)
        c0 = cq[0:1, :]
        bias_q = cq - c0
    nk = _num_key_tiles(i, bq, bk, q_off, nk_max)

    def body(j, carry):
        m, l, acc = carry
        start = pl.multiple_of(j * bk, bk)
        kt = k_ref[0, pl.ds(start, bk), :]
        vt = v_ref[0, pl.ds(start, bk), :]
        s = _dot_nt(q, kt)
        kpos = j * bk + lax.broadcasted_iota(jnp.int32, (1, bk), 1)
        if fox:
            s = s + bias_q - (ck_ref[0, 0, j] - c0)
            mask = kpos <= qpos
        else:
            mask = (kpos >> 6) <= (qpos >> 6)
        s = jnp.where(mask, s, NEG_INF)
        m_new = jnp.maximum(m, jnp.max(s, axis=1, keepdims=True))
        alpha = jnp.exp(m - m_new)
        p = jnp.exp(s - m_new)
        l = alpha * l + jnp.sum(p, axis=1, keepdims=True)
        acc = alpha * acc + _dot(p.astype(jnp.bfloat16), vt)
        return m_new, l, acc

    init = (jnp.full((bq, 1), NEG_INF, jnp.float32), jnp.zeros((bq, 1), jnp.float32),
            jnp.zeros((bq, v_ref.shape[2]), jnp.float32))
    m, l, acc = lax.fori_loop(0, nk, body, init)
    o_ref[0] = (acc / l).astype(o_ref.dtype)


def flash_attention(q, k, v, cq=None, ck=None, *, heads, dqk, dv, bq, bk, q_off):
    b, lq, _ = q.shape
    sp = k.shape[1]
    nk_max = sp // bk
    fox = cq is not None
    in_specs = [pl.BlockSpec((1, bq, dqk), lambda b_, h, i: (b_, i, h)),
                _resident((1, sp, dqk), lambda b_, h, i: (b_, 0, h)),
                _resident((1, sp, dv), lambda b_, h, i: (b_, 0, h))]
    args = [q, k, v]
    if fox:
        in_specs += [pl.BlockSpec((1, bq, LANE), lambda b_, h, i: (b_, i, 0)),
                     _resident((1, 1, nk_max, 1, bk), lambda b_, h, i: (b_, h, 0, 0, 0))]
        args += [cq, ck]
    return pl.pallas_call(
        functools.partial(_flash_kernel, bq=bq, bk=bk, q_off=q_off, nk_max=nk_max, fox=fox),
        grid=(b, heads, lq // bq),
        in_specs=in_specs,
        out_specs=pl.BlockSpec((1, bq, dv), lambda b_, h, i: (b_, i, h)),
        out_shape=jax.ShapeDtypeStruct((b, lq, heads * dv), jnp.bfloat16),
        compiler_params=_cparams(("parallel", "parallel", "arbitrary")),
        name="fox_attention" if fox else "mla_attention",
    )(*args)


_KEY_NEG_INF = -2139095041
_F32_LOWEST = -3.4028234663852886e38


def _key_to_f32(key):
    bits = jnp.where(key >= 0, key, key ^ jnp.int32(0x7FFFFFFF))
    return lax.bitcast_convert_type(bits, jnp.float32)


def _dsa_kernel(iq_ref, iqs_ref, iw_ref, dq_ref, ik_ref, dk_ref, dv_ref, o_ref, sc_ref,
                *, bq, bk, q_off, nk_max, topk):
    i = pl.program_id(1)
    nk = _num_key_tiles(i, bq, bk, q_off, nk_max)
    qpos = q_off + i * bq + lax.broadcasted_iota(jnp.int32, (bq, 1), 0)
    klim = ((qpos >> 6) + 1) << 6
    iw = iw_ref[0]

    def score_tile(j, _):
        kt = ik_ref[0, pl.ds(pl.multiple_of(j * bk, bk), bk), :]
        acc = jnp.zeros((bq, bk), jnp.float32)
        for g in range(IDX_HEADS // 2):
            se = _dot_nt(iq_ref[0, :, g * LANE:(g + 1) * LANE], kt)
            so = _dot_nt(iqs_ref[0, :, g * LANE:(g + 1) * LANE], kt)
            acc = acc + iw[:, 2 * g:2 * g + 1] * jnp.maximum(se, 0.0)
            acc = acc + iw[:, 2 * g + 1:2 * g + 2] * jnp.maximum(so, 0.0)
        kpos = j * bk + lax.broadcasted_iota(jnp.int32, (1, bk), 1)
        sc_ref[j] = jnp.where(kpos < klim, acc, -jnp.inf)
        return 0

    lax.fori_loop(0, nk, score_tile, 0)

    def count_ge(pivot):
        def body(j, acc):
            t = sc_ref[j]
            for c in range(bk // LANE):
                acc = acc + jnp.where(t[:, c * LANE:(c + 1) * LANE] >= pivot, 1.0, 0.0)
            return acc
        acc = lax.fori_loop(0, nk, body, jnp.zeros((bq, LANE), jnp.float32))
        return jnp.sum(acc, axis=1, keepdims=True)

    kf = jnp.float32(topk)
    nonneg = count_ge(jnp.zeros((bq, 1), jnp.float32)) >= kf
    key0 = jnp.where(nonneg, jnp.int32(0), jnp.int32(-2 ** 31))

    def bit_step(it, key):
        cand = key | (jnp.int32(1) << (30 - it))
        ok = (count_ge(_key_to_f32(cand)) >= kf) | (cand <= _KEY_NEG_INF)
        return jnp.where(ok, cand, key)

    key = lax.fori_loop(0, 31, bit_step, key0)
    thr = jnp.maximum(_key_to_f32(key), _F32_LOWEST)

    for h in range(DSA_HEADS):
        g = h // DSA_GROUP
        q = dq_ref[0, :, h * LANE:(h + 1) * LANE]

        def body(j, carry, q=q, g=g):
            m, l, acc = carry
            start = pl.multiple_of(j * bk, bk)
            kt = dk_ref[0, pl.ds(start, bk), g * LANE:(g + 1) * LANE]
            vt = dv_ref[0, pl.ds(start, bk), g * LANE:(g + 1) * LANE]
            s = jnp.where(sc_ref[j] >= thr, _dot_nt(q, kt), NEG_INF)
            m_new = jnp.maximum(m, jnp.max(s, axis=1, keepdims=True))
            alpha = jnp.exp(m - m_new)
            p = jnp.exp(s - m_new)
            l = alpha * l + jnp.sum(p, axis=1, keepdims=True)
            acc = alpha * acc + _dot(p.astype(jnp.bfloat16), vt)
            return m_new, l, acc

        init = (jnp.full((bq, 1), NEG_INF, jnp.float32), jnp.zeros((bq, 1), jnp.float32),
                jnp.zeros((bq, LANE), jnp.float32))
        m, l, acc = lax.fori_loop(0, nk, body, init)
        o_ref[0, :, h * LANE:(h + 1) * LANE] = (acc / l).astype(o_ref.dtype)


def dsa_attention(iq, iqs, iw, dq, ik, dk, dv, *, bq, bk, q_off, topk):
    b, lq, _ = dq.shape
    sp = ik.shape[1]
    nk_max = sp // bk
    qrow = lambda w: pl.BlockSpec((1, bq, w), lambda b_, i: (b_, i, 0))
    krow = lambda w: _resident((1, sp, w), lambda b_, i: (b_, 0, 0))
    return pl.pallas_call(
        functools.partial(_dsa_kernel, bq=bq, bk=bk, q_off=q_off, nk_max=nk_max, topk=topk),
        grid=(b, lq // bq),
        in_specs=[qrow(IDX_W), qrow(IDX_W), qrow(LANE), qrow(DSA_W), krow(LANE), krow(DSA_KV_W), krow(DSA_KV_W)],
        out_specs=qrow(DSA_W),
        out_shape=jax.ShapeDtypeStruct((b, lq, DSA_W), jnp.bfloat16),
        scratch_shapes=[pltpu.VMEM((nk_max, bq, bk), jnp.float32)],
        compiler_params=_cparams(("parallel", "arbitrary")),
        name="dsa_attention",
    )(iq, iqs, iw, dq, ik, dk, dv)


def _out_proj_kernel(x_ref, a_ref, b_ref, c_ref, wa_ref, wb_ref, wc_ref, o_ref):
    o_ref[...] = (x_ref[...] + _dot(a_ref[...], wa_ref[...]) + _dot(b_ref[...], wb_ref[...])
                  + _dot(c_ref[...], wc_ref[...]))


def out_proj(x, a, b, c, wa, wb, wc, *, bm):
    m, d = x.shape
    row = lambda w: pl.BlockSpec((bm, w), lambda i: (i, 0))
    full = lambda w: _resident(w.shape, lambda i: (0, 0))
    return pl.pallas_call(
        _out_proj_kernel,
        grid=(m // bm,),
        in_specs=[row(d), row(a.shape[1]), row(b.shape[1]), row(c.shape[1]), full(wa), full(wb), full(wc)],
        out_specs=row(d),
        out_shape=jax.ShapeDtypeStruct((m, d), jnp.float32),
        compiler_params=_cparams(("parallel",)),
        name="out_proj",
    )(x, a, b, c, wa, wb, wc)


def _ffn_kernel(x_ref, g_ref, wg_ref, wu_ref, wd_ref, o_ref, xn_ref, acc_ref):
    j = pl.program_id(1)

    @pl.when(j == 0)
    def _():
        xn_ref[...] = _rms(x_ref[...], g_ref[...]).astype(jnp.bfloat16)
        acc_ref[...] = jnp.zeros_like(acc_ref)

    xn = xn_ref[...]
    gate = _dot(xn, wg_ref[...])
    up = _dot(xn, wu_ref[...])
    hidden = (gate * (1.0 / (1.0 + jnp.exp(-gate))) * up).astype(jnp.bfloat16)
    acc_ref[...] += _dot(hidden, wd_ref[...])

    @pl.when(j == pl.num_programs(1) - 1)
    def _():
        o_ref[...] = x_ref[...] + acc_ref[...]


def ffn(x, g, wg, wu, wd, *, bm, bf):
    m, d = x.shape
    f = wg.shape[1]
    return pl.pallas_call(
        _ffn_kernel,
        grid=(m // bm, f // bf),
        in_specs=[pl.BlockSpec((bm, d), lambda i, j: (i, 0)),
                  pl.BlockSpec((1, d), lambda i, j: (0, 0)),
                  pl.BlockSpec((d, bf), lambda i, j: (0, j)),
                  pl.BlockSpec((d, bf), lambda i, j: (0, j)),
                  pl.BlockSpec((bf, d), lambda i, j: (j, 0))],
        out_specs=pl.BlockSpec((bm, d), lambda i, j: (i, 0)),
        out_shape=jax.ShapeDtypeStruct((m, d), jnp.float32),
        scratch_shapes=[pltpu.VMEM((bm, d), jnp.bfloat16), pltpu.VMEM((bm, d), jnp.float32)],
        compiler_params=_cparams(("parallel", "arbitrary")),
        name="swiglu",
    )(x, g.reshape(1, d), wg, wu, wd)


def _final_norm_kernel(x_ref, g_ref, o_ref):
    o_ref[...] = _rms(x_ref[...], g_ref[...])


def final_norm(x, g, *, bm):
    m, d = x.shape
    return pl.pallas_call(
        _final_norm_kernel,
        grid=(m // bm,),
        in_specs=[pl.BlockSpec((bm, d), lambda i: (i, 0)), pl.BlockSpec((1, d), lambda i: (0, 0))],
        out_specs=pl.BlockSpec((bm, d), lambda i: (i, 0)),
        out_shape=jax.ShapeDtypeStruct((m, d), jnp.float32),
        compiler_params=_cparams(("parallel",)),
        name="final_norm",
    )(x, g.reshape(1, d))


def _pad_cols(a, width):
    return jnp.pad(a, [(0, 0)] * (a.ndim - 1) + [(0, width - a.shape[-1])])


def _layout_w_in(w_in):
    sizes = (FOX_W, FOX_W, FOX_W, FOX_HEADS, DSA_W, DSA_KV_W, DSA_KV_W, IDX_W, IDX_DIM, IDX_HEADS,
             MLA_Q_LORA, MLA_KV_LORA, MLA_ROPE)
    splits = np.cumsum(sizes)[:-1]
    q_a, k_a, v_a, f_a, q_b, k_b, v_b, q_i, k_i, w_i, c_q, c_kv, k_r = jnp.split(w_in, splits, axis=-1)
    parts = {"fq": q_a, "fk": k_a, "fv": v_a, "dq": q_b, "dk": k_b, "dv": v_b, "iq": q_i, "cq": c_q,
             "ckv": c_kv, "fa": f_a, "ik": k_i, "iw": w_i, "kr": k_r}
    cols = [_pad_cols(parts[n], w) for n, w in _SEGS]
    return _pad_cols(jnp.concatenate(cols, axis=-1), PROJ_W).astype(jnp.bfloat16)


def _layout_w_uq(w_uq):
    dp = w_uq.shape[0]
    w = w_uq.reshape(dp, MLA_Q_LORA, MLA_HEADS, MLA_NOPE + MLA_ROPE)
    return _pad_cols(w, MLA_QP).reshape(dp, MLA_Q_LORA, MLA_QW).astype(jnp.bfloat16)


def _layout_w_ukv(w_ukv):
    dp = w_ukv.shape[0]
    w = w_ukv.reshape(dp, MLA_KV_LORA, MLA_HEADS, MLA_NOPE + MLA_V)
    kn = w[..., :MLA_NOPE].reshape(dp, MLA_KV_LORA, MLA_HEADS * MLA_NOPE)
    vv = w[..., MLA_NOPE:].reshape(dp, MLA_KV_LORA, MLA_VW)
    return jnp.concatenate([kn, vv], axis=-1).astype(jnp.bfloat16)


def _rope_table(pos, rot, period):
    half = rot // 2
    inv_freq = ROPE_THETA ** (-jnp.arange(half, dtype=jnp.float32) / half)
    ang = pos.astype(jnp.float32)[:, None] * inv_freq[None, :]
    cos, sin = jnp.cos(ang), jnp.sin(ang)
    r = pos.shape[0]
    ones = jnp.ones((r, period - rot), jnp.float32)
    zeros = jnp.zeros((r, period - rot), jnp.float32)
    zh = jnp.zeros((r, half), jnp.float32)
    c = jnp.concatenate([cos, cos, ones], axis=1)
    s1 = jnp.concatenate([zh, sin, zeros], axis=1)
    s2 = jnp.concatenate([-sin, zh, zeros], axis=1)
    rep = LANE // period
    return jnp.concatenate([jnp.tile(c, (1, rep)), jnp.tile(s1, (1, rep)), jnp.tile(s2, (1, rep))], axis=1)


def _pick(n, candidates):
    for c in candidates:
        if n % c == 0:
            return c
    raise ValueError(f"no block size among {candidates} divides {n}")


def _pad_seq(a, sp):
    return jnp.pad(a, [(0, 0), (0, sp - a.shape[1])] + [(0, 0)] * (a.ndim - 2))


def kernel(x_prompt, x_sample, cache_fox_k, cache_fox_v, cache_fox_logf, cache_dsa_k, cache_dsa_v, cache_idx_k, cache_mla_ckv, cache_mla_krope, w_in, fox_bias, mla_q_norm, mla_w_uq, mla_kv_norm, mla_w_ukv, w_o, attn_norm, ffn_norm, w_gate, w_up, w_down, final_norm_w):
    bf = jnp.bfloat16
    depth = w_in.shape[0]
    nb, seq, d = x_prompt.shape
    db, dec, _ = x_sample.shape
    past = cache_fox_k.shape[2]
    assert nb == 1 and seq % CHUNK == 0 and dec % CHUNK == 0 and past % CHUNK == 0
    mp, ms = nb * seq, db * dec
    m = mp + ms
    s_len = past + dec

    bm = _pick(m, (512, 256, 128, 64))
    bq_p = _pick(seq, (256, 128, 64))
    bk_p = _pick(seq, (512, 256, 128))
    bq_d = _pick(seq, (128, 64))
    bq_s = _pick(dec, (64,))
    bk_s = 256
    sp = ((s_len + bk_s - 1) // bk_s) * bk_s
    topk_p = min(IDX_TOPK_MAX, seq // 4)
    topk_s = min(IDX_TOPK_MAX, s_len // 4)
    d_ff = w_gate.shape[2]
    bff = _pick(d_ff, (512, 256, 128))

    w_in_p = _layout_w_in(w_in)
    wuq_p = _layout_w_uq(mla_w_uq)
    wukv_p = _layout_w_ukv(mla_w_ukv)
    w_o_b = w_o.astype(bf)
    wg_b, wu_b, wd_b = w_gate.astype(bf), w_up.astype(bf), w_down.astype(bf)
    fox_bias_p = _pad_cols(fox_bias, LANE).reshape(depth, 1, LANE)

    pos = jnp.concatenate([jnp.tile(jnp.arange(seq, dtype=jnp.int32), nb),
                           jnp.tile(past + jnp.arange(dec, dtype=jnp.int32), db)])
    tab = jnp.concatenate([_rope_table(pos, PARTIAL_ROT, HEAD_DIM), _rope_table(pos, IDX_ROT, IDX_DIM),
                           _rope_table(pos, MLA_ROPE, LANE)], axis=1)

    x = jnp.concatenate([x_prompt.reshape(mp, d), x_sample.reshape(ms, d)], axis=0)
    p_rows, s_rows = [], []
    for l in range(depth):
        proj = norm_matmul(x, attn_norm[l], w_in_p[l], bm=bm, bn=512)
        (fq, fk, fv, fkb, fvb, lf, dq, dk, dv, dkb, dvb, iq, iqs, ik, ikb, iw, mq, ckv, kr) = post_proj(
            proj, tab, fox_bias_p[l], mla_q_norm[l].reshape(1, -1), mla_kv_norm[l].reshape(1, -1), wuq_p[l], bm=bm)

        pr = lambda a: a[:mp].reshape(nb, seq, -1)
        sr = lambda a: a[mp:].reshape(db, dec, -1)
        p_rows.append((pr(fk).reshape(nb, seq, FOX_HEADS, HEAD_DIM), pr(fv).reshape(nb, seq, FOX_HEADS, HEAD_DIM),
                       pr(lf)[..., :FOX_HEADS], pr(dk).reshape(nb, seq, DSA_KV_HEADS, HEAD_DIM),
                       pr(dv).reshape(nb, seq, DSA_KV_HEADS, HEAD_DIM), pr(ik)[..., :IDX_DIM], pr(ckv),
                       pr(kr)[..., :MLA_ROPE]))
        s_rows.append((sr(fk).reshape(db, dec, FOX_HEADS, HEAD_DIM), sr(fv).reshape(db, dec, FOX_HEADS, HEAD_DIM),
                       sr(lf)[..., :FOX_HEADS], sr(dk).reshape(db, dec, DSA_KV_HEADS, HEAD_DIM),
                       sr(dv).reshape(db, dec, DSA_KV_HEADS, HEAD_DIM), sr(ik)[..., :IDX_DIM], sr(ckv),
                       sr(kr)[..., :MLA_ROPE]))

        cat = lambda c, new: _pad_seq(jnp.concatenate([c, new], axis=1), sp)
        s_fkb = cat(cache_fox_k[l].reshape(db, past, FOX_W).astype(bf), sr(fkb))
        s_fvb = cat(cache_fox_v[l].reshape(db, past, FOX_W).astype(bf), sr(fvb))
        s_lf = cat(_pad_cols(cache_fox_logf[l], LANE), sr(lf))
        s_dkb = cat(cache_dsa_k[l].reshape(db, past, DSA_KV_W).astype(bf), sr(dkb))
        s_dvb = cat(cache_dsa_v[l].reshape(db, past, DSA_KV_W).astype(bf), sr(dvb))
        s_ikb = cat(_pad_cols(cache_idx_k[l], LANE).astype(bf), sr(ikb))
        s_ckv = cat(cache_mla_ckv[l], sr(ckv))
        s_kr = cat(_pad_cols(cache_mla_krope[l], LANE), sr(kr))

        ckv_all = jnp.concatenate([ckv[:mp], s_ckv.reshape(db * sp, -1)], axis=0)
        kr_all = jnp.concatenate([kr[:mp], s_kr.reshape(db * sp, -1)], axis=0)
        kmla, vmla = kv_up(ckv_all, kr_all, wukv_p[l], bm=_pick(ckv_all.shape[0], (512, 256, 128, 64)))
        p_kmla, p_vmla = kmla[:mp].reshape(nb, seq, -1), vmla[:mp].reshape(nb, seq, -1)
        s_kmla, s_vmla = kmla[mp:].reshape(db, sp, -1), vmla[mp:].reshape(db, sp, -1)

        p_c, p_ct = cumsum_rows(pr(lf), bt=_pick(seq, (256, 128)))
        s_c, s_ct = cumsum_rows(s_lf, bt=_pick(sp, (256, 128)))
        ck_form = lambda ct, bk: ct[:, :FOX_HEADS].reshape(ct.shape[0], FOX_HEADS, ct.shape[2] // bk, 1, bk)

        out_a = flash_attention(pr(fq), pr(fkb), pr(fvb), p_c, ck_form(p_ct, bk_p), heads=FOX_HEADS,
                                dqk=HEAD_DIM, dv=HEAD_DIM, bq=bq_p, bk=bk_p, q_off=0)
        out_b = dsa_attention(pr(iq), pr(iqs), pr(iw), pr(dq), pr(ikb), pr(dkb), pr(dvb),
                              bq=bq_d, bk=bk_p, q_off=0, topk=topk_p)
        out_c = flash_attention(pr(mq), p_kmla, p_vmla, heads=MLA_HEADS, dqk=MLA_QP, dv=MLA_V,
                                bq=bq_p, bk=bk_p, q_off=0)
        s_out_a = flash_attention(sr(fq), s_fkb, s_fvb, s_c[:, past:past + dec], ck_form(s_ct, bk_s),
                                  heads=FOX_HEADS, dqk=HEAD_DIM, dv=HEAD_DIM, bq=bq_s, bk=bk_s, q_off=past)
        s_out_b = dsa_attention(sr(iq), sr(iqs), sr(iw), sr(dq), s_ikb, s_dkb, s_dvb,
                                bq=bq_s, bk=bk_s, q_off=past, topk=topk_s)
        s_out_c = flash_attention(sr(mq), s_kmla, s_vmla, heads=MLA_HEADS, dqk=MLA_QP, dv=MLA_V,
                                  bq=bq_s, bk=bk_s, q_off=past)

        rows = lambda p, s: jnp.concatenate([p.reshape(mp, -1), s.reshape(ms, -1)], axis=0)
        x = out_proj(x, rows(out_a, s_out_a), rows(out_b, s_out_b), rows(out_c, s_out_c),
                     w_o_b[l, :FOX_W], w_o_b[l, FOX_W:FOX_W + DSA_W], w_o_b[l, FOX_W + DSA_W:], bm=bm)
        x = ffn(x, ffn_norm[l], wg_b[l], wu_b[l], wd_b[l], bm=bm, bf=bff)

    y = final_norm(x, final_norm_w, bm=bm)
    y_prompt = y[:mp].reshape(nb, seq, d)
    y_sample = y[mp:].reshape(db, dec, d)
    stack = lambda rows_: tuple(jnp.stack(t) for t in zip(*rows_))
    return (y_prompt, y_sample) + stack(p_rows) + stack(s_rows)
```

```python
import functools
import math

import jax
import jax.numpy as jnp
import numpy as np
from jax import lax
from jax.experimental import pallas as pl
from jax.experimental.pallas import tpu as pltpu

CHUNK = 64
HEAD_DIM = 128
FOX_HEADS = 6
DSA_HEADS = 6
DSA_KV_HEADS = 2
DSA_GROUP = DSA_HEADS // DSA_KV_HEADS
IDX_HEADS = 16
IDX_DIM = 64
IDX_TOPK_MAX = 256
MLA_HEADS = 4
MLA_Q_LORA = 512
MLA_KV_LORA = 256
MLA_NOPE = 128
MLA_ROPE = 64
MLA_V = 128
PARTIAL_ROT = HEAD_DIM // 4
IDX_ROT = IDX_DIM // 4
ROPE_THETA = 500000.0
EPS = 1e-6
NEG_INF = -1e30
FOX_SCALE = HEAD_DIM ** -0.5
DSA_SCALE = HEAD_DIM ** -0.5
MLA_SCALE = (MLA_NOPE + MLA_ROPE) ** -0.5
IDX_W_SCALE = (IDX_HEADS * IDX_DIM) ** -0.5
LOG2E = math.log2(math.e)

LANE = 128
VMEM_LIMIT = 56 * 1024 * 1024

FOX_W = FOX_HEADS * HEAD_DIM
DSA_W = DSA_HEADS * HEAD_DIM
DSA_KV_W = DSA_KV_HEADS * HEAD_DIM
IDX_W = IDX_HEADS * IDX_DIM
MLA_QP = 2 * LANE
MLA_QW = MLA_HEADS * MLA_QP
MLA_VW = MLA_HEADS * MLA_V
VR = HEAD_DIM + 16
FOX_QP = 2 * LANE

_SEGS = (("fq", FOX_W), ("fk", FOX_W), ("fv", FOX_W), ("dq", DSA_W), ("dk", DSA_KV_W), ("dv", DSA_KV_W),
         ("iq", IDX_W), ("cq", MLA_Q_LORA), ("ckv", MLA_KV_LORA),
         ("fa", LANE), ("ik", LANE), ("iw", LANE), ("kr", LANE))
_OFF = {}
_o = 0
for _n, _w in _SEGS:
    _OFF[_n] = _o
    _o += _w
PROJ_W = ((_o + 511) // 512) * 512

_NT = (((1,), (1,)), ((), ()))


def _dot(a, b):
    return jnp.dot(a, b, preferred_element_type=jnp.float32)


def _dot_nt(a, b):
    return lax.dot_general(a, b, _NT, preferred_element_type=jnp.float32)


def _cparams(sem):
    return pltpu.CompilerParams(dimension_semantics=sem, vmem_limit_bytes=VMEM_LIMIT)


def _resident(block_shape, index_map):
    return pl.BlockSpec(block_shape, index_map, pipeline_mode=pl.Buffered(1))


def _rms(x, g):
    return x * lax.rsqrt(jnp.mean(x * x, axis=-1, keepdims=True) + EPS) * g


def _norm_matmul_kernel(x_ref, g_ref, w_ref, o_ref, xn_ref):
    @pl.when(pl.program_id(1) == 0)
    def _():
        xn_ref[...] = _rms(x_ref[...], g_ref[...]).astype(jnp.bfloat16)

    o_ref[...] = _dot(xn_ref[...], w_ref[...])


def norm_matmul(x, g, w, *, bm, bn):
    m, d = x.shape
    n = w.shape[1]
    return pl.pallas_call(
        _norm_matmul_kernel,
        grid=(m // bm, n // bn),
        in_specs=[pl.BlockSpec((bm, d), lambda i, j: (i, 0)),
                  pl.BlockSpec((1, d), lambda i, j: (0, 0)),
                  pl.BlockSpec((d, bn), lambda i, j: (0, j))],
        out_specs=pl.BlockSpec((bm, bn), lambda i, j: (i, j)),
        out_shape=jax.ShapeDtypeStruct((m, n), jnp.float32),
        scratch_shapes=[pltpu.VMEM((bm, d), jnp.bfloat16)],
        compiler_params=_cparams(("parallel", "arbitrary")),
        name="norm_matmul",
    )(x, g.reshape(1, d), w)


def _rope(x, tab, half):
    c, s1, s2 = tab[:, 0:LANE], tab[:, LANE:2 * LANE], tab[:, 2 * LANE:3 * LANE]
    return x * c + pltpu.roll(x, half, 1) * s1 + pltpu.roll(x, LANE - half, 1) * s2


def _value_t_tail(width):
    r = lax.broadcasted_iota(jnp.int32, (16, width), 0)
    return jnp.where(r == 0, 1.0, 0.0).astype(jnp.bfloat16)


def _store_value_t(o_ref, h, v):
    o_ref[0, h * VR:h * VR + HEAD_DIM, :] = v.T.astype(jnp.bfloat16)
    o_ref[0, h * VR + HEAD_DIM:(h + 1) * VR, :] = _value_t_tail(v.shape[0])


def _post_kernel(p_ref, tab_ref, fb_ref, qn_ref, kvn_ref, wuq_ref,
                 fq_o, fk_o, fv_o, fkb_o, fvb_o, fvt_o, lf_o,
                 dq_o, dk_o, dv_o, dkb_o, dvb_o, dvt_o,
                 iq_o, iqs_o, ik_o, ikb_o, iw_o, iwt_o,
                 mq_o, ckv_o, kr_o):
    bf = jnp.bfloat16
    seg = lambda name, w: p_ref[:, _OFF[name]:_OFF[name] + w]
    tab_d = tab_ref[:, 0:3 * LANE]
    tab_i = tab_ref[:, 3 * LANE:6 * LANE]
    tab_m = tab_ref[:, 6 * LANE:9 * LANE]

    fq_o[...] = (seg("fq", FOX_W) * (FOX_SCALE * LOG2E)).astype(bf)
    fk = seg("fk", FOX_W)
    fk_o[...] = fk
    fkb_o[...] = fk.astype(bf)
    fv = seg("fv", FOX_W)
    fv_o[...] = fv
    fvb_o[...] = fv.astype(bf)
    for h in range(FOX_HEADS):
        _store_value_t(fvt_o, h, p_ref[:, _OFF["fv"] + h * LANE:_OFF["fv"] + (h + 1) * LANE])
    z = seg("fa", LANE) + fb_ref[...]
    lf_o[...] = jnp.minimum(z, 0.0) - jnp.log1p(jnp.exp(-jnp.abs(z)))

    for h in range(DSA_HEADS):
        x = p_ref[:, _OFF["dq"] + h * LANE:_OFF["dq"] + (h + 1) * LANE]
        dq_o[:, h * LANE:(h + 1) * LANE] = (_rope(x, tab_d, PARTIAL_ROT // 2) * (DSA_SCALE * LOG2E)).astype(bf)
    for h in range(DSA_KV_HEADS):
        x = p_ref[:, _OFF["dk"] + h * LANE:_OFF["dk"] + (h + 1) * LANE]
        y = _rope(x, tab_d, PARTIAL_ROT // 2)
        dk_o[:, h * LANE:(h + 1) * LANE] = y
        dkb_o[:, h * LANE:(h + 1) * LANE] = y.astype(bf)
        _store_value_t(dvt_o, h, p_ref[:, _OFF["dv"] + h * LANE:_OFF["dv"] + (h + 1) * LANE])
    dv = seg("dv", DSA_KV_W)
    dv_o[...] = dv
    dvb_o[...] = dv.astype(bf)
    for j in range(IDX_W // LANE):
        x = p_ref[:, _OFF["iq"] + j * LANE:_OFF["iq"] + (j + 1) * LANE]
        y = _rope(x, tab_i, IDX_ROT // 2)
        iq_o[:, j * LANE:(j + 1) * LANE] = y.astype(bf)
        iqs_o[:, j * LANE:(j + 1) * LANE] = pltpu.roll(y, IDX_DIM, 1).astype(bf)
    y = _rope(seg("ik", LANE), tab_i, IDX_ROT // 2)
    ik_o[...] = y
    ikb_o[...] = y.astype(bf)
    iw = seg("iw", LANE) * IDX_W_SCALE
    iw_o[...] = iw
    iwt_o[...] = iw.T

    cqn = _rms(seg("cq", MLA_Q_LORA), qn_ref[...]).astype(bf)
    mq = _dot(cqn, wuq_ref[...])
    for h in range(MLA_HEADS):
        a = h * MLA_QP
        mq_o[:, a:a + LANE] = (mq[:, a:a + LANE] * (MLA_SCALE * LOG2E)).astype(bf)
        r = _rope(mq[:, a + LANE:a + 2 * LANE], tab_m, MLA_ROPE // 2)
        mq_o[:, a + LANE:a + 2 * LANE] = (r * (MLA_SCALE * LOG2E)).astype(bf)
    ckv_o[...] = _rms(seg("ckv", MLA_KV_LORA), kvn_ref[...])
    kr_o[...] = _rope(seg("kr", LANE), tab_m, MLA_ROPE // 2)


def post_proj(proj, tab, fox_bias_p, q_norm, kv_norm, wuq_p, *, bm, bk):
    m = proj.shape[0]
    r = bk // bm
    f32, bf = jnp.float32, jnp.bfloat16
    row = lambda w: pl.BlockSpec((bm, w), lambda i: (i, 0))
    full = lambda a: pl.BlockSpec(a.shape, lambda i: (0,) * a.ndim)
    vt = lambda heads: pl.BlockSpec((1, heads * VR, bm), lambda i: (i // r, 0, i % r))
    rowo = lambda w, dt: (row(w), jax.ShapeDtypeStruct((m, w), dt))
    vto = lambda heads: (vt(heads), jax.ShapeDtypeStruct((m // bk, heads * VR, bk), bf))
    outs = [rowo(FOX_W, bf), rowo(FOX_W, f32), rowo(FOX_W, f32), rowo(FOX_W, bf), rowo(FOX_W, bf), vto(FOX_HEADS),
            rowo(LANE, f32),
            rowo(DSA_W, bf), rowo(DSA_KV_W, f32), rowo(DSA_KV_W, f32), rowo(DSA_KV_W, bf), rowo(DSA_KV_W, bf),
            vto(DSA_KV_HEADS),
            rowo(IDX_W, bf), rowo(IDX_W, bf), rowo(LANE, f32), rowo(LANE, bf), rowo(LANE, f32),
            (pl.BlockSpec((LANE, bm), lambda i: (0, i)), jax.ShapeDtypeStruct((LANE, m), f32)),
            rowo(MLA_QW, bf), rowo(MLA_KV_LORA, f32), rowo(LANE, f32)]
    return pl.pallas_call(
        _post_kernel,
        grid=(m // bm,),
        in_specs=[row(PROJ_W), row(9 * LANE), full(fox_bias_p), full(q_norm), full(kv_norm), full(wuq_p)],
        out_specs=[s for s, _ in outs],
        out_shape=[o for _, o in outs],
        compiler_params=_cparams(("parallel",)),
        name="post_proj",
    )(proj, tab, fox_bias_p, q_norm, kv_norm, wuq_p)


def _kv_up_kernel(ckv_ref, kr_ref, w_ref, k_o, v_o, vt_o):
    bf = jnp.bfloat16
    kv = _dot(ckv_ref[...].astype(bf), w_ref[...])
    kr = kr_ref[...].astype(bf)
    for h in range(MLA_HEADS):
        k_o[:, h * MLA_QP:h * MLA_QP + LANE] = kv[:, h * LANE:(h + 1) * LANE].astype(bf)
        k_o[:, h * MLA_QP + LANE:(h + 1) * MLA_QP] = kr
        _store_value_t(vt_o, h, kv[:, (MLA_HEADS + h) * LANE:(MLA_HEADS + h + 1) * LANE])
    v_o[...] = kv[:, MLA_HEADS * LANE:].astype(bf)


def kv_up(ckv, kr, w, *, bm):
    r = ckv.shape[0]
    row = lambda w_: pl.BlockSpec((bm, w_), lambda i: (i, 0))
    return pl.pallas_call(
        _kv_up_kernel,
        grid=(r // bm,),
        in_specs=[row(MLA_KV_LORA), row(LANE), pl.BlockSpec(w.shape, lambda i: (0, 0))],
        out_specs=[row(MLA_QW), row(MLA_VW), pl.BlockSpec((1, MLA_HEADS * VR, bm), lambda i: (i, 0, 0))],
        out_shape=[jax.ShapeDtypeStruct((r, MLA_QW), jnp.bfloat16),
                   jax.ShapeDtypeStruct((r, MLA_VW), jnp.bfloat16),
                   jax.ShapeDtypeStruct((r // bm, MLA_HEADS * VR, bm), jnp.bfloat16)],
        compiler_params=_cparams(("parallel",)),
        name="mla_kv_up",
    )(ckv, kr, w)


def _cumsum_kernel(x_ref, c_o, ct_o, carry_ref, carry_t_ref):
    t = x_ref.shape[1]

    @pl.when(pl.program_id(1) == 0)
    def _():
        carry_ref[...] = jnp.zeros_like(carry_ref)
        carry_t_ref[...] = jnp.zeros_like(carry_t_ref)

    x = x_ref[0]
    r = lax.broadcasted_iota(jnp.int32, (t, t), 0)
    c = lax.broadcasted_iota(jnp.int32, (t, t), 1)
    lower = jnp.where(c <= r, 1.0, 0.0).astype(jnp.float32)
    upper = jnp.where(r <= c, 1.0, 0.0).astype(jnp.float32)
    cs = jnp.dot(lower, x, preferred_element_type=jnp.float32, precision=lax.Precision.HIGHEST)
    cs = cs + carry_ref[...]
    c_o[0] = cs * LOG2E
    carry_ref[...] = cs[t - 1:t, :]
    cst = jnp.dot(x.T, upper, preferred_element_type=jnp.float32, precision=lax.Precision.HIGHEST)
    cst = cst + carry_t_ref[...]
    ct_o[0] = cst[0:8, :] * LOG2E
    carry_t_ref[...] = cst[:, t - 1:t]


def cumsum_rows(x, *, bt):
    b, s, _ = x.shape
    return pl.pallas_call(
        _cumsum_kernel,
        grid=(b, s // bt),
        in_specs=[pl.BlockSpec((1, bt, LANE), lambda i, j: (i, j, 0))],
        out_specs=[pl.BlockSpec((1, bt, LANE), lambda i, j: (i, j, 0)),
                   pl.BlockSpec((1, 8, bt), lambda i, j: (i, 0, j))],
        out_shape=[jax.ShapeDtypeStruct((b, s, LANE), jnp.float32),
                   jax.ShapeDtypeStruct((b, 8, s), jnp.float32)],
        scratch_shapes=[pltpu.VMEM((1, LANE), jnp.float32), pltpu.VMEM((LANE, 1), jnp.float32)],
        compiler_params=_cparams(("parallel", "arbitrary")),
        name="logf_cumsum",
    )(x)


def _num_key_tiles(i, bq, bk, q_off, nk_max):
    last = q_off + (i + 1) * bq
    return jnp.minimum((last + bk - 1) // bk, nk_max)


def _softmax_step(s, vt, m_ref, l_ref, acc_ref, t):
    m_prev = m_ref[t]
    m_new = jnp.maximum(m_prev, jnp.max(s, axis=1, keepdims=True))
    alpha = jnp.exp2(m_prev - m_new)
    p = jnp.exp2(s - m_new)
    l_ref[t] = alpha * l_ref[t] + jnp.sum(p, axis=1, keepdims=True)
    acc_ref[t] = alpha * acc_ref[t] + _dot(p.astype(jnp.bfloat16), vt)
    m_ref[t] = m_new


def _flash_kernel(*refs, bq, bk, q_off, nk_max, fox, hg, dqk, dv):
    if fox:
        q_ref, k_ref, v_ref, cq_ref, ck_ref, o_ref, m_ref, l_ref, acc_ref = refs
    else:
        q_ref, k_ref, v_ref, o_ref, m_ref, l_ref, acc_ref = refs
    gi = pl.program_id(1)
    i = pl.program_id(2)
    first = q_off + i * bq
    qpos = first + lax.broadcasted_iota(jnp.int32, (bq, 1), 0)
    if fox:
        lane = lax.broadcasted_iota(jnp.int32, (bq, LANE), 1)
        cq_all = cq_ref[0]
        cqs = [jnp.sum(jnp.where(lane == gi * hg + t, cq_all, 0.0), axis=1, keepdims=True) for t in range(hg)]
        c0 = [c[0:1, :] for c in cqs]
        bias_q = [c - z for c, z in zip(cqs, c0)]
    nk = _num_key_tiles(i, bq, bk, q_off, nk_max)
    n_full = jnp.minimum((first + (1 if fox else CHUNK)) // bk, nk)

    m_ref[...] = jnp.full(m_ref.shape, NEG_INF, jnp.float32)
    l_ref[...] = jnp.zeros(l_ref.shape, jnp.float32)
    acc_ref[...] = jnp.zeros(acc_ref.shape, jnp.float32)

    def make_body(masked):
        def body(j, _):
            start = pl.multiple_of(j * bk, bk)
            if masked:
                kpos = j * bk + lax.broadcasted_iota(jnp.int32, (1, bk), 1)
                mask = (kpos <= qpos) if fox else ((kpos >> 6) <= (qpos >> 6))
            for t in range(hg):
                kt = k_ref[0, pl.ds(start, bk), t * dqk:(t + 1) * dqk]
                vt = v_ref[0, pl.ds(start, bk), t * dv:(t + 1) * dv]
                s = _dot_nt(q_ref[0, :, t * dqk:(t + 1) * dqk], kt)
                if fox:
                    s = s + (bias_q[t] - (ck_ref[0, t, j] - c0[t]))
                if masked:
                    s = jnp.where(mask, s, NEG_INF)
                _softmax_step(s, vt, m_ref, l_ref, acc_ref, t)
            return 0
        return body

    lax.fori_loop(0, n_full, make_body(False), 0)
    lax.fori_loop(n_full, nk, make_body(True), 0)
    for t in range(hg):
        o_ref[0, :, t * dv:(t + 1) * dv] = (acc_ref[t] / l_ref[t]).astype(o_ref.dtype)


def flash_attention(q, k, v, cq=None, ck=None, *, lq, sp, heads, hg, dqk, dv, bq, bk, q_off):
    b = q.shape[0]
    nk_max = sp // bk
    fox = cq is not None
    in_specs = [pl.BlockSpec((1, bq, hg * dqk), lambda b_, g, i: (b_, i, g)),
                _resident((1, sp, hg * dqk), lambda b_, g, i: (b_, 0, g)),
                _resident((1, sp, hg * dv), lambda b_, g, i: (b_, 0, g))]
    args = [q, k, v]
    if fox:
        in_specs += [pl.BlockSpec((1, bq, LANE), lambda b_, g, i: (b_, i, 0)),
                     _resident((1, hg, nk_max, 1, bk), lambda b_, g, i: (b_, g, 0, 0, 0))]
        args += [cq, ck]
    return pl.pallas_call(
        functools.partial(_flash_kernel, bq=bq, bk=bk, q_off=q_off, nk_max=nk_max, fox=fox, hg=hg, dqk=dqk, dv=dv),
        grid=(b, heads // hg, lq // bq),
        in_specs=in_specs,
        out_specs=pl.BlockSpec((1, bq, hg * dv), lambda b_, g, i: (b_, i, g)),
        out_shape=jax.ShapeDtypeStruct((b, lq, heads * dv), jnp.bfloat16),
        scratch_shapes=[pltpu.VMEM((hg, bq, 1), jnp.float32), pltpu.VMEM((hg, bq, 1), jnp.float32),
                        pltpu.VMEM((hg, bq, dv), jnp.float32)],
        compiler_params=_cparams(("parallel", "parallel", "arbitrary")),
        name="fox_attention" if fox else "mla_attention",
    )(*args)


_KEY_NEG_INF = -2139095041
_F32_LOWEST = -3.4028234663852886e38


_FOLD = 64
_MAX_PROBES = 8 * 34


def _key_to_f32(key):
    bits = jnp.where(key >= 0, key, key ^ jnp.int32(0x7FFFFFFF))
    return lax.bitcast_convert_type(bits, jnp.float32)


def _f32_to_key(x):
    bits = lax.bitcast_convert_type(x, jnp.int32)
    return jnp.where(bits >= 0, bits, bits ^ jnp.int32(0x7FFFFFFF))


def _dsa_kernel(iq_ref, iqs_ref, iw_ref, dq_ref, ik_ref, dk_ref, dv_ref, o_ref,
                sc_ref, qg_ref, m_ref, l_ref, acc_ref, *, bq, bk, q_off, nk_max, topk):
    i = pl.program_id(1)
    nk = _num_key_tiles(i, bq, bk, q_off, nk_max)
    qpos = q_off + i * bq + lax.broadcasted_iota(jnp.int32, (bq, 1), 0)
    klim = ((qpos >> 6) + 1) << 6
    iw = iw_ref[0]

    def score_tile(j, _):
        kt = ik_ref[0, pl.ds(pl.multiple_of(j * bk, bk), bk), :]
        acc = jnp.zeros((bq, bk), jnp.float32)
        for g in range(IDX_HEADS // 2):
            se = _dot_nt(iq_ref[0, :, g * LANE:(g + 1) * LANE], kt)
            so = _dot_nt(iqs_ref[0, :, g * LANE:(g + 1) * LANE], kt)
            acc = acc + iw[:, 2 * g:2 * g + 1] * jnp.maximum(se, 0.0)
            acc = acc + iw[:, 2 * g + 1:2 * g + 2] * jnp.maximum(so, 0.0)
        kpos = j * bk + lax.broadcasted_iota(jnp.int32, (1, bk), 1)
        sc_ref[j] = jnp.where(kpos < klim, acc, -jnp.inf)
        return 0

    lax.fori_loop(0, nk, score_tile, 0)

    def count_ge(pivot):
        def body(j, acc):
            t = sc_ref[j]
            for c in range(bk // LANE):
                acc = acc + jnp.where(t[:, c * LANE:(c + 1) * LANE] >= pivot, 1.0, 0.0)
            return acc
        acc = lax.fori_loop(0, nk, body, jnp.zeros((bq, LANE), jnp.float32))
        return jnp.sum(acc, axis=1, keepdims=True)

    kf = jnp.float32(topk)
    nonneg = count_ge(jnp.zeros((bq, 1), jnp.float32)) >= kf
    key0 = jnp.where(nonneg, jnp.int32(0), jnp.int32(-2 ** 31))

    def bit_step(it, key):
        cand = key | (jnp.int32(1) << (30 - it))
        ok = (count_ge(_key_to_f32(cand)) >= kf) | (cand <= _KEY_NEG_INF)
        return jnp.where(ok, cand, key)

    key = lax.fori_loop(0, 31, bit_step, key0)
    thr = jnp.maximum(_key_to_f32(key), _F32_LOWEST)

    for h in range(DSA_HEADS):
        g, r = divmod(h, DSA_GROUP)
        qg_ref[g, r * bq:(r + 1) * bq, :] = dq_ref[0, :, h * LANE:(h + 1) * LANE]
    m_ref[...] = jnp.full(m_ref.shape, NEG_INF, jnp.float32)
    l_ref[...] = jnp.zeros(l_ref.shape, jnp.float32)
    acc_ref[...] = jnp.zeros(acc_ref.shape, jnp.float32)

    def att_tile(j, _):
        start = pl.multiple_of(j * bk, bk)
        drop = jnp.where(sc_ref[j] >= thr, 0.0, NEG_INF)
        drop = jnp.concatenate([drop] * DSA_GROUP, axis=0)
        for g in range(DSA_KV_HEADS):
            kt = dk_ref[0, pl.ds(start, bk), g * LANE:(g + 1) * LANE]
            vt = dv_ref[0, pl.ds(start, bk), g * LANE:(g + 1) * LANE]
            s = _dot_nt(qg_ref[g], kt) + drop
            _softmax_step(s, vt, m_ref, l_ref, acc_ref, g)
        return 0

    lax.fori_loop(0, nk, att_tile, 0)
    for h in range(DSA_HEADS):
        g, r = divmod(h, DSA_GROUP)
        rows = slice(r * bq, (r + 1) * bq)
        o_ref[0, :, h * LANE:(h + 1) * LANE] = (acc_ref[g, rows, :] / l_ref[g, rows, :]).astype(o_ref.dtype)


def dsa_attention(iq, iqs, iw, dq, ik, dk, dv, *, lq, sp, bq, bk, q_off, topk):
    b = dq.shape[0]
    nk_max = sp // bk
    qrow = lambda w: pl.BlockSpec((1, bq, w), lambda b_, i: (b_, i, 0))
    krow = lambda w: _resident((1, sp, w), lambda b_, i: (b_, 0, 0))
    gr = DSA_GROUP * bq
    return pl.pallas_call(
        functools.partial(_dsa_kernel, bq=bq, bk=bk, q_off=q_off, nk_max=nk_max, topk=topk),
        grid=(b, lq // bq),
        in_specs=[qrow(IDX_W), qrow(IDX_W), qrow(LANE), qrow(DSA_W), krow(LANE), krow(DSA_KV_W), krow(DSA_KV_W)],
        out_specs=qrow(DSA_W),
        out_shape=jax.ShapeDtypeStruct((b, lq, DSA_W), jnp.bfloat16),
        scratch_shapes=[pltpu.VMEM((nk_max, bq, bk), jnp.float32),
                        pltpu.VMEM((DSA_KV_HEADS, gr, LANE), jnp.bfloat16),
                        pltpu.VMEM((DSA_KV_HEADS, gr, 1), jnp.float32),
                        pltpu.VMEM((DSA_KV_HEADS, gr, 1), jnp.float32),
                        pltpu.VMEM((DSA_KV_HEADS, gr, LANE), jnp.float32)],
        compiler_params=_cparams(("parallel", "arbitrary")),
        name="dsa_attention",
    )(iq, iqs, iw, dq, ik, dk, dv)


def _split3(x):
    hi = x.astype(jnp.bfloat16).astype(jnp.float32)
    r = x - hi
    mid = r.astype(jnp.bfloat16).astype(jnp.float32)
    lo = (r - mid).astype(jnp.bfloat16).astype(jnp.float32)
    return hi, mid, lo


def _fox_pack_kernel(q_ref, k_ref, c_ref, qa_o, ka_o):
    rows = q_ref.shape[0]
    lane = lax.broadcasted_iota(jnp.int32, (rows, LANE), 1)
    c = c_ref[...]
    zero = jnp.zeros((rows, LANE), jnp.float32)
    for h in range(FOX_HEADS):
        ch = jnp.sum(jnp.where(lane == h, c, 0.0), axis=1, keepdims=True)
        hi, mid, lo = _split3(ch)
        terms = jnp.where(lane == 0, hi, jnp.where(lane == 1, mid, jnp.where(lane == 2, lo, zero)))
        ones_hi = jnp.where((lane >= 3) & (lane < 6), 1.0, 0.0)
        qa_o[:, h * FOX_QP:h * FOX_QP + LANE] = q_ref[:, h * LANE:(h + 1) * LANE]
        qa_o[:, h * FOX_QP + LANE:(h + 1) * FOX_QP] = (terms + ones_hi).astype(jnp.bfloat16)
        ones_lo = jnp.where(lane < 3, 1.0, 0.0)
        ka_o[:, h * FOX_QP:h * FOX_QP + LANE] = k_ref[:, h * LANE:(h + 1) * LANE]
        ka_o[:, h * FOX_QP + LANE:(h + 1) * FOX_QP] = (ones_lo - pltpu.roll(terms, 3, 1)).astype(jnp.bfloat16)


def fox_pack(q, k, c, *, rows, bm):
    row = lambda w: pl.BlockSpec((bm, w), lambda i: (i, 0))
    return pl.pallas_call(
        _fox_pack_kernel,
        grid=(rows // bm,),
        in_specs=[row(FOX_W), row(FOX_W), row(LANE)],
        out_specs=[row(FOX_HEADS * FOX_QP), row(FOX_HEADS * FOX_QP)],
        out_shape=[jax.ShapeDtypeStruct((rows, FOX_HEADS * FOX_QP), jnp.bfloat16)] * 2,
        compiler_params=_cparams(("parallel",)),
        name="fox_pack",
    )(q, k, c)


def _softmax_steps_t(sts, vtts, m_ref, acc_ref):
    ps, alphas = [], []
    for t, st in enumerate(sts):
        m_prev = m_ref[t]
        m_new = jnp.maximum(m_prev, jnp.max(st, axis=0, keepdims=True))
        alphas.append(jnp.exp2(m_prev - m_new))
        ps.append(jnp.exp2(st - m_new).astype(jnp.bfloat16))
        m_ref[t] = m_new
    for t, (p, alpha) in enumerate(zip(ps, alphas)):
        acc_ref[t] = alpha * acc_ref[t] + _dot(vtts[t], p)


def _finish_t(acc):
    return (acc[0:HEAD_DIM, :] / acc[HEAD_DIM:HEAD_DIM + 1, :]).T


def _flash_t_kernel(q_ref, k_ref, vt_ref, o_ref, m_ref, acc_ref, *, bq, bk, nk_max, frame_causal, hg, dqk):
    i = pl.program_id(2)
    first = i * bq
    nk = _num_key_tiles(i, bq, bk, 0, nk_max)
    n_full = jnp.minimum((first + (1 if frame_causal else CHUNK)) // bk, nk)
    m_ref[...] = jnp.full(m_ref.shape, NEG_INF, jnp.float32)
    acc_ref[...] = jnp.zeros(acc_ref.shape, jnp.float32)

    def make_body(masked):
        def body(j, _):
            start = pl.multiple_of(j * bk, bk)
            if masked:
                kpos = j * bk + lax.broadcasted_iota(jnp.int32, (bk, bq), 0)
                qpos = first + lax.broadcasted_iota(jnp.int32, (bk, bq), 1)
                mask = (kpos <= qpos) if frame_causal else ((kpos >> 6) <= (qpos >> 6))
            sts = []
            for t in range(hg):
                kt = k_ref[0, pl.ds(start, bk), t * dqk:(t + 1) * dqk]
                st = _dot_nt(kt, q_ref[0, :, t * dqk:(t + 1) * dqk])
                sts.append(jnp.where(mask, st, NEG_INF) if masked else st)
            _softmax_steps_t(sts, [vt_ref[0, j, t * VR:(t + 1) * VR, :] for t in range(hg)], m_ref, acc_ref)
            return 0
        return body

    lax.fori_loop(0, n_full, make_body(False), 0)
    lax.fori_loop(n_full, nk, make_body(True), 0)
    for t in range(hg):
        o_ref[0, :, t * HEAD_DIM:(t + 1) * HEAD_DIM] = _finish_t(acc_ref[t]).astype(o_ref.dtype)


def flash_attention_t(q, k, vt, *, lq, sp, heads, hg, dqk, bq, bk, frame_causal, name):
    nk_max = sp // bk
    return pl.pallas_call(
        functools.partial(_flash_t_kernel, bq=bq, bk=bk, nk_max=nk_max, frame_causal=frame_causal, hg=hg, dqk=dqk),
        grid=(1, heads // hg, lq // bq),
        in_specs=[pl.BlockSpec((1, bq, hg * dqk), lambda b_, g, i: (b_, i, g)),
                  _resident((1, sp, hg * dqk), lambda b_, g, i: (b_, 0, g)),
                  _resident((1, nk_max, hg * VR, bk), lambda b_, g, i: (b_, 0, g, 0))],
        out_specs=pl.BlockSpec((1, bq, hg * HEAD_DIM), lambda b_, g, i: (b_, i, g)),
        out_shape=jax.ShapeDtypeStruct((1, lq, heads * HEAD_DIM), jnp.bfloat16),
        scratch_shapes=[pltpu.VMEM((hg, 1, bq), jnp.float32), pltpu.VMEM((hg, VR, bq), jnp.float32)],
        compiler_params=_cparams(("parallel", "parallel", "arbitrary")),
        name=name,
    )(q, k, vt)


def _dsa_t_kernel(iq_ref, iqs_ref, iwt_ref, dq_ref, ik_ref, dk_ref, dvt_ref, o_ref,
                  sc_ref, m_ref, acc_ref, *, bq, bk, nk_max, topk):
    i = pl.program_id(1)
    nk = _num_key_tiles(i, bq, bk, 0, nk_max)
    qpos = i * bq + lax.broadcasted_iota(jnp.int32, (1, bq), 1)
    klim = ((qpos >> 6) + 1) << 6

    fold = lambda a, op: op(a.reshape(bk // _FOLD, _FOLD, bq), axis=0)

    def score_tile(j, carry):
        hi, lo = carry
        kt = ik_ref[0, pl.ds(pl.multiple_of(j * bk, bk), bk), :]
        acc = jnp.zeros((bk, bq), jnp.float32)
        for g in range(IDX_HEADS // 2):
            se = _dot_nt(kt, iq_ref[0, :, g * LANE:(g + 1) * LANE])
            so = _dot_nt(kt, iqs_ref[0, :, g * LANE:(g + 1) * LANE])
            acc = acc + iwt_ref[2 * g:2 * g + 1, :] * jnp.maximum(se, 0.0)
            acc = acc + iwt_ref[2 * g + 1:2 * g + 2, :] * jnp.maximum(so, 0.0)
        adm = (j * bk + lax.broadcasted_iota(jnp.int32, (bk, bq), 0)) < klim
        sc = jnp.where(adm, acc, -jnp.inf)
        sc_ref[j] = sc
        return (jnp.maximum(hi, fold(sc, jnp.max)), jnp.minimum(lo, fold(jnp.where(adm, acc, jnp.inf), jnp.min)))

    hi, lo = lax.fori_loop(0, nk, score_tile, (jnp.full((_FOLD, bq), -jnp.inf, jnp.float32),
                                               jnp.full((_FOLD, bq), jnp.inf, jnp.float32)))
    hi = jnp.max(hi, axis=0, keepdims=True)
    lo = jnp.min(lo, axis=0, keepdims=True)

    def count_ge(pivot):
        def body(j, acc):
            return acc + fold(jnp.where(sc_ref[j] >= pivot, 1.0, 0.0), jnp.sum)
        acc = lax.fori_loop(0, nk, body, jnp.zeros((_FOLD, bq), jnp.float32))
        return jnp.sum(acc, axis=0, keepdims=True)

    kf = jnp.float32(topk)
    hi = _key_to_f32(_f32_to_key(hi) + 1)
    c_lo = klim.astype(jnp.float32)
    c_hi = jnp.zeros((1, bq), jnp.float32)

    def settled(lo, hi, c_lo):
        width = lax.bitcast_convert_type(_f32_to_key(hi) - _f32_to_key(lo), jnp.uint32)
        return (c_lo <= kf) | (width <= 1)

    def search_cond(state):
        it, n_open = state[0], state[1]
        return (n_open > 0) & (it < _MAX_PROBES)

    def search_step(state):
        it, _, lo, hi, c_lo, c_hi = state
        done = settled(lo, hi, c_lo)
        k_lo, k_hi = _f32_to_key(lo), _f32_to_key(hi)
        width = lax.bitcast_convert_type(k_hi - k_lo, jnp.uint32)
        k_mid = k_lo + lax.bitcast_convert_type(width >> 1, jnp.int32)
        k_val = jnp.clip(_f32_to_key(lo + (hi - lo) * 0.5), k_lo + 1, k_hi - 1)
        probe = _key_to_f32(jnp.where((it & 7) == 7, k_mid, k_val))
        probe = jnp.where(done, lo, probe)
        c = count_ge(probe)
        up = (c >= kf) & ~done
        down = (c < kf) & ~done
        lo, c_lo = jnp.where(up, probe, lo), jnp.where(up, c, c_lo)
        hi, c_hi = jnp.where(down, probe, hi), jnp.where(down, c, c_hi)
        n_open = jnp.sum(jnp.where(settled(lo, hi, c_lo), 0, 1))
        return it + 1, n_open, lo, hi, c_lo, c_hi

    state = (jnp.int32(0), jnp.sum(jnp.where(settled(lo, hi, c_lo), 0, 1)), lo, hi, c_lo, c_hi)
    thr = lax.while_loop(search_cond, search_step, state)[2]

    m_ref[...] = jnp.full(m_ref.shape, NEG_INF, jnp.float32)
    acc_ref[...] = jnp.zeros(acc_ref.shape, jnp.float32)

    def att_tile(j, _):
        start = pl.multiple_of(j * bk, bk)
        drop = jnp.where(sc_ref[j] >= thr, 0.0, NEG_INF)
        sts, vtts = [], []
        for h in range(DSA_HEADS):
            g = h // DSA_GROUP
            kt = dk_ref[0, pl.ds(start, bk), g * LANE:(g + 1) * LANE]
            sts.append(_dot_nt(kt, dq_ref[0, :, h * LANE:(h + 1) * LANE]) + drop)
            vtts.append(dvt_ref[0, j, g * VR:(g + 1) * VR, :])
        _softmax_steps_t(sts, vtts, m_ref, acc_ref)
        return 0

    lax.fori_loop(0, nk, att_tile, 0)
    for h in range(DSA_HEADS):
        o_ref[0, :, h * LANE:(h + 1) * LANE] = _finish_t(acc_ref[h]).astype(o_ref.dtype)


def dsa_attention_t(iq, iqs, iwt, dq, ik, dk, dvt, *, lq, sp, bq, bk, topk):
    nk_max = sp // bk
    qrow = lambda w: pl.BlockSpec((1, bq, w), lambda b_, i: (b_, i, 0))
    krow = lambda w: _resident((1, sp, w), lambda b_, i: (b_, 0, 0))
    return pl.pallas_call(
        functools.partial(_dsa_t_kernel, bq=bq, bk=bk, nk_max=nk_max, topk=topk),
        grid=(1, lq // bq),
        in_specs=[qrow(IDX_W), qrow(IDX_W), pl.BlockSpec((LANE, bq), lambda b_, i: (0, i)), qrow(DSA_W),
                  krow(LANE), krow(DSA_KV_W),
                  _resident((1, nk_max, DSA_KV_HEADS * VR, bk), lambda b_, i: (b_, 0, 0, 0))],
        out_specs=qrow(DSA_W),
        out_shape=jax.ShapeDtypeStruct((1, lq, DSA_W), jnp.bfloat16),
        scratch_shapes=[pltpu.VMEM((nk_max, bk, bq), jnp.float32),
                        pltpu.VMEM((DSA_HEADS, 1, bq), jnp.float32),
                        pltpu.VMEM((DSA_HEADS, VR, bq), jnp.float32)],
        compiler_params=_cparams(("parallel", "arbitrary")),
        name="dsa_attention_t",
    )(iq, iqs, iwt, dq, ik, dk, dvt)


def _out_proj_kernel(x_ref, a_ref, b_ref, c_ref, wa_ref, wb_ref, wc_ref, o_ref):
    o_ref[...] = (x_ref[...] + _dot(a_ref[...], wa_ref[...]) + _dot(b_ref[...], wb_ref[...])
                  + _dot(c_ref[...], wc_ref[...]))


def out_proj(x, a, b, c, wa, wb, wc, *, bm):
    m, d = x.shape
    row = lambda w: pl.BlockSpec((bm, w), lambda i: (i, 0))
    full = lambda w: _resident(w.shape, lambda i: (0, 0))
    return pl.pallas_call(
        _out_proj_kernel,
        grid=(m // bm,),
        in_specs=[row(d), row(a.shape[1]), row(b.shape[1]), row(c.shape[1]), full(wa), full(wb), full(wc)],
        out_specs=row(d),
        out_shape=jax.ShapeDtypeStruct((m, d), jnp.float32),
        compiler_params=_cparams(("parallel",)),
        name="out_proj",
    )(x, a, b, c, wa, wb, wc)


def _ffn_kernel(x_ref, g_ref, wg_ref, wu_ref, wd_ref, o_ref, xn_ref, acc_ref):
    j = pl.program_id(1)

    @pl.when(j == 0)
    def _():
        xn_ref[...] = _rms(x_ref[...], g_ref[...]).astype(jnp.bfloat16)
        acc_ref[...] = jnp.zeros_like(acc_ref)

    xn = xn_ref[...]
    gate = _dot(xn, wg_ref[...])
    up = _dot(xn, wu_ref[...])
    hidden = (gate * (1.0 / (1.0 + jnp.exp(-gate))) * up).astype(jnp.bfloat16)
    acc_ref[...] += _dot(hidden, wd_ref[...])

    @pl.when(j == pl.num_programs(1) - 1)
    def _():
        o_ref[...] = x_ref[...] + acc_ref[...]


def ffn(x, g, wg, wu, wd, *, bm, bf):
    m, d = x.shape
    f = wg.shape[1]
    return pl.pallas_call(
        _ffn_kernel,
        grid=(m // bm, f // bf),
        in_specs=[pl.BlockSpec((bm, d), lambda i, j: (i, 0)),
                  pl.BlockSpec((1, d), lambda i, j: (0, 0)),
                  pl.BlockSpec((d, bf), lambda i, j: (0, j)),
                  pl.BlockSpec((d, bf), lambda i, j: (0, j)),
                  pl.BlockSpec((bf, d), lambda i, j: (j, 0))],
        out_specs=pl.BlockSpec((bm, d), lambda i, j: (i, 0)),
        out_shape=jax.ShapeDtypeStruct((m, d), jnp.float32),
        scratch_shapes=[pltpu.VMEM((bm, d), jnp.bfloat16), pltpu.VMEM((bm, d), jnp.float32)],
        compiler_params=_cparams(("parallel", "arbitrary")),
        name="swiglu",
    )(x, g.reshape(1, d), wg, wu, wd)


def _final_norm_kernel(x_ref, g_ref, o_ref):
    o_ref[...] = _rms(x_ref[...], g_ref[...])


def final_norm(x, g, *, bm):
    m, d = x.shape
    return pl.pallas_call(
        _final_norm_kernel,
        grid=(m // bm,),
        in_specs=[pl.BlockSpec((bm, d), lambda i: (i, 0)), pl.BlockSpec((1, d), lambda i: (0, 0))],
        out_specs=pl.BlockSpec((bm, d), lambda i: (i, 0)),
        out_shape=jax.ShapeDtypeStruct((m, d), jnp.float32),
        compiler_params=_cparams(("parallel",)),
        name="final_norm",
    )(x, g.reshape(1, d))


def _pad_cols(a, width):
    return jnp.pad(a, [(0, 0)] * (a.ndim - 1) + [(0, width - a.shape[-1])])


def _layout_w_in(w_in):
    sizes = (FOX_W, FOX_W, FOX_W, FOX_HEADS, DSA_W, DSA_KV_W, DSA_KV_W, IDX_W, IDX_DIM, IDX_HEADS,
             MLA_Q_LORA, MLA_KV_LORA, MLA_ROPE)
    splits = np.cumsum(sizes)[:-1]
    q_a, k_a, v_a, f_a, q_b, k_b, v_b, q_i, k_i, w_i, c_q, c_kv, k_r = jnp.split(w_in, splits, axis=-1)
    parts = {"fq": q_a, "fk": k_a, "fv": v_a, "dq": q_b, "dk": k_b, "dv": v_b, "iq": q_i, "cq": c_q,
             "ckv": c_kv, "fa": f_a, "ik": k_i, "iw": w_i, "kr": k_r}
    cols = [_pad_cols(parts[n], w) for n, w in _SEGS]
    return _pad_cols(jnp.concatenate(cols, axis=-1), PROJ_W).astype(jnp.bfloat16)


def _layout_w_uq(w_uq):
    dp = w_uq.shape[0]
    w = w_uq.reshape(dp, MLA_Q_LORA, MLA_HEADS, MLA_NOPE + MLA_ROPE)
    return _pad_cols(w, MLA_QP).reshape(dp, MLA_Q_LORA, MLA_QW).astype(jnp.bfloat16)


def _layout_w_ukv(w_ukv):
    dp = w_ukv.shape[0]
    w = w_ukv.reshape(dp, MLA_KV_LORA, MLA_HEADS, MLA_NOPE + MLA_V)
    kn = w[..., :MLA_NOPE].reshape(dp, MLA_KV_LORA, MLA_HEADS * MLA_NOPE)
    vv = w[..., MLA_NOPE:].reshape(dp, MLA_KV_LORA, MLA_VW)
    return jnp.concatenate([kn, vv], axis=-1).astype(jnp.bfloat16)


def _rope_table(pos, rot, period):
    half = rot // 2
    inv_freq = ROPE_THETA ** (-jnp.arange(half, dtype=jnp.float32) / half)
    ang = pos.astype(jnp.float32)[:, None] * inv_freq[None, :]
    cos, sin = jnp.cos(ang), jnp.sin(ang)
    r = pos.shape[0]
    ones = jnp.ones((r, period - rot), jnp.float32)
    zeros = jnp.zeros((r, period - rot), jnp.float32)
    zh = jnp.zeros((r, half), jnp.float32)
    c = jnp.concatenate([cos, cos, ones], axis=1)
    s1 = jnp.concatenate([zh, sin, zeros], axis=1)
    s2 = jnp.concatenate([-sin, zh, zeros], axis=1)
    rep = LANE // period
    return jnp.concatenate([jnp.tile(c, (1, rep)), jnp.tile(s1, (1, rep)), jnp.tile(s2, (1, rep))], axis=1)


def _pick(n, candidates):
    for c in candidates:
        if n % c == 0:
            return c
    raise ValueError(f"no block size among {candidates} divides {n}")


def _pad_seq(a, sp):
    return jnp.pad(a, [(0, 0), (0, sp - a.shape[1])] + [(0, 0)] * (a.ndim - 2))


def kernel(x_prompt, x_sample, cache_fox_k, cache_fox_v, cache_fox_logf, cache_dsa_k, cache_dsa_v, cache_idx_k, cache_mla_ckv, cache_mla_krope, w_in, fox_bias, mla_q_norm, mla_w_uq, mla_kv_norm, mla_w_ukv, w_o, attn_norm, ffn_norm, w_gate, w_up, w_down, final_norm_w):
    bf = jnp.bfloat16
    depth = w_in.shape[0]
    nb, seq, d = x_prompt.shape
    db, dec, _ = x_sample.shape
    past = cache_fox_k.shape[2]
    assert nb == 1 and seq % CHUNK == 0 and dec % CHUNK == 0 and past % CHUNK == 0
    mp, ms = nb * seq, db * dec
    m = mp + ms
    s_len = past + dec

    bm = _pick(m, (512, 256, 128, 64))
    assert mp % bm == 0
    bq_p = _pick(seq, (256, 128))
    bk_p = bm
    bq_s = _pick(dec, (64,))
    bk_s = 256
    sp = ((s_len + bk_s - 1) // bk_s) * bk_s
    topk_p = min(IDX_TOPK_MAX, seq // 4)
    topk_s = min(IDX_TOPK_MAX, s_len // 4)
    hg_p = 2
    fox_hg, mla_hg = 3, 2
    d_ff = w_gate.shape[2]
    bff = _pick(d_ff, (512, 256, 128))

    w_in_p = _layout_w_in(w_in)
    wuq_p = _layout_w_uq(mla_w_uq)
    wukv_p = _layout_w_ukv(mla_w_ukv)
    w_o_b = w_o.astype(bf)
    wg_b, wu_b, wd_b = w_gate.astype(bf), w_up.astype(bf), w_down.astype(bf)
    fox_bias_p = _pad_cols(fox_bias, LANE).reshape(depth, 1, LANE)

    pos = jnp.concatenate([jnp.tile(jnp.arange(seq, dtype=jnp.int32), nb),
                           jnp.tile(past + jnp.arange(dec, dtype=jnp.int32), db)])
    tab = jnp.concatenate([_rope_table(pos, PARTIAL_ROT, HEAD_DIM), _rope_table(pos, IDX_ROT, IDX_DIM),
                           _rope_table(pos, MLA_ROPE, LANE)], axis=1)

    x = jnp.concatenate([x_prompt.reshape(mp, d), x_sample.reshape(ms, d)], axis=0)
    p_rows, s_rows = [], []
    for l in range(depth):
        proj = norm_matmul(x, attn_norm[l], w_in_p[l], bm=bm, bn=512)
        (fq, fk, fv, fkb, fvb, fvt, lf, dq, dk, dv, dkb, dvb, dvt, iq, iqs, ik, ikb, iw, iwt, mq, ckv, kr) = post_proj(
            proj, tab, fox_bias_p[l], mla_q_norm[l].reshape(1, -1), mla_kv_norm[l].reshape(1, -1), wuq_p[l],
            bm=bm // 2, bk=bm)

        pr = lambda a: a[:mp].reshape(nb, seq, -1)
        sr = lambda a: a[mp:].reshape(db, dec, -1)
        p_rows.append((pr(fk).reshape(nb, seq, FOX_HEADS, HEAD_DIM), pr(fv).reshape(nb, seq, FOX_HEADS, HEAD_DIM),
                       pr(lf)[..., :FOX_HEADS], pr(dk).reshape(nb, seq, DSA_KV_HEADS, HEAD_DIM),
                       pr(dv).reshape(nb, seq, DSA_KV_HEADS, HEAD_DIM), pr(ik)[..., :IDX_DIM], pr(ckv),
                       pr(kr)[..., :MLA_ROPE]))
        s_rows.append((sr(fk).reshape(db, dec, FOX_HEADS, HEAD_DIM), sr(fv).reshape(db, dec, FOX_HEADS, HEAD_DIM),
                       sr(lf)[..., :FOX_HEADS], sr(dk).reshape(db, dec, DSA_KV_HEADS, HEAD_DIM),
                       sr(dv).reshape(db, dec, DSA_KV_HEADS, HEAD_DIM), sr(ik)[..., :IDX_DIM], sr(ckv),
                       sr(kr)[..., :MLA_ROPE]))

        cat = lambda c, new: _pad_seq(jnp.concatenate([c, new], axis=1), sp)
        s_fkb = cat(cache_fox_k[l].reshape(db, past, FOX_W).astype(bf), sr(fkb))
        s_fvb = cat(cache_fox_v[l].reshape(db, past, FOX_W).astype(bf), sr(fvb))
        s_lf = cat(_pad_cols(cache_fox_logf[l], LANE), sr(lf))
        s_dkb = cat(cache_dsa_k[l].reshape(db, past, DSA_KV_W).astype(bf), sr(dkb))
        s_dvb = cat(cache_dsa_v[l].reshape(db, past, DSA_KV_W).astype(bf), sr(dvb))
        s_ikb = cat(_pad_cols(cache_idx_k[l], LANE).astype(bf), sr(ikb))
        s_ckv = cat(cache_mla_ckv[l], sr(ckv))
        s_kr = cat(_pad_cols(cache_mla_krope[l], LANE), sr(kr))

        ckv_all = jnp.concatenate([ckv[:mp], s_ckv.reshape(db * sp, -1)], axis=0)
        kr_all = jnp.concatenate([kr[:mp], s_kr.reshape(db * sp, -1)], axis=0)
        kmla, vmla, vtmla = kv_up(ckv_all, kr_all, wukv_p[l], bm=bm)
        s_kmla, s_vmla = kmla[mp:].reshape(db, sp, -1), vmla[mp:].reshape(db, sp, -1)

        p_c, _ = cumsum_rows(pr(lf), bt=_pick(seq, (256, 128)))
        s_c, s_ct = cumsum_rows(s_lf, bt=_pick(sp, (256, 128)))
        ck_form = lambda ct, bk: ct[:, :FOX_HEADS].reshape(ct.shape[0], FOX_HEADS, ct.shape[2] // bk, 1, bk)

        al = lambda a: a.reshape(1, *a.shape)
        fqa, fka = fox_pack(fq, fkb, p_c[0], rows=mp, bm=bm)
        out_a = flash_attention_t(al(fqa), al(fka), al(fvt), lq=seq, sp=seq, heads=FOX_HEADS, hg=hg_p, dqk=FOX_QP,
                                  bq=bq_p, bk=bk_p, frame_causal=True, name="fox_attention_t")
        out_b = dsa_attention_t(al(iq), al(iqs), iwt, al(dq), al(ikb), al(dkb), al(dvt), lq=seq, sp=seq,
                                bq=bq_p, bk=bk_p, topk=topk_p)
        out_c = flash_attention_t(al(mq), al(kmla), al(vtmla), lq=seq, sp=seq, heads=MLA_HEADS, hg=hg_p, dqk=MLA_QP,
                                  bq=bq_p, bk=bk_p, frame_causal=False, name="mla_attention_t")
        s_out_a = flash_attention(sr(fq), s_fkb, s_fvb, s_c[:, past:past + dec], ck_form(s_ct, bk_s), lq=dec, sp=sp,
                                  heads=FOX_HEADS, hg=fox_hg, dqk=HEAD_DIM, dv=HEAD_DIM, bq=bq_s, bk=bk_s, q_off=past)
        s_out_b = dsa_attention(sr(iq), sr(iqs), sr(iw), sr(dq), s_ikb, s_dkb, s_dvb, lq=dec, sp=sp,
                                bq=bq_s, bk=bk_s, q_off=past, topk=topk_s)
        s_out_c = flash_attention(sr(mq), s_kmla, s_vmla, lq=dec, sp=sp, heads=MLA_HEADS, hg=mla_hg,
                                  dqk=MLA_QP, dv=MLA_V, bq=bq_s, bk=bk_s, q_off=past)

        rows = lambda p, s: jnp.concatenate([p.reshape(mp, -1), s.reshape(ms, -1)], axis=0)
        x = out_proj(x, rows(out_a, s_out_a), rows(out_b, s_out_b), rows(out_c, s_out_c),
                     w_o_b[l, :FOX_W], w_o_b[l, FOX_W:FOX_W + DSA_W], w_o_b[l, FOX_W + DSA_W:], bm=bm)
        x = ffn(x, ffn_norm[l], wg_b[l], wu_b[l], wd_b[l], bm=bm, bf=bff)

    y = final_norm(x, final_norm_w, bm=bm)
    y_prompt = y[:mp].reshape(nb, seq, d)
    y_sample = y[mp:].reshape(db, dec, d)
    stack = lambda rows_: tuple(jnp.stack(t) for t in zip(*rows_))
    return (y_prompt, y_sample) + stack(p_rows) + stack(s_rows)
```

```python
import functools
import math

import jax
import jax.numpy as jnp
import numpy as np
from jax import lax
from jax.experimental import pallas as pl
from jax.experimental.pallas import tpu as pltpu

CHUNK = 64
HEAD_DIM = 128
FOX_HEADS = 6
DSA_HEADS = 6
DSA_KV_HEADS = 2
DSA_GROUP = DSA_HEADS // DSA_KV_HEADS
IDX_HEADS = 16
IDX_DIM = 64
IDX_TOPK_MAX = 256
MLA_HEADS = 4
MLA_Q_LORA = 512
MLA_KV_LORA = 256
MLA_NOPE = 128
MLA_ROPE = 64
MLA_V = 128
PARTIAL_ROT = HEAD_DIM // 4
IDX_ROT = IDX_DIM // 4
ROPE_THETA = 500000.0
EPS = 1e-6
NEG_INF = -1e30
FOX_SCALE = HEAD_DIM ** -0.5
DSA_SCALE = HEAD_DIM ** -0.5
MLA_SCALE = (MLA_NOPE + MLA_ROPE) ** -0.5
IDX_W_SCALE = (IDX_HEADS * IDX_DIM) ** -0.5
LOG2E = math.log2(math.e)

LANE = 128
VMEM_LIMIT = 56 * 1024 * 1024

FOX_W = FOX_HEADS * HEAD_DIM
DSA_W = DSA_HEADS * HEAD_DIM
DSA_KV_W = DSA_KV_HEADS * HEAD_DIM
IDX_W = IDX_HEADS * IDX_DIM
MLA_QP = 2 * LANE
MLA_QW = MLA_HEADS * MLA_QP
MLA_VW = MLA_HEADS * MLA_V
VR = HEAD_DIM + 16
FOX_QP = 2 * LANE

_SEGS = (("fq", FOX_W), ("fk", FOX_W), ("fv", FOX_W), ("dq", DSA_W), ("dk", DSA_KV_W), ("dv", DSA_KV_W),
         ("iq", IDX_W), ("cq", MLA_Q_LORA), ("ckv", MLA_KV_LORA),
         ("fa", LANE), ("ik", LANE), ("iw", LANE), ("kr", LANE))
_OFF = {}
_o = 0
for _n, _w in _SEGS:
    _OFF[_n] = _o
    _o += _w
PROJ_W = ((_o + 511) // 512) * 512

_NT = (((1,), (1,)), ((), ()))


def _dot(a, b):
    return jnp.dot(a, b, preferred_element_type=jnp.float32)


def _dot_nt(a, b):
    return lax.dot_general(a, b, _NT, preferred_element_type=jnp.float32)


def _cparams(sem):
    return pltpu.CompilerParams(dimension_semantics=sem, vmem_limit_bytes=VMEM_LIMIT)


def _resident(block_shape, index_map):
    return pl.BlockSpec(block_shape, index_map, pipeline_mode=pl.Buffered(1))


def _rms(x, g):
    return x * lax.rsqrt(jnp.mean(x * x, axis=-1, keepdims=True) + EPS) * g


def _norm_matmul_kernel(x_ref, g_ref, w_ref, o_ref, xn_ref):
    @pl.when(pl.program_id(1) == 0)
    def _():
        xn_ref[...] = _rms(x_ref[...], g_ref[...]).astype(jnp.bfloat16)

    o_ref[...] = _dot(xn_ref[...], w_ref[...])


def norm_matmul(x, g, w, *, bm, bn):
    m, d = x.shape
    n = w.shape[1]
    return pl.pallas_call(
        _norm_matmul_kernel,
        grid=(m // bm, n // bn),
        in_specs=[pl.BlockSpec((bm, d), lambda i, j: (i, 0)),
                  pl.BlockSpec((1, d), lambda i, j: (0, 0)),
                  pl.BlockSpec((d, bn), lambda i, j: (0, j))],
        out_specs=pl.BlockSpec((bm, bn), lambda i, j: (i, j)),
        out_shape=jax.ShapeDtypeStruct((m, n), jnp.float32),
        scratch_shapes=[pltpu.VMEM((bm, d), jnp.bfloat16)],
        compiler_params=_cparams(("parallel", "arbitrary")),
        name="norm_matmul",
    )(x, g.reshape(1, d), w)


def _rope(x, tab, half):
    c, s1, s2 = tab[:, 0:LANE], tab[:, LANE:2 * LANE], tab[:, 2 * LANE:3 * LANE]
    return x * c + pltpu.roll(x, half, 1) * s1 + pltpu.roll(x, LANE - half, 1) * s2


def _value_t_tail(width):
    r = lax.broadcasted_iota(jnp.int32, (16, width), 0)
    return jnp.where(r == 0, 1.0, 0.0).astype(jnp.bfloat16)


def _store_value_t(o_ref, h, v):
    o_ref[0, h * VR:h * VR + HEAD_DIM, :] = v.T.astype(jnp.bfloat16)
    o_ref[0, h * VR + HEAD_DIM:(h + 1) * VR, :] = _value_t_tail(v.shape[0])


def _post_kernel(p_ref, tab_ref, fb_ref, qn_ref, kvn_ref, wuq_ref,
                 fq_o, fk_o, fv_o, fkb_o, fvb_o, fvt_o, lf_o,
                 dq_o, dk_o, dv_o, dkb_o, dvb_o, dvt_o,
                 iq_o, iqs_o, ik_o, ikb_o, iw_o, iwt_o,
                 mq_o, ckv_o, kr_o):
    bf = jnp.bfloat16
    seg = lambda name, w: p_ref[:, _OFF[name]:_OFF[name] + w]
    tab_d = tab_ref[:, 0:3 * LANE]
    tab_i = tab_ref[:, 3 * LANE:6 * LANE]
    tab_m = tab_ref[:, 6 * LANE:9 * LANE]

    fq_o[...] = (seg("fq", FOX_W) * (FOX_SCALE * LOG2E)).astype(bf)
    fk = seg("fk", FOX_W)
    fk_o[...] = fk
    fkb_o[...] = fk.astype(bf)
    fv = seg("fv", FOX_W)
    fv_o[...] = fv
    fvb_o[...] = fv.astype(bf)
    for h in range(FOX_HEADS):
        _store_value_t(fvt_o, h, p_ref[:, _OFF["fv"] + h * LANE:_OFF["fv"] + (h + 1) * LANE])
    z = seg("fa", LANE) + fb_ref[...]
    lf_o[...] = jnp.minimum(z, 0.0) - jnp.log1p(jnp.exp(-jnp.abs(z)))

    for h in range(DSA_HEADS):
        x = p_ref[:, _OFF["dq"] + h * LANE:_OFF["dq"] + (h + 1) * LANE]
        dq_o[:, h * LANE:(h + 1) * LANE] = (_rope(x, tab_d, PARTIAL_ROT // 2) * (DSA_SCALE * LOG2E)).astype(bf)
    for h in range(DSA_KV_HEADS):
        x = p_ref[:, _OFF["dk"] + h * LANE:_OFF["dk"] + (h + 1) * LANE]
        y = _rope(x, tab_d, PARTIAL_ROT // 2)
        dk_o[:, h * LANE:(h + 1) * LANE] = y
        dkb_o[:, h * LANE:(h + 1) * LANE] = y.astype(bf)
        _store_value_t(dvt_o, h, p_ref[:, _OFF["dv"] + h * LANE:_OFF["dv"] + (h + 1) * LANE])
    dv = seg("dv", DSA_KV_W)
    dv_o[...] = dv
    dvb_o[...] = dv.astype(bf)
    for j in range(IDX_W // LANE):
        x = p_ref[:, _OFF["iq"] + j * LANE:_OFF["iq"] + (j + 1) * LANE]
        y = _rope(x, tab_i, IDX_ROT // 2)
        iq_o[:, j * LANE:(j + 1) * LANE] = y.astype(bf)
        iqs_o[:, j * LANE:(j + 1) * LANE] = pltpu.roll(y, IDX_DIM, 1).astype(bf)
    y = _rope(seg("ik", LANE), tab_i, IDX_ROT // 2)
    ik_o[...] = y
    ikb_o[...] = y.astype(bf)
    iw = seg("iw", LANE) * IDX_W_SCALE
    iw_o[...] = iw
    iwt_o[...] = iw.T

    cqn = _rms(seg("cq", MLA_Q_LORA), qn_ref[...]).astype(bf)
    mq = _dot(cqn, wuq_ref[...])
    for h in range(MLA_HEADS):
        a = h * MLA_QP
        mq_o[:, a:a + LANE] = (mq[:, a:a + LANE] * (MLA_SCALE * LOG2E)).astype(bf)
        r = _rope(mq[:, a + LANE:a + 2 * LANE], tab_m, MLA_ROPE // 2)
        mq_o[:, a + LANE:a + 2 * LANE] = (r * (MLA_SCALE * LOG2E)).astype(bf)
    ckv_o[...] = _rms(seg("ckv", MLA_KV_LORA), kvn_ref[...])
    kr_o[...] = _rope(seg("kr", LANE), tab_m, MLA_ROPE // 2)


def post_proj(proj, tab, fox_bias_p, q_norm, kv_norm, wuq_p, *, bm, bk):
    m = proj.shape[0]
    r = bk // bm
    f32, bf = jnp.float32, jnp.bfloat16
    row = lambda w: pl.BlockSpec((bm, w), lambda i: (i, 0))
    full = lambda a: pl.BlockSpec(a.shape, lambda i: (0,) * a.ndim)
    vt = lambda heads: pl.BlockSpec((1, heads * VR, bm), lambda i: (i // r, 0, i % r))
    rowo = lambda w, dt: (row(w), jax.ShapeDtypeStruct((m, w), dt))
    vto = lambda heads: (vt(heads), jax.ShapeDtypeStruct((m // bk, heads * VR, bk), bf))
    outs = [rowo(FOX_W, bf), rowo(FOX_W, f32), rowo(FOX_W, f32), rowo(FOX_W, bf), rowo(FOX_W, bf), vto(FOX_HEADS),
            rowo(LANE, f32),
            rowo(DSA_W, bf), rowo(DSA_KV_W, f32), rowo(DSA_KV_W, f32), rowo(DSA_KV_W, bf), rowo(DSA_KV_W, bf),
            vto(DSA_KV_HEADS),
            rowo(IDX_W, bf), rowo(IDX_W, bf), rowo(LANE, f32), rowo(LANE, bf), rowo(LANE, f32),
            (pl.BlockSpec((LANE, bm), lambda i: (0, i)), jax.ShapeDtypeStruct((LANE, m), f32)),
            rowo(MLA_QW, bf), rowo(MLA_KV_LORA, f32), rowo(LANE, f32)]
    return pl.pallas_call(
        _post_kernel,
        grid=(m // bm,),
        in_specs=[row(PROJ_W), row(9 * LANE), full(fox_bias_p), full(q_norm), full(kv_norm), full(wuq_p)],
        out_specs=[s for s, _ in outs],
        out_shape=[o for _, o in outs],
        compiler_params=_cparams(("parallel",)),
        name="post_proj",
    )(proj, tab, fox_bias_p, q_norm, kv_norm, wuq_p)


def _kv_up_kernel(ckv_ref, kr_ref, w_ref, k_o, v_o, vt_o):
    bf = jnp.bfloat16
    kv = _dot(ckv_ref[...].astype(bf), w_ref[...])
    kr = kr_ref[...].astype(bf)
    for h in range(MLA_HEADS):
        k_o[:, h * MLA_QP:h * MLA_QP + LANE] = kv[:, h * LANE:(h + 1) * LANE].astype(bf)
        k_o[:, h * MLA_QP + LANE:(h + 1) * MLA_QP] = kr
        _store_value_t(vt_o, h, kv[:, (MLA_HEADS + h) * LANE:(MLA_HEADS + h + 1) * LANE])
    v_o[...] = kv[:, MLA_HEADS * LANE:].astype(bf)


def kv_up(ckv, kr, w, *, bm):
    r = ckv.shape[0]
    row = lambda w_: pl.BlockSpec((bm, w_), lambda i: (i, 0))
    return pl.pallas_call(
        _kv_up_kernel,
        grid=(r // bm,),
        in_specs=[row(MLA_KV_LORA), row(LANE), pl.BlockSpec(w.shape, lambda i: (0, 0))],
        out_specs=[row(MLA_QW), row(MLA_VW), pl.BlockSpec((1, MLA_HEADS * VR, bm), lambda i: (i, 0, 0))],
        out_shape=[jax.ShapeDtypeStruct((r, MLA_QW), jnp.bfloat16),
                   jax.ShapeDtypeStruct((r, MLA_VW), jnp.bfloat16),
                   jax.ShapeDtypeStruct((r // bm, MLA_HEADS * VR, bm), jnp.bfloat16)],
        compiler_params=_cparams(("parallel",)),
        name="mla_kv_up",
    )(ckv, kr, w)


def _cumsum_kernel(x_ref, c_o, ct_o, carry_ref, carry_t_ref):
    t = x_ref.shape[1]

    @pl.when(pl.program_id(1) == 0)
    def _():
        carry_ref[...] = jnp.zeros_like(carry_ref)
        carry_t_ref[...] = jnp.zeros_like(carry_t_ref)

    x = x_ref[0]
    r = lax.broadcasted_iota(jnp.int32, (t, t), 0)
    c = lax.broadcasted_iota(jnp.int32, (t, t), 1)
    lower = jnp.where(c <= r, 1.0, 0.0).astype(jnp.float32)
    upper = jnp.where(r <= c, 1.0, 0.0).astype(jnp.float32)
    cs = jnp.dot(lower, x, preferred_element_type=jnp.float32, precision=lax.Precision.HIGHEST)
    cs = cs + carry_ref[...]
    c_o[0] = cs * LOG2E
    carry_ref[...] = cs[t - 1:t, :]
    cst = jnp.dot(x.T, upper, preferred_element_type=jnp.float32, precision=lax.Precision.HIGHEST)
    cst = cst + carry_t_ref[...]
    ct_o[0] = cst[0:8, :] * LOG2E
    carry_t_ref[...] = cst[:, t - 1:t]


def cumsum_rows(x, *, bt):
    b, s, _ = x.shape
    return pl.pallas_call(
        _cumsum_kernel,
        grid=(b, s // bt),
        in_specs=[pl.BlockSpec((1, bt, LANE), lambda i, j: (i, j, 0))],
        out_specs=[pl.BlockSpec((1, bt, LANE), lambda i, j: (i, j, 0)),
                   pl.BlockSpec((1, 8, bt), lambda i, j: (i, 0, j))],
        out_shape=[jax.ShapeDtypeStruct((b, s, LANE), jnp.float32),
                   jax.ShapeDtypeStruct((b, 8, s), jnp.float32)],
        scratch_shapes=[pltpu.VMEM((1, LANE), jnp.float32), pltpu.VMEM((LANE, 1), jnp.float32)],
        compiler_params=_cparams(("parallel", "arbitrary")),
        name="logf_cumsum",
    )(x)


def _num_key_tiles(i, bq, bk, q_off, nk_max):
    last = q_off + (i + 1) * bq
    return jnp.minimum((last + bk - 1) // bk, nk_max)


def _softmax_step(s, vt, m_ref, l_ref, acc_ref, t):
    m_prev = m_ref[t]
    m_new = jnp.maximum(m_prev, jnp.max(s, axis=1, keepdims=True))
    alpha = jnp.exp2(m_prev - m_new)
    p = jnp.exp2(s - m_new)
    l_ref[t] = alpha * l_ref[t] + jnp.sum(p, axis=1, keepdims=True)
    acc_ref[t] = alpha * acc_ref[t] + _dot(p.astype(jnp.bfloat16), vt)
    m_ref[t] = m_new


def _flash_kernel(*refs, bq, bk, q_off, nk_max, fox, hg, dqk, dv):
    if fox:
        q_ref, k_ref, v_ref, cq_ref, ck_ref, o_ref, m_ref, l_ref, acc_ref = refs
    else:
        q_ref, k_ref, v_ref, o_ref, m_ref, l_ref, acc_ref = refs
    gi = pl.program_id(1)
    i = pl.program_id(2)
    first = q_off + i * bq
    qpos = first + lax.broadcasted_iota(jnp.int32, (bq, 1), 0)
    if fox:
        lane = lax.broadcasted_iota(jnp.int32, (bq, LANE), 1)
        cq_all = cq_ref[0]
        cqs = [jnp.sum(jnp.where(lane == gi * hg + t, cq_all, 0.0), axis=1, keepdims=True) for t in range(hg)]
        c0 = [c[0:1, :] for c in cqs]
        bias_q = [c - z for c, z in zip(cqs, c0)]
    nk = _num_key_tiles(i, bq, bk, q_off, nk_max)
    n_full = jnp.minimum((first + (1 if fox else CHUNK)) // bk, nk)

    m_ref[...] = jnp.full(m_ref.shape, NEG_INF, jnp.float32)
    l_ref[...] = jnp.zeros(l_ref.shape, jnp.float32)
    acc_ref[...] = jnp.zeros(acc_ref.shape, jnp.float32)

    def make_body(masked):
        def body(j, _):
            start = pl.multiple_of(j * bk, bk)
            if masked:
                kpos = j * bk + lax.broadcasted_iota(jnp.int32, (1, bk), 1)
                mask = (kpos <= qpos) if fox else ((kpos >> 6) <= (qpos >> 6))
            for t in range(hg):
                kt = k_ref[0, pl.ds(start, bk), t * dqk:(t + 1) * dqk]
                vt = v_ref[0, pl.ds(start, bk), t * dv:(t + 1) * dv]
                s = _dot_nt(q_ref[0, :, t * dqk:(t + 1) * dqk], kt)
                if fox:
                    s = s + (bias_q[t] - (ck_ref[0, t, j] - c0[t]))
                if masked:
                    s = jnp.where(mask, s, NEG_INF)
                _softmax_step(s, vt, m_ref, l_ref, acc_ref, t)
            return 0
        return body

    lax.fori_loop(0, n_full, make_body(False), 0)
    lax.fori_loop(n_full, nk, make_body(True), 0)
    for t in range(hg):
        o_ref[0, :, t * dv:(t + 1) * dv] = (acc_ref[t] / l_ref[t]).astype(o_ref.dtype)


def flash_attention(q, k, v, cq=None, ck=None, *, lq, sp, heads, hg, dqk, dv, bq, bk, q_off):
    b = q.shape[0]
    nk_max = sp // bk
    fox = cq is not None
    in_specs = [pl.BlockSpec((1, bq, hg * dqk), lambda b_, g, i: (b_, i, g)),
                _resident((1, sp, hg * dqk), lambda b_, g, i: (b_, 0, g)),
                _resident((1, sp, hg * dv), lambda b_, g, i: (b_, 0, g))]
    args = [q, k, v]
    if fox:
        in_specs += [pl.BlockSpec((1, bq, LANE), lambda b_, g, i: (b_, i, 0)),
                     _resident((1, hg, nk_max, 1, bk), lambda b_, g, i: (b_, g, 0, 0, 0))]
        args += [cq, ck]
    return pl.pallas_call(
        functools.partial(_flash_kernel, bq=bq, bk=bk, q_off=q_off, nk_max=nk_max, fox=fox, hg=hg, dqk=dqk, dv=dv),
        grid=(b, heads // hg, lq // bq),
        in_specs=in_specs,
        out_specs=pl.BlockSpec((1, bq, hg * dv), lambda b_, g, i: (b_, i, g)),
        out_shape=jax.ShapeDtypeStruct((b, lq, heads * dv), jnp.bfloat16),
        scratch_shapes=[pltpu.VMEM((hg, bq, 1), jnp.float32), pltpu.VMEM((hg, bq, 1), jnp.float32),
                        pltpu.VMEM((hg, bq, dv), jnp.float32)],
        compiler_params=_cparams(("parallel", "parallel", "arbitrary")),
        name="fox_attention" if fox else "mla_attention",
    )(*args)


_KEY_NEG_INF = -2139095041
_F32_LOWEST = -3.4028234663852886e38


_FOLD = 64
_MAX_PROBES = 8 * 34


def _key_to_f32(key):
    bits = jnp.where(key >= 0, key, key ^ jnp.int32(0x7FFFFFFF))
    return lax.bitcast_convert_type(bits, jnp.float32)


def _f32_to_key(x):
    bits = lax.bitcast_convert_type(x, jnp.int32)
    return jnp.where(bits >= 0, bits, bits ^ jnp.int32(0x7FFFFFFF))


def _dsa_kernel(iq_ref, iqs_ref, iw_ref, dq_ref, ik_ref, dk_ref, dv_ref, o_ref,
                sc_ref, qg_ref, m_ref, l_ref, acc_ref, *, bq, bk, q_off, nk_max, topk):
    i = pl.program_id(1)
    nk = _num_key_tiles(i, bq, bk, q_off, nk_max)
    qpos = q_off + i * bq + lax.broadcasted_iota(jnp.int32, (bq, 1), 0)
    klim = ((qpos >> 6) + 1) << 6
    iw = iw_ref[0]

    def score_tile(j, _):
        kt = ik_ref[0, pl.ds(pl.multiple_of(j * bk, bk), bk), :]
        acc = jnp.zeros((bq, bk), jnp.float32)
        for g in range(IDX_HEADS // 2):
            se = _dot_nt(iq_ref[0, :, g * LANE:(g + 1) * LANE], kt)
            so = _dot_nt(iqs_ref[0, :, g * LANE:(g + 1) * LANE], kt)
            acc = acc + iw[:, 2 * g:2 * g + 1] * jnp.maximum(se, 0.0)
            acc = acc + iw[:, 2 * g + 1:2 * g + 2] * jnp.maximum(so, 0.0)
        kpos = j * bk + lax.broadcasted_iota(jnp.int32, (1, bk), 1)
        sc_ref[j] = jnp.where(kpos < klim, acc, -jnp.inf)
        return 0

    lax.fori_loop(0, nk, score_tile, 0)

    def count_ge(pivot):
        def body(j, acc):
            t = sc_ref[j]
            for c in range(bk // LANE):
                acc = acc + jnp.where(t[:, c * LANE:(c + 1) * LANE] >= pivot, 1.0, 0.0)
            return acc
        acc = lax.fori_loop(0, nk, body, jnp.zeros((bq, LANE), jnp.float32))
        return jnp.sum(acc, axis=1, keepdims=True)

    kf = jnp.float32(topk)
    nonneg = count_ge(jnp.zeros((bq, 1), jnp.float32)) >= kf
    key0 = jnp.where(nonneg, jnp.int32(0), jnp.int32(-2 ** 31))

    def bit_step(it, key):
        cand = key | (jnp.int32(1) << (30 - it))
        ok = (count_ge(_key_to_f32(cand)) >= kf) | (cand <= _KEY_NEG_INF)
        return jnp.where(ok, cand, key)

    key = lax.fori_loop(0, 31, bit_step, key0)
    thr = jnp.maximum(_key_to_f32(key), _F32_LOWEST)

    for h in range(DSA_HEADS):
        g, r = divmod(h, DSA_GROUP)
        qg_ref[g, r * bq:(r + 1) * bq, :] = dq_ref[0, :, h * LANE:(h + 1) * LANE]
    m_ref[...] = jnp.full(m_ref.shape, NEG_INF, jnp.float32)
    l_ref[...] = jnp.zeros(l_ref.shape, jnp.float32)
    acc_ref[...] = jnp.zeros(acc_ref.shape, jnp.float32)

    def att_tile(j, _):
        start = pl.multiple_of(j * bk, bk)
        drop = jnp.where(sc_ref[j] >= thr, 0.0, NEG_INF)
        drop = jnp.concatenate([drop] * DSA_GROUP, axis=0)
        for g in range(DSA_KV_HEADS):
            kt = dk_ref[0, pl.ds(start, bk), g * LANE:(g + 1) * LANE]
            vt = dv_ref[0, pl.ds(start, bk), g * LANE:(g + 1) * LANE]
            s = _dot_nt(qg_ref[g], kt) + drop
            _softmax_step(s, vt, m_ref, l_ref, acc_ref, g)
        return 0

    lax.fori_loop(0, nk, att_tile, 0)
    for h in range(DSA_HEADS):
        g, r = divmod(h, DSA_GROUP)
        rows = slice(r * bq, (r + 1) * bq)
        o_ref[0, :, h * LANE:(h + 1) * LANE] = (acc_ref[g, rows, :] / l_ref[g, rows, :]).astype(o_ref.dtype)


def dsa_attention(iq, iqs, iw, dq, ik, dk, dv, *, lq, sp, bq, bk, q_off, topk):
    b = dq.shape[0]
    nk_max = sp // bk
    qrow = lambda w: pl.BlockSpec((1, bq, w), lambda b_, i: (b_, i, 0))
    krow = lambda w: _resident((1, sp, w), lambda b_, i: (b_, 0, 0))
    gr = DSA_GROUP * bq
    return pl.pallas_call(
        functools.partial(_dsa_kernel, bq=bq, bk=bk, q_off=q_off, nk_max=nk_max, topk=topk),
        grid=(b, lq // bq),
        in_specs=[qrow(IDX_W), qrow(IDX_W), qrow(LANE), qrow(DSA_W), krow(LANE), krow(DSA_KV_W), krow(DSA_KV_W)],
        out_specs=qrow(DSA_W),
        out_shape=jax.ShapeDtypeStruct((b, lq, DSA_W), jnp.bfloat16),
        scratch_shapes=[pltpu.VMEM((nk_max, bq, bk), jnp.float32),
                        pltpu.VMEM((DSA_KV_HEADS, gr, LANE), jnp.bfloat16),
                        pltpu.VMEM((DSA_KV_HEADS, gr, 1), jnp.float32),
                        pltpu.VMEM((DSA_KV_HEADS, gr, 1), jnp.float32),
                        pltpu.VMEM((DSA_KV_HEADS, gr, LANE), jnp.float32)],
        compiler_params=_cparams(("parallel", "arbitrary")),
        name="dsa_attention",
    )(iq, iqs, iw, dq, ik, dk, dv)


def _split3(x):
    hi = x.astype(jnp.bfloat16).astype(jnp.float32)
    r = x - hi
    mid = r.astype(jnp.bfloat16).astype(jnp.float32)
    lo = (r - mid).astype(jnp.bfloat16).astype(jnp.float32)
    return hi, mid, lo


def _fox_pack_kernel(q_ref, k_ref, c_ref, qa_o, ka_o):
    rows = q_ref.shape[0]
    lane = lax.broadcasted_iota(jnp.int32, (rows, LANE), 1)
    c = c_ref[...]
    zero = jnp.zeros((rows, LANE), jnp.float32)
    for h in range(FOX_HEADS):
        ch = jnp.sum(jnp.where(lane == h, c, 0.0), axis=1, keepdims=True)
        hi, mid, lo = _split3(ch)
        terms = jnp.where(lane == 0, hi, jnp.where(lane == 1, mid, jnp.where(lane == 2, lo, zero)))
        ones_hi = jnp.where((lane >= 3) & (lane < 6), 1.0, 0.0)
        qa_o[:, h * FOX_QP:h * FOX_QP + LANE] = q_ref[:, h * LANE:(h + 1) * LANE]
        qa_o[:, h * FOX_QP + LANE:(h + 1) * FOX_QP] = (terms + ones_hi).astype(jnp.bfloat16)
        ones_lo = jnp.where(lane < 3, 1.0, 0.0)
        ka_o[:, h * FOX_QP:h * FOX_QP + LANE] = k_ref[:, h * LANE:(h + 1) * LANE]
        ka_o[:, h * FOX_QP + LANE:(h + 1) * FOX_QP] = (ones_lo - pltpu.roll(terms, 3, 1)).astype(jnp.bfloat16)


def fox_pack(q, k, c, *, rows, bm):
    row = lambda w: pl.BlockSpec((bm, w), lambda i: (i, 0))
    return pl.pallas_call(
        _fox_pack_kernel,
        grid=(rows // bm,),
        in_specs=[row(FOX_W), row(FOX_W), row(LANE)],
        out_specs=[row(FOX_HEADS * FOX_QP), row(FOX_HEADS * FOX_QP)],
        out_shape=[jax.ShapeDtypeStruct((rows, FOX_HEADS * FOX_QP), jnp.bfloat16)] * 2,
        compiler_params=_cparams(("parallel",)),
        name="fox_pack",
    )(q, k, c)


def _logits_stage(slot, sts, s_ref, cm_ref):
    for t, st in enumerate(sts):
        s_ref[slot, t] = st
        cm_ref[slot, t] = jnp.max(st, axis=0, keepdims=True)


def _values_stage(slot, vtts, s_ref, cm_ref, m_ref, acc_ref):
    ps, alphas = [], []
    for t in range(len(vtts)):
        m_prev = m_ref[t]
        m_new = jnp.maximum(m_prev, cm_ref[slot, t])
        alphas.append(jnp.exp2(m_prev - m_new))
        ps.append(jnp.exp2(s_ref[slot, t] - m_new).astype(jnp.bfloat16))
        m_ref[t] = m_new
    for t, (p, alpha) in enumerate(zip(ps, alphas)):
        acc_ref[t] = alpha * acc_ref[t] + _dot(vtts[t], p)


def _pipelined_tiles(nk, nk_max, logits, values):
    last = nk_max - 1
    logits(0, 0)

    def pair(jj, _):
        j = 2 * jj
        logits(jnp.minimum(j + 1, last), 1)
        values(j, 0)
        logits(jnp.minimum(j + 2, last), 0)
        values(j + 1, 1)
        return 0

    lax.fori_loop(0, lax.shift_right_logical(nk, 1), pair, 0)

    @pl.when((nk & 1) == 1)
    def _():
        values(nk - 1, 0)


def _finish_t(acc):
    return (acc[0:HEAD_DIM, :] / acc[HEAD_DIM:HEAD_DIM + 1, :]).T


def _flash_t_kernel(q_ref, k_ref, vt_ref, o_ref, m_ref, acc_ref, s_ref, cm_ref, d_ref,
                    *, bq, bk, nk_max, frame_causal, hg, dqk):
    i = pl.program_id(2)
    first = i * bq
    nk = _num_key_tiles(i, bq, bk, 0, nk_max)
    m_ref[...] = jnp.full(m_ref.shape, NEG_INF, jnp.float32)
    acc_ref[...] = jnp.zeros(acc_ref.shape, jnp.float32)
    r = lax.broadcasted_iota(jnp.int32, (bk, bq), 0)
    c = lax.broadcasted_iota(jnp.int32, (bk, bq), 1)
    d_ref[...] = (r - c) if frame_causal else ((r >> 6) - (c >> 6))

    def logits(j, slot):
        start = pl.multiple_of(j * bk, bk)
        gap = first - j * bk
        visible = gap if frame_causal else (gap >> 6)
        mask = d_ref[...] <= visible
        sts = []
        for t in range(hg):
            kt = k_ref[0, pl.ds(start, bk), t * dqk:(t + 1) * dqk]
            st = _dot_nt(kt, q_ref[0, :, t * dqk:(t + 1) * dqk])
            sts.append(jnp.where(mask, st, NEG_INF))
        _logits_stage(slot, sts, s_ref, cm_ref)

    def values(j, slot):
        _values_stage(slot, [vt_ref[0, j, t * VR:(t + 1) * VR, :] for t in range(hg)], s_ref, cm_ref, m_ref, acc_ref)

    _pipelined_tiles(nk, nk_max, logits, values)
    for t in range(hg):
        o_ref[0, :, t * HEAD_DIM:(t + 1) * HEAD_DIM] = _finish_t(acc_ref[t]).astype(o_ref.dtype)


def flash_attention_t(q, k, vt, *, lq, sp, heads, hg, dqk, bq, bk, frame_causal, name):
    nk_max = sp // bk
    return pl.pallas_call(
        functools.partial(_flash_t_kernel, bq=bq, bk=bk, nk_max=nk_max, frame_causal=frame_causal, hg=hg, dqk=dqk),
        grid=(1, heads // hg, lq // bq),
        in_specs=[pl.BlockSpec((1, bq, hg * dqk), lambda b_, g, i: (b_, i, g)),
                  _resident((1, sp, hg * dqk), lambda b_, g, i: (b_, 0, g)),
                  _resident((1, nk_max, hg * VR, bk), lambda b_, g, i: (b_, 0, g, 0))],
        out_specs=pl.BlockSpec((1, bq, hg * HEAD_DIM), lambda b_, g, i: (b_, i, g)),
        out_shape=jax.ShapeDtypeStruct((1, lq, heads * HEAD_DIM), jnp.bfloat16),
        scratch_shapes=[pltpu.VMEM((hg, 1, bq), jnp.float32), pltpu.VMEM((hg, VR, bq), jnp.float32),
                        pltpu.VMEM((2, hg, bk, bq), jnp.float32), pltpu.VMEM((2, hg, 1, bq), jnp.float32),
                        pltpu.VMEM((bk, bq), jnp.int32)],
        compiler_params=_cparams(("parallel", "parallel", "arbitrary")),
        name=name,
    )(q, k, vt)


def _dsa_t_kernel(iq_ref, iqs_ref, iwt_ref, dq_ref, ik_ref, dk_ref, dvt_ref, o_ref,
                  sc_ref, m_ref, acc_ref, s_ref, cm_ref, *, bq, bk, nk_max, topk):
    i = pl.program_id(1)
    nk = _num_key_tiles(i, bq, bk, 0, nk_max)
    qpos = i * bq + lax.broadcasted_iota(jnp.int32, (1, bq), 1)
    klim = ((qpos >> 6) + 1) << 6

    fold = lambda a, op: op(a.reshape(bk // _FOLD, _FOLD, bq), axis=0)

    def score_tile(j, carry):
        hi, lo = carry
        kt = ik_ref[0, pl.ds(pl.multiple_of(j * bk, bk), bk), :]
        acc = jnp.zeros((bk, bq), jnp.float32)
        for g in range(IDX_HEADS // 2):
            se = _dot_nt(kt, iq_ref[0, :, g * LANE:(g + 1) * LANE])
            so = _dot_nt(kt, iqs_ref[0, :, g * LANE:(g + 1) * LANE])
            acc = acc + iwt_ref[2 * g:2 * g + 1, :] * jnp.maximum(se, 0.0)
            acc = acc + iwt_ref[2 * g + 1:2 * g + 2, :] * jnp.maximum(so, 0.0)
        adm = (j * bk + lax.broadcasted_iota(jnp.int32, (bk, bq), 0)) < klim
        sc = jnp.where(adm, acc, -jnp.inf)
        sc_ref[j] = sc
        return (jnp.maximum(hi, fold(sc, jnp.max)), jnp.minimum(lo, fold(jnp.where(adm, acc, jnp.inf), jnp.min)))

    hi, lo = lax.fori_loop(0, nk, score_tile, (jnp.full((_FOLD, bq), -jnp.inf, jnp.float32),
                                               jnp.full((_FOLD, bq), jnp.inf, jnp.float32)))
    hi = jnp.max(hi, axis=0, keepdims=True)
    lo = jnp.min(lo, axis=0, keepdims=True)

    def count_ge(pivot):
        def body(j, acc):
            return acc + fold(jnp.where(sc_ref[j] >= pivot, 1.0, 0.0), jnp.sum)
        acc = lax.fori_loop(0, nk, body, jnp.zeros((_FOLD, bq), jnp.float32))
        return jnp.sum(acc, axis=0, keepdims=True)

    kf = jnp.float32(topk)
    hi = _key_to_f32(_f32_to_key(hi) + 1)
    c_lo = klim.astype(jnp.float32)
    c_hi = jnp.zeros((1, bq), jnp.float32)

    def settled(lo, hi, c_lo):
        width = lax.bitcast_convert_type(_f32_to_key(hi) - _f32_to_key(lo), jnp.uint32)
        return (c_lo <= kf) | (width <= 1)

    def search_cond(state):
        it, n_open = state[0], state[1]
        return (n_open > 0) & (it < _MAX_PROBES)

    def search_step(state):
        it, _, lo, hi, c_lo, c_hi = state
        done = settled(lo, hi, c_lo)
        k_lo, k_hi = _f32_to_key(lo), _f32_to_key(hi)
        width = lax.bitcast_convert_type(k_hi - k_lo, jnp.uint32)
        k_mid = k_lo + lax.bitcast_convert_type(width >> 1, jnp.int32)
        k_val = jnp.clip(_f32_to_key(lo + (hi - lo) * 0.5), k_lo + 1, k_hi - 1)
        probe = _key_to_f32(jnp.where((it & 7) == 7, k_mid, k_val))
        probe = jnp.where(done, lo, probe)
        c = count_ge(probe)
        up = (c >= kf) & ~done
        down = (c < kf) & ~done
        lo, c_lo = jnp.where(up, probe, lo), jnp.where(up, c, c_lo)
        hi, c_hi = jnp.where(down, probe, hi), jnp.where(down, c, c_hi)
        n_open = jnp.sum(jnp.where(settled(lo, hi, c_lo), 0, 1))
        return it + 1, n_open, lo, hi, c_lo, c_hi

    state = (jnp.int32(0), jnp.sum(jnp.where(settled(lo, hi, c_lo), 0, 1)), lo, hi, c_lo, c_hi)
    thr = lax.while_loop(search_cond, search_step, state)[2]

    m_ref[...] = jnp.full(m_ref.shape, NEG_INF, jnp.float32)
    acc_ref[...] = jnp.zeros(acc_ref.shape, jnp.float32)

    def logits(j, slot):
        start = pl.multiple_of(j * bk, bk)
        drop = jnp.where(sc_ref[j] >= thr, 0.0, NEG_INF)
        sts = []
        for h in range(DSA_HEADS):
            g = h // DSA_GROUP
            kt = dk_ref[0, pl.ds(start, bk), g * LANE:(g + 1) * LANE]
            sts.append(_dot_nt(kt, dq_ref[0, :, h * LANE:(h + 1) * LANE]) + drop)
        _logits_stage(slot, sts, s_ref, cm_ref)

    def values(j, slot):
        vtts = [dvt_ref[0, j, (h // DSA_GROUP) * VR:(h // DSA_GROUP + 1) * VR, :] for h in range(DSA_HEADS)]
        _values_stage(slot, vtts, s_ref, cm_ref, m_ref, acc_ref)

    _pipelined_tiles(nk, nk_max, logits, values)
    for h in range(DSA_HEADS):
        o_ref[0, :, h * LANE:(h + 1) * LANE] = _finish_t(acc_ref[h]).astype(o_ref.dtype)


def dsa_attention_t(iq, iqs, iwt, dq, ik, dk, dvt, *, lq, sp, bq, bk, topk):
    nk_max = sp // bk
    qrow = lambda w: pl.BlockSpec((1, bq, w), lambda b_, i: (b_, i, 0))
    krow = lambda w: _resident((1, sp, w), lambda b_, i: (b_, 0, 0))
    return pl.pallas_call(
        functools.partial(_dsa_t_kernel, bq=bq, bk=bk, nk_max=nk_max, topk=topk),
        grid=(1, lq // bq),
        in_specs=[qrow(IDX_W), qrow(IDX_W), pl.BlockSpec((LANE, bq), lambda b_, i: (0, i)), qrow(DSA_W),
                  krow(LANE), krow(DSA_KV_W),
                  _resident((1, nk_max, DSA_KV_HEADS * VR, bk), lambda b_, i: (b_, 0, 0, 0))],
        out_specs=qrow(DSA_W),
        out_shape=jax.ShapeDtypeStruct((1, lq, DSA_W), jnp.bfloat16),
        scratch_shapes=[pltpu.VMEM((nk_max, bk, bq), jnp.float32),
                        pltpu.VMEM((DSA_HEADS, 1, bq), jnp.float32),
                        pltpu.VMEM((DSA_HEADS, VR, bq), jnp.float32),
                        pltpu.VMEM((2, DSA_HEADS, bk, bq), jnp.float32),
                        pltpu.VMEM((2, DSA_HEADS, 1, bq), jnp.float32)],
        compiler_params=_cparams(("parallel", "arbitrary")),
        name="dsa_attention_t",
    )(iq, iqs, iwt, dq, ik, dk, dvt)


def _out_proj_kernel(x_ref, a_ref, b_ref, c_ref, wa_ref, wb_ref, wc_ref, o_ref):
    o_ref[...] = (x_ref[...] + _dot(a_ref[...], wa_ref[...]) + _dot(b_ref[...], wb_ref[...])
                  + _dot(c_ref[...], wc_ref[...]))


def out_proj(x, a, b, c, wa, wb, wc, *, bm):
    m, d = x.shape
    row = lambda w: pl.BlockSpec((bm, w), lambda i: (i, 0))
    full = lambda w: _resident(w.shape, lambda i: (0, 0))
    return pl.pallas_call(
        _out_proj_kernel,
        grid=(m // bm,),
        in_specs=[row(d), row(a.shape[1]), row(b.shape[1]), row(c.shape[1]), full(wa), full(wb), full(wc)],
        out_specs=row(d),
        out_shape=jax.ShapeDtypeStruct((m, d), jnp.float32),
        compiler_params=_cparams(("parallel",)),
        name="out_proj",
    )(x, a, b, c, wa, wb, wc)


def _ffn_kernel(x_ref, g_ref, wg_ref, wu_ref, wd_ref, o_ref, xn_ref, acc_ref):
    j = pl.program_id(1)

    @pl.when(j == 0)
    def _():
        xn_ref[...] = _rms(x_ref[...], g_ref[...]).astype(jnp.bfloat16)
        acc_ref[...] = jnp.zeros_like(acc_ref)

    xn = xn_ref[...]
    gate = _dot(xn, wg_ref[...])
    up = _dot(xn, wu_ref[...])
    hidden = (gate * (1.0 / (1.0 + jnp.exp(-gate))) * up).astype(jnp.bfloat16)
    acc_ref[...] += _dot(hidden, wd_ref[...])

    @pl.when(j == pl.num_programs(1) - 1)
    def _():
        o_ref[...] = x_ref[...] + acc_ref[...]


def ffn(x, g, wg, wu, wd, *, bm, bf):
    m, d = x.shape
    f = wg.shape[1]
    return pl.pallas_call(
        _ffn_kernel,
        grid=(m // bm, f // bf),
        in_specs=[pl.BlockSpec((bm, d), lambda i, j: (i, 0)),
                  pl.BlockSpec((1, d), lambda i, j: (0, 0)),
                  pl.BlockSpec((d, bf), lambda i, j: (0, j)),
                  pl.BlockSpec((d, bf), lambda i, j: (0, j)),
                  pl.BlockSpec((bf, d), lambda i, j: (j, 0))],
        out_specs=pl.BlockSpec((bm, d), lambda i, j: (i, 0)),
        out_shape=jax.ShapeDtypeStruct((m, d), jnp.float32),
        scratch_shapes=[pltpu.VMEM((bm, d), jnp.bfloat16), pltpu.VMEM((bm, d), jnp.float32)],
        compiler_params=_cparams(("parallel", "arbitrary")),
        name="swiglu",
    )(x, g.reshape(1, d), wg, wu, wd)


def _final_norm_kernel(x_ref, g_ref, o_ref):
    o_ref[...] = _rms(x_ref[...], g_ref[...])


def final_norm(x, g, *, bm):
    m, d = x.shape
    return pl.pallas_call(
        _final_norm_kernel,
        grid=(m // bm,),
        in_specs=[pl.BlockSpec((bm, d), lambda i: (i, 0)), pl.BlockSpec((1, d), lambda i: (0, 0))],
        out_specs=pl.BlockSpec((bm, d), lambda i: (i, 0)),
        out_shape=jax.ShapeDtypeStruct((m, d), jnp.float32),
        compiler_params=_cparams(("parallel",)),
        name="final_norm",
    )(x, g.reshape(1, d))


def _pad_cols(a, width):
    return jnp.pad(a, [(0, 0)] * (a.ndim - 1) + [(0, width - a.shape[-1])])


def _layout_w_in(w_in):
    sizes = (FOX_W, FOX_W, FOX_W, FOX_HEADS, DSA_W, DSA_KV_W, DSA_KV_W, IDX_W, IDX_DIM, IDX_HEADS,
             MLA_Q_LORA, MLA_KV_LORA, MLA_ROPE)
    splits = np.cumsum(sizes)[:-1]
    q_a, k_a, v_a, f_a, q_b, k_b, v_b, q_i, k_i, w_i, c_q, c_kv, k_r = jnp.split(w_in, splits, axis=-1)
    parts = {"fq": q_a, "fk": k_a, "fv": v_a, "dq": q_b, "dk": k_b, "dv": v_b, "iq": q_i, "cq": c_q,
             "ckv": c_kv, "fa": f_a, "ik": k_i, "iw": w_i, "kr": k_r}
    cols = [_pad_cols(parts[n], w) for n, w in _SEGS]
    return _pad_cols(jnp.concatenate(cols, axis=-1), PROJ_W).astype(jnp.bfloat16)


def _layout_w_uq(w_uq):
    dp = w_uq.shape[0]
    w = w_uq.reshape(dp, MLA_Q_LORA, MLA_HEADS, MLA_NOPE + MLA_ROPE)
    return _pad_cols(w, MLA_QP).reshape(dp, MLA_Q_LORA, MLA_QW).astype(jnp.bfloat16)


def _layout_w_ukv(w_ukv):
    dp = w_ukv.shape[0]
    w = w_ukv.reshape(dp, MLA_KV_LORA, MLA_HEADS, MLA_NOPE + MLA_V)
    kn = w[..., :MLA_NOPE].reshape(dp, MLA_KV_LORA, MLA_HEADS * MLA_NOPE)
    vv = w[..., MLA_NOPE:].reshape(dp, MLA_KV_LORA, MLA_VW)
    return jnp.concatenate([kn, vv], axis=-1).astype(jnp.bfloat16)


def _rope_table(pos, rot, period):
    half = rot // 2
    inv_freq = ROPE_THETA ** (-jnp.arange(half, dtype=jnp.float32) / half)
    ang = pos.astype(jnp.float32)[:, None] * inv_freq[None, :]
    cos, sin = jnp.cos(ang), jnp.sin(ang)
    r = pos.shape[0]
    ones = jnp.ones((r, period - rot), jnp.float32)
    zeros = jnp.zeros((r, period - rot), jnp.float32)
    zh = jnp.zeros((r, half), jnp.float32)
    c = jnp.concatenate([cos, cos, ones], axis=1)
    s1 = jnp.concatenate([zh, sin, zeros], axis=1)
    s2 = jnp.concatenate([-sin, zh, zeros], axis=1)
    rep = LANE // period
    return jnp.concatenate([jnp.tile(c, (1, rep)), jnp.tile(s1, (1, rep)), jnp.tile(s2, (1, rep))], axis=1)


def _pick(n, candidates):
    for c in candidates:
        if n % c == 0:
            return c
    raise ValueError(f"no block size among {candidates} divides {n}")


def _pad_seq(a, sp):
    return jnp.pad(a, [(0, 0), (0, sp - a.shape[1])] + [(0, 0)] * (a.ndim - 2))


def kernel(x_prompt, x_sample, cache_fox_k, cache_fox_v, cache_fox_logf, cache_dsa_k, cache_dsa_v, cache_idx_k, cache_mla_ckv, cache_mla_krope, w_in, fox_bias, mla_q_norm, mla_w_uq, mla_kv_norm, mla_w_ukv, w_o, attn_norm, ffn_norm, w_gate, w_up, w_down, final_norm_w):
    bf = jnp.bfloat16
    depth = w_in.shape[0]
    nb, seq, d = x_prompt.shape
    db, dec, _ = x_sample.shape
    past = cache_fox_k.shape[2]
    assert nb == 1 and seq % CHUNK == 0 and dec % CHUNK == 0 and past % CHUNK == 0
    mp, ms = nb * seq, db * dec
    m = mp + ms
    s_len = past + dec

    bm = _pick(m, (512, 256, 128, 64))
    assert mp % bm == 0
    bq_p = _pick(seq, (256, 128))
    bk_p = bm
    bq_s = _pick(dec, (64,))
    bk_s = 256
    sp = ((s_len + bk_s - 1) // bk_s) * bk_s
    topk_p = min(IDX_TOPK_MAX, seq // 4)
    topk_s = min(IDX_TOPK_MAX, s_len // 4)
    hg_p = 2
    fox_hg, mla_hg = 3, 2
    d_ff = w_gate.shape[2]
    bff = _pick(d_ff, (512, 256, 128))

    w_in_p = _layout_w_in(w_in)
    wuq_p = _layout_w_uq(mla_w_uq)
    wukv_p = _layout_w_ukv(mla_w_ukv)
    w_o_b = w_o.astype(bf)
    wg_b, wu_b, wd_b = w_gate.astype(bf), w_up.astype(bf), w_down.astype(bf)
    fox_bias_p = _pad_cols(fox_bias, LANE).reshape(depth, 1, LANE)

    pos = jnp.concatenate([jnp.tile(jnp.arange(seq, dtype=jnp.int32), nb),
                           jnp.tile(past + jnp.arange(dec, dtype=jnp.int32), db)])
    tab = jnp.concatenate([_rope_table(pos, PARTIAL_ROT, HEAD_DIM), _rope_table(pos, IDX_ROT, IDX_DIM),
                           _rope_table(pos, MLA_ROPE, LANE)], axis=1)

    x = jnp.concatenate([x_prompt.reshape(mp, d), x_sample.reshape(ms, d)], axis=0)
    p_rows, s_rows = [], []
    for l in range(depth):
        proj = norm_matmul(x, attn_norm[l], w_in_p[l], bm=bm, bn=_pick(PROJ_W, (1536, 1024, 512)))
        (fq, fk, fv, fkb, fvb, fvt, lf, dq, dk, dv, dkb, dvb, dvt, iq, iqs, ik, ikb, iw, iwt, mq, ckv, kr) = post_proj(
            proj, tab, fox_bias_p[l], mla_q_norm[l].reshape(1, -1), mla_kv_norm[l].reshape(1, -1), wuq_p[l],
            bm=bm // 2, bk=bm)

        pr = lambda a: a[:mp].reshape(nb, seq, -1)
        sr = lambda a: a[mp:].reshape(db, dec, -1)
        p_rows.append((pr(fk).reshape(nb, seq, FOX_HEADS, HEAD_DIM), pr(fv).reshape(nb, seq, FOX_HEADS, HEAD_DIM),
                       pr(lf)[..., :FOX_HEADS], pr(dk).reshape(nb, seq, DSA_KV_HEADS, HEAD_DIM),
                       pr(dv).reshape(nb, seq, DSA_KV_HEADS, HEAD_DIM), pr(ik)[..., :IDX_DIM], pr(ckv),
                       pr(kr)[..., :MLA_ROPE]))
        s_rows.append((sr(fk).reshape(db, dec, FOX_HEADS, HEAD_DIM), sr(fv).reshape(db, dec, FOX_HEADS, HEAD_DIM),
                       sr(lf)[..., :FOX_HEADS], sr(dk).reshape(db, dec, DSA_KV_HEADS, HEAD_DIM),
                       sr(dv).reshape(db, dec, DSA_KV_HEADS, HEAD_DIM), sr(ik)[..., :IDX_DIM], sr(ckv),
                       sr(kr)[..., :MLA_ROPE]))

        cat = lambda c, new: _pad_seq(jnp.concatenate([c, new], axis=1), sp)
        s_fkb = cat(cache_fox_k[l].reshape(db, past, FOX_W).astype(bf), sr(fkb))
        s_fvb = cat(cache_fox_v[l].reshape(db, past, FOX_W).astype(bf), sr(fvb))
        s_lf = cat(_pad_cols(cache_fox_logf[l], LANE), sr(lf))
        s_dkb = cat(cache_dsa_k[l].reshape(db, past, DSA_KV_W).astype(bf), sr(dkb))
        s_dvb = cat(cache_dsa_v[l].reshape(db, past, DSA_KV_W).astype(bf), sr(dvb))
        s_ikb = cat(_pad_cols(cache_idx_k[l], LANE).astype(bf), sr(ikb))
        s_ckv = cat(cache_mla_ckv[l], sr(ckv))
        s_kr = cat(_pad_cols(cache_mla_krope[l], LANE), sr(kr))

        ckv_all = jnp.concatenate([ckv[:mp], s_ckv.reshape(db * sp, -1)], axis=0)
        kr_all = jnp.concatenate([kr[:mp], s_kr.reshape(db * sp, -1)], axis=0)
        kmla, vmla, vtmla = kv_up(ckv_all, kr_all, wukv_p[l], bm=bm)
        s_kmla, s_vmla = kmla[mp:].reshape(db, sp, -1), vmla[mp:].reshape(db, sp, -1)

        p_c, _ = cumsum_rows(pr(lf), bt=_pick(seq, (256, 128)))
        s_c, s_ct = cumsum_rows(s_lf, bt=_pick(sp, (256, 128)))
        ck_form = lambda ct, bk: ct[:, :FOX_HEADS].reshape(ct.shape[0], FOX_HEADS, ct.shape[2] // bk, 1, bk)

        al = lambda a: a.reshape(1, *a.shape)
        fqa, fka = fox_pack(fq, fkb, p_c[0], rows=mp, bm=bm)
        out_a = flash_attention_t(al(fqa), al(fka), al(fvt), lq=seq, sp=seq, heads=FOX_HEADS, hg=hg_p, dqk=FOX_QP,
                                  bq=bq_p, bk=bk_p, frame_causal=True, name="fox_attention_t")
        out_b = dsa_attention_t(al(iq), al(iqs), iwt, al(dq), al(ikb), al(dkb), al(dvt), lq=seq, sp=seq,
                                bq=bq_p, bk=bk_p, topk=topk_p)
        out_c = flash_attention_t(al(mq), al(kmla), al(vtmla), lq=seq, sp=seq, heads=MLA_HEADS, hg=hg_p, dqk=MLA_QP,
                                  bq=bq_p, bk=bk_p, frame_causal=False, name="mla_attention_t")
        s_out_a = flash_attention(sr(fq), s_fkb, s_fvb, s_c[:, past:past + dec], ck_form(s_ct, bk_s), lq=dec, sp=sp,
                                  heads=FOX_HEADS, hg=fox_hg, dqk=HEAD_DIM, dv=HEAD_DIM, bq=bq_s, bk=bk_s, q_off=past)
        s_out_b = dsa_attention(sr(iq), sr(iqs), sr(iw), sr(dq), s_ikb, s_dkb, s_dvb, lq=dec, sp=sp,
                                bq=bq_s, bk=bk_s, q_off=past, topk=topk_s)
        s_out_c = flash_attention(sr(mq), s_kmla, s_vmla, lq=dec, sp=sp, heads=MLA_HEADS, hg=mla_hg,
                                  dqk=MLA_QP, dv=MLA_V, bq=bq_s, bk=bk_s, q_off=past)

        rows = lambda p, s: jnp.concatenate([p.reshape(mp, -1), s.reshape(ms, -1)], axis=0)
        x = out_proj(x, rows(out_a, s_out_a), rows(out_b, s_out_b), rows(out_c, s_out_c),
                     w_o_b[l, :FOX_W], w_o_b[l, FOX_W:FOX_W + DSA_W], w_o_b[l, FOX_W + DSA_W:], bm=bm)
        x = ffn(x, ffn_norm[l], wg_b[l], wu_b[l], wd_b[l], bm=bm, bf=bff)

    y = final_norm(x, final_norm_w, bm=bm)
    y_prompt = y[:mp].reshape(nb, seq, d)
    y_sample = y[mp:].reshape(db, dec, d)
    stack = lambda rows_: tuple(jnp.stack(t) for t in zip(*rows_))
    return (y_prompt, y_sample) + stack(p_rows) + stack(s_rows)
```

```python
import functools
import math

import jax
import jax.numpy as jnp
import numpy as np
from jax import lax
from jax.experimental import pallas as pl
from jax.experimental.pallas import tpu as pltpu

CHUNK = 64
HEAD_DIM = 128
FOX_HEADS = 6
DSA_HEADS = 6
DSA_KV_HEADS = 2
DSA_GROUP = DSA_HEADS // DSA_KV_HEADS
IDX_HEADS = 16
IDX_DIM = 64
IDX_TOPK_MAX = 256
MLA_HEADS = 4
MLA_Q_LORA = 512
MLA_KV_LORA = 256
MLA_NOPE = 128
MLA_ROPE = 64
MLA_V = 128
PARTIAL_ROT = HEAD_DIM // 4
IDX_ROT = IDX_DIM // 4
ROPE_THETA = 500000.0
EPS = 1e-6
NEG_INF = -1e30
FOX_SCALE = HEAD_DIM ** -0.5
DSA_SCALE = HEAD_DIM ** -0.5
MLA_SCALE = (MLA_NOPE + MLA_ROPE) ** -0.5
IDX_W_SCALE = (IDX_HEADS * IDX_DIM) ** -0.5
LOG2E = math.log2(math.e)

LANE = 128
VMEM_LIMIT = 56 * 1024 * 1024

FOX_W = FOX_HEADS * HEAD_DIM
DSA_W = DSA_HEADS * HEAD_DIM
DSA_KV_W = DSA_KV_HEADS * HEAD_DIM
IDX_W = IDX_HEADS * IDX_DIM
MLA_QP = 2 * LANE
MLA_QW = MLA_HEADS * MLA_QP
MLA_VW = MLA_HEADS * MLA_V
VR = HEAD_DIM + 16
FOX_QP = 2 * LANE

_SEGS = (("fq", FOX_W), ("fk", FOX_W), ("fv", FOX_W), ("dq", DSA_W), ("dk", DSA_KV_W), ("dv", DSA_KV_W),
         ("iq", IDX_W), ("cq", MLA_Q_LORA), ("ckv", MLA_KV_LORA),
         ("fa", LANE), ("ik", LANE), ("iw", LANE), ("kr", LANE))
_OFF = {}
_o = 0
for _n, _w in _SEGS:
    _OFF[_n] = _o
    _o += _w
PROJ_W = ((_o + 511) // 512) * 512

_NT = (((1,), (1,)), ((), ()))


def _dot(a, b):
    return jnp.dot(a, b, preferred_element_type=jnp.float32)


def _dot_nt(a, b):
    return lax.dot_general(a, b, _NT, preferred_element_type=jnp.float32)


def _cparams(sem):
    return pltpu.CompilerParams(dimension_semantics=sem, vmem_limit_bytes=VMEM_LIMIT)


def _resident(block_shape, index_map):
    return pl.BlockSpec(block_shape, index_map, pipeline_mode=pl.Buffered(1))


def _rms(x, g):
    return x * lax.rsqrt(jnp.mean(x * x, axis=-1, keepdims=True) + EPS) * g


def _norm_matmul_kernel(x_ref, g_ref, w_ref, o_ref, xn_ref):
    @pl.when(pl.program_id(1) == 0)
    def _():
        xn_ref[...] = _rms(x_ref[...], g_ref[...]).astype(jnp.bfloat16)

    o_ref[...] = _dot(xn_ref[...], w_ref[...])


def norm_matmul(x, g, w, *, bm, bn):
    m, d = x.shape
    n = w.shape[1]
    return pl.pallas_call(
        _norm_matmul_kernel,
        grid=(m // bm, n // bn),
        in_specs=[pl.BlockSpec((bm, d), lambda i, j: (i, 0)),
                  pl.BlockSpec((1, d), lambda i, j: (0, 0)),
                  pl.BlockSpec((d, bn), lambda i, j: (0, j))],
        out_specs=pl.BlockSpec((bm, bn), lambda i, j: (i, j)),
        out_shape=jax.ShapeDtypeStruct((m, n), jnp.float32),
        scratch_shapes=[pltpu.VMEM((bm, d), jnp.bfloat16)],
        compiler_params=_cparams(("parallel", "arbitrary")),
        name="norm_matmul",
    )(x, g.reshape(1, d), w)


def _rope(x, tab, half):
    c, s1, s2 = tab[:, 0:LANE], tab[:, LANE:2 * LANE], tab[:, 2 * LANE:3 * LANE]
    return x * c + pltpu.roll(x, half, 1) * s1 + pltpu.roll(x, LANE - half, 1) * s2


def _value_t_tail(width):
    r = lax.broadcasted_iota(jnp.int32, (16, width), 0)
    return jnp.where(r == 0, 1.0, 0.0).astype(jnp.bfloat16)


def _store_value_t(o_ref, h, v):
    o_ref[0, h * VR:h * VR + HEAD_DIM, :] = v.T.astype(jnp.bfloat16)
    o_ref[0, h * VR + HEAD_DIM:(h + 1) * VR, :] = _value_t_tail(v.shape[0])


def _post_kernel(p_ref, tab_ref, fb_ref, qn_ref, kvn_ref, wuq_ref,
                 fq_o, fk_o, fv_o, fkb_o, fvb_o, fvt_o, lf_o,
                 dq_o, dk_o, dv_o, dkb_o, dvb_o, dvt_o,
                 iq_o, iqs_o, ik_o, ikb_o, iw_o, iwt_o,
                 mq_o, ckv_o, kr_o):
    bf = jnp.bfloat16
    seg = lambda name, w: p_ref[:, _OFF[name]:_OFF[name] + w]
    tab_d = tab_ref[:, 0:3 * LANE]
    tab_i = tab_ref[:, 3 * LANE:6 * LANE]
    tab_m = tab_ref[:, 6 * LANE:9 * LANE]

    fq_o[...] = (seg("fq", FOX_W) * (FOX_SCALE * LOG2E)).astype(bf)
    fk = seg("fk", FOX_W)
    fk_o[...] = fk
    fkb_o[...] = fk.astype(bf)
    fv = seg("fv", FOX_W)
    fv_o[...] = fv
    fvb_o[...] = fv.astype(bf)
    for h in range(FOX_HEADS):
        _store_value_t(fvt_o, h, p_ref[:, _OFF["fv"] + h * LANE:_OFF["fv"] + (h + 1) * LANE])
    z = seg("fa", LANE) + fb_ref[...]
    lf_o[...] = jnp.minimum(z, 0.0) - jnp.log1p(jnp.exp(-jnp.abs(z)))

    for h in range(DSA_HEADS):
        x = p_ref[:, _OFF["dq"] + h * LANE:_OFF["dq"] + (h + 1) * LANE]
        dq_o[:, h * LANE:(h + 1) * LANE] = (_rope(x, tab_d, PARTIAL_ROT // 2) * (DSA_SCALE * LOG2E)).astype(bf)
    for h in range(DSA_KV_HEADS):
        x = p_ref[:, _OFF["dk"] + h * LANE:_OFF["dk"] + (h + 1) * LANE]
        y = _rope(x, tab_d, PARTIAL_ROT // 2)
        dk_o[:, h * LANE:(h + 1) * LANE] = y
        dkb_o[:, h * LANE:(h + 1) * LANE] = y.astype(bf)
        _store_value_t(dvt_o, h, p_ref[:, _OFF["dv"] + h * LANE:_OFF["dv"] + (h + 1) * LANE])
    dv = seg("dv", DSA_KV_W)
    dv_o[...] = dv
    dvb_o[...] = dv.astype(bf)
    for j in range(IDX_W // LANE):
        x = p_ref[:, _OFF["iq"] + j * LANE:_OFF["iq"] + (j + 1) * LANE]
        y = _rope(x, tab_i, IDX_ROT // 2)
        iq_o[:, j * LANE:(j + 1) * LANE] = y.astype(bf)
        iqs_o[:, j * LANE:(j + 1) * LANE] = pltpu.roll(y, IDX_DIM, 1).astype(bf)
    y = _rope(seg("ik", LANE), tab_i, IDX_ROT // 2)
    ik_o[...] = y
    ikb_o[...] = y.astype(bf)
    iw = seg("iw", LANE) * IDX_W_SCALE
    iw_o[...] = iw
    iwt_o[...] = iw.T

    cqn = _rms(seg("cq", MLA_Q_LORA), qn_ref[...]).astype(bf)
    mq = _dot(cqn, wuq_ref[...])
    for h in range(MLA_HEADS):
        a = h * MLA_QP
        mq_o[:, a:a + LANE] = (mq[:, a:a + LANE] * (MLA_SCALE * LOG2E)).astype(bf)
        r = _rope(mq[:, a + LANE:a + 2 * LANE], tab_m, MLA_ROPE // 2)
        mq_o[:, a + LANE:a + 2 * LANE] = (r * (MLA_SCALE * LOG2E)).astype(bf)
    ckv_o[...] = _rms(seg("ckv", MLA_KV_LORA), kvn_ref[...])
    kr_o[...] = _rope(seg("kr", LANE), tab_m, MLA_ROPE // 2)


def post_proj(proj, tab, fox_bias_p, q_norm, kv_norm, wuq_p, *, bm, bk):
    m = proj.shape[0]
    r = bk // bm
    f32, bf = jnp.float32, jnp.bfloat16
    row = lambda w: pl.BlockSpec((bm, w), lambda i: (i, 0))
    full = lambda a: pl.BlockSpec(a.shape, lambda i: (0,) * a.ndim)
    vt = lambda heads: pl.BlockSpec((1, heads * VR, bm), lambda i: (i // r, 0, i % r))
    rowo = lambda w, dt: (row(w), jax.ShapeDtypeStruct((m, w), dt))
    vto = lambda heads: (vt(heads), jax.ShapeDtypeStruct((m // bk, heads * VR, bk), bf))
    outs = [rowo(FOX_W, bf), rowo(FOX_W, f32), rowo(FOX_W, f32), rowo(FOX_W, bf), rowo(FOX_W, bf), vto(FOX_HEADS),
            rowo(LANE, f32),
            rowo(DSA_W, bf), rowo(DSA_KV_W, f32), rowo(DSA_KV_W, f32), rowo(DSA_KV_W, bf), rowo(DSA_KV_W, bf),
            vto(DSA_KV_HEADS),
            rowo(IDX_W, bf), rowo(IDX_W, bf), rowo(LANE, f32), rowo(LANE, bf), rowo(LANE, f32),
            (pl.BlockSpec((LANE, bm), lambda i: (0, i)), jax.ShapeDtypeStruct((LANE, m), f32)),
            rowo(MLA_QW, bf), rowo(MLA_KV_LORA, f32), rowo(LANE, f32)]
    return pl.pallas_call(
        _post_kernel,
        grid=(m // bm,),
        in_specs=[row(PROJ_W), row(9 * LANE), full(fox_bias_p), full(q_norm), full(kv_norm), full(wuq_p)],
        out_specs=[s for s, _ in outs],
        out_shape=[o for _, o in outs],
        compiler_params=_cparams(("parallel",)),
        name="post_proj",
    )(proj, tab, fox_bias_p, q_norm, kv_norm, wuq_p)


def _kv_up_kernel(ckv_ref, kr_ref, w_ref, k_o, v_o, vt_o):
    bf = jnp.bfloat16
    kv = _dot(ckv_ref[...].astype(bf), w_ref[...])
    kr = kr_ref[...].astype(bf)
    for h in range(MLA_HEADS):
        k_o[:, h * MLA_QP:h * MLA_QP + LANE] = kv[:, h * LANE:(h + 1) * LANE].astype(bf)
        k_o[:, h * MLA_QP + LANE:(h + 1) * MLA_QP] = kr
        _store_value_t(vt_o, h, kv[:, (MLA_HEADS + h) * LANE:(MLA_HEADS + h + 1) * LANE])
    v_o[...] = kv[:, MLA_HEADS * LANE:].astype(bf)


def kv_up(ckv, kr, w, *, bm):
    r = ckv.shape[0]
    row = lambda w_: pl.BlockSpec((bm, w_), lambda i: (i, 0))
    return pl.pallas_call(
        _kv_up_kernel,
        grid=(r // bm,),
        in_specs=[row(MLA_KV_LORA), row(LANE), pl.BlockSpec(w.shape, lambda i: (0, 0))],
        out_specs=[row(MLA_QW), row(MLA_VW), pl.BlockSpec((1, MLA_HEADS * VR, bm), lambda i: (i, 0, 0))],
        out_shape=[jax.ShapeDtypeStruct((r, MLA_QW), jnp.bfloat16),
                   jax.ShapeDtypeStruct((r, MLA_VW), jnp.bfloat16),
                   jax.ShapeDtypeStruct((r // bm, MLA_HEADS * VR, bm), jnp.bfloat16)],
        compiler_params=_cparams(("parallel",)),
        name="mla_kv_up",
    )(ckv, kr, w)


def _cumsum_kernel(x_ref, c_o, ct_o, carry_ref, carry_t_ref):
    t = x_ref.shape[1]

    @pl.when(pl.program_id(1) == 0)
    def _():
        carry_ref[...] = jnp.zeros_like(carry_ref)
        carry_t_ref[...] = jnp.zeros_like(carry_t_ref)

    x = x_ref[0]
    r = lax.broadcasted_iota(jnp.int32, (t, t), 0)
    c = lax.broadcasted_iota(jnp.int32, (t, t), 1)
    lower = jnp.where(c <= r, 1.0, 0.0).astype(jnp.float32)
    upper = jnp.where(r <= c, 1.0, 0.0).astype(jnp.float32)
    cs = jnp.dot(lower, x, preferred_element_type=jnp.float32, precision=lax.Precision.HIGHEST)
    cs = cs + carry_ref[...]
    c_o[0] = cs * LOG2E
    carry_ref[...] = cs[t - 1:t, :]
    cst = jnp.dot(x.T, upper, preferred_element_type=jnp.float32, precision=lax.Precision.HIGHEST)
    cst = cst + carry_t_ref[...]
    ct_o[0] = cst[0:8, :] * LOG2E
    carry_t_ref[...] = cst[:, t - 1:t]


def cumsum_rows(x, *, bt):
    b, s, _ = x.shape
    return pl.pallas_call(
        _cumsum_kernel,
        grid=(b, s // bt),
        in_specs=[pl.BlockSpec((1, bt, LANE), lambda i, j: (i, j, 0))],
        out_specs=[pl.BlockSpec((1, bt, LANE), lambda i, j: (i, j, 0)),
                   pl.BlockSpec((1, 8, bt), lambda i, j: (i, 0, j))],
        out_shape=[jax.ShapeDtypeStruct((b, s, LANE), jnp.float32),
                   jax.ShapeDtypeStruct((b, 8, s), jnp.float32)],
        scratch_shapes=[pltpu.VMEM((1, LANE), jnp.float32), pltpu.VMEM((LANE, 1), jnp.float32)],
        compiler_params=_cparams(("parallel", "arbitrary")),
        name="logf_cumsum",
    )(x)


def _num_key_tiles(i, bq, bk, q_off, nk_max):
    last = q_off + (i + 1) * bq
    return jnp.minimum((last + bk - 1) // bk, nk_max)


def _softmax_step(s, vt, m_ref, l_ref, acc_ref, t):
    m_prev = m_ref[t]
    m_new = jnp.maximum(m_prev, jnp.max(s, axis=1, keepdims=True))
    alpha = jnp.exp2(m_prev - m_new)
    p = jnp.exp2(s - m_new)
    l_ref[t] = alpha * l_ref[t] + jnp.sum(p, axis=1, keepdims=True)
    acc_ref[t] = alpha * acc_ref[t] + _dot(p.astype(jnp.bfloat16), vt)
    m_ref[t] = m_new


def _flash_kernel(*refs, bq, bk, q_off, nk_max, fox, hg, dqk, dv):
    if fox:
        q_ref, k_ref, v_ref, cq_ref, ck_ref, o_ref, m_ref, l_ref, acc_ref = refs
    else:
        q_ref, k_ref, v_ref, o_ref, m_ref, l_ref, acc_ref = refs
    gi = pl.program_id(1)
    i = pl.program_id(2)
    first = q_off + i * bq
    qpos = first + lax.broadcasted_iota(jnp.int32, (bq, 1), 0)
    if fox:
        lane = lax.broadcasted_iota(jnp.int32, (bq, LANE), 1)
        cq_all = cq_ref[0]
        cqs = [jnp.sum(jnp.where(lane == gi * hg + t, cq_all, 0.0), axis=1, keepdims=True) for t in range(hg)]
        c0 = [c[0:1, :] for c in cqs]
        bias_q = [c - z for c, z in zip(cqs, c0)]
    nk = _num_key_tiles(i, bq, bk, q_off, nk_max)
    n_full = jnp.minimum((first + (1 if fox else CHUNK)) // bk, nk)

    m_ref[...] = jnp.full(m_ref.shape, NEG_INF, jnp.float32)
    l_ref[...] = jnp.zeros(l_ref.shape, jnp.float32)
    acc_ref[...] = jnp.zeros(acc_ref.shape, jnp.float32)

    def make_body(masked):
        def body(j, _):
            start = pl.multiple_of(j * bk, bk)
            if masked:
                kpos = j * bk + lax.broadcasted_iota(jnp.int32, (1, bk), 1)
                mask = (kpos <= qpos) if fox else ((kpos >> 6) <= (qpos >> 6))
            for t in range(hg):
                kt = k_ref[0, pl.ds(start, bk), t * dqk:(t + 1) * dqk]
                vt = v_ref[0, pl.ds(start, bk), t * dv:(t + 1) * dv]
                s = _dot_nt(q_ref[0, :, t * dqk:(t + 1) * dqk], kt)
                if fox:
                    s = s + (bias_q[t] - (ck_ref[0, t, j] - c0[t]))
                if masked:
                    s = jnp.where(mask, s, NEG_INF)
                _softmax_step(s, vt, m_ref, l_ref, acc_ref, t)
            return 0
        return body

    lax.fori_loop(0, n_full, make_body(False), 0)
    lax.fori_loop(n_full, nk, make_body(True), 0)
    for t in range(hg):
        o_ref[0, :, t * dv:(t + 1) * dv] = (acc_ref[t] / l_ref[t]).astype(o_ref.dtype)


def flash_attention(q, k, v, cq=None, ck=None, *, lq, sp, heads, hg, dqk, dv, bq, bk, q_off):
    b = q.shape[0]
    nk_max = sp // bk
    fox = cq is not None
    in_specs = [pl.BlockSpec((1, bq, hg * dqk), lambda b_, g, i: (b_, i, g)),
                pl.BlockSpec((1, sp, hg * dqk), lambda b_, g, i: (b_, 0, g)),
                pl.BlockSpec((1, sp, hg * dv), lambda b_, g, i: (b_, 0, g))]
    args = [q, k, v]
    if fox:
        in_specs += [pl.BlockSpec((1, bq, LANE), lambda b_, g, i: (b_, i, 0)),
                     pl.BlockSpec((1, hg, nk_max, 1, bk), lambda b_, g, i: (b_, g, 0, 0, 0))]
        args += [cq, ck]
    return pl.pallas_call(
        functools.partial(_flash_kernel, bq=bq, bk=bk, q_off=q_off, nk_max=nk_max, fox=fox, hg=hg, dqk=dqk, dv=dv),
        grid=(b, heads // hg, lq // bq),
        in_specs=in_specs,
        out_specs=pl.BlockSpec((1, bq, hg * dv), lambda b_, g, i: (b_, i, g)),
        out_shape=jax.ShapeDtypeStruct((b, lq, heads * dv), jnp.bfloat16),
        scratch_shapes=[pltpu.VMEM((hg, bq, 1), jnp.float32), pltpu.VMEM((hg, bq, 1), jnp.float32),
                        pltpu.VMEM((hg, bq, dv), jnp.float32)],
        compiler_params=_cparams(("parallel", "parallel", "arbitrary")),
        name="fox_attention" if fox else "mla_attention",
    )(*args)


_KEY_NEG_INF = -2139095041
_F32_LOWEST = -3.4028234663852886e38


_FOLD = 64
_MAX_PROBES = 8 * 34


def _key_to_f32(key):
    bits = jnp.where(key >= 0, key, key ^ jnp.int32(0x7FFFFFFF))
    return lax.bitcast_convert_type(bits, jnp.float32)


def _f32_to_key(x):
    bits = lax.bitcast_convert_type(x, jnp.int32)
    return jnp.where(bits >= 0, bits, bits ^ jnp.int32(0x7FFFFFFF))


def _dsa_kernel(iq_ref, iqs_ref, iw_ref, dq_ref, ik_ref, dk_ref, dv_ref, o_ref,
                sc_ref, qg_ref, m_ref, l_ref, acc_ref, *, bq, bk, q_off, nk_max, topk):
    i = pl.program_id(1)
    nk = _num_key_tiles(i, bq, bk, q_off, nk_max)
    qpos = q_off + i * bq + lax.broadcasted_iota(jnp.int32, (bq, 1), 0)
    klim = ((qpos >> 6) + 1) << 6
    iw = iw_ref[0]

    def score_tile(j, _):
        kt = ik_ref[0, pl.ds(pl.multiple_of(j * bk, bk), bk), :]
        acc = jnp.zeros((bq, bk), jnp.float32)
        for g in range(IDX_HEADS // 2):
            se = _dot_nt(iq_ref[0, :, g * LANE:(g + 1) * LANE], kt)
            so = _dot_nt(iqs_ref[0, :, g * LANE:(g + 1) * LANE], kt)
            acc = acc + iw[:, 2 * g:2 * g + 1] * jnp.maximum(se, 0.0)
            acc = acc + iw[:, 2 * g + 1:2 * g + 2] * jnp.maximum(so, 0.0)
        kpos = j * bk + lax.broadcasted_iota(jnp.int32, (1, bk), 1)
        sc_ref[j] = jnp.where(kpos < klim, acc, -jnp.inf)
        return 0

    lax.fori_loop(0, nk, score_tile, 0)

    def count(hit):
        def body(j, acc):
            h = hit(sc_ref[j], j)
            for c in range(bk // LANE):
                acc = acc + h[:, c * LANE:(c + 1) * LANE]
            return acc
        acc = lax.fori_loop(0, nk, body, jnp.zeros((bq, LANE), jnp.float32))
        return jnp.sum(acc, axis=1, keepdims=True)

    count_ge = lambda pivot: count(lambda t, j: jnp.where(t >= pivot, 1.0, 0.0))
    kf = jnp.float32(topk)
    nonneg = count_ge(jnp.zeros((bq, 1), jnp.float32)) >= kf
    key0 = jnp.where(nonneg, jnp.int32(0), jnp.int32(-2 ** 31))

    def bit_step(it, key):
        cand = key | (jnp.int32(1) << (30 - it))
        ok = (count_ge(_key_to_f32(cand)) >= kf) | (cand <= _KEY_NEG_INF)
        return jnp.where(ok, cand, key)

    key = lax.fori_loop(0, 31, bit_step, key0)
    thr = jnp.maximum(_key_to_f32(key), _F32_LOWEST)

    tie = count_ge(thr) > kf
    total = nk_max * bk

    @pl.when(jnp.sum(jnp.where(tie, 1, 0)) > 0)
    def _():
        need = kf - count(lambda t, j: jnp.where(t > thr, 1.0, 0.0))
        cols = lax.broadcasted_iota(jnp.int32, (bq, bk), 1)

        def count_equal_upto(j_max):
            return count(lambda t, j: jnp.where(t == thr, jnp.where(cols <= j_max - j * bk, 1.0, 0.0), 0.0))

        def index_step(_, bracket):
            j_lo, j_hi = bracket
            mid = (j_lo + j_hi) >> 1
            ok = count_equal_upto(mid) >= need
            return jnp.where(ok, j_lo, mid), jnp.where(ok, mid, j_hi)

        bracket = (jnp.full((bq, 1), -1, jnp.int32), jnp.full((bq, 1), total - 1, jnp.int32))
        _, j_cut = lax.fori_loop(0, total.bit_length() + 1, index_step, bracket)
        j_cut = jnp.where(tie, j_cut, total)

        def drop_late_ties(j, _):
            t = sc_ref[j]
            sc_ref[j] = jnp.where(t == thr, jnp.where(cols > j_cut - j * bk, -jnp.inf, t), t)
            return 0

        lax.fori_loop(0, nk, drop_late_ties, 0)

    for h in range(DSA_HEADS):
        g, r = divmod(h, DSA_GROUP)
        qg_ref[g, r * bq:(r + 1) * bq, :] = dq_ref[0, :, h * LANE:(h + 1) * LANE]
    m_ref[...] = jnp.full(m_ref.shape, NEG_INF, jnp.float32)
    l_ref[...] = jnp.zeros(l_ref.shape, jnp.float32)
    acc_ref[...] = jnp.zeros(acc_ref.shape, jnp.float32)

    def att_tile(j, _):
        start = pl.multiple_of(j * bk, bk)
        drop = jnp.where(sc_ref[j] >= thr, 0.0, NEG_INF)
        drop = jnp.concatenate([drop] * DSA_GROUP, axis=0)
        for g in range(DSA_KV_HEADS):
            kt = dk_ref[0, pl.ds(start, bk), g * LANE:(g + 1) * LANE]
            vt = dv_ref[0, pl.ds(start, bk), g * LANE:(g + 1) * LANE]
            s = _dot_nt(qg_ref[g], kt) + drop
            _softmax_step(s, vt, m_ref, l_ref, acc_ref, g)
        return 0

    lax.fori_loop(0, nk, att_tile, 0)
    for h in range(DSA_HEADS):
        g, r = divmod(h, DSA_GROUP)
        rows = slice(r * bq, (r + 1) * bq)
        o_ref[0, :, h * LANE:(h + 1) * LANE] = (acc_ref[g, rows, :] / l_ref[g, rows, :]).astype(o_ref.dtype)


def dsa_attention(iq, iqs, iw, dq, ik, dk, dv, *, lq, sp, bq, bk, q_off, topk):
    b = dq.shape[0]
    nk_max = sp // bk
    qrow = lambda w: pl.BlockSpec((1, bq, w), lambda b_, i: (b_, i, 0))
    krow = lambda w: pl.BlockSpec((1, sp, w), lambda b_, i: (b_, 0, 0))
    gr = DSA_GROUP * bq
    return pl.pallas_call(
        functools.partial(_dsa_kernel, bq=bq, bk=bk, q_off=q_off, nk_max=nk_max, topk=topk),
        grid=(b, lq // bq),
        in_specs=[qrow(IDX_W), qrow(IDX_W), qrow(LANE), qrow(DSA_W), krow(LANE), krow(DSA_KV_W), krow(DSA_KV_W)],
        out_specs=qrow(DSA_W),
        out_shape=jax.ShapeDtypeStruct((b, lq, DSA_W), jnp.bfloat16),
        scratch_shapes=[pltpu.VMEM((nk_max, bq, bk), jnp.float32),
                        pltpu.VMEM((DSA_KV_HEADS, gr, LANE), jnp.bfloat16),
                        pltpu.VMEM((DSA_KV_HEADS, gr, 1), jnp.float32),
                        pltpu.VMEM((DSA_KV_HEADS, gr, 1), jnp.float32),
                        pltpu.VMEM((DSA_KV_HEADS, gr, LANE), jnp.float32)],
        compiler_params=_cparams(("parallel", "arbitrary")),
        name="dsa_attention",
    )(iq, iqs, iw, dq, ik, dk, dv)


def _split3(x):
    hi = x.astype(jnp.bfloat16).astype(jnp.float32)
    r = x - hi
    mid = r.astype(jnp.bfloat16).astype(jnp.float32)
    lo = (r - mid).astype(jnp.bfloat16).astype(jnp.float32)
    return hi, mid, lo


def _fox_pack_kernel(q_ref, k_ref, c_ref, qa_o, ka_o):
    rows = q_ref.shape[0]
    lane = lax.broadcasted_iota(jnp.int32, (rows, LANE), 1)
    c = c_ref[...]
    zero = jnp.zeros((rows, LANE), jnp.float32)
    for h in range(FOX_HEADS):
        ch = jnp.sum(jnp.where(lane == h, c, 0.0), axis=1, keepdims=True)
        hi, mid, lo = _split3(ch)
        terms = jnp.where(lane == 0, hi, jnp.where(lane == 1, mid, jnp.where(lane == 2, lo, zero)))
        ones_hi = jnp.where((lane >= 3) & (lane < 6), 1.0, 0.0)
        qa_o[:, h * FOX_QP:h * FOX_QP + LANE] = q_ref[:, h * LANE:(h + 1) * LANE]
        qa_o[:, h * FOX_QP + LANE:(h + 1) * FOX_QP] = (terms + ones_hi).astype(jnp.bfloat16)
        ones_lo = jnp.where(lane < 3, 1.0, 0.0)
        ka_o[:, h * FOX_QP:h * FOX_QP + LANE] = k_ref[:, h * LANE:(h + 1) * LANE]
        ka_o[:, h * FOX_QP + LANE:(h + 1) * FOX_QP] = (ones_lo - pltpu.roll(terms, 3, 1)).astype(jnp.bfloat16)


def fox_pack(q, k, c, *, rows, bm):
    row = lambda w: pl.BlockSpec((bm, w), lambda i: (i, 0))
    return pl.pallas_call(
        _fox_pack_kernel,
        grid=(rows // bm,),
        in_specs=[row(FOX_W), row(FOX_W), row(LANE)],
        out_specs=[row(FOX_HEADS * FOX_QP), row(FOX_HEADS * FOX_QP)],
        out_shape=[jax.ShapeDtypeStruct((rows, FOX_HEADS * FOX_QP), jnp.bfloat16)] * 2,
        compiler_params=_cparams(("parallel",)),
        name="fox_pack",
    )(q, k, c)


def _logits_stage(slot, sts, s_ref, cm_ref):
    for t, st in enumerate(sts):
        s_ref[slot, t] = st
        cm_ref[slot, t] = jnp.max(st, axis=0, keepdims=True)


def _values_stage(slot, vtts, s_ref, cm_ref, m_ref, acc_ref):
    ps, alphas = [], []
    for t in range(len(vtts)):
        m_prev = m_ref[t]
        m_new = jnp.maximum(m_prev, cm_ref[slot, t])
        alphas.append(jnp.exp2(m_prev - m_new))
        ps.append(jnp.exp2(s_ref[slot, t] - m_new).astype(jnp.bfloat16))
        m_ref[t] = m_new
    for t, (p, alpha) in enumerate(zip(ps, alphas)):
        acc_ref[t] = alpha * acc_ref[t] + _dot(vtts[t], p)


def _pipelined_tiles(nk, nk_max, logits, values):
    last = nk_max - 1
    logits(0, 0)

    def pair(jj, _):
        j = 2 * jj
        logits(jnp.minimum(j + 1, last), 1)
        values(j, 0)
        logits(jnp.minimum(j + 2, last), 0)
        values(j + 1, 1)
        return 0

    lax.fori_loop(0, lax.shift_right_logical(nk, 1), pair, 0)

    @pl.when((nk & 1) == 1)
    def _():
        values(nk - 1, 0)


def _finish_t(acc):
    return (acc[0:HEAD_DIM, :] / acc[HEAD_DIM:HEAD_DIM + 1, :]).T


def _flash_t_kernel(q_ref, k_ref, vt_ref, o_ref, m_ref, acc_ref, s_ref, cm_ref, d_ref,
                    *, bq, bk, nk_max, frame_causal, hg, dqk):
    i = pl.program_id(2)
    first = i * bq
    nk = _num_key_tiles(i, bq, bk, 0, nk_max)
    m_ref[...] = jnp.full(m_ref.shape, NEG_INF, jnp.float32)
    acc_ref[...] = jnp.zeros(acc_ref.shape, jnp.float32)
    r = lax.broadcasted_iota(jnp.int32, (bk, bq), 0)
    c = lax.broadcasted_iota(jnp.int32, (bk, bq), 1)
    d_ref[...] = (r - c) if frame_causal else ((r >> 6) - (c >> 6))

    def logits(j, slot):
        start = pl.multiple_of(j * bk, bk)
        gap = first - j * bk
        visible = gap if frame_causal else (gap >> 6)
        mask = d_ref[...] <= visible
        sts = []
        for t in range(hg):
            kt = k_ref[0, pl.ds(start, bk), t * dqk:(t + 1) * dqk]
            st = _dot_nt(kt, q_ref[0, :, t * dqk:(t + 1) * dqk])
            sts.append(jnp.where(mask, st, NEG_INF))
        _logits_stage(slot, sts, s_ref, cm_ref)

    def values(j, slot):
        _values_stage(slot, [vt_ref[0, j, t * VR:(t + 1) * VR, :] for t in range(hg)], s_ref, cm_ref, m_ref, acc_ref)

    _pipelined_tiles(nk, nk_max, logits, values)
    for t in range(hg):
        o_ref[0, :, t * HEAD_DIM:(t + 1) * HEAD_DIM] = _finish_t(acc_ref[t]).astype(o_ref.dtype)


def flash_attention_t(q, k, vt, *, lq, sp, heads, hg, dqk, bq, bk, frame_causal, name):
    nk_max = sp // bk
    return pl.pallas_call(
        functools.partial(_flash_t_kernel, bq=bq, bk=bk, nk_max=nk_max, frame_causal=frame_causal, hg=hg, dqk=dqk),
        grid=(1, heads // hg, lq // bq),
        in_specs=[pl.BlockSpec((1, bq, hg * dqk), lambda b_, g, i: (b_, i, g)),
                  _resident((1, sp, hg * dqk), lambda b_, g, i: (b_, 0, g)),
                  _resident((1, nk_max, hg * VR, bk), lambda b_, g, i: (b_, 0, g, 0))],
        out_specs=pl.BlockSpec((1, bq, hg * HEAD_DIM), lambda b_, g, i: (b_, i, g)),
        out_shape=jax.ShapeDtypeStruct((1, lq, heads * HEAD_DIM), jnp.bfloat16),
        scratch_shapes=[pltpu.VMEM((hg, 1, bq), jnp.float32), pltpu.VMEM((hg, VR, bq), jnp.float32),
                        pltpu.VMEM((2, hg, bk, bq), jnp.float32), pltpu.VMEM((2, hg, 1, bq), jnp.float32),
                        pltpu.VMEM((bk, bq), jnp.int32)],
        compiler_params=_cparams(("parallel", "parallel", "arbitrary")),
        name=name,
    )(q, k, vt)


def _dsa_t_kernel(iq_ref, iqs_ref, iwt_ref, dq_ref, ik_ref, dk_ref, dvt_ref, o_ref,
                  sc_ref, m_ref, acc_ref, s_ref, cm_ref, *, bq, bk, nk_max, topk):
    i = pl.program_id(1)
    nk = _num_key_tiles(i, bq, bk, 0, nk_max)
    qpos = i * bq + lax.broadcasted_iota(jnp.int32, (1, bq), 1)
    klim = ((qpos >> 6) + 1) << 6

    fold = lambda a, op: op(a.reshape(bk // _FOLD, _FOLD, bq), axis=0)

    def score_tile(j, carry):
        hi, lo = carry
        kt = ik_ref[0, pl.ds(pl.multiple_of(j * bk, bk), bk), :]
        acc = jnp.zeros((bk, bq), jnp.float32)
        for g in range(IDX_HEADS // 2):
            se = _dot_nt(kt, iq_ref[0, :, g * LANE:(g + 1) * LANE])
            so = _dot_nt(kt, iqs_ref[0, :, g * LANE:(g + 1) * LANE])
            acc = acc + iwt_ref[2 * g:2 * g + 1, :] * jnp.maximum(se, 0.0)
            acc = acc + iwt_ref[2 * g + 1:2 * g + 2, :] * jnp.maximum(so, 0.0)
        adm = (j * bk + lax.broadcasted_iota(jnp.int32, (bk, bq), 0)) < klim
        sc = jnp.where(adm, acc, -jnp.inf)
        sc_ref[j] = sc
        return (jnp.maximum(hi, fold(sc, jnp.max)), jnp.minimum(lo, fold(jnp.where(adm, acc, jnp.inf), jnp.min)))

    hi, lo = lax.fori_loop(0, nk, score_tile, (jnp.full((_FOLD, bq), -jnp.inf, jnp.float32),
                                               jnp.full((_FOLD, bq), jnp.inf, jnp.float32)))
    hi = jnp.max(hi, axis=0, keepdims=True)
    lo = jnp.min(lo, axis=0, keepdims=True)

    def count_ge(pivot):
        def body(j, acc):
            for r in range(bk // _FOLD):
                acc = acc + jnp.where(sc_ref[j, r * _FOLD:(r + 1) * _FOLD, :] >= pivot, 1.0, 0.0)
            return acc
        acc = lax.fori_loop(0, nk, body, jnp.zeros((_FOLD, bq), jnp.float32))
        return jnp.sum(acc, axis=0, keepdims=True)

    kf = jnp.float32(topk)
    hi = _key_to_f32(_f32_to_key(hi) + 1)
    c_lo = klim.astype(jnp.float32)
    c_hi = jnp.zeros((1, bq), jnp.float32)

    def settled(lo, hi, c_lo):
        width = lax.bitcast_convert_type(_f32_to_key(hi) - _f32_to_key(lo), jnp.uint32)
        return (c_lo <= kf) | (width <= 1)

    def search_cond(state):
        it, n_open = state[0], state[1]
        return (n_open > 0) & (it < _MAX_PROBES)

    def search_step(state):
        it, _, lo, hi, c_lo, c_hi = state
        done = settled(lo, hi, c_lo)
        k_lo, k_hi = _f32_to_key(lo), _f32_to_key(hi)
        width = lax.bitcast_convert_type(k_hi - k_lo, jnp.uint32)
        k_mid = k_lo + lax.bitcast_convert_type(width >> 1, jnp.int32)
        k_val = jnp.clip(_f32_to_key(lo + (hi - lo) * 0.5), k_lo + 1, k_hi - 1)
        probe = _key_to_f32(jnp.where((it & 7) == 7, k_mid, k_val))
        probe = jnp.where(done, lo, probe)
        c = count_ge(probe)
        up = (c >= kf) & ~done
        down = (c < kf) & ~done
        lo, c_lo = jnp.where(up, probe, lo), jnp.where(up, c, c_lo)
        hi, c_hi = jnp.where(down, probe, hi), jnp.where(down, c, c_hi)
        n_open = jnp.sum(jnp.where(settled(lo, hi, c_lo), 0, 1))
        return it + 1, n_open, lo, hi, c_lo, c_hi

    state = (jnp.int32(0), jnp.sum(jnp.where(settled(lo, hi, c_lo), 0, 1)), lo, hi, c_lo, c_hi)
    _, _, thr, _, c_lo, c_hi = lax.while_loop(search_cond, search_step, state)

    tie = c_lo > kf
    total = nk_max * bk

    @pl.when(jnp.sum(jnp.where(tie, 1, 0)) > 0)
    def _():
        need = kf - c_hi
        rows = lax.broadcasted_iota(jnp.int32, (bk, bq), 0)

        def count_equal_upto(j_max):
            def body(j, acc):
                hit = jnp.where(sc_ref[j] == thr, jnp.where(rows <= j_max - j * bk, 1.0, 0.0), 0.0)
                return acc + fold(hit, jnp.sum)
            acc = lax.fori_loop(0, nk, body, jnp.zeros((_FOLD, bq), jnp.float32))
            return jnp.sum(acc, axis=0, keepdims=True)

        def index_step(_, bracket):
            j_lo, j_hi = bracket
            mid = (j_lo + j_hi) >> 1
            ok = count_equal_upto(mid) >= need
            return jnp.where(ok, j_lo, mid), jnp.where(ok, mid, j_hi)

        bracket = (jnp.full((1, bq), -1, jnp.int32), jnp.full((1, bq), total - 1, jnp.int32))
        _, j_cut = lax.fori_loop(0, total.bit_length() + 1, index_step, bracket)
        j_cut = jnp.where(tie, j_cut, total)

        def drop_late_ties(j, _):
            t = sc_ref[j]
            sc_ref[j] = jnp.where(t == thr, jnp.where(rows > j_cut - j * bk, -jnp.inf, t), t)
            return 0

        lax.fori_loop(0, nk, drop_late_ties, 0)

    m_ref[...] = jnp.full(m_ref.shape, NEG_INF, jnp.float32)
    acc_ref[...] = jnp.zeros(acc_ref.shape, jnp.float32)

    def logits(j, slot):
        start = pl.multiple_of(j * bk, bk)
        drop = jnp.where(sc_ref[j] >= thr, 0.0, NEG_INF)
        sts = []
        for h in range(DSA_HEADS):
            g = h // DSA_GROUP
            kt = dk_ref[0, pl.ds(start, bk), g * LANE:(g + 1) * LANE]
            sts.append(_dot_nt(kt, dq_ref[0, :, h * LANE:(h + 1) * LANE]) + drop)
        _logits_stage(slot, sts, s_ref, cm_ref)

    def values(j, slot):
        vtts = [dvt_ref[0, j, (h // DSA_GROUP) * VR:(h // DSA_GROUP + 1) * VR, :] for h in range(DSA_HEADS)]
        _values_stage(slot, vtts, s_ref, cm_ref, m_ref, acc_ref)

    _pipelined_tiles(nk, nk_max, logits, values)
    for h in range(DSA_HEADS):
        o_ref[0, :, h * LANE:(h + 1) * LANE] = _finish_t(acc_ref[h]).astype(o_ref.dtype)


def dsa_attention_t(iq, iqs, iwt, dq, ik, dk, dvt, *, lq, sp, bq, bk, topk):
    nk_max = sp // bk
    qrow = lambda w: pl.BlockSpec((1, bq, w), lambda b_, i: (b_, i, 0))
    krow = lambda w: _resident((1, sp, w), lambda b_, i: (b_, 0, 0))
    return pl.pallas_call(
        functools.partial(_dsa_t_kernel, bq=bq, bk=bk, nk_max=nk_max, topk=topk),
        grid=(1, lq // bq),
        in_specs=[qrow(IDX_W), qrow(IDX_W), pl.BlockSpec((LANE, bq), lambda b_, i: (0, i)), qrow(DSA_W),
                  krow(LANE), krow(DSA_KV_W),
                  _resident((1, nk_max, DSA_KV_HEADS * VR, bk), lambda b_, i: (b_, 0, 0, 0))],
        out_specs=qrow(DSA_W),
        out_shape=jax.ShapeDtypeStruct((1, lq, DSA_W), jnp.bfloat16),
        scratch_shapes=[pltpu.VMEM((nk_max, bk, bq), jnp.float32),
                        pltpu.VMEM((DSA_HEADS, 1, bq), jnp.float32),
                        pltpu.VMEM((DSA_HEADS, VR, bq), jnp.float32),
                        pltpu.VMEM((2, DSA_HEADS, bk, bq), jnp.float32),
                        pltpu.VMEM((2, DSA_HEADS, 1, bq), jnp.float32)],
        compiler_params=_cparams(("parallel", "arbitrary")),
        name="dsa_attention_t",
    )(iq, iqs, iwt, dq, ik, dk, dvt)


def _out_proj_kernel(x_ref, a_ref, b_ref, c_ref, wa_ref, wb_ref, wc_ref, o_ref):
    o_ref[...] = (x_ref[...] + _dot(a_ref[...], wa_ref[...]) + _dot(b_ref[...], wb_ref[...])
                  + _dot(c_ref[...], wc_ref[...]))


def out_proj(x, a, b, c, wa, wb, wc, *, bm):
    m, d = x.shape
    row = lambda w: pl.BlockSpec((bm, w), lambda i: (i, 0))
    full = lambda w: _resident(w.shape, lambda i: (0, 0))
    return pl.pallas_call(
        _out_proj_kernel,
        grid=(m // bm,),
        in_specs=[row(d), row(a.shape[1]), row(b.shape[1]), row(c.shape[1]), full(wa), full(wb), full(wc)],
        out_specs=row(d),
        out_shape=jax.ShapeDtypeStruct((m, d), jnp.float32),
        compiler_params=_cparams(("parallel",)),
        name="out_proj",
    )(x, a, b, c, wa, wb, wc)


def _ffn_kernel(x_ref, g_ref, wg_ref, wu_ref, wd_ref, o_ref, xn_ref, acc_ref):
    j = pl.program_id(1)

    @pl.when(j == 0)
    def _():
        xn_ref[...] = _rms(x_ref[...], g_ref[...]).astype(jnp.bfloat16)
        acc_ref[...] = jnp.zeros_like(acc_ref)

    xn = xn_ref[...]
    gate = _dot(xn, wg_ref[...])
    up = _dot(xn, wu_ref[...])
    hidden = (gate * (1.0 / (1.0 + jnp.exp(-gate))) * up).astype(jnp.bfloat16)
    acc_ref[...] += _dot(hidden, wd_ref[...])

    @pl.when(j == pl.num_programs(1) - 1)
    def _():
        o_ref[...] = x_ref[...] + acc_ref[...]


def ffn(x, g, wg, wu, wd, *, bm, bf):
    m, d = x.shape
    f = wg.shape[1]
    return pl.pallas_call(
        _ffn_kernel,
        grid=(m // bm, f // bf),
        in_specs=[pl.BlockSpec((bm, d), lambda i, j: (i, 0)),
                  pl.BlockSpec((1, d), lambda i, j: (0, 0)),
                  pl.BlockSpec((d, bf), lambda i, j: (0, j)),
                  pl.BlockSpec((d, bf), lambda i, j: (0, j)),
                  pl.BlockSpec((bf, d), lambda i, j: (j, 0))],
        out_specs=pl.BlockSpec((bm, d), lambda i, j: (i, 0)),
        out_shape=jax.ShapeDtypeStruct((m, d), jnp.float32),
        scratch_shapes=[pltpu.VMEM((bm, d), jnp.bfloat16), pltpu.VMEM((bm, d), jnp.float32)],
        compiler_params=_cparams(("parallel", "arbitrary")),
        name="swiglu",
    )(x, g.reshape(1, d), wg, wu, wd)


def _final_norm_kernel(x_ref, g_ref, o_ref):
    o_ref[...] = _rms(x_ref[...], g_ref[...])


def final_norm(x, g, *, bm):
    m, d = x.shape
    return pl.pallas_call(
        _final_norm_kernel,
        grid=(m // bm,),
        in_specs=[pl.BlockSpec((bm, d), lambda i: (i, 0)), pl.BlockSpec((1, d), lambda i: (0, 0))],
        out_specs=pl.BlockSpec((bm, d), lambda i: (i, 0)),
        out_shape=jax.ShapeDtypeStruct((m, d), jnp.float32),
        compiler_params=_cparams(("parallel",)),
        name="final_norm",
    )(x, g.reshape(1, d))


def _pad_cols(a, width):
    return jnp.pad(a, [(0, 0)] * (a.ndim - 1) + [(0, width - a.shape[-1])])


def _layout_w_in(w_in):
    sizes = (FOX_W, FOX_W, FOX_W, FOX_HEADS, DSA_W, DSA_KV_W, DSA_KV_W, IDX_W, IDX_DIM, IDX_HEADS,
             MLA_Q_LORA, MLA_KV_LORA, MLA_ROPE)
    splits = np.cumsum(sizes)[:-1]
    q_a, k_a, v_a, f_a, q_b, k_b, v_b, q_i, k_i, w_i, c_q, c_kv, k_r = jnp.split(w_in, splits, axis=-1)
    parts = {"fq": q_a, "fk": k_a, "fv": v_a, "dq": q_b, "dk": k_b, "dv": v_b, "iq": q_i, "cq": c_q,
             "ckv": c_kv, "fa": f_a, "ik": k_i, "iw": w_i, "kr": k_r}
    cols = [_pad_cols(parts[n], w) for n, w in _SEGS]
    return _pad_cols(jnp.concatenate(cols, axis=-1), PROJ_W).astype(jnp.bfloat16)


def _layout_w_uq(w_uq):
    dp = w_uq.shape[0]
    w = w_uq.reshape(dp, MLA_Q_LORA, MLA_HEADS, MLA_NOPE + MLA_ROPE)
    return _pad_cols(w, MLA_QP).reshape(dp, MLA_Q_LORA, MLA_QW).astype(jnp.bfloat16)


def _layout_w_ukv(w_ukv):
    dp = w_ukv.shape[0]
    w = w_ukv.reshape(dp, MLA_KV_LORA, MLA_HEADS, MLA_NOPE + MLA_V)
    kn = w[..., :MLA_NOPE].reshape(dp, MLA_KV_LORA, MLA_HEADS * MLA_NOPE)
    vv = w[..., MLA_NOPE:].reshape(dp, MLA_KV_LORA, MLA_VW)
    return jnp.concatenate([kn, vv], axis=-1).astype(jnp.bfloat16)


def _rope_table(pos, rot, period):
    half = rot // 2
    inv_freq = ROPE_THETA ** (-jnp.arange(half, dtype=jnp.float32) / half)
    ang = pos.astype(jnp.float32)[:, None] * inv_freq[None, :]
    cos, sin = jnp.cos(ang), jnp.sin(ang)
    r = pos.shape[0]
    ones = jnp.ones((r, period - rot), jnp.float32)
    zeros = jnp.zeros((r, period - rot), jnp.float32)
    zh = jnp.zeros((r, half), jnp.float32)
    c = jnp.concatenate([cos, cos, ones], axis=1)
    s1 = jnp.concatenate([zh, sin, zeros], axis=1)
    s2 = jnp.concatenate([-sin, zh, zeros], axis=1)
    rep = LANE // period
    return jnp.concatenate([jnp.tile(c, (1, rep)), jnp.tile(s1, (1, rep)), jnp.tile(s2, (1, rep))], axis=1)


def _pick(n, candidates):
    for c in candidates:
        if n % c == 0:
            return c
    raise ValueError(f"no block size among {candidates} divides {n}")


def _pad_seq(a, sp):
    return jnp.pad(a, [(0, 0), (0, sp - a.shape[1])] + [(0, 0)] * (a.ndim - 2))


def kernel(x_prompt, x_sample, cache_fox_k, cache_fox_v, cache_fox_logf, cache_dsa_k, cache_dsa_v, cache_idx_k, cache_mla_ckv, cache_mla_krope, w_in, fox_bias, mla_q_norm, mla_w_uq, mla_kv_norm, mla_w_ukv, w_o, attn_norm, ffn_norm, w_gate, w_up, w_down, final_norm_w):
    bf = jnp.bfloat16
    depth = w_in.shape[0]
    nb, seq, d = x_prompt.shape
    db, dec, _ = x_sample.shape
    past = cache_fox_k.shape[2]
    assert nb == 1 and seq % CHUNK == 0 and dec % CHUNK == 0 and past % CHUNK == 0
    mp, ms = nb * seq, db * dec
    m = mp + ms
    s_len = past + dec

    bm = _pick(m, (512, 256, 128, 64))
    assert mp % bm == 0
    bq_p = _pick(seq, (256, 128))
    bk_p = bm
    bq_s = _pick(dec, (64,))
    bk_s = 256
    sp = ((s_len + bk_s - 1) // bk_s) * bk_s
    topk_p = min(IDX_TOPK_MAX, seq // 4)
    topk_s = min(IDX_TOPK_MAX, s_len // 4)
    fox_hg, mla_hg = 3, 2
    d_ff = w_gate.shape[2]
    bff = _pick(d_ff, (512, 256, 128))

    w_in_p = _layout_w_in(w_in)
    wuq_p = _layout_w_uq(mla_w_uq)
    wukv_p = _layout_w_ukv(mla_w_ukv)
    w_o_b = w_o.astype(bf)
    wg_b, wu_b, wd_b = w_gate.astype(bf), w_up.astype(bf), w_down.astype(bf)
    fox_bias_p = _pad_cols(fox_bias, LANE).reshape(depth, 1, LANE)

    pos = jnp.concatenate([jnp.tile(jnp.arange(seq, dtype=jnp.int32), nb),
                           jnp.tile(past + jnp.arange(dec, dtype=jnp.int32), db)])
    tab = jnp.concatenate([_rope_table(pos, PARTIAL_ROT, HEAD_DIM), _rope_table(pos, IDX_ROT, IDX_DIM),
                           _rope_table(pos, MLA_ROPE, LANE)], axis=1)

    x = jnp.concatenate([x_prompt.reshape(mp, d), x_sample.reshape(ms, d)], axis=0)
    p_rows, s_rows = [], []
    for l in range(depth):
        proj = norm_matmul(x, attn_norm[l], w_in_p[l], bm=bm, bn=_pick(PROJ_W, (1536, 1024, 512)))
        (fq, fk, fv, fkb, fvb, fvt, lf, dq, dk, dv, dkb, dvb, dvt, iq, iqs, ik, ikb, iw, iwt, mq, ckv, kr) = post_proj(
            proj, tab, fox_bias_p[l], mla_q_norm[l].reshape(1, -1), mla_kv_norm[l].reshape(1, -1), wuq_p[l],
            bm=bm // 2, bk=bm)

        pr = lambda a: a[:mp].reshape(nb, seq, -1)
        sr = lambda a: a[mp:].reshape(db, dec, -1)
        p_rows.append((pr(fk).reshape(nb, seq, FOX_HEADS, HEAD_DIM), pr(fv).reshape(nb, seq, FOX_HEADS, HEAD_DIM),
                       pr(lf)[..., :FOX_HEADS], pr(dk).reshape(nb, seq, DSA_KV_HEADS, HEAD_DIM),
                       pr(dv).reshape(nb, seq, DSA_KV_HEADS, HEAD_DIM), pr(ik)[..., :IDX_DIM], pr(ckv),
                       pr(kr)[..., :MLA_ROPE]))
        s_rows.append((sr(fk).reshape(db, dec, FOX_HEADS, HEAD_DIM), sr(fv).reshape(db, dec, FOX_HEADS, HEAD_DIM),
                       sr(lf)[..., :FOX_HEADS], sr(dk).reshape(db, dec, DSA_KV_HEADS, HEAD_DIM),
                       sr(dv).reshape(db, dec, DSA_KV_HEADS, HEAD_DIM), sr(ik)[..., :IDX_DIM], sr(ckv),
                       sr(kr)[..., :MLA_ROPE]))

        cat = lambda c, new: _pad_seq(jnp.concatenate([c, new], axis=1), sp)
        s_fkb = cat(cache_fox_k[l].reshape(db, past, FOX_W).astype(bf), sr(fkb))
        s_fvb = cat(cache_fox_v[l].reshape(db, past, FOX_W).astype(bf), sr(fvb))
        s_lf = cat(_pad_cols(cache_fox_logf[l], LANE), sr(lf))
        s_dkb = cat(cache_dsa_k[l].reshape(db, past, DSA_KV_W).astype(bf), sr(dkb))
        s_dvb = cat(cache_dsa_v[l].reshape(db, past, DSA_KV_W).astype(bf), sr(dvb))
        s_ikb = cat(_pad_cols(cache_idx_k[l], LANE).astype(bf), sr(ikb))
        s_ckv = cat(cache_mla_ckv[l], sr(ckv))
        s_kr = cat(_pad_cols(cache_mla_krope[l], LANE), sr(kr))

        ckv_all = jnp.concatenate([ckv[:mp], s_ckv.reshape(db * sp, -1)], axis=0)
        kr_all = jnp.concatenate([kr[:mp], s_kr.reshape(db * sp, -1)], axis=0)
        kmla, vmla, vtmla = kv_up(ckv_all, kr_all, wukv_p[l], bm=bm)
        s_kmla, s_vmla = kmla[mp:].reshape(db, sp, -1), vmla[mp:].reshape(db, sp, -1)

        p_c, _ = cumsum_rows(pr(lf), bt=_pick(seq, (256, 128)))
        s_c, s_ct = cumsum_rows(s_lf, bt=_pick(sp, (256, 128)))
        ck_form = lambda ct, bk: ct[:, :FOX_HEADS].reshape(ct.shape[0], FOX_HEADS, ct.shape[2] // bk, 1, bk)

        al = lambda a: a.reshape(1, *a.shape)
        fqa, fka = fox_pack(fq, fkb, p_c[0], rows=mp, bm=bm)
        out_a = flash_attention_t(al(fqa), al(fka), al(fvt), lq=seq, sp=seq, heads=FOX_HEADS, hg=fox_hg, dqk=FOX_QP,
                                  bq=bq_p, bk=bk_p, frame_causal=True, name="fox_attention_t")
        out_b = dsa_attention_t(al(iq), al(iqs), iwt, al(dq), al(ikb), al(dkb), al(dvt), lq=seq, sp=seq,
                                bq=bq_p, bk=bk_p, topk=topk_p)
        out_c = flash_attention_t(al(mq), al(kmla), al(vtmla), lq=seq, sp=seq, heads=MLA_HEADS, hg=mla_hg, dqk=MLA_QP,
                                  bq=bq_p, bk=bk_p, frame_causal=False, name="mla_attention_t")
        s_out_a = flash_attention(sr(fq), s_fkb, s_fvb, s_c[:, past:past + dec], ck_form(s_ct, bk_s), lq=dec, sp=sp,
                                  heads=FOX_HEADS, hg=fox_hg, dqk=HEAD_DIM, dv=HEAD_DIM, bq=bq_s, bk=bk_s, q_off=past)
        s_out_b = dsa_attention(sr(iq), sr(iqs), sr(iw), sr(dq), s_ikb, s_dkb, s_dvb, lq=dec, sp=sp,
                                bq=bq_s, bk=bk_s, q_off=past, topk=topk_s)
        s_out_c = flash_attention(sr(mq), s_kmla, s_vmla, lq=dec, sp=sp, heads=MLA_HEADS, hg=mla_hg,
                                  dqk=MLA_QP, dv=MLA_V, bq=bq_s, bk=bk_s, q_off=past)

        rows = lambda p, s: jnp.concatenate([p.reshape(mp, -1), s.reshape(ms, -1)], axis=0)
        x = out_proj(x, rows(out_a, s_out_a), rows(out_b, s_out_b), rows(out_c, s_out_c),
                     w_o_b[l, :FOX_W], w_o_b[l, FOX_W:FOX_W + DSA_W], w_o_b[l, FOX_W + DSA_W:], bm=bm)
        x = ffn(x, ffn_norm[l], wg_b[l], wu_b[l], wd_b[l], bm=bm, bf=bff)

    y = final_norm(x, final_norm_w, bm=bm)
    y_prompt = y[:mp].reshape(nb, seq, d)
    y_sample = y[mp:].reshape(db, dec, d)
    stack = lambda rows_: tuple(jnp.stack(t) for t in zip(*rows_))
    return (y_prompt, y_sample) + stack(p_rows) + stack(s_rows)
```

```python
import functools
import math

import jax
import jax.numpy as jnp
import numpy as np
from jax import lax
from jax.experimental import pallas as pl
from jax.experimental.pallas import tpu as pltpu

CHUNK = 64
HEAD_DIM = 128
FOX_HEADS = 6
DSA_HEADS = 6
DSA_KV_HEADS = 2
DSA_GROUP = DSA_HEADS // DSA_KV_HEADS
IDX_HEADS = 16
IDX_DIM = 64
IDX_TOPK_MAX = 256
MLA_HEADS = 4
MLA_Q_LORA = 512
MLA_KV_LORA = 256
MLA_NOPE = 128
MLA_ROPE = 64
MLA_V = 128
PARTIAL_ROT = HEAD_DIM // 4
IDX_ROT = IDX_DIM // 4
ROPE_THETA = 500000.0
EPS = 1e-6
NEG_INF = -1e30
FOX_SCALE = HEAD_DIM ** -0.5
DSA_SCALE = HEAD_DIM ** -0.5
MLA_SCALE = (MLA_NOPE + MLA_ROPE) ** -0.5
IDX_W_SCALE = (IDX_HEADS * IDX_DIM) ** -0.5
LOG2E = math.log2(math.e)

LANE = 128
VMEM_LIMIT = 56 * 1024 * 1024

FOX_W = FOX_HEADS * HEAD_DIM
DSA_W = DSA_HEADS * HEAD_DIM
DSA_KV_W = DSA_KV_HEADS * HEAD_DIM
IDX_W = IDX_HEADS * IDX_DIM
MLA_QP = 2 * LANE
MLA_QW = MLA_HEADS * MLA_QP
MLA_VW = MLA_HEADS * MLA_V
VR = HEAD_DIM + 16
FOX_QP = 2 * LANE

_SEGS = (("fq", FOX_W), ("fk", FOX_W), ("fv", FOX_W), ("dq", DSA_W), ("dk", DSA_KV_W), ("dv", DSA_KV_W),
         ("iq", IDX_W), ("cq", MLA_Q_LORA), ("ckv", MLA_KV_LORA),
         ("fa", LANE), ("ik", LANE), ("iw", LANE), ("kr", LANE))
_OFF = {}
_o = 0
for _n, _w in _SEGS:
    _OFF[_n] = _o
    _o += _w
PROJ_W = ((_o + 511) // 512) * 512

_NT = (((1,), (1,)), ((), ()))


def _dot(a, b):
    return jnp.dot(a, b, preferred_element_type=jnp.float32)


def _dot_nt(a, b):
    return lax.dot_general(a, b, _NT, preferred_element_type=jnp.float32)


def _cparams(sem):
    return pltpu.CompilerParams(dimension_semantics=sem, vmem_limit_bytes=VMEM_LIMIT)


def _resident(block_shape, index_map):
    return pl.BlockSpec(block_shape, index_map, pipeline_mode=pl.Buffered(1))


def _rms(x, g):
    return x * lax.rsqrt(jnp.mean(x * x, axis=-1, keepdims=True) + EPS) * g


def _norm_matmul_kernel(x_ref, g_ref, w_ref, o_ref, xn_ref):
    @pl.when(pl.program_id(1) == 0)
    def _():
        xn_ref[...] = _rms(x_ref[...], g_ref[...]).astype(jnp.bfloat16)

    o_ref[...] = _dot(xn_ref[...], w_ref[...])


def norm_matmul(x, g, w, *, bm, bn):
    m, d = x.shape
    n = w.shape[1]
    return pl.pallas_call(
        _norm_matmul_kernel,
        grid=(m // bm, n // bn),
        in_specs=[pl.BlockSpec((bm, d), lambda i, j: (i, 0)),
                  pl.BlockSpec((1, d), lambda i, j: (0, 0)),
                  pl.BlockSpec((d, bn), lambda i, j: (0, j))],
        out_specs=pl.BlockSpec((bm, bn), lambda i, j: (i, j)),
        out_shape=jax.ShapeDtypeStruct((m, n), jnp.float32),
        scratch_shapes=[pltpu.VMEM((bm, d), jnp.bfloat16)],
        compiler_params=_cparams(("parallel", "arbitrary")),
        name="norm_matmul",
    )(x, g.reshape(1, d), w)


def _rope(x, tab, half):
    c, s1, s2 = tab[:, 0:LANE], tab[:, LANE:2 * LANE], tab[:, 2 * LANE:3 * LANE]
    return x * c + pltpu.roll(x, half, 1) * s1 + pltpu.roll(x, LANE - half, 1) * s2


def _value_t_tail(width):
    r = lax.broadcasted_iota(jnp.int32, (16, width), 0)
    return jnp.where(r == 0, 1.0, 0.0).astype(jnp.bfloat16)


def _store_value_t(o_ref, h, v):
    o_ref[0, h * VR:h * VR + HEAD_DIM, :] = v.T.astype(jnp.bfloat16)
    o_ref[0, h * VR + HEAD_DIM:(h + 1) * VR, :] = _value_t_tail(v.shape[0])


def _post_kernel(p_ref, tab_ref, fb_ref, qn_ref, kvn_ref, wuq_ref,
                 fq_o, fk_o, fv_o, fkb_o, fvb_o, fvt_o, lf_o,
                 dq_o, dk_o, dv_o, dkb_o, dvb_o, dvt_o,
                 iq_o, iqs_o, ik_o, ikb_o, iw_o, iwt_o,
                 mq_o, ckv_o, kr_o):
    bf = jnp.bfloat16
    seg = lambda name, w: p_ref[:, _OFF[name]:_OFF[name] + w]
    tab_d = tab_ref[:, 0:3 * LANE]
    tab_i = tab_ref[:, 3 * LANE:6 * LANE]
    tab_m = tab_ref[:, 6 * LANE:9 * LANE]

    fq_o[...] = (seg("fq", FOX_W) * (FOX_SCALE * LOG2E)).astype(bf)
    fk = seg("fk", FOX_W)
    fk_o[...] = fk
    fkb_o[...] = fk.astype(bf)
    fv = seg("fv", FOX_W)
    fv_o[...] = fv
    fvb_o[...] = fv.astype(bf)
    for h in range(FOX_HEADS):
        _store_value_t(fvt_o, h, p_ref[:, _OFF["fv"] + h * LANE:_OFF["fv"] + (h + 1) * LANE])
    z = seg("fa", LANE) + fb_ref[...]
    lf_o[...] = jnp.minimum(z, 0.0) - jnp.log1p(jnp.exp(-jnp.abs(z)))

    for h in range(DSA_HEADS):
        x = p_ref[:, _OFF["dq"] + h * LANE:_OFF["dq"] + (h + 1) * LANE]
        dq_o[:, h * LANE:(h + 1) * LANE] = (_rope(x, tab_d, PARTIAL_ROT // 2) * (DSA_SCALE * LOG2E)).astype(bf)
    for h in range(DSA_KV_HEADS):
        x = p_ref[:, _OFF["dk"] + h * LANE:_OFF["dk"] + (h + 1) * LANE]
        y = _rope(x, tab_d, PARTIAL_ROT // 2)
        dk_o[:, h * LANE:(h + 1) * LANE] = y
        dkb_o[:, h * LANE:(h + 1) * LANE] = y.astype(bf)
        _store_value_t(dvt_o, h, p_ref[:, _OFF["dv"] + h * LANE:_OFF["dv"] + (h + 1) * LANE])
    dv = seg("dv", DSA_KV_W)
    dv_o[...] = dv
    dvb_o[...] = dv.astype(bf)
    for j in range(IDX_W // LANE):
        x = p_ref[:, _OFF["iq"] + j * LANE:_OFF["iq"] + (j + 1) * LANE]
        y = _rope(x, tab_i, IDX_ROT // 2)
        iq_o[:, j * LANE:(j + 1) * LANE] = y.astype(bf)
        iqs_o[:, j * LANE:(j + 1) * LANE] = pltpu.roll(y, IDX_DIM, 1).astype(bf)
    y = _rope(seg("ik", LANE), tab_i, IDX_ROT // 2)
    ik_o[...] = y
    ikb_o[...] = y.astype(bf)
    iw = seg("iw", LANE) * IDX_W_SCALE
    iw_o[...] = iw
    iwt_o[...] = iw.T

    cqn = _rms(seg("cq", MLA_Q_LORA), qn_ref[...]).astype(bf)
    mq = _dot(cqn, wuq_ref[...])
    for h in range(MLA_HEADS):
        a = h * MLA_QP
        mq_o[:, a:a + LANE] = (mq[:, a:a + LANE] * (MLA_SCALE * LOG2E)).astype(bf)
        r = _rope(mq[:, a + LANE:a + 2 * LANE], tab_m, MLA_ROPE // 2)
        mq_o[:, a + LANE:a + 2 * LANE] = (r * (MLA_SCALE * LOG2E)).astype(bf)
    ckv_o[...] = _rms(seg("ckv", MLA_KV_LORA), kvn_ref[...])
    kr_o[...] = _rope(seg("kr", LANE), tab_m, MLA_ROPE // 2)


def post_proj(proj, tab, fox_bias_p, q_norm, kv_norm, wuq_p, *, bm, bk):
    m = proj.shape[0]
    r = bk // bm
    f32, bf = jnp.float32, jnp.bfloat16
    row = lambda w: pl.BlockSpec((bm, w), lambda i: (i, 0))
    full = lambda a: pl.BlockSpec(a.shape, lambda i: (0,) * a.ndim)
    vt = lambda heads: pl.BlockSpec((1, heads * VR, bm), lambda i: (i // r, 0, i % r))
    rowo = lambda w, dt: (row(w), jax.ShapeDtypeStruct((m, w), dt))
    vto = lambda heads: (vt(heads), jax.ShapeDtypeStruct((m // bk, heads * VR, bk), bf))
    outs = [rowo(FOX_W, bf), rowo(FOX_W, f32), rowo(FOX_W, f32), rowo(FOX_W, bf), rowo(FOX_W, bf), vto(FOX_HEADS),
            rowo(LANE, f32),
            rowo(DSA_W, bf), rowo(DSA_KV_W, f32), rowo(DSA_KV_W, f32), rowo(DSA_KV_W, bf), rowo(DSA_KV_W, bf),
            vto(DSA_KV_HEADS),
            rowo(IDX_W, bf), rowo(IDX_W, bf), rowo(LANE, f32), rowo(LANE, bf), rowo(LANE, f32),
            (pl.BlockSpec((LANE, bm), lambda i: (0, i)), jax.ShapeDtypeStruct((LANE, m), f32)),
            rowo(MLA_QW, bf), rowo(MLA_KV_LORA, f32), rowo(LANE, f32)]
    return pl.pallas_call(
        _post_kernel,
        grid=(m // bm,),
        in_specs=[row(PROJ_W), row(9 * LANE), full(fox_bias_p), full(q_norm), full(kv_norm), full(wuq_p)],
        out_specs=[s for s, _ in outs],
        out_shape=[o for _, o in outs],
        compiler_params=_cparams(("parallel",)),
        name="post_proj",
    )(proj, tab, fox_bias_p, q_norm, kv_norm, wuq_p)


def _kv_up_kernel(ckv_ref, kr_ref, w_ref, k_o, v_o, vt_o):
    bf = jnp.bfloat16
    kv = _dot(ckv_ref[...].astype(bf), w_ref[...])
    kr = kr_ref[...].astype(bf)
    for h in range(MLA_HEADS):
        k_o[:, h * MLA_QP:h * MLA_QP + LANE] = kv[:, h * LANE:(h + 1) * LANE].astype(bf)
        k_o[:, h * MLA_QP + LANE:(h + 1) * MLA_QP] = kr
        _store_value_t(vt_o, h, kv[:, (MLA_HEADS + h) * LANE:(MLA_HEADS + h + 1) * LANE])
    v_o[...] = kv[:, MLA_HEADS * LANE:].astype(bf)


def kv_up(ckv, kr, w, *, bm):
    r = ckv.shape[0]
    row = lambda w_: pl.BlockSpec((bm, w_), lambda i: (i, 0))
    return pl.pallas_call(
        _kv_up_kernel,
        grid=(r // bm,),
        in_specs=[row(MLA_KV_LORA), row(LANE), pl.BlockSpec(w.shape, lambda i: (0, 0))],
        out_specs=[row(MLA_QW), row(MLA_VW), pl.BlockSpec((1, MLA_HEADS * VR, bm), lambda i: (i, 0, 0))],
        out_shape=[jax.ShapeDtypeStruct((r, MLA_QW), jnp.bfloat16),
                   jax.ShapeDtypeStruct((r, MLA_VW), jnp.bfloat16),
                   jax.ShapeDtypeStruct((r // bm, MLA_HEADS * VR, bm), jnp.bfloat16)],
        compiler_params=_cparams(("parallel",)),
        name="mla_kv_up",
    )(ckv, kr, w)


def _cumsum_kernel(x_ref, c_o, ct_o, carry_ref, carry_t_ref):
    t = x_ref.shape[1]

    @pl.when(pl.program_id(1) == 0)
    def _():
        carry_ref[...] = jnp.zeros_like(carry_ref)
        carry_t_ref[...] = jnp.zeros_like(carry_t_ref)

    x = x_ref[0]
    r = lax.broadcasted_iota(jnp.int32, (t, t), 0)
    c = lax.broadcasted_iota(jnp.int32, (t, t), 1)
    lower = jnp.where(c <= r, 1.0, 0.0).astype(jnp.float32)
    upper = jnp.where(r <= c, 1.0, 0.0).astype(jnp.float32)
    cs = jnp.dot(lower, x, preferred_element_type=jnp.float32, precision=lax.Precision.HIGHEST)
    cs = cs + carry_ref[...]
    c_o[0] = cs * LOG2E
    carry_ref[...] = cs[t - 1:t, :]
    cst = jnp.dot(x.T, upper, preferred_element_type=jnp.float32, precision=lax.Precision.HIGHEST)
    cst = cst + carry_t_ref[...]
    ct_o[0] = cst[0:8, :] * LOG2E
    carry_t_ref[...] = cst[:, t - 1:t]


def cumsum_rows(x, *, bt):
    b, s, _ = x.shape
    return pl.pallas_call(
        _cumsum_kernel,
        grid=(b, s // bt),
        in_specs=[pl.BlockSpec((1, bt, LANE), lambda i, j: (i, j, 0))],
        out_specs=[pl.BlockSpec((1, bt, LANE), lambda i, j: (i, j, 0)),
                   pl.BlockSpec((1, 8, bt), lambda i, j: (i, 0, j))],
        out_shape=[jax.ShapeDtypeStruct((b, s, LANE), jnp.float32),
                   jax.ShapeDtypeStruct((b, 8, s), jnp.float32)],
        scratch_shapes=[pltpu.VMEM((1, LANE), jnp.float32), pltpu.VMEM((LANE, 1), jnp.float32)],
        compiler_params=_cparams(("parallel", "arbitrary")),
        name="logf_cumsum",
    )(x)


def _num_key_tiles(i, bq, bk, q_off, nk_max):
    last = q_off + (i + 1) * bq
    return jnp.minimum((last + bk - 1) // bk, nk_max)


def _softmax_steps(ss, vts, m_ref, l_ref, acc_ref):
    ps, alphas = [], []
    for t, s in enumerate(ss):
        m_prev = m_ref[t]
        m_new = jnp.maximum(m_prev, jnp.max(s, axis=1, keepdims=True))
        alpha = jnp.exp2(m_prev - m_new)
        p = jnp.exp2(s - m_new)
        l_ref[t] = alpha * l_ref[t] + jnp.sum(p, axis=1, keepdims=True)
        m_ref[t] = m_new
        ps.append(p.astype(jnp.bfloat16))
        alphas.append(alpha)
    for t, (p, alpha) in enumerate(zip(ps, alphas)):
        acc_ref[t] = alpha * acc_ref[t] + _dot(p, vts[t])


def _flash_kernel(*refs, bq, bk, q_off, nk_max, fox, hg, dqk, dv):
    if fox:
        q_ref, k_ref, v_ref, cq_ref, ck_ref, o_ref, m_ref, l_ref, acc_ref = refs
    else:
        q_ref, k_ref, v_ref, o_ref, m_ref, l_ref, acc_ref = refs
    gi = pl.program_id(1)
    i = pl.program_id(2)
    first = q_off + i * bq
    qpos = first + lax.broadcasted_iota(jnp.int32, (bq, 1), 0)
    if fox:
        lane = lax.broadcasted_iota(jnp.int32, (bq, LANE), 1)
        cq_all = cq_ref[0]
        cqs = [jnp.sum(jnp.where(lane == gi * hg + t, cq_all, 0.0), axis=1, keepdims=True) for t in range(hg)]
        c0 = [c[0:1, :] for c in cqs]
        bias_q = [c - z for c, z in zip(cqs, c0)]
    nk = _num_key_tiles(i, bq, bk, q_off, nk_max)
    n_full = jnp.minimum((first + (1 if fox else CHUNK)) // bk, nk)

    m_ref[...] = jnp.full(m_ref.shape, NEG_INF, jnp.float32)
    l_ref[...] = jnp.zeros(l_ref.shape, jnp.float32)
    acc_ref[...] = jnp.zeros(acc_ref.shape, jnp.float32)

    def make_body(masked):
        def body(j, _):
            start = pl.multiple_of(j * bk, bk)
            if masked:
                kpos = j * bk + lax.broadcasted_iota(jnp.int32, (1, bk), 1)
                mask = (kpos <= qpos) if fox else ((kpos >> 6) <= (qpos >> 6))
            ss = []
            for t in range(hg):
                kt = k_ref[0, pl.ds(start, bk), t * dqk:(t + 1) * dqk]
                s = _dot_nt(q_ref[0, :, t * dqk:(t + 1) * dqk], kt)
                if fox:
                    s = s + (bias_q[t] - (ck_ref[0, t, j] - c0[t]))
                ss.append(jnp.where(mask, s, NEG_INF) if masked else s)
            vts = [v_ref[0, pl.ds(start, bk), t * dv:(t + 1) * dv] for t in range(hg)]
            _softmax_steps(ss, vts, m_ref, l_ref, acc_ref)
            return 0
        return body

    lax.fori_loop(0, n_full, make_body(False), 0)
    lax.fori_loop(n_full, nk, make_body(True), 0)
    for t in range(hg):
        o_ref[0, :, t * dv:(t + 1) * dv] = (acc_ref[t] / l_ref[t]).astype(o_ref.dtype)


def flash_attention(q, k, v, cq=None, ck=None, *, lq, sp, heads, hg, dqk, dv, bq, bk, q_off):
    b = q.shape[0]
    nk_max = sp // bk
    fox = cq is not None
    in_specs = [pl.BlockSpec((1, bq, hg * dqk), lambda b_, g, i: (b_, i, g)),
                pl.BlockSpec((1, sp, hg * dqk), lambda b_, g, i: (b_, 0, g)),
                pl.BlockSpec((1, sp, hg * dv), lambda b_, g, i: (b_, 0, g))]
    args = [q, k, v]
    if fox:
        in_specs += [pl.BlockSpec((1, bq, LANE), lambda b_, g, i: (b_, i, 0)),
                     pl.BlockSpec((1, hg, nk_max, 1, bk), lambda b_, g, i: (b_, g, 0, 0, 0))]
        args += [cq, ck]
    return pl.pallas_call(
        functools.partial(_flash_kernel, bq=bq, bk=bk, q_off=q_off, nk_max=nk_max, fox=fox, hg=hg, dqk=dqk, dv=dv),
        grid=(b, heads // hg, lq // bq),
        in_specs=in_specs,
        out_specs=pl.BlockSpec((1, bq, hg * dv), lambda b_, g, i: (b_, i, g)),
        out_shape=jax.ShapeDtypeStruct((b, lq, heads * dv), jnp.bfloat16),
        scratch_shapes=[pltpu.VMEM((hg, bq, 1), jnp.float32), pltpu.VMEM((hg, bq, 1), jnp.float32),
                        pltpu.VMEM((hg, bq, dv), jnp.float32)],
        compiler_params=_cparams(("parallel", "parallel", "arbitrary")),
        name="fox_attention" if fox else "mla_attention",
    )(*args)


_KEY_NEG_INF = -2139095041
_F32_LOWEST = -3.4028234663852886e38


_FOLD = 64
_MAX_PROBES = 8 * 34


def _key_to_f32(key):
    bits = jnp.where(key >= 0, key, key ^ jnp.int32(0x7FFFFFFF))
    return lax.bitcast_convert_type(bits, jnp.float32)


def _f32_to_key(x):
    bits = lax.bitcast_convert_type(x, jnp.int32)
    return jnp.where(bits >= 0, bits, bits ^ jnp.int32(0x7FFFFFFF))


def _dsa_kernel(iq_ref, iqs_ref, iw_ref, dq_ref, ik_ref, dk_ref, dv_ref, o_ref,
                sc_ref, qg_ref, m_ref, l_ref, acc_ref, *, bq, bk, q_off, nk_max, topk):
    i = pl.program_id(1)
    nk = _num_key_tiles(i, bq, bk, q_off, nk_max)
    qpos = q_off + i * bq + lax.broadcasted_iota(jnp.int32, (bq, 1), 0)
    klim = ((qpos >> 6) + 1) << 6
    iw = iw_ref[0]

    def score_tile(j, _):
        kt = ik_ref[0, pl.ds(pl.multiple_of(j * bk, bk), bk), :]
        acc = jnp.zeros((bq, bk), jnp.float32)
        for g in range(IDX_HEADS // 2):
            se = _dot_nt(iq_ref[0, :, g * LANE:(g + 1) * LANE], kt)
            so = _dot_nt(iqs_ref[0, :, g * LANE:(g + 1) * LANE], kt)
            acc = acc + iw[:, 2 * g:2 * g + 1] * jnp.maximum(se, 0.0)
            acc = acc + iw[:, 2 * g + 1:2 * g + 2] * jnp.maximum(so, 0.0)
        kpos = j * bk + lax.broadcasted_iota(jnp.int32, (1, bk), 1)
        sc_ref[j] = jnp.where(kpos < klim, acc, -jnp.inf)
        return 0

    lax.fori_loop(0, nk, score_tile, 0)

    def count(hit):
        def body(j, acc):
            h = hit(sc_ref[j], j)
            for c in range(bk // LANE):
                acc = acc + h[:, c * LANE:(c + 1) * LANE]
            return acc
        acc = lax.fori_loop(0, nk, body, jnp.zeros((bq, LANE), jnp.float32))
        return jnp.sum(acc, axis=1, keepdims=True)

    count_ge = lambda pivot: count(lambda t, j: jnp.where(t >= pivot, 1.0, 0.0))
    kf = jnp.float32(topk)
    nonneg = count_ge(jnp.zeros((bq, 1), jnp.float32)) >= kf
    key0 = jnp.where(nonneg, jnp.int32(0), jnp.int32(-2 ** 31))

    def bit_step(it, key):
        cand = key | (jnp.int32(1) << (30 - it))
        ok = (count_ge(_key_to_f32(cand)) >= kf) | (cand <= _KEY_NEG_INF)
        return jnp.where(ok, cand, key)

    key = lax.fori_loop(0, 31, bit_step, key0)
    thr = jnp.maximum(_key_to_f32(key), _F32_LOWEST)

    tie = count_ge(thr) > kf
    total = nk_max * bk

    @pl.when(jnp.sum(jnp.where(tie, 1, 0)) > 0)
    def _():
        need = kf - count(lambda t, j: jnp.where(t > thr, 1.0, 0.0))
        cols = lax.broadcasted_iota(jnp.int32, (bq, bk), 1)

        def count_equal_upto(j_max):
            return count(lambda t, j: jnp.where(t == thr, jnp.where(cols <= j_max - j * bk, 1.0, 0.0), 0.0))

        def index_step(_, bracket):
            j_lo, j_hi = bracket
            mid = (j_lo + j_hi) >> 1
            ok = count_equal_upto(mid) >= need
            return jnp.where(ok, j_lo, mid), jnp.where(ok, mid, j_hi)

        bracket = (jnp.full((bq, 1), -1, jnp.int32), jnp.full((bq, 1), total - 1, jnp.int32))
        _, j_cut = lax.fori_loop(0, total.bit_length() + 1, index_step, bracket)
        j_cut = jnp.where(tie, j_cut, total)

        def drop_late_ties(j, _):
            t = sc_ref[j]
            sc_ref[j] = jnp.where(t == thr, jnp.where(cols > j_cut - j * bk, -jnp.inf, t), t)
            return 0

        lax.fori_loop(0, nk, drop_late_ties, 0)

    for h in range(DSA_HEADS):
        g, r = divmod(h, DSA_GROUP)
        qg_ref[g, r * bq:(r + 1) * bq, :] = dq_ref[0, :, h * LANE:(h + 1) * LANE]
    m_ref[...] = jnp.full(m_ref.shape, NEG_INF, jnp.float32)
    l_ref[...] = jnp.zeros(l_ref.shape, jnp.float32)
    acc_ref[...] = jnp.zeros(acc_ref.shape, jnp.float32)

    def att_tile(j, _):
        start = pl.multiple_of(j * bk, bk)
        drop = jnp.where(sc_ref[j] >= thr, 0.0, NEG_INF)
        drop = jnp.concatenate([drop] * DSA_GROUP, axis=0)
        ss = [_dot_nt(qg_ref[g], dk_ref[0, pl.ds(start, bk), g * LANE:(g + 1) * LANE]) + drop
              for g in range(DSA_KV_HEADS)]
        vts = [dv_ref[0, pl.ds(start, bk), g * LANE:(g + 1) * LANE] for g in range(DSA_KV_HEADS)]
        _softmax_steps(ss, vts, m_ref, l_ref, acc_ref)
        return 0

    lax.fori_loop(0, nk, att_tile, 0)
    for h in range(DSA_HEADS):
        g, r = divmod(h, DSA_GROUP)
        rows = slice(r * bq, (r + 1) * bq)
        o_ref[0, :, h * LANE:(h + 1) * LANE] = (acc_ref[g, rows, :] / l_ref[g, rows, :]).astype(o_ref.dtype)


def dsa_attention(iq, iqs, iw, dq, ik, dk, dv, *, lq, sp, bq, bk, q_off, topk):
    b = dq.shape[0]
    nk_max = sp // bk
    qrow = lambda w: pl.BlockSpec((1, bq, w), lambda b_, i: (b_, i, 0))
    krow = lambda w: pl.BlockSpec((1, sp, w), lambda b_, i: (b_, 0, 0))
    gr = DSA_GROUP * bq
    return pl.pallas_call(
        functools.partial(_dsa_kernel, bq=bq, bk=bk, q_off=q_off, nk_max=nk_max, topk=topk),
        grid=(b, lq // bq),
        in_specs=[qrow(IDX_W), qrow(IDX_W), qrow(LANE), qrow(DSA_W), krow(LANE), krow(DSA_KV_W), krow(DSA_KV_W)],
        out_specs=qrow(DSA_W),
        out_shape=jax.ShapeDtypeStruct((b, lq, DSA_W), jnp.bfloat16),
        scratch_shapes=[pltpu.VMEM((nk_max, bq, bk), jnp.float32),
                        pltpu.VMEM((DSA_KV_HEADS, gr, LANE), jnp.bfloat16),
                        pltpu.VMEM((DSA_KV_HEADS, gr, 1), jnp.float32),
                        pltpu.VMEM((DSA_KV_HEADS, gr, 1), jnp.float32),
                        pltpu.VMEM((DSA_KV_HEADS, gr, LANE), jnp.float32)],
        compiler_params=_cparams(("parallel", "arbitrary")),
        name="dsa_attention",
    )(iq, iqs, iw, dq, ik, dk, dv)


def _split3(x):
    hi = x.astype(jnp.bfloat16).astype(jnp.float32)
    r = x - hi
    mid = r.astype(jnp.bfloat16).astype(jnp.float32)
    lo = (r - mid).astype(jnp.bfloat16).astype(jnp.float32)
    return hi, mid, lo


def _fox_pack_kernel(q_ref, k_ref, c_ref, qa_o, ka_o):
    rows = q_ref.shape[0]
    lane = lax.broadcasted_iota(jnp.int32, (rows, LANE), 1)
    c = c_ref[...]
    zero = jnp.zeros((rows, LANE), jnp.float32)
    for h in range(FOX_HEADS):
        ch = jnp.sum(jnp.where(lane == h, c, 0.0), axis=1, keepdims=True)
        hi, mid, lo = _split3(ch)
        terms = jnp.where(lane == 0, hi, jnp.where(lane == 1, mid, jnp.where(lane == 2, lo, zero)))
        ones_hi = jnp.where((lane >= 3) & (lane < 6), 1.0, 0.0)
        qa_o[:, h * FOX_QP:h * FOX_QP + LANE] = q_ref[:, h * LANE:(h + 1) * LANE]
        qa_o[:, h * FOX_QP + LANE:(h + 1) * FOX_QP] = (terms + ones_hi).astype(jnp.bfloat16)
        ones_lo = jnp.where(lane < 3, 1.0, 0.0)
        ka_o[:, h * FOX_QP:h * FOX_QP + LANE] = k_ref[:, h * LANE:(h + 1) * LANE]
        ka_o[:, h * FOX_QP + LANE:(h + 1) * FOX_QP] = (ones_lo - pltpu.roll(terms, 3, 1)).astype(jnp.bfloat16)


def fox_pack(q, k, c, *, rows, bm):
    row = lambda w: pl.BlockSpec((bm, w), lambda i: (i, 0))
    return pl.pallas_call(
        _fox_pack_kernel,
        grid=(rows // bm,),
        in_specs=[row(FOX_W), row(FOX_W), row(LANE)],
        out_specs=[row(FOX_HEADS * FOX_QP), row(FOX_HEADS * FOX_QP)],
        out_shape=[jax.ShapeDtypeStruct((rows, FOX_HEADS * FOX_QP), jnp.bfloat16)] * 2,
        compiler_params=_cparams(("parallel",)),
        name="fox_pack",
    )(q, k, c)


def _logits_stage(slot, sts, s_ref, cm_ref):
    for t, st in enumerate(sts):
        s_ref[slot, t] = st
        cm_ref[slot, t] = jnp.max(st, axis=0, keepdims=True)


def _values_stage(slot, vtts, s_ref, cm_ref, m_ref, acc_ref):
    ps, alphas = [], []
    for t in range(len(vtts)):
        m_prev = m_ref[t]
        m_new = jnp.maximum(m_prev, cm_ref[slot, t])
        alphas.append(jnp.exp2(m_prev - m_new))
        ps.append(jnp.exp2(s_ref[slot, t] - m_new).astype(jnp.bfloat16))
        m_ref[t] = m_new
    for t, (p, alpha) in enumerate(zip(ps, alphas)):
        acc_ref[t] = alpha * acc_ref[t] + _dot(vtts[t], p)


def _pipelined_tiles(nk, nk_max, logits, values):
    last = nk_max - 1
    logits(0, 0)

    def pair(jj, _):
        j = 2 * jj
        logits(jnp.minimum(j + 1, last), 1)
        values(j, 0)
        logits(jnp.minimum(j + 2, last), 0)
        values(j + 1, 1)
        return 0

    lax.fori_loop(0, lax.shift_right_logical(nk, 1), pair, 0)

    @pl.when((nk & 1) == 1)
    def _():
        values(nk - 1, 0)


def _finish_t(acc):
    return (acc[0:HEAD_DIM, :] / acc[HEAD_DIM:HEAD_DIM + 1, :]).T


def _flash_t_kernel(q_ref, k_ref, vt_ref, o_ref, m_ref, acc_ref, s_ref, cm_ref, d_ref,
                    *, bq, bk, nk_max, frame_causal, hg, dqk):
    i = pl.program_id(2)
    first = i * bq
    nk = _num_key_tiles(i, bq, bk, 0, nk_max)
    m_ref[...] = jnp.full(m_ref.shape, NEG_INF, jnp.float32)
    acc_ref[...] = jnp.zeros(acc_ref.shape, jnp.float32)
    r = lax.broadcasted_iota(jnp.int32, (bk, bq), 0)
    c = lax.broadcasted_iota(jnp.int32, (bk, bq), 1)
    d_ref[...] = (r - c) if frame_causal else ((r >> 6) - (c >> 6))

    def logits(j, slot):
        start = pl.multiple_of(j * bk, bk)
        gap = first - j * bk
        visible = gap if frame_causal else (gap >> 6)
        mask = d_ref[...] <= visible
        sts = []
        for t in range(hg):
            kt = k_ref[0, pl.ds(start, bk), t * dqk:(t + 1) * dqk]
            st = _dot_nt(kt, q_ref[0, :, t * dqk:(t + 1) * dqk])
            sts.append(jnp.where(mask, st, NEG_INF))
        _logits_stage(slot, sts, s_ref, cm_ref)

    def values(j, slot):
        _values_stage(slot, [vt_ref[0, j, t * VR:(t + 1) * VR, :] for t in range(hg)], s_ref, cm_ref, m_ref, acc_ref)

    _pipelined_tiles(nk, nk_max, logits, values)
    for t in range(hg):
        o_ref[0, :, t * HEAD_DIM:(t + 1) * HEAD_DIM] = _finish_t(acc_ref[t]).astype(o_ref.dtype)


def flash_attention_t(q, k, vt, *, lq, sp, heads, hg, dqk, bq, bk, frame_causal, name):
    nk_max = sp // bk
    return pl.pallas_call(
        functools.partial(_flash_t_kernel, bq=bq, bk=bk, nk_max=nk_max, frame_causal=frame_causal, hg=hg, dqk=dqk),
        grid=(1, heads // hg, lq // bq),
        in_specs=[pl.BlockSpec((1, bq, hg * dqk), lambda b_, g, i: (b_, i, g)),
                  _resident((1, sp, hg * dqk), lambda b_, g, i: (b_, 0, g)),
                  _resident((1, nk_max, hg * VR, bk), lambda b_, g, i: (b_, 0, g, 0))],
        out_specs=pl.BlockSpec((1, bq, hg * HEAD_DIM), lambda b_, g, i: (b_, i, g)),
        out_shape=jax.ShapeDtypeStruct((1, lq, heads * HEAD_DIM), jnp.bfloat16),
        scratch_shapes=[pltpu.VMEM((hg, 1, bq), jnp.float32), pltpu.VMEM((hg, VR, bq), jnp.float32),
                        pltpu.VMEM((2, hg, bk, bq), jnp.float32), pltpu.VMEM((2, hg, 1, bq), jnp.float32),
                        pltpu.VMEM((bk, bq), jnp.int32)],
        compiler_params=_cparams(("parallel", "parallel", "arbitrary")),
        name=name,
    )(q, k, vt)


def _dsa_t_kernel(iq_ref, iqs_ref, iwt_ref, dq_ref, ik_ref, dk_ref, dvt_ref, o_ref,
                  sc_ref, m_ref, acc_ref, s_ref, cm_ref, *, bq, bk, nk_max, topk):
    i = pl.program_id(1)
    nk = _num_key_tiles(i, bq, bk, 0, nk_max)
    qpos = i * bq + lax.broadcasted_iota(jnp.int32, (1, bq), 1)
    klim = ((qpos >> 6) + 1) << 6

    fold = lambda a, op: op(a.reshape(bk // _FOLD, _FOLD, bq), axis=0)

    def score_tile(j, carry):
        hi, lo = carry
        kt = ik_ref[0, pl.ds(pl.multiple_of(j * bk, bk), bk), :]
        acc = jnp.zeros((bk, bq), jnp.float32)
        for g in range(IDX_HEADS // 2):
            se = _dot_nt(kt, iq_ref[0, :, g * LANE:(g + 1) * LANE])
            so = _dot_nt(kt, iqs_ref[0, :, g * LANE:(g + 1) * LANE])
            acc = acc + iwt_ref[2 * g:2 * g + 1, :] * jnp.maximum(se, 0.0)
            acc = acc + iwt_ref[2 * g + 1:2 * g + 2, :] * jnp.maximum(so, 0.0)
        adm = (j * bk + lax.broadcasted_iota(jnp.int32, (bk, bq), 0)) < klim
        sc = jnp.where(adm, acc, -jnp.inf)
        sc_ref[j] = sc
        return (jnp.maximum(hi, fold(sc, jnp.max)), jnp.minimum(lo, fold(jnp.where(adm, acc, jnp.inf), jnp.min)))

    hi, lo = lax.fori_loop(0, nk, score_tile, (jnp.full((_FOLD, bq), -jnp.inf, jnp.float32),
                                               jnp.full((_FOLD, bq), jnp.inf, jnp.float32)))
    hi = jnp.max(hi, axis=0, keepdims=True)
    lo = jnp.min(lo, axis=0, keepdims=True)

    def count_ge(pivot):
        def body(j, acc):
            for r in range(bk // _FOLD):
                acc = acc + jnp.where(sc_ref[j, r * _FOLD:(r + 1) * _FOLD, :] >= pivot, 1.0, 0.0)
            return acc
        acc = lax.fori_loop(0, nk, body, jnp.zeros((_FOLD, bq), jnp.float32))
        return jnp.sum(acc, axis=0, keepdims=True)

    kf = jnp.float32(topk)
    hi = _key_to_f32(_f32_to_key(hi) + 1)
    c_lo = klim.astype(jnp.float32)
    c_hi = jnp.zeros((1, bq), jnp.float32)

    def settled(lo, hi, c_lo):
        width = lax.bitcast_convert_type(_f32_to_key(hi) - _f32_to_key(lo), jnp.uint32)
        return (c_lo <= kf) | (width <= 1)

    def search_cond(state):
        it, n_open = state[0], state[1]
        return (n_open > 0) & (it < _MAX_PROBES)

    def search_step(state):
        it, _, lo, hi, c_lo, c_hi = state
        done = settled(lo, hi, c_lo)
        k_lo, k_hi = _f32_to_key(lo), _f32_to_key(hi)
        width = lax.bitcast_convert_type(k_hi - k_lo, jnp.uint32)
        k_mid = k_lo + lax.bitcast_convert_type(width >> 1, jnp.int32)
        k_val = jnp.clip(_f32_to_key(lo + (hi - lo) * 0.5), k_lo + 1, k_hi - 1)
        probe = _key_to_f32(jnp.where((it & 7) == 7, k_mid, k_val))
        probe = jnp.where(done, lo, probe)
        c = count_ge(probe)
        up = (c >= kf) & ~done
        down = (c < kf) & ~done
        lo, c_lo = jnp.where(up, probe, lo), jnp.where(up, c, c_lo)
        hi, c_hi = jnp.where(down, probe, hi), jnp.where(down, c, c_hi)
        n_open = jnp.sum(jnp.where(settled(lo, hi, c_lo), 0, 1))
        return it + 1, n_open, lo, hi, c_lo, c_hi

    state = (jnp.int32(0), jnp.sum(jnp.where(settled(lo, hi, c_lo), 0, 1)), lo, hi, c_lo, c_hi)
    _, _, thr, _, c_lo, c_hi = lax.while_loop(search_cond, search_step, state)

    tie = c_lo > kf
    total = nk_max * bk

    @pl.when(jnp.sum(jnp.where(tie, 1, 0)) > 0)
    def _():
        need = kf - c_hi
        rows = lax.broadcasted_iota(jnp.int32, (bk, bq), 0)

        def count_equal_upto(j_max):
            def body(j, acc):
                hit = jnp.where(sc_ref[j] == thr, jnp.where(rows <= j_max - j * bk, 1.0, 0.0), 0.0)
                return acc + fold(hit, jnp.sum)
            acc = lax.fori_loop(0, nk, body, jnp.zeros((_FOLD, bq), jnp.float32))
            return jnp.sum(acc, axis=0, keepdims=True)

        def index_step(_, bracket):
            j_lo, j_hi = bracket
            mid = (j_lo + j_hi) >> 1
            ok = count_equal_upto(mid) >= need
            return jnp.where(ok, j_lo, mid), jnp.where(ok, mid, j_hi)

        bracket = (jnp.full((1, bq), -1, jnp.int32), jnp.full((1, bq), total - 1, jnp.int32))
        _, j_cut = lax.fori_loop(0, total.bit_length() + 1, index_step, bracket)
        j_cut = jnp.where(tie, j_cut, total)

        def drop_late_ties(j, _):
            t = sc_ref[j]
            sc_ref[j] = jnp.where(t == thr, jnp.where(rows > j_cut - j * bk, -jnp.inf, t), t)
            return 0

        lax.fori_loop(0, nk, drop_late_ties, 0)

    m_ref[...] = jnp.full(m_ref.shape, NEG_INF, jnp.float32)
    acc_ref[...] = jnp.zeros(acc_ref.shape, jnp.float32)

    def logits(j, slot):
        start = pl.multiple_of(j * bk, bk)
        drop = jnp.where(sc_ref[j] >= thr, 0.0, NEG_INF)
        sts = []
        for h in range(DSA_HEADS):
            g = h // DSA_GROUP
            kt = dk_ref[0, pl.ds(start, bk), g * LANE:(g + 1) * LANE]
            sts.append(_dot_nt(kt, dq_ref[0, :, h * LANE:(h + 1) * LANE]) + drop)
        _logits_stage(slot, sts, s_ref, cm_ref)

    def values(j, slot):
        vtts = [dvt_ref[0, j, (h // DSA_GROUP) * VR:(h // DSA_GROUP + 1) * VR, :] for h in range(DSA_HEADS)]
        _values_stage(slot, vtts, s_ref, cm_ref, m_ref, acc_ref)

    _pipelined_tiles(nk, nk_max, logits, values)
    for h in range(DSA_HEADS):
        o_ref[0, :, h * LANE:(h + 1) * LANE] = _finish_t(acc_ref[h]).astype(o_ref.dtype)


def dsa_attention_t(iq, iqs, iwt, dq, ik, dk, dvt, *, lq, sp, bq, bk, topk):
    nk_max = sp // bk
    qrow = lambda w: pl.BlockSpec((1, bq, w), lambda b_, i: (b_, i, 0))
    krow = lambda w: _resident((1, sp, w), lambda b_, i: (b_, 0, 0))
    return pl.pallas_call(
        functools.partial(_dsa_t_kernel, bq=bq, bk=bk, nk_max=nk_max, topk=topk),
        grid=(1, lq // bq),
        in_specs=[qrow(IDX_W), qrow(IDX_W), pl.BlockSpec((LANE, bq), lambda b_, i: (0, i)), qrow(DSA_W),
                  krow(LANE), krow(DSA_KV_W),
                  _resident((1, nk_max, DSA_KV_HEADS * VR, bk), lambda b_, i: (b_, 0, 0, 0))],
        out_specs=qrow(DSA_W),
        out_shape=jax.ShapeDtypeStruct((1, lq, DSA_W), jnp.bfloat16),
        scratch_shapes=[pltpu.VMEM((nk_max, bk, bq), jnp.float32),
                        pltpu.VMEM((DSA_HEADS, 1, bq), jnp.float32),
                        pltpu.VMEM((DSA_HEADS, VR, bq), jnp.float32),
                        pltpu.VMEM((2, DSA_HEADS, bk, bq), jnp.float32),
                        pltpu.VMEM((2, DSA_HEADS, 1, bq), jnp.float32)],
        compiler_params=_cparams(("parallel", "arbitrary")),
        name="dsa_attention_t",
    )(iq, iqs, iwt, dq, ik, dk, dvt)


def _out_proj_kernel(x_ref, a_ref, b_ref, c_ref, wa_ref, wb_ref, wc_ref, o_ref):
    o_ref[...] = (x_ref[...] + _dot(a_ref[...], wa_ref[...]) + _dot(b_ref[...], wb_ref[...])
                  + _dot(c_ref[...], wc_ref[...]))


def out_proj(x, a, b, c, wa, wb, wc, *, bm):
    m, d = x.shape
    row = lambda w: pl.BlockSpec((bm, w), lambda i: (i, 0))
    full = lambda w: _resident(w.shape, lambda i: (0, 0))
    return pl.pallas_call(
        _out_proj_kernel,
        grid=(m // bm,),
        in_specs=[row(d), row(a.shape[1]), row(b.shape[1]), row(c.shape[1]), full(wa), full(wb), full(wc)],
        out_specs=row(d),
        out_shape=jax.ShapeDtypeStruct((m, d), jnp.float32),
        compiler_params=_cparams(("parallel",)),
        name="out_proj",
    )(x, a, b, c, wa, wb, wc)


def _ffn_kernel(x_ref, g_ref, wg_ref, wu_ref, wd_ref, o_ref, xn_ref, acc_ref):
    j = pl.program_id(1)

    @pl.when(j == 0)
    def _():
        xn_ref[...] = _rms(x_ref[...], g_ref[...]).astype(jnp.bfloat16)
        acc_ref[...] = jnp.zeros_like(acc_ref)

    xn = xn_ref[...]
    gate = _dot(xn, wg_ref[...])
    up = _dot(xn, wu_ref[...])
    hidden = (gate * (1.0 / (1.0 + jnp.exp(-gate))) * up).astype(jnp.bfloat16)
    acc_ref[...] += _dot(hidden, wd_ref[...])

    @pl.when(j == pl.num_programs(1) - 1)
    def _():
        o_ref[...] = x_ref[...] + acc_ref[...]


def ffn(x, g, wg, wu, wd, *, bm, bf):
    m, d = x.shape
    f = wg.shape[1]
    return pl.pallas_call(
        _ffn_kernel,
        grid=(m // bm, f // bf),
        in_specs=[pl.BlockSpec((bm, d), lambda i, j: (i, 0)),
                  pl.BlockSpec((1, d), lambda i, j: (0, 0)),
                  pl.BlockSpec((d, bf), lambda i, j: (0, j)),
                  pl.BlockSpec((d, bf), lambda i, j: (0, j)),
                  pl.BlockSpec((bf, d), lambda i, j: (j, 0))],
        out_specs=pl.BlockSpec((bm, d), lambda i, j: (i, 0)),
        out_shape=jax.ShapeDtypeStruct((m, d), jnp.float32),
        scratch_shapes=[pltpu.VMEM((bm, d), jnp.bfloat16), pltpu.VMEM((bm, d), jnp.float32)],
        compiler_params=_cparams(("parallel", "arbitrary")),
        name="swiglu",
    )(x, g.reshape(1, d), wg, wu, wd)


def _final_norm_kernel(x_ref, g_ref, o_ref):
    o_ref[...] = _rms(x_ref[...], g_ref[...])


def final_norm(x, g, *, bm):
    m, d = x.shape
    return pl.pallas_call(
        _final_norm_kernel,
        grid=(m // bm,),
        in_specs=[pl.BlockSpec((bm, d), lambda i: (i, 0)), pl.BlockSpec((1, d), lambda i: (0, 0))],
        out_specs=pl.BlockSpec((bm, d), lambda i: (i, 0)),
        out_shape=jax.ShapeDtypeStruct((m, d), jnp.float32),
        compiler_params=_cparams(("parallel",)),
        name="final_norm",
    )(x, g.reshape(1, d))


def _pad_cols(a, width):
    return jnp.pad(a, [(0, 0)] * (a.ndim - 1) + [(0, width - a.shape[-1])])


def _layout_w_in(w_in):
    sizes = (FOX_W, FOX_W, FOX_W, FOX_HEADS, DSA_W, DSA_KV_W, DSA_KV_W, IDX_W, IDX_DIM, IDX_HEADS,
             MLA_Q_LORA, MLA_KV_LORA, MLA_ROPE)
    splits = np.cumsum(sizes)[:-1]
    q_a, k_a, v_a, f_a, q_b, k_b, v_b, q_i, k_i, w_i, c_q, c_kv, k_r = jnp.split(w_in, splits, axis=-1)
    parts = {"fq": q_a, "fk": k_a, "fv": v_a, "dq": q_b, "dk": k_b, "dv": v_b, "iq": q_i, "cq": c_q,
             "ckv": c_kv, "fa": f_a, "ik": k_i, "iw": w_i, "kr": k_r}
    cols = [_pad_cols(parts[n], w) for n, w in _SEGS]
    return _pad_cols(jnp.concatenate(cols, axis=-1), PROJ_W).astype(jnp.bfloat16)


def _layout_w_uq(w_uq):
    dp = w_uq.shape[0]
    w = w_uq.reshape(dp, MLA_Q_LORA, MLA_HEADS, MLA_NOPE + MLA_ROPE)
    return _pad_cols(w, MLA_QP).reshape(dp, MLA_Q_LORA, MLA_QW).astype(jnp.bfloat16)


def _layout_w_ukv(w_ukv):
    dp = w_ukv.shape[0]
    w = w_ukv.reshape(dp, MLA_KV_LORA, MLA_HEADS, MLA_NOPE + MLA_V)
    kn = w[..., :MLA_NOPE].reshape(dp, MLA_KV_LORA, MLA_HEADS * MLA_NOPE)
    vv = w[..., MLA_NOPE:].reshape(dp, MLA_KV_LORA, MLA_VW)
    return jnp.concatenate([kn, vv], axis=-1).astype(jnp.bfloat16)


def _rope_table(pos, rot, period):
    half = rot // 2
    inv_freq = ROPE_THETA ** (-jnp.arange(half, dtype=jnp.float32) / half)
    ang = pos.astype(jnp.float32)[:, None] * inv_freq[None, :]
    cos, sin = jnp.cos(ang), jnp.sin(ang)
    r = pos.shape[0]
    ones = jnp.ones((r, period - rot), jnp.float32)
    zeros = jnp.zeros((r, period - rot), jnp.float32)
    zh = jnp.zeros((r, half), jnp.float32)
    c = jnp.concatenate([cos, cos, ones], axis=1)
    s1 = jnp.concatenate([zh, sin, zeros], axis=1)
    s2 = jnp.concatenate([-sin, zh, zeros], axis=1)
    rep = LANE // period
    return jnp.concatenate([jnp.tile(c, (1, rep)), jnp.tile(s1, (1, rep)), jnp.tile(s2, (1, rep))], axis=1)


def _pick(n, candidates):
    for c in candidates:
        if n % c == 0:
            return c
    raise ValueError(f"no block size among {candidates} divides {n}")


def _pad_seq(a, sp):
    return jnp.pad(a, [(0, 0), (0, sp - a.shape[1])] + [(0, 0)] * (a.ndim - 2))


def kernel(x_prompt, x_sample, cache_fox_k, cache_fox_v, cache_fox_logf, cache_dsa_k, cache_dsa_v, cache_idx_k, cache_mla_ckv, cache_mla_krope, w_in, fox_bias, mla_q_norm, mla_w_uq, mla_kv_norm, mla_w_ukv, w_o, attn_norm, ffn_norm, w_gate, w_up, w_down, final_norm_w):
    bf = jnp.bfloat16
    depth = w_in.shape[0]
    nb, seq, d = x_prompt.shape
    db, dec, _ = x_sample.shape
    past = cache_fox_k.shape[2]
    assert nb == 1 and seq % CHUNK == 0 and dec % CHUNK == 0 and past % CHUNK == 0
    mp, ms = nb * seq, db * dec
    m = mp + ms
    s_len = past + dec

    bm = _pick(m, (512, 256, 128, 64))
    assert mp % bm == 0
    bq_p = _pick(seq, (256, 128))
    bk_p = bm
    bq_s = _pick(dec, (64,))
    bk_s = 256
    sp = ((s_len + bk_s - 1) // bk_s) * bk_s
    topk_p = min(IDX_TOPK_MAX, seq // 4)
    topk_s = min(IDX_TOPK_MAX, s_len // 4)
    fox_hg, mla_hg = 3, 2
    d_ff = w_gate.shape[2]
    bff = _pick(d_ff, (512, 256, 128))

    w_in_p = _layout_w_in(w_in)
    wuq_p = _layout_w_uq(mla_w_uq)
    wukv_p = _layout_w_ukv(mla_w_ukv)
    w_o_b = w_o.astype(bf)
    wg_b, wu_b, wd_b = w_gate.astype(bf), w_up.astype(bf), w_down.astype(bf)
    fox_bias_p = _pad_cols(fox_bias, LANE).reshape(depth, 1, LANE)

    pos = jnp.concatenate([jnp.tile(jnp.arange(seq, dtype=jnp.int32), nb),
                           jnp.tile(past + jnp.arange(dec, dtype=jnp.int32), db)])
    tab = jnp.concatenate([_rope_table(pos, PARTIAL_ROT, HEAD_DIM), _rope_table(pos, IDX_ROT, IDX_DIM),
                           _rope_table(pos, MLA_ROPE, LANE)], axis=1)

    x = jnp.concatenate([x_prompt.reshape(mp, d), x_sample.reshape(ms, d)], axis=0)
    new_rows = []
    for l in range(depth):
        proj = norm_matmul(x, attn_norm[l], w_in_p[l], bm=bm, bn=_pick(PROJ_W, (1536, 1024, 512)))
        (fq, fk, fv, fkb, fvb, fvt, lf, dq, dk, dv, dkb, dvb, dvt, iq, iqs, ik, ikb, iw, iwt, mq, ckv, kr) = post_proj(
            proj, tab, fox_bias_p[l], mla_q_norm[l].reshape(1, -1), mla_kv_norm[l].reshape(1, -1), wuq_p[l],
            bm=bm // 2, bk=bm)

        pr = lambda a: a[:mp].reshape(nb, seq, -1)
        sr = lambda a: a[mp:].reshape(db, dec, -1)
        new_rows.append((fk, fv, lf, dk, dv, ik, ckv, kr))

        cat = lambda c, new: _pad_seq(jnp.concatenate([c, new], axis=1), sp)
        s_fkb = cat(cache_fox_k[l].reshape(db, past, FOX_W).astype(bf), sr(fkb))
        s_fvb = cat(cache_fox_v[l].reshape(db, past, FOX_W).astype(bf), sr(fvb))
        s_lf = cat(_pad_cols(cache_fox_logf[l], LANE), sr(lf))
        s_dkb = cat(cache_dsa_k[l].reshape(db, past, DSA_KV_W).astype(bf), sr(dkb))
        s_dvb = cat(cache_dsa_v[l].reshape(db, past, DSA_KV_W).astype(bf), sr(dvb))
        s_ikb = cat(_pad_cols(cache_idx_k[l], LANE).astype(bf), sr(ikb))
        s_ckv = cat(cache_mla_ckv[l], sr(ckv))
        s_kr = cat(_pad_cols(cache_mla_krope[l], LANE), sr(kr))

        ckv_all = jnp.concatenate([ckv[:mp], s_ckv.reshape(db * sp, -1)], axis=0)
        kr_all = jnp.concatenate([kr[:mp], s_kr.reshape(db * sp, -1)], axis=0)
        kmla, vmla, vtmla = kv_up(ckv_all, kr_all, wukv_p[l], bm=bm)
        s_kmla, s_vmla = kmla[mp:].reshape(db, sp, -1), vmla[mp:].reshape(db, sp, -1)

        p_c, _ = cumsum_rows(pr(lf), bt=_pick(seq, (256, 128)))
        s_c, s_ct = cumsum_rows(s_lf, bt=_pick(sp, (256, 128)))
        ck_form = lambda ct, bk: ct[:, :FOX_HEADS].reshape(ct.shape[0], FOX_HEADS, ct.shape[2] // bk, 1, bk)

        al = lambda a: a.reshape(1, *a.shape)
        fqa, fka = fox_pack(fq, fkb, p_c[0], rows=mp, bm=bm)
        out_a = flash_attention_t(al(fqa), al(fka), al(fvt), lq=seq, sp=seq, heads=FOX_HEADS, hg=fox_hg, dqk=FOX_QP,
                                  bq=bq_p, bk=bk_p, frame_causal=True, name="fox_attention_t")
        out_b = dsa_attention_t(al(iq), al(iqs), iwt, al(dq), al(ikb), al(dkb), al(dvt), lq=seq, sp=seq,
                                bq=bq_p, bk=bk_p, topk=topk_p)
        out_c = flash_attention_t(al(mq), al(kmla), al(vtmla), lq=seq, sp=seq, heads=MLA_HEADS, hg=mla_hg, dqk=MLA_QP,
                                  bq=bq_p, bk=bk_p, frame_causal=False, name="mla_attention_t")
        s_out_a = flash_attention(sr(fq), s_fkb, s_fvb, s_c[:, past:past + dec], ck_form(s_ct, bk_s), lq=dec, sp=sp,
                                  heads=FOX_HEADS, hg=fox_hg, dqk=HEAD_DIM, dv=HEAD_DIM, bq=bq_s, bk=bk_s, q_off=past)
        s_out_b = dsa_attention(sr(iq), sr(iqs), sr(iw), sr(dq), s_ikb, s_dkb, s_dvb, lq=dec, sp=sp,
                                bq=bq_s, bk=bk_s, q_off=past, topk=topk_s)
        s_out_c = flash_attention(sr(mq), s_kmla, s_vmla, lq=dec, sp=sp, heads=MLA_HEADS, hg=mla_hg,
                                  dqk=MLA_QP, dv=MLA_V, bq=bq_s, bk=bk_s, q_off=past)

        rows = lambda p, s: jnp.concatenate([p.reshape(mp, -1), s.reshape(ms, -1)], axis=0)
        x = out_proj(x, rows(out_a, s_out_a), rows(out_b, s_out_b), rows(out_c, s_out_c),
                     w_o_b[l, :FOX_W], w_o_b[l, FOX_W:FOX_W + DSA_W], w_o_b[l, FOX_W + DSA_W:], bm=bm)
        x = ffn(x, ffn_norm[l], wg_b[l], wu_b[l], wd_b[l], bm=bm, bf=bff)

    y = final_norm(x, final_norm_w, bm=bm)
    y_prompt = y[:mp].reshape(nb, seq, d)
    y_sample = y[mp:].reshape(db, dec, d)
    tails = ((FOX_HEADS, HEAD_DIM), (FOX_HEADS, HEAD_DIM), (FOX_HEADS,), (DSA_KV_HEADS, HEAD_DIM),
             (DSA_KV_HEADS, HEAD_DIM), (IDX_DIM,), (MLA_KV_LORA,), (MLA_ROPE,))
    p_out, s_out = [], []
    for arrays, tail in zip(zip(*new_rows), tails):
        a = jnp.stack(arrays)[..., :math.prod(tail)]
        p_out.append(a[:, :mp].reshape((depth, nb, seq) + tail))
        s_out.append(a[:, mp:].reshape((depth, db, dec) + tail))
    return (y_prompt, y_sample) + tuple(p_out) + tuple(s_out)
```

```python
import functools
import math

import jax
import jax.numpy as jnp
import numpy as np
from jax import lax
from jax.experimental import pallas as pl
from jax.experimental.pallas import tpu as pltpu

CHUNK = 64
HEAD_DIM = 128
FOX_HEADS = 6
DSA_HEADS = 6
DSA_KV_HEADS = 2
DSA_GROUP = DSA_HEADS // DSA_KV_HEADS
IDX_HEADS = 16
IDX_DIM = 64
IDX_TOPK_MAX = 256
MLA_HEADS = 4
MLA_Q_LORA = 512
MLA_KV_LORA = 256
MLA_NOPE = 128
MLA_ROPE = 64
MLA_V = 128
PARTIAL_ROT = HEAD_DIM // 4
IDX_ROT = IDX_DIM // 4
ROPE_THETA = 500000.0
EPS = 1e-6
NEG_INF = -1e30
FOX_SCALE = HEAD_DIM ** -0.5
DSA_SCALE = HEAD_DIM ** -0.5
MLA_SCALE = (MLA_NOPE + MLA_ROPE) ** -0.5
IDX_W_SCALE = (IDX_HEADS * IDX_DIM) ** -0.5
LOG2E = math.log2(math.e)

LANE = 128
VMEM_LIMIT = 56 * 1024 * 1024

FOX_W = FOX_HEADS * HEAD_DIM
DSA_W = DSA_HEADS * HEAD_DIM
DSA_KV_W = DSA_KV_HEADS * HEAD_DIM
IDX_W = IDX_HEADS * IDX_DIM
MLA_QP = 2 * LANE
MLA_QW = MLA_HEADS * MLA_QP
MLA_VW = MLA_HEADS * MLA_V
VR = HEAD_DIM + 16
FOX_QP = 2 * LANE

_SEGS = (("fq", FOX_W), ("fk", FOX_W), ("fv", FOX_W), ("dq", DSA_W), ("dk", DSA_KV_W), ("dv", DSA_KV_W),
         ("iq", IDX_W), ("cq", MLA_Q_LORA), ("ckv", MLA_KV_LORA),
         ("fa", LANE), ("ik", LANE), ("iw", LANE), ("kr", LANE))
_OFF = {}
_o = 0
for _n, _w in _SEGS:
    _OFF[_n] = _o
    _o += _w
PROJ_W = ((_o + 511) // 512) * 512

_NT = (((1,), (1,)), ((), ()))


def _dot(a, b):
    return jnp.dot(a, b, preferred_element_type=jnp.float32)


def _dot_nt(a, b):
    return lax.dot_general(a, b, _NT, preferred_element_type=jnp.float32)


def _cparams(sem):
    return pltpu.CompilerParams(dimension_semantics=sem, vmem_limit_bytes=VMEM_LIMIT)


def _resident(block_shape, index_map):
    return pl.BlockSpec(block_shape, index_map, pipeline_mode=pl.Buffered(1))


def _rms(x, g):
    return x * lax.rsqrt(jnp.mean(x * x, axis=-1, keepdims=True) + EPS) * g


def _norm_matmul_kernel(x_ref, g_ref, w_ref, o_ref, xn_ref):
    @pl.when(pl.program_id(1) == 0)
    def _():
        xn_ref[...] = _rms(x_ref[...], g_ref[...]).astype(jnp.bfloat16)

    o_ref[...] = _dot(xn_ref[...], w_ref[...])


def norm_matmul(x, g, w, layer, *, bm, bn):
    m, d = x.shape
    n = w.shape[2]
    return pl.pallas_call(
        _norm_matmul_kernel,
        grid=(m // bm, n // bn),
        in_specs=[pl.BlockSpec((bm, d), lambda i, j: (i, 0)),
                  pl.BlockSpec((1, d), lambda i, j: (0, 0)),
                  pl.BlockSpec((None, d, bn), lambda i, j: (layer, 0, j))],
        out_specs=pl.BlockSpec((bm, bn), lambda i, j: (i, j)),
        out_shape=jax.ShapeDtypeStruct((m, n), jnp.float32),
        scratch_shapes=[pltpu.VMEM((bm, d), jnp.bfloat16)],
        compiler_params=_cparams(("parallel", "arbitrary")),
        name="norm_matmul",
    )(x, g.reshape(1, d), w)


def _rope(x, tab, half):
    c, s1, s2 = tab[:, 0:LANE], tab[:, LANE:2 * LANE], tab[:, 2 * LANE:3 * LANE]
    return x * c + pltpu.roll(x, half, 1) * s1 + pltpu.roll(x, LANE - half, 1) * s2


def _value_t_tail(width):
    r = lax.broadcasted_iota(jnp.int32, (16, width), 0)
    return jnp.where(r == 0, 1.0, 0.0).astype(jnp.bfloat16)


def _store_value_t(o_ref, h, v):
    o_ref[0, h * VR:h * VR + HEAD_DIM, :] = v.T.astype(jnp.bfloat16)
    o_ref[0, h * VR + HEAD_DIM:(h + 1) * VR, :] = _value_t_tail(v.shape[0])


def _post_kernel(p_ref, tab_ref, fb_ref, qn_ref, kvn_ref, wuq_ref,
                 fq_o, fk_o, fv_o, fkb_o, fvb_o, fvt_o, lf_o,
                 dq_o, dk_o, dv_o, dkb_o, dvb_o, dvt_o,
                 iq_o, iqs_o, ik_o, ikb_o, iw_o, iwt_o,
                 mq_o, ckv_o, kr_o):
    bf = jnp.bfloat16
    seg = lambda name, w: p_ref[:, _OFF[name]:_OFF[name] + w]
    tab_d = tab_ref[:, 0:3 * LANE]
    tab_i = tab_ref[:, 3 * LANE:6 * LANE]
    tab_m = tab_ref[:, 6 * LANE:9 * LANE]

    fq_o[...] = (seg("fq", FOX_W) * (FOX_SCALE * LOG2E)).astype(bf)
    fk = seg("fk", FOX_W)
    fk_o[...] = fk
    fkb_o[...] = fk.astype(bf)
    fv = seg("fv", FOX_W)
    fv_o[...] = fv
    fvb_o[...] = fv.astype(bf)
    for h in range(FOX_HEADS):
        _store_value_t(fvt_o, h, p_ref[:, _OFF["fv"] + h * LANE:_OFF["fv"] + (h + 1) * LANE])
    z = seg("fa", LANE) + fb_ref[...]
    lf_o[...] = jnp.minimum(z, 0.0) - jnp.log1p(jnp.exp(-jnp.abs(z)))

    for h in range(DSA_HEADS):
        x = p_ref[:, _OFF["dq"] + h * LANE:_OFF["dq"] + (h + 1) * LANE]
        dq_o[:, h * LANE:(h + 1) * LANE] = (_rope(x, tab_d, PARTIAL_ROT // 2) * (DSA_SCALE * LOG2E)).astype(bf)
    for h in range(DSA_KV_HEADS):
        x = p_ref[:, _OFF["dk"] + h * LANE:_OFF["dk"] + (h + 1) * LANE]
        y = _rope(x, tab_d, PARTIAL_ROT // 2)
        dk_o[:, h * LANE:(h + 1) * LANE] = y
        dkb_o[:, h * LANE:(h + 1) * LANE] = y.astype(bf)
        _store_value_t(dvt_o, h, p_ref[:, _OFF["dv"] + h * LANE:_OFF["dv"] + (h + 1) * LANE])
    dv = seg("dv", DSA_KV_W)
    dv_o[...] = dv
    dvb_o[...] = dv.astype(bf)
    for j in range(IDX_W // LANE):
        x = p_ref[:, _OFF["iq"] + j * LANE:_OFF["iq"] + (j + 1) * LANE]
        y = _rope(x, tab_i, IDX_ROT // 2)
        iq_o[:, j * LANE:(j + 1) * LANE] = y.astype(bf)
        iqs_o[:, j * LANE:(j + 1) * LANE] = pltpu.roll(y, IDX_DIM, 1).astype(bf)
    y = _rope(seg("ik", LANE), tab_i, IDX_ROT // 2)
    ik_o[...] = y
    ikb_o[...] = y.astype(bf)
    iw = seg("iw", LANE) * IDX_W_SCALE
    iw_o[...] = iw
    iwt_o[...] = iw.T

    cqn = _rms(seg("cq", MLA_Q_LORA), qn_ref[...]).astype(bf)
    mq = _dot(cqn, wuq_ref[...])
    for h in range(MLA_HEADS):
        a = h * MLA_QP
        mq_o[:, a:a + LANE] = (mq[:, a:a + LANE] * (MLA_SCALE * LOG2E)).astype(bf)
        r = _rope(mq[:, a + LANE:a + 2 * LANE], tab_m, MLA_ROPE // 2)
        mq_o[:, a + LANE:a + 2 * LANE] = (r * (MLA_SCALE * LOG2E)).astype(bf)
    ckv_o[...] = _rms(seg("ckv", MLA_KV_LORA), kvn_ref[...])
    kr_o[...] = _rope(seg("kr", LANE), tab_m, MLA_ROPE // 2)


def post_proj(proj, tab, fox_bias_p, q_norm, kv_norm, wuq_p, *, bm, bk):
    m = proj.shape[0]
    r = bk // bm
    f32, bf = jnp.float32, jnp.bfloat16
    row = lambda w: pl.BlockSpec((bm, w), lambda i: (i, 0))
    full = lambda a: pl.BlockSpec(a.shape, lambda i: (0,) * a.ndim)
    vt = lambda heads: pl.BlockSpec((1, heads * VR, bm), lambda i: (i // r, 0, i % r))
    rowo = lambda w, dt: (row(w), jax.ShapeDtypeStruct((m, w), dt))
    vto = lambda heads: (vt(heads), jax.ShapeDtypeStruct((m // bk, heads * VR, bk), bf))
    outs = [rowo(FOX_W, bf), rowo(FOX_W, f32), rowo(FOX_W, f32), rowo(FOX_W, bf), rowo(FOX_W, bf), vto(FOX_HEADS),
            rowo(LANE, f32),
            rowo(DSA_W, bf), rowo(DSA_KV_W, f32), rowo(DSA_KV_W, f32), rowo(DSA_KV_W, bf), rowo(DSA_KV_W, bf),
            vto(DSA_KV_HEADS),
            rowo(IDX_W, bf), rowo(IDX_W, bf), rowo(LANE, f32), rowo(LANE, bf), rowo(LANE, f32),
            (pl.BlockSpec((LANE, bm), lambda i: (0, i)), jax.ShapeDtypeStruct((LANE, m), f32)),
            rowo(MLA_QW, bf), rowo(MLA_KV_LORA, f32), rowo(LANE, f32)]
    return pl.pallas_call(
        _post_kernel,
        grid=(m // bm,),
        in_specs=[row(PROJ_W), row(9 * LANE), full(fox_bias_p), full(q_norm), full(kv_norm), full(wuq_p)],
        out_specs=[s for s, _ in outs],
        out_shape=[o for _, o in outs],
        compiler_params=_cparams(("parallel",)),
        name="post_proj",
    )(proj, tab, fox_bias_p, q_norm, kv_norm, wuq_p)


def _kv_up_kernel(ckv_ref, kr_ref, w_ref, k_o, v_o, vt_o):
    bf = jnp.bfloat16
    kv = _dot(ckv_ref[...].astype(bf), w_ref[...])
    kr = kr_ref[...].astype(bf)
    for h in range(MLA_HEADS):
        k_o[:, h * MLA_QP:h * MLA_QP + LANE] = kv[:, h * LANE:(h + 1) * LANE].astype(bf)
        k_o[:, h * MLA_QP + LANE:(h + 1) * MLA_QP] = kr
        _store_value_t(vt_o, h, kv[:, (MLA_HEADS + h) * LANE:(MLA_HEADS + h + 1) * LANE])
    v_o[...] = kv[:, MLA_HEADS * LANE:].astype(bf)


def kv_up(ckv, kr, w, *, bm):
    r = ckv.shape[0]
    row = lambda w_: pl.BlockSpec((bm, w_), lambda i: (i, 0))
    return pl.pallas_call(
        _kv_up_kernel,
        grid=(r // bm,),
        in_specs=[row(MLA_KV_LORA), row(LANE), pl.BlockSpec(w.shape, lambda i: (0, 0))],
        out_specs=[row(MLA_QW), row(MLA_VW), pl.BlockSpec((1, MLA_HEADS * VR, bm), lambda i: (i, 0, 0))],
        out_shape=[jax.ShapeDtypeStruct((r, MLA_QW), jnp.bfloat16),
                   jax.ShapeDtypeStruct((r, MLA_VW), jnp.bfloat16),
                   jax.ShapeDtypeStruct((r // bm, MLA_HEADS * VR, bm), jnp.bfloat16)],
        compiler_params=_cparams(("parallel",)),
        name="mla_kv_up",
    )(ckv, kr, w)


def _cumsum_kernel(x_ref, c_o, ct_o, carry_ref, carry_t_ref):
    t = x_ref.shape[1]

    @pl.when(pl.program_id(1) == 0)
    def _():
        carry_ref[...] = jnp.zeros_like(carry_ref)
        carry_t_ref[...] = jnp.zeros_like(carry_t_ref)

    x = x_ref[0]
    r = lax.broadcasted_iota(jnp.int32, (t, t), 0)
    c = lax.broadcasted_iota(jnp.int32, (t, t), 1)
    lower = jnp.where(c <= r, 1.0, 0.0).astype(jnp.float32)
    upper = jnp.where(r <= c, 1.0, 0.0).astype(jnp.float32)
    cs = jnp.dot(lower, x, preferred_element_type=jnp.float32, precision=lax.Precision.HIGHEST)
    cs = cs + carry_ref[...]
    c_o[0] = cs * LOG2E
    carry_ref[...] = cs[t - 1:t, :]
    cst = jnp.dot(x.T, upper, preferred_element_type=jnp.float32, precision=lax.Precision.HIGHEST)
    cst = cst + carry_t_ref[...]
    ct_o[0] = cst[0:8, :] * LOG2E
    carry_t_ref[...] = cst[:, t - 1:t]


def cumsum_rows(x, *, bt):
    b, s, _ = x.shape
    return pl.pallas_call(
        _cumsum_kernel,
        grid=(b, s // bt),
        in_specs=[pl.BlockSpec((1, bt, LANE), lambda i, j: (i, j, 0))],
        out_specs=[pl.BlockSpec((1, bt, LANE), lambda i, j: (i, j, 0)),
                   pl.BlockSpec((1, 8, bt), lambda i, j: (i, 0, j))],
        out_shape=[jax.ShapeDtypeStruct((b, s, LANE), jnp.float32),
                   jax.ShapeDtypeStruct((b, 8, s), jnp.float32)],
        scratch_shapes=[pltpu.VMEM((1, LANE), jnp.float32), pltpu.VMEM((LANE, 1), jnp.float32)],
        compiler_params=_cparams(("parallel", "arbitrary")),
        name="logf_cumsum",
    )(x)


def _num_key_tiles(i, bq, bk, q_off, nk_max):
    last = q_off + (i + 1) * bq
    return jnp.minimum((last + bk - 1) // bk, nk_max)


def _softmax_steps(ss, vts, m_ref, l_ref, acc_ref):
    ps, alphas = [], []
    for t, s in enumerate(ss):
        m_prev = m_ref[t]
        m_new = jnp.maximum(m_prev, jnp.max(s, axis=1, keepdims=True))
        alpha = jnp.exp2(m_prev - m_new)
        p = jnp.exp2(s - m_new)
        l_ref[t] = alpha * l_ref[t] + jnp.sum(p, axis=1, keepdims=True)
        m_ref[t] = m_new
        ps.append(p.astype(jnp.bfloat16))
        alphas.append(alpha)
    for t, (p, alpha) in enumerate(zip(ps, alphas)):
        acc_ref[t] = alpha * acc_ref[t] + _dot(p, vts[t])


def _flash_kernel(*refs, bq, bk, q_off, nk_max, fox, hg, dqk, dv):
    if fox:
        q_ref, k_ref, v_ref, cq_ref, ck_ref, o_ref, m_ref, l_ref, acc_ref = refs
    else:
        q_ref, k_ref, v_ref, o_ref, m_ref, l_ref, acc_ref = refs
    gi = pl.program_id(1)
    i = pl.program_id(2)
    first = q_off + i * bq
    qpos = first + lax.broadcasted_iota(jnp.int32, (bq, 1), 0)
    if fox:
        lane = lax.broadcasted_iota(jnp.int32, (bq, LANE), 1)
        cq_all = cq_ref[0]
        cqs = [jnp.sum(jnp.where(lane == gi * hg + t, cq_all, 0.0), axis=1, keepdims=True) for t in range(hg)]
        c0 = [c[0:1, :] for c in cqs]
        bias_q = [c - z for c, z in zip(cqs, c0)]
    nk = _num_key_tiles(i, bq, bk, q_off, nk_max)
    n_full = jnp.minimum((first + (1 if fox else CHUNK)) // bk, nk)

    m_ref[...] = jnp.full(m_ref.shape, NEG_INF, jnp.float32)
    l_ref[...] = jnp.zeros(l_ref.shape, jnp.float32)
    acc_ref[...] = jnp.zeros(acc_ref.shape, jnp.float32)

    def make_body(masked):
        def body(j, _):
            start = pl.multiple_of(j * bk, bk)
            if masked:
                kpos = j * bk + lax.broadcasted_iota(jnp.int32, (1, bk), 1)
                mask = (kpos <= qpos) if fox else ((kpos >> 6) <= (qpos >> 6))
            ss = []
            for t in range(hg):
                kt = k_ref[0, pl.ds(start, bk), t * dqk:(t + 1) * dqk]
                s = _dot_nt(q_ref[0, :, t * dqk:(t + 1) * dqk], kt)
                if fox:
                    s = s + (bias_q[t] - (ck_ref[0, t, j] - c0[t]))
                ss.append(jnp.where(mask, s, NEG_INF) if masked else s)
            vts = [v_ref[0, pl.ds(start, bk), t * dv:(t + 1) * dv] for t in range(hg)]
            _softmax_steps(ss, vts, m_ref, l_ref, acc_ref)
            return 0
        return body

    lax.fori_loop(0, n_full, make_body(False), 0)
    lax.fori_loop(n_full, nk, make_body(True), 0)
    for t in range(hg):
        o_ref[0, :, t * dv:(t + 1) * dv] = (acc_ref[t] / l_ref[t]).astype(o_ref.dtype)


def flash_attention(q, k, v, cq=None, ck=None, *, lq, sp, heads, hg, dqk, dv, bq, bk, q_off):
    b = q.shape[0]
    nk_max = sp // bk
    fox = cq is not None
    in_specs = [pl.BlockSpec((1, bq, hg * dqk), lambda b_, g, i: (b_, i, g)),
                pl.BlockSpec((1, sp, hg * dqk), lambda b_, g, i: (b_, 0, g)),
                pl.BlockSpec((1, sp, hg * dv), lambda b_, g, i: (b_, 0, g))]
    args = [q, k, v]
    if fox:
        in_specs += [pl.BlockSpec((1, bq, LANE), lambda b_, g, i: (b_, i, 0)),
                     pl.BlockSpec((1, hg, nk_max, 1, bk), lambda b_, g, i: (b_, g, 0, 0, 0))]
        args += [cq, ck]
    return pl.pallas_call(
        functools.partial(_flash_kernel, bq=bq, bk=bk, q_off=q_off, nk_max=nk_max, fox=fox, hg=hg, dqk=dqk, dv=dv),
        grid=(b, heads // hg, lq // bq),
        in_specs=in_specs,
        out_specs=pl.BlockSpec((1, bq, hg * dv), lambda b_, g, i: (b_, i, g)),
        out_shape=jax.ShapeDtypeStruct((b, lq, heads * dv), jnp.bfloat16),
        scratch_shapes=[pltpu.VMEM((hg, bq, 1), jnp.float32), pltpu.VMEM((hg, bq, 1), jnp.float32),
                        pltpu.VMEM((hg, bq, dv), jnp.float32)],
        compiler_params=_cparams(("parallel", "parallel", "arbitrary")),
        name="fox_attention" if fox else "mla_attention",
    )(*args)


_KEY_NEG_INF = -2139095041
_F32_LOWEST = -3.4028234663852886e38


_FOLD = 64
_MAX_PROBES = 8 * 34


def _key_to_f32(key):
    bits = jnp.where(key >= 0, key, key ^ jnp.int32(0x7FFFFFFF))
    return lax.bitcast_convert_type(bits, jnp.float32)


def _f32_to_key(x):
    bits = lax.bitcast_convert_type(x, jnp.int32)
    return jnp.where(bits >= 0, bits, bits ^ jnp.int32(0x7FFFFFFF))


def _dsa_kernel(iq_ref, iqs_ref, iw_ref, dq_ref, ik_ref, dk_ref, dv_ref, o_ref,
                sc_ref, qg_ref, m_ref, l_ref, acc_ref, *, bq, bk, q_off, nk_max, topk):
    i = pl.program_id(1)
    nk = _num_key_tiles(i, bq, bk, q_off, nk_max)
    qpos = q_off + i * bq + lax.broadcasted_iota(jnp.int32, (bq, 1), 0)
    klim = ((qpos >> 6) + 1) << 6
    iw = iw_ref[0]

    def score_tile(j, _):
        kt = ik_ref[0, pl.ds(pl.multiple_of(j * bk, bk), bk), :]
        acc = jnp.zeros((bq, bk), jnp.float32)
        for g in range(IDX_HEADS // 2):
            se = _dot_nt(iq_ref[0, :, g * LANE:(g + 1) * LANE], kt)
            so = _dot_nt(iqs_ref[0, :, g * LANE:(g + 1) * LANE], kt)
            acc = acc + iw[:, 2 * g:2 * g + 1] * jnp.maximum(se, 0.0)
            acc = acc + iw[:, 2 * g + 1:2 * g + 2] * jnp.maximum(so, 0.0)
        kpos = j * bk + lax.broadcasted_iota(jnp.int32, (1, bk), 1)
        sc_ref[j] = jnp.where(kpos < klim, acc, -jnp.inf)
        return 0

    lax.fori_loop(0, nk, score_tile, 0)

    def count(hit):
        def body(j, acc):
            h = hit(sc_ref[j], j)
            for c in range(bk // LANE):
                acc = acc + h[:, c * LANE:(c + 1) * LANE]
            return acc
        acc = lax.fori_loop(0, nk, body, jnp.zeros((bq, LANE), jnp.float32))
        return jnp.sum(acc, axis=1, keepdims=True)

    count_ge = lambda pivot: count(lambda t, j: jnp.where(t >= pivot, 1.0, 0.0))
    kf = jnp.float32(topk)
    nonneg = count_ge(jnp.zeros((bq, 1), jnp.float32)) >= kf
    key0 = jnp.where(nonneg, jnp.int32(0), jnp.int32(-2 ** 31))

    def bit_step(it, key):
        cand = key | (jnp.int32(1) << (30 - it))
        ok = (count_ge(_key_to_f32(cand)) >= kf) | (cand <= _KEY_NEG_INF)
        return jnp.where(ok, cand, key)

    key = lax.fori_loop(0, 31, bit_step, key0)
    thr = jnp.maximum(_key_to_f32(key), _F32_LOWEST)

    tie = count_ge(thr) > kf
    total = nk_max * bk

    @pl.when(jnp.sum(jnp.where(tie, 1, 0)) > 0)
    def _():
        need = kf - count(lambda t, j: jnp.where(t > thr, 1.0, 0.0))
        cols = lax.broadcasted_iota(jnp.int32, (bq, bk), 1)

        def count_equal_upto(j_max):
            return count(lambda t, j: jnp.where(t == thr, jnp.where(cols <= j_max - j * bk, 1.0, 0.0), 0.0))

        def index_step(_, bracket):
            j_lo, j_hi = bracket
            mid = (j_lo + j_hi) >> 1
            ok = count_equal_upto(mid) >= need
            return jnp.where(ok, j_lo, mid), jnp.where(ok, mid, j_hi)

        bracket = (jnp.full((bq, 1), -1, jnp.int32), jnp.full((bq, 1), total - 1, jnp.int32))
        _, j_cut = lax.fori_loop(0, total.bit_length() + 1, index_step, bracket)
        j_cut = jnp.where(tie, j_cut, total)

        def drop_late_ties(j, _):
            t = sc_ref[j]
            sc_ref[j] = jnp.where(t == thr, jnp.where(cols > j_cut - j * bk, -jnp.inf, t), t)
            return 0

        lax.fori_loop(0, nk, drop_late_ties, 0)

    for h in range(DSA_HEADS):
        g, r = divmod(h, DSA_GROUP)
        qg_ref[g, r * bq:(r + 1) * bq, :] = dq_ref[0, :, h * LANE:(h + 1) * LANE]
    m_ref[...] = jnp.full(m_ref.shape, NEG_INF, jnp.float32)
    l_ref[...] = jnp.zeros(l_ref.shape, jnp.float32)
    acc_ref[...] = jnp.zeros(acc_ref.shape, jnp.float32)

    def att_tile(j, _):
        start = pl.multiple_of(j * bk, bk)
        drop = jnp.where(sc_ref[j] >= thr, 0.0, NEG_INF)
        drop = jnp.concatenate([drop] * DSA_GROUP, axis=0)
        ss = [_dot_nt(qg_ref[g], dk_ref[0, pl.ds(start, bk), g * LANE:(g + 1) * LANE]) + drop
              for g in range(DSA_KV_HEADS)]
        vts = [dv_ref[0, pl.ds(start, bk), g * LANE:(g + 1) * LANE] for g in range(DSA_KV_HEADS)]
        _softmax_steps(ss, vts, m_ref, l_ref, acc_ref)
        return 0

    lax.fori_loop(0, nk, att_tile, 0)
    for h in range(DSA_HEADS):
        g, r = divmod(h, DSA_GROUP)
        rows = slice(r * bq, (r + 1) * bq)
        o_ref[0, :, h * LANE:(h + 1) * LANE] = (acc_ref[g, rows, :] / l_ref[g, rows, :]).astype(o_ref.dtype)


def dsa_attention(iq, iqs, iw, dq, ik, dk, dv, *, lq, sp, bq, bk, q_off, topk):
    b = dq.shape[0]
    nk_max = sp // bk
    qrow = lambda w: pl.BlockSpec((1, bq, w), lambda b_, i: (b_, i, 0))
    krow = lambda w: pl.BlockSpec((1, sp, w), lambda b_, i: (b_, 0, 0))
    gr = DSA_GROUP * bq
    return pl.pallas_call(
        functools.partial(_dsa_kernel, bq=bq, bk=bk, q_off=q_off, nk_max=nk_max, topk=topk),
        grid=(b, lq // bq),
        in_specs=[qrow(IDX_W), qrow(IDX_W), qrow(LANE), qrow(DSA_W), krow(LANE), krow(DSA_KV_W), krow(DSA_KV_W)],
        out_specs=qrow(DSA_W),
        out_shape=jax.ShapeDtypeStruct((b, lq, DSA_W), jnp.bfloat16),
        scratch_shapes=[pltpu.VMEM((nk_max, bq, bk), jnp.float32),
                        pltpu.VMEM((DSA_KV_HEADS, gr, LANE), jnp.bfloat16),
                        pltpu.VMEM((DSA_KV_HEADS, gr, 1), jnp.float32),
                        pltpu.VMEM((DSA_KV_HEADS, gr, 1), jnp.float32),
                        pltpu.VMEM((DSA_KV_HEADS, gr, LANE), jnp.float32)],
        compiler_params=_cparams(("parallel", "arbitrary")),
        name="dsa_attention",
    )(iq, iqs, iw, dq, ik, dk, dv)


def _split3(x):
    hi = x.astype(jnp.bfloat16).astype(jnp.float32)
    r = x - hi
    mid = r.astype(jnp.bfloat16).astype(jnp.float32)
    lo = (r - mid).astype(jnp.bfloat16).astype(jnp.float32)
    return hi, mid, lo


def _fox_pack_kernel(q_ref, k_ref, c_ref, qa_o, ka_o):
    rows = q_ref.shape[0]
    lane = lax.broadcasted_iota(jnp.int32, (rows, LANE), 1)
    c = c_ref[...]
    zero = jnp.zeros((rows, LANE), jnp.float32)
    for h in range(FOX_HEADS):
        ch = jnp.sum(jnp.where(lane == h, c, 0.0), axis=1, keepdims=True)
        hi, mid, lo = _split3(ch)
        terms = jnp.where(lane == 0, hi, jnp.where(lane == 1, mid, jnp.where(lane == 2, lo, zero)))
        ones_hi = jnp.where((lane >= 3) & (lane < 6), 1.0, 0.0)
        qa_o[:, h * FOX_QP:h * FOX_QP + LANE] = q_ref[:, h * LANE:(h + 1) * LANE]
        qa_o[:, h * FOX_QP + LANE:(h + 1) * FOX_QP] = (terms + ones_hi).astype(jnp.bfloat16)
        ones_lo = jnp.where(lane < 3, 1.0, 0.0)
        ka_o[:, h * FOX_QP:h * FOX_QP + LANE] = k_ref[:, h * LANE:(h + 1) * LANE]
        ka_o[:, h * FOX_QP + LANE:(h + 1) * FOX_QP] = (ones_lo - pltpu.roll(terms, 3, 1)).astype(jnp.bfloat16)


def fox_pack(q, k, c, *, rows, bm):
    row = lambda w: pl.BlockSpec((bm, w), lambda i: (i, 0))
    return pl.pallas_call(
        _fox_pack_kernel,
        grid=(rows // bm,),
        in_specs=[row(FOX_W), row(FOX_W), row(LANE)],
        out_specs=[row(FOX_HEADS * FOX_QP), row(FOX_HEADS * FOX_QP)],
        out_shape=[jax.ShapeDtypeStruct((rows, FOX_HEADS * FOX_QP), jnp.bfloat16)] * 2,
        compiler_params=_cparams(("parallel",)),
        name="fox_pack",
    )(q, k, c)


def _logits_stage(slot, sts, s_ref, cm_ref):
    for t, st in enumerate(sts):
        s_ref[slot, t] = st
        cm_ref[slot, t] = jnp.max(st, axis=0, keepdims=True)


def _values_stage(slot, vtts, s_ref, cm_ref, m_ref, acc_ref):
    ps, alphas = [], []
    for t in range(len(vtts)):
        m_prev = m_ref[t]
        m_new = jnp.maximum(m_prev, cm_ref[slot, t])
        alphas.append(jnp.exp2(m_prev - m_new))
        ps.append(jnp.exp2(s_ref[slot, t] - m_new).astype(jnp.bfloat16))
        m_ref[t] = m_new
    for t, (p, alpha) in enumerate(zip(ps, alphas)):
        acc_ref[t] = alpha * acc_ref[t] + _dot(vtts[t], p)


def _pipelined_tiles(nk, nk_max, logits, values):
    last = nk_max - 1
    logits(0, 0)

    def pair(jj, _):
        j = 2 * jj
        logits(jnp.minimum(j + 1, last), 1)
        values(j, 0)
        logits(jnp.minimum(j + 2, last), 0)
        values(j + 1, 1)
        return 0

    lax.fori_loop(0, lax.shift_right_logical(nk, 1), pair, 0)

    @pl.when((nk & 1) == 1)
    def _():
        values(nk - 1, 0)


def _finish_t(acc):
    return (acc[0:HEAD_DIM, :] / acc[HEAD_DIM:HEAD_DIM + 1, :]).T


def _flash_t_kernel(q_ref, k_ref, vt_ref, o_ref, m_ref, acc_ref, s_ref, cm_ref, d_ref,
                    *, bq, bk, nk_max, frame_causal, hg, dqk):
    i = pl.program_id(2)
    first = i * bq
    nk = _num_key_tiles(i, bq, bk, 0, nk_max)
    m_ref[...] = jnp.full(m_ref.shape, NEG_INF, jnp.float32)
    acc_ref[...] = jnp.zeros(acc_ref.shape, jnp.float32)
    r = lax.broadcasted_iota(jnp.int32, (bk, bq), 0)
    c = lax.broadcasted_iota(jnp.int32, (bk, bq), 1)
    d_ref[...] = (r - c) if frame_causal else ((r >> 6) - (c >> 6))

    def logits(j, slot):
        start = pl.multiple_of(j * bk, bk)
        gap = first - j * bk
        visible = gap if frame_causal else (gap >> 6)
        mask = d_ref[...] <= visible
        sts = []
        for t in range(hg):
            kt = k_ref[0, pl.ds(start, bk), t * dqk:(t + 1) * dqk]
            st = _dot_nt(kt, q_ref[0, :, t * dqk:(t + 1) * dqk])
            sts.append(jnp.where(mask, st, NEG_INF))
        _logits_stage(slot, sts, s_ref, cm_ref)

    def values(j, slot):
        _values_stage(slot, [vt_ref[0, j, t * VR:(t + 1) * VR, :] for t in range(hg)], s_ref, cm_ref, m_ref, acc_ref)

    _pipelined_tiles(nk, nk_max, logits, values)
    for t in range(hg):
        o_ref[0, :, t * HEAD_DIM:(t + 1) * HEAD_DIM] = _finish_t(acc_ref[t]).astype(o_ref.dtype)


def flash_attention_t(q, k, vt, *, lq, sp, heads, hg, dqk, bq, bk, frame_causal, name):
    nk_max = sp // bk
    return pl.pallas_call(
        functools.partial(_flash_t_kernel, bq=bq, bk=bk, nk_max=nk_max, frame_causal=frame_causal, hg=hg, dqk=dqk),
        grid=(1, heads // hg, lq // bq),
        in_specs=[pl.BlockSpec((1, bq, hg * dqk), lambda b_, g, i: (b_, i, g)),
                  _resident((1, sp, hg * dqk), lambda b_, g, i: (b_, 0, g)),
                  _resident((1, nk_max, hg * VR, bk), lambda b_, g, i: (b_, 0, g, 0))],
        out_specs=pl.BlockSpec((1, bq, hg * HEAD_DIM), lambda b_, g, i: (b_, i, g)),
        out_shape=jax.ShapeDtypeStruct((1, lq, heads * HEAD_DIM), jnp.bfloat16),
        scratch_shapes=[pltpu.VMEM((hg, 1, bq), jnp.float32), pltpu.VMEM((hg, VR, bq), jnp.float32),
                        pltpu.VMEM((2, hg, bk, bq), jnp.float32), pltpu.VMEM((2, hg, 1, bq), jnp.float32),
                        pltpu.VMEM((bk, bq), jnp.int32)],
        compiler_params=_cparams(("parallel", "parallel", "arbitrary")),
        name=name,
    )(q, k, vt)


def _dsa_t_kernel(iq_ref, iqs_ref, iwt_ref, dq_ref, ik_ref, dk_ref, dvt_ref, o_ref,
                  sc_ref, m_ref, acc_ref, s_ref, cm_ref, *, bq, bk, nk_max, topk):
    i = pl.program_id(1)
    nk = _num_key_tiles(i, bq, bk, 0, nk_max)
    qpos = i * bq + lax.broadcasted_iota(jnp.int32, (1, bq), 1)
    klim = ((qpos >> 6) + 1) << 6

    fold = lambda a, op: op(a.reshape(bk // _FOLD, _FOLD, bq), axis=0)

    def score_tile(j, carry):
        hi, lo = carry
        kt = ik_ref[0, pl.ds(pl.multiple_of(j * bk, bk), bk), :]
        acc = jnp.zeros((bk, bq), jnp.float32)
        for g in range(IDX_HEADS // 2):
            se = _dot_nt(kt, iq_ref[0, :, g * LANE:(g + 1) * LANE])
            so = _dot_nt(kt, iqs_ref[0, :, g * LANE:(g + 1) * LANE])
            acc = acc + iwt_ref[2 * g:2 * g + 1, :] * jnp.maximum(se, 0.0)
            acc = acc + iwt_ref[2 * g + 1:2 * g + 2, :] * jnp.maximum(so, 0.0)
        adm = (j * bk + lax.broadcasted_iota(jnp.int32, (bk, bq), 0)) < klim
        sc = jnp.where(adm, acc, -jnp.inf)
        sc_ref[j] = sc
        return (jnp.maximum(hi, fold(sc, jnp.max)), jnp.minimum(lo, fold(jnp.where(adm, acc, jnp.inf), jnp.min)))

    hi, lo = lax.fori_loop(0, nk, score_tile, (jnp.full((_FOLD, bq), -jnp.inf, jnp.float32),
                                               jnp.full((_FOLD, bq), jnp.inf, jnp.float32)))
    hi = jnp.max(hi, axis=0, keepdims=True)
    lo = jnp.min(lo, axis=0, keepdims=True)

    def count_ge(pivot):
        def body(j, acc):
            for r in range(bk // _FOLD):
                acc = acc + jnp.where(sc_ref[j, r * _FOLD:(r + 1) * _FOLD, :] >= pivot, 1.0, 0.0)
            return acc
        acc = lax.fori_loop(0, nk, body, jnp.zeros((_FOLD, bq), jnp.float32))
        return jnp.sum(acc, axis=0, keepdims=True)

    kf = jnp.float32(topk)
    hi = _key_to_f32(_f32_to_key(hi) + 1)
    c_lo = klim.astype(jnp.float32)
    c_hi = jnp.zeros((1, bq), jnp.float32)

    def settled(lo, hi, c_lo):
        width = lax.bitcast_convert_type(_f32_to_key(hi) - _f32_to_key(lo), jnp.uint32)
        return (c_lo <= kf) | (width <= 1)

    def search_cond(state):
        it, n_open = state[0], state[1]
        return (n_open > 0) & (it < _MAX_PROBES)

    def search_step(state):
        it, _, lo, hi, c_lo, c_hi = state
        done = settled(lo, hi, c_lo)
        k_lo, k_hi = _f32_to_key(lo), _f32_to_key(hi)
        width = lax.bitcast_convert_type(k_hi - k_lo, jnp.uint32)
        k_mid = k_lo + lax.bitcast_convert_type(width >> 1, jnp.int32)
        k_val = jnp.clip(_f32_to_key(lo + (hi - lo) * 0.5), k_lo + 1, k_hi - 1)
        probe = _key_to_f32(jnp.where((it & 7) == 7, k_mid, k_val))
        probe = jnp.where(done, lo, probe)
        c = count_ge(probe)
        up = (c >= kf) & ~done
        down = (c < kf) & ~done
        lo, c_lo = jnp.where(up, probe, lo), jnp.where(up, c, c_lo)
        hi, c_hi = jnp.where(down, probe, hi), jnp.where(down, c, c_hi)
        n_open = jnp.sum(jnp.where(settled(lo, hi, c_lo), 0, 1))
        return it + 1, n_open, lo, hi, c_lo, c_hi

    state = (jnp.int32(0), jnp.sum(jnp.where(settled(lo, hi, c_lo), 0, 1)), lo, hi, c_lo, c_hi)
    _, _, thr, _, c_lo, c_hi = lax.while_loop(search_cond, search_step, state)

    tie = c_lo > kf
    total = nk_max * bk

    @pl.when(jnp.sum(jnp.where(tie, 1, 0)) > 0)
    def _():
        need = kf - c_hi
        rows = lax.broadcasted_iota(jnp.int32, (bk, bq), 0)

        def count_equal_upto(j_max):
            def body(j, acc):
                hit = jnp.where(sc_ref[j] == thr, jnp.where(rows <= j_max - j * bk, 1.0, 0.0), 0.0)
                return acc + fold(hit, jnp.sum)
            acc = lax.fori_loop(0, nk, body, jnp.zeros((_FOLD, bq), jnp.float32))
            return jnp.sum(acc, axis=0, keepdims=True)

        def index_step(_, bracket):
            j_lo, j_hi = bracket
            mid = (j_lo + j_hi) >> 1
            ok = count_equal_upto(mid) >= need
            return jnp.where(ok, j_lo, mid), jnp.where(ok, mid, j_hi)

        bracket = (jnp.full((1, bq), -1, jnp.int32), jnp.full((1, bq), total - 1, jnp.int32))
        _, j_cut = lax.fori_loop(0, total.bit_length() + 1, index_step, bracket)
        j_cut = jnp.where(tie, j_cut, total)

        def drop_late_ties(j, _):
            t = sc_ref[j]
            sc_ref[j] = jnp.where(t == thr, jnp.where(rows > j_cut - j * bk, -jnp.inf, t), t)
            return 0

        lax.fori_loop(0, nk, drop_late_ties, 0)

    m_ref[...] = jnp.full(m_ref.shape, NEG_INF, jnp.float32)
    acc_ref[...] = jnp.zeros(acc_ref.shape, jnp.float32)

    def logits(j, slot):
        start = pl.multiple_of(j * bk, bk)
        drop = jnp.where(sc_ref[j] >= thr, 0.0, NEG_INF)
        sts = []
        for h in range(DSA_HEADS):
            g = h // DSA_GROUP
            kt = dk_ref[0, pl.ds(start, bk), g * LANE:(g + 1) * LANE]
            sts.append(_dot_nt(kt, dq_ref[0, :, h * LANE:(h + 1) * LANE]) + drop)
        _logits_stage(slot, sts, s_ref, cm_ref)

    def values(j, slot):
        vtts = [dvt_ref[0, j, (h // DSA_GROUP) * VR:(h // DSA_GROUP + 1) * VR, :] for h in range(DSA_HEADS)]
        _values_stage(slot, vtts, s_ref, cm_ref, m_ref, acc_ref)

    _pipelined_tiles(nk, nk_max, logits, values)
    for h in range(DSA_HEADS):
        o_ref[0, :, h * LANE:(h + 1) * LANE] = _finish_t(acc_ref[h]).astype(o_ref.dtype)


def dsa_attention_t(iq, iqs, iwt, dq, ik, dk, dvt, *, lq, sp, bq, bk, topk):
    nk_max = sp // bk
    qrow = lambda w: pl.BlockSpec((1, bq, w), lambda b_, i: (b_, i, 0))
    krow = lambda w: _resident((1, sp, w), lambda b_, i: (b_, 0, 0))
    return pl.pallas_call(
        functools.partial(_dsa_t_kernel, bq=bq, bk=bk, nk_max=nk_max, topk=topk),
        grid=(1, lq // bq),
        in_specs=[qrow(IDX_W), qrow(IDX_W), pl.BlockSpec((LANE, bq), lambda b_, i: (0, i)), qrow(DSA_W),
                  krow(LANE), krow(DSA_KV_W),
                  _resident((1, nk_max, DSA_KV_HEADS * VR, bk), lambda b_, i: (b_, 0, 0, 0))],
        out_specs=qrow(DSA_W),
        out_shape=jax.ShapeDtypeStruct((1, lq, DSA_W), jnp.bfloat16),
        scratch_shapes=[pltpu.VMEM((nk_max, bk, bq), jnp.float32),
                        pltpu.VMEM((DSA_HEADS, 1, bq), jnp.float32),
                        pltpu.VMEM((DSA_HEADS, VR, bq), jnp.float32),
                        pltpu.VMEM((2, DSA_HEADS, bk, bq), jnp.float32),
                        pltpu.VMEM((2, DSA_HEADS, 1, bq), jnp.float32)],
        compiler_params=_cparams(("parallel", "arbitrary")),
        name="dsa_attention_t",
    )(iq, iqs, iwt, dq, ik, dk, dvt)


def _out_proj_kernel(x_ref, a_ref, b_ref, c_ref, wa_ref, wb_ref, wc_ref, o_ref):
    o_ref[...] = (x_ref[...] + _dot(a_ref[...], wa_ref[...]) + _dot(b_ref[...], wb_ref[...])
                  + _dot(c_ref[...], wc_ref[...]))


def out_proj(x, a, b, c, w, layer, *, bm):
    m, d = x.shape
    wa, wb, wc = a.shape[1], b.shape[1], c.shape[1]
    assert wa == wb and (wa + wb) % wc == 0
    row = lambda w_: pl.BlockSpec((bm, w_), lambda i: (i, 0))
    band = lambda rows, blk: _resident((None, rows, d), lambda i: (layer, blk, 0))
    return pl.pallas_call(
        _out_proj_kernel,
        grid=(m // bm,),
        in_specs=[row(d), row(wa), row(wb), row(wc), band(wa, 0), band(wb, 1), band(wc, (wa + wb) // wc)],
        out_specs=row(d),
        out_shape=jax.ShapeDtypeStruct((m, d), jnp.float32),
        compiler_params=_cparams(("parallel",)),
        name="out_proj",
    )(x, a, b, c, w, w, w)


def _ffn_kernel(x_ref, g_ref, wg_ref, wu_ref, wd_ref, o_ref, xn_ref, acc_ref):
    j = pl.program_id(1)

    @pl.when(j == 0)
    def _():
        xn_ref[...] = _rms(x_ref[...], g_ref[...]).astype(jnp.bfloat16)
        acc_ref[...] = jnp.zeros_like(acc_ref)

    xn = xn_ref[...]
    gate = _dot(xn, wg_ref[...])
    up = _dot(xn, wu_ref[...])
    hidden = (gate * (1.0 / (1.0 + jnp.exp(-gate))) * up).astype(jnp.bfloat16)
    acc_ref[...] += _dot(hidden, wd_ref[...])

    @pl.when(j == pl.num_programs(1) - 1)
    def _():
        o_ref[...] = x_ref[...] + acc_ref[...]


def ffn(x, g, wg, wu, wd, layer, *, bm, bf):
    m, d = x.shape
    f = wg.shape[2]
    return pl.pallas_call(
        _ffn_kernel,
        grid=(m // bm, f // bf),
        in_specs=[pl.BlockSpec((bm, d), lambda i, j: (i, 0)),
                  pl.BlockSpec((1, d), lambda i, j: (0, 0)),
                  pl.BlockSpec((None, d, bf), lambda i, j: (layer, 0, j)),
                  pl.BlockSpec((None, d, bf), lambda i, j: (layer, 0, j)),
                  pl.BlockSpec((None, bf, d), lambda i, j: (layer, j, 0))],
        out_specs=pl.BlockSpec((bm, d), lambda i, j: (i, 0)),
        out_shape=jax.ShapeDtypeStruct((m, d), jnp.float32),
        scratch_shapes=[pltpu.VMEM((bm, d), jnp.bfloat16), pltpu.VMEM((bm, d), jnp.float32)],
        compiler_params=_cparams(("parallel", "arbitrary")),
        name="swiglu",
    )(x, g.reshape(1, d), wg, wu, wd)


def _final_norm_kernel(x_ref, g_ref, o_ref):
    o_ref[...] = _rms(x_ref[...], g_ref[...])


def final_norm(x, g, *, bm):
    m, d = x.shape
    return pl.pallas_call(
        _final_norm_kernel,
        grid=(m // bm,),
        in_specs=[pl.BlockSpec((bm, d), lambda i: (i, 0)), pl.BlockSpec((1, d), lambda i: (0, 0))],
        out_specs=pl.BlockSpec((bm, d), lambda i: (i, 0)),
        out_shape=jax.ShapeDtypeStruct((m, d), jnp.float32),
        compiler_params=_cparams(("parallel",)),
        name="final_norm",
    )(x, g.reshape(1, d))


def _pad_cols(a, width):
    return jnp.pad(a, [(0, 0)] * (a.ndim - 1) + [(0, width - a.shape[-1])])


def _layout_w_in(w_in):
    sizes = (FOX_W, FOX_W, FOX_W, FOX_HEADS, DSA_W, DSA_KV_W, DSA_KV_W, IDX_W, IDX_DIM, IDX_HEADS,
             MLA_Q_LORA, MLA_KV_LORA, MLA_ROPE)
    splits = np.cumsum(sizes)[:-1]
    q_a, k_a, v_a, f_a, q_b, k_b, v_b, q_i, k_i, w_i, c_q, c_kv, k_r = jnp.split(w_in, splits, axis=-1)
    parts = {"fq": q_a, "fk": k_a, "fv": v_a, "dq": q_b, "dk": k_b, "dv": v_b, "iq": q_i, "cq": c_q,
             "ckv": c_kv, "fa": f_a, "ik": k_i, "iw": w_i, "kr": k_r}
    cols = [_pad_cols(parts[n], w) for n, w in _SEGS]
    return _pad_cols(jnp.concatenate(cols, axis=-1), PROJ_W).astype(jnp.bfloat16)


def _layout_w_uq(w_uq):
    dp = w_uq.shape[0]
    w = w_uq.reshape(dp, MLA_Q_LORA, MLA_HEADS, MLA_NOPE + MLA_ROPE)
    return _pad_cols(w, MLA_QP).reshape(dp, MLA_Q_LORA, MLA_QW).astype(jnp.bfloat16)


def _layout_w_ukv(w_ukv):
    dp = w_ukv.shape[0]
    w = w_ukv.reshape(dp, MLA_KV_LORA, MLA_HEADS, MLA_NOPE + MLA_V)
    kn = w[..., :MLA_NOPE].reshape(dp, MLA_KV_LORA, MLA_HEADS * MLA_NOPE)
    vv = w[..., MLA_NOPE:].reshape(dp, MLA_KV_LORA, MLA_VW)
    return jnp.concatenate([kn, vv], axis=-1).astype(jnp.bfloat16)


def _rope_table(pos, rot, period):
    half = rot // 2
    inv_freq = ROPE_THETA ** (-jnp.arange(half, dtype=jnp.float32) / half)
    ang = pos.astype(jnp.float32)[:, None] * inv_freq[None, :]
    cos, sin = jnp.cos(ang), jnp.sin(ang)
    r = pos.shape[0]
    ones = jnp.ones((r, period - rot), jnp.float32)
    zeros = jnp.zeros((r, period - rot), jnp.float32)
    zh = jnp.zeros((r, half), jnp.float32)
    c = jnp.concatenate([cos, cos, ones], axis=1)
    s1 = jnp.concatenate([zh, sin, zeros], axis=1)
    s2 = jnp.concatenate([-sin, zh, zeros], axis=1)
    rep = LANE // period
    return jnp.concatenate([jnp.tile(c, (1, rep)), jnp.tile(s1, (1, rep)), jnp.tile(s2, (1, rep))], axis=1)


def _pick(n, candidates):
    for c in candidates:
        if n % c == 0:
            return c
    raise ValueError(f"no block size among {candidates} divides {n}")


def _pad_seq(a, sp):
    return jnp.pad(a, [(0, 0), (0, sp - a.shape[1])] + [(0, 0)] * (a.ndim - 2))


def kernel(x_prompt, x_sample, cache_fox_k, cache_fox_v, cache_fox_logf, cache_dsa_k, cache_dsa_v, cache_idx_k, cache_mla_ckv, cache_mla_krope, w_in, fox_bias, mla_q_norm, mla_w_uq, mla_kv_norm, mla_w_ukv, w_o, attn_norm, ffn_norm, w_gate, w_up, w_down, final_norm_w):
    bf = jnp.bfloat16
    depth = w_in.shape[0]
    nb, seq, d = x_prompt.shape
    db, dec, _ = x_sample.shape
    past = cache_fox_k.shape[2]
    assert nb == 1 and seq % CHUNK == 0 and dec % CHUNK == 0 and past % CHUNK == 0
    mp, ms = nb * seq, db * dec
    m = mp + ms
    s_len = past + dec

    bm = _pick(m, (512, 256, 128, 64))
    assert mp % bm == 0
    bq_p = _pick(seq, (256, 128))
    bk_p = bm
    bq_s = _pick(dec, (64,))
    bk_s = 256
    sp = ((s_len + bk_s - 1) // bk_s) * bk_s
    topk_p = min(IDX_TOPK_MAX, seq // 4)
    topk_s = min(IDX_TOPK_MAX, s_len // 4)
    fox_hg, mla_hg = 3, 2
    d_ff = w_gate.shape[2]
    bff = _pick(d_ff, (512, 256, 128))

    w_in_p = _layout_w_in(w_in)
    wuq_p = _layout_w_uq(mla_w_uq)
    wukv_p = _layout_w_ukv(mla_w_ukv)
    w_o_b = w_o.astype(bf)
    wg_b, wu_b, wd_b = w_gate.astype(bf), w_up.astype(bf), w_down.astype(bf)
    fox_bias_p = _pad_cols(fox_bias, LANE).reshape(depth, 1, LANE)

    pos = jnp.concatenate([jnp.tile(jnp.arange(seq, dtype=jnp.int32), nb),
                           jnp.tile(past + jnp.arange(dec, dtype=jnp.int32), db)])
    tab = jnp.concatenate([_rope_table(pos, PARTIAL_ROT, HEAD_DIM), _rope_table(pos, IDX_ROT, IDX_DIM),
                           _rope_table(pos, MLA_ROPE, LANE)], axis=1)

    x = jnp.concatenate([x_prompt.reshape(mp, d), x_sample.reshape(ms, d)], axis=0)
    new_rows = []
    for l in range(depth):
        proj = norm_matmul(x, attn_norm[l], w_in_p, l, bm=bm, bn=_pick(PROJ_W, (1536, 1024, 512)))
        (fq, fk, fv, fkb, fvb, fvt, lf, dq, dk, dv, dkb, dvb, dvt, iq, iqs, ik, ikb, iw, iwt, mq, ckv, kr) = post_proj(
            proj, tab, fox_bias_p[l], mla_q_norm[l].reshape(1, -1), mla_kv_norm[l].reshape(1, -1), wuq_p[l],
            bm=bm // 2, bk=bm)

        pr = lambda a: a[:mp].reshape(nb, seq, -1)
        sr = lambda a: a[mp:].reshape(db, dec, -1)
        new_rows.append((fk, fv, lf, dk, dv, ik, ckv, kr))

        cat = lambda c, new: _pad_seq(jnp.concatenate([c, new], axis=1), sp)
        s_fkb = cat(cache_fox_k[l].reshape(db, past, FOX_W).astype(bf), sr(fkb))
        s_fvb = cat(cache_fox_v[l].reshape(db, past, FOX_W).astype(bf), sr(fvb))
        s_lf = cat(_pad_cols(cache_fox_logf[l], LANE), sr(lf))
        s_dkb = cat(cache_dsa_k[l].reshape(db, past, DSA_KV_W).astype(bf), sr(dkb))
        s_dvb = cat(cache_dsa_v[l].reshape(db, past, DSA_KV_W).astype(bf), sr(dvb))
        s_ikb = cat(_pad_cols(cache_idx_k[l], LANE).astype(bf), sr(ikb))
        s_ckv = cat(cache_mla_ckv[l], sr(ckv))
        s_kr = cat(_pad_cols(cache_mla_krope[l], LANE), sr(kr))

        ckv_all = jnp.concatenate([ckv[:mp], s_ckv.reshape(db * sp, -1)], axis=0)
        kr_all = jnp.concatenate([kr[:mp], s_kr.reshape(db * sp, -1)], axis=0)
        kmla, vmla, vtmla = kv_up(ckv_all, kr_all, wukv_p[l], bm=bm)
        s_kmla, s_vmla = kmla[mp:].reshape(db, sp, -1), vmla[mp:].reshape(db, sp, -1)

        p_c, _ = cumsum_rows(pr(lf), bt=_pick(seq, (256, 128)))
        s_c, s_ct = cumsum_rows(s_lf, bt=_pick(sp, (256, 128)))
        ck_form = lambda ct, bk: ct[:, :FOX_HEADS].reshape(ct.shape[0], FOX_HEADS, ct.shape[2] // bk, 1, bk)

        al = lambda a: a.reshape(1, *a.shape)
        fqa, fka = fox_pack(fq, fkb, p_c[0], rows=mp, bm=bm)
        out_a = flash_attention_t(al(fqa), al(fka), al(fvt), lq=seq, sp=seq, heads=FOX_HEADS, hg=fox_hg, dqk=FOX_QP,
                                  bq=bq_p, bk=bk_p, frame_causal=True, name="fox_attention_t")
        out_b = dsa_attention_t(al(iq), al(iqs), iwt, al(dq), al(ikb), al(dkb), al(dvt), lq=seq, sp=seq,
                                bq=bq_p, bk=bk_p, topk=topk_p)
        out_c = flash_attention_t(al(mq), al(kmla), al(vtmla), lq=seq, sp=seq, heads=MLA_HEADS, hg=mla_hg, dqk=MLA_QP,
                                  bq=bq_p, bk=bk_p, frame_causal=False, name="mla_attention_t")
        s_out_a = flash_attention(sr(fq), s_fkb, s_fvb, s_c[:, past:past + dec], ck_form(s_ct, bk_s), lq=dec, sp=sp,
                                  heads=FOX_HEADS, hg=fox_hg, dqk=HEAD_DIM, dv=HEAD_DIM, bq=bq_s, bk=bk_s, q_off=past)
        s_out_b = dsa_attention(sr(iq), sr(iqs), sr(iw), sr(dq), s_ikb, s_dkb, s_dvb, lq=dec, sp=sp,
                                bq=bq_s, bk=bk_s, q_off=past, topk=topk_s)
        s_out_c = flash_attention(sr(mq), s_kmla, s_vmla, lq=dec, sp=sp, heads=MLA_HEADS, hg=mla_hg,
                                  dqk=MLA_QP, dv=MLA_V, bq=bq_s, bk=bk_s, q_off=past)

        rows = lambda p, s: jnp.concatenate([p.reshape(mp, -1), s.reshape(ms, -1)], axis=0)
        x = out_proj(x, rows(out_a, s_out_a), rows(out_b, s_out_b), rows(out_c, s_out_c), w_o_b, l, bm=bm)
        x = ffn(x, ffn_norm[l], wg_b, wu_b, wd_b, l, bm=bm, bf=bff)

    y = final_norm(x, final_norm_w, bm=bm)
    y_prompt = y[:mp].reshape(nb, seq, d)
    y_sample = y[mp:].reshape(db, dec, d)
    tails = ((FOX_HEADS, HEAD_DIM), (FOX_HEADS, HEAD_DIM), (FOX_HEADS,), (DSA_KV_HEADS, HEAD_DIM),
             (DSA_KV_HEADS, HEAD_DIM), (IDX_DIM,), (MLA_KV_LORA,), (MLA_ROPE,))
    p_out, s_out = [], []
    for arrays, tail in zip(zip(*new_rows), tails):
        w = math.prod(tail)
        p_out.append(jnp.stack([a[:mp, :w] for a in arrays]).reshape((depth, nb, seq) + tail))
        s_out.append(jnp.stack([a[mp:, :w] for a in arrays]).reshape((depth, db, dec) + tail))
    return (y_prompt, y_sample) + tuple(p_out) + tuple(s_out)
```

```python
import functools
import math

import jax
import jax.numpy as jnp
import numpy as np
from jax import lax
from jax.experimental import pallas as pl
from jax.experimental.pallas import tpu as pltpu

CHUNK = 64
HEAD_DIM = 128
FOX_HEADS = 6
DSA_HEADS = 6
DSA_KV_HEADS = 2
DSA_GROUP = DSA_HEADS // DSA_KV_HEADS
IDX_HEADS = 16
IDX_DIM = 64
IDX_TOPK_MAX = 256
MLA_HEADS = 4
MLA_Q_LORA = 512
MLA_KV_LORA = 256
MLA_NOPE = 128
MLA_ROPE = 64
MLA_V = 128
PARTIAL_ROT = HEAD_DIM // 4
IDX_ROT = IDX_DIM // 4
ROPE_THETA = 500000.0
EPS = 1e-6
NEG_INF = -1e30
FOX_SCALE = HEAD_DIM ** -0.5
DSA_SCALE = HEAD_DIM ** -0.5
MLA_SCALE = (MLA_NOPE + MLA_ROPE) ** -0.5
IDX_W_SCALE = (IDX_HEADS * IDX_DIM) ** -0.5
LOG2E = math.log2(math.e)

LANE = 128
VMEM_LIMIT = 56 * 1024 * 1024

FOX_W = FOX_HEADS * HEAD_DIM
DSA_W = DSA_HEADS * HEAD_DIM
DSA_KV_W = DSA_KV_HEADS * HEAD_DIM
IDX_W = IDX_HEADS * IDX_DIM
MLA_QP = 2 * LANE
MLA_QW = MLA_HEADS * MLA_QP
MLA_VW = MLA_HEADS * MLA_V
VR = HEAD_DIM + 16
FOX_QP = 2 * LANE

_SEGS = (("fq", FOX_W), ("fk", FOX_W), ("fv", FOX_W), ("dq", DSA_W), ("dk", DSA_KV_W), ("dv", DSA_KV_W),
         ("iq", IDX_W), ("cq", MLA_Q_LORA), ("ckv", MLA_KV_LORA),
         ("fa", LANE), ("ik", LANE), ("iw", LANE), ("kr", LANE))
_OFF = {}
_o = 0
for _n, _w in _SEGS:
    _OFF[_n] = _o
    _o += _w
PROJ_W = ((_o + 511) // 512) * 512

_NT = (((1,), (1,)), ((), ()))


def _dot(a, b):
    return jnp.dot(a, b, preferred_element_type=jnp.float32)


def _dot_nt(a, b):
    return lax.dot_general(a, b, _NT, preferred_element_type=jnp.float32)


def _cparams(sem):
    return pltpu.CompilerParams(dimension_semantics=sem, vmem_limit_bytes=VMEM_LIMIT)


def _resident(block_shape, index_map):
    return pl.BlockSpec(block_shape, index_map, pipeline_mode=pl.Buffered(1))


def _rms(x, g):
    return x * lax.rsqrt(jnp.mean(x * x, axis=-1, keepdims=True) + EPS) * g


def _norm_matmul_kernel(x_ref, g_ref, w_ref, o_ref, xn_ref):
    @pl.when(pl.program_id(1) == 0)
    def _():
        xn_ref[...] = _rms(x_ref[...], g_ref[...]).astype(jnp.bfloat16)

    o_ref[...] = _dot(xn_ref[...], w_ref[...])


def norm_matmul(x, g, w, layer, *, bm, bn):
    m, d = x.shape
    n = w.shape[2]
    return pl.pallas_call(
        _norm_matmul_kernel,
        grid=(m // bm, n // bn),
        in_specs=[pl.BlockSpec((bm, d), lambda i, j: (i, 0)),
                  pl.BlockSpec((1, d), lambda i, j: (0, 0)),
                  pl.BlockSpec((None, d, bn), lambda i, j: (layer, 0, j))],
        out_specs=pl.BlockSpec((bm, bn), lambda i, j: (i, j)),
        out_shape=jax.ShapeDtypeStruct((m, n), jnp.float32),
        scratch_shapes=[pltpu.VMEM((bm, d), jnp.bfloat16)],
        compiler_params=_cparams(("parallel", "arbitrary")),
        name="norm_matmul",
    )(x, g.reshape(1, d), w)


def _rope(x, tab, half):
    c, s1, s2 = tab[:, 0:LANE], tab[:, LANE:2 * LANE], tab[:, 2 * LANE:3 * LANE]
    return x * c + pltpu.roll(x, half, 1) * s1 + pltpu.roll(x, LANE - half, 1) * s2


def _value_t_tail(width):
    r = lax.broadcasted_iota(jnp.int32, (16, width), 0)
    return jnp.where(r == 0, 1.0, 0.0).astype(jnp.bfloat16)


def _store_value_t(o_ref, h, v):
    o_ref[0, h * VR:h * VR + HEAD_DIM, :] = v.T.astype(jnp.bfloat16)
    o_ref[0, h * VR + HEAD_DIM:(h + 1) * VR, :] = _value_t_tail(v.shape[0])


def _post_kernel(p_ref, tab_ref, fb_ref, qn_ref, kvn_ref, wuq_ref,
                 fq_o, fk_o, fv_o, fkb_o, fvb_o, fvt_o, lf_o,
                 dq_o, dk_o, dv_o, dkb_o, dvb_o, dvt_o,
                 iq_o, iqs_o, ik_o, ikb_o, iw_o, iwt_o,
                 mq_o, ckv_o, kr_o):
    bf = jnp.bfloat16
    seg = lambda name, w: p_ref[:, _OFF[name]:_OFF[name] + w]
    tab_d = tab_ref[:, 0:3 * LANE]
    tab_i = tab_ref[:, 3 * LANE:6 * LANE]
    tab_m = tab_ref[:, 6 * LANE:9 * LANE]

    fq_o[...] = (seg("fq", FOX_W) * (FOX_SCALE * LOG2E)).astype(bf)
    fk = seg("fk", FOX_W)
    fk_o[...] = fk
    fkb_o[...] = fk.astype(bf)
    fv = seg("fv", FOX_W)
    fv_o[...] = fv
    fvb_o[...] = fv.astype(bf)
    for h in range(FOX_HEADS):
        _store_value_t(fvt_o, h, p_ref[:, _OFF["fv"] + h * LANE:_OFF["fv"] + (h + 1) * LANE])
    z = seg("fa", LANE) + fb_ref[...]
    lf_o[...] = jnp.minimum(z, 0.0) - jnp.log1p(jnp.exp(-jnp.abs(z)))

    for h in range(DSA_HEADS):
        x = p_ref[:, _OFF["dq"] + h * LANE:_OFF["dq"] + (h + 1) * LANE]
        dq_o[:, h * LANE:(h + 1) * LANE] = (_rope(x, tab_d, PARTIAL_ROT // 2) * (DSA_SCALE * LOG2E)).astype(bf)
    for h in range(DSA_KV_HEADS):
        x = p_ref[:, _OFF["dk"] + h * LANE:_OFF["dk"] + (h + 1) * LANE]
        y = _rope(x, tab_d, PARTIAL_ROT // 2)
        dk_o[:, h * LANE:(h + 1) * LANE] = y
        dkb_o[:, h * LANE:(h + 1) * LANE] = y.astype(bf)
        _store_value_t(dvt_o, h, p_ref[:, _OFF["dv"] + h * LANE:_OFF["dv"] + (h + 1) * LANE])
    dv = seg("dv", DSA_KV_W)
    dv_o[...] = dv
    dvb_o[...] = dv.astype(bf)
    for j in range(IDX_W // LANE):
        x = p_ref[:, _OFF["iq"] + j * LANE:_OFF["iq"] + (j + 1) * LANE]
        y = _rope(x, tab_i, IDX_ROT // 2)
        iq_o[:, j * LANE:(j + 1) * LANE] = y.astype(bf)
        iqs_o[:, j * LANE:(j + 1) * LANE] = pltpu.roll(y, IDX_DIM, 1).astype(bf)
    y = _rope(seg("ik", LANE), tab_i, IDX_ROT // 2)
    ik_o[...] = y
    ikb_o[...] = y.astype(bf)
    iw = seg("iw", LANE) * IDX_W_SCALE
    iw_o[...] = iw
    iwt_o[...] = iw.T

    cqn = _rms(seg("cq", MLA_Q_LORA), qn_ref[...]).astype(bf)
    mq = _dot(cqn, wuq_ref[...])
    for h in range(MLA_HEADS):
        a = h * MLA_QP
        mq_o[:, a:a + LANE] = (mq[:, a:a + LANE] * (MLA_SCALE * LOG2E)).astype(bf)
        r = _rope(mq[:, a + LANE:a + 2 * LANE], tab_m, MLA_ROPE // 2)
        mq_o[:, a + LANE:a + 2 * LANE] = (r * (MLA_SCALE * LOG2E)).astype(bf)
    ckv_o[...] = _rms(seg("ckv", MLA_KV_LORA), kvn_ref[...])
    kr_o[...] = _rope(seg("kr", LANE), tab_m, MLA_ROPE // 2)


def post_proj(proj, tab, fox_bias_p, q_norm, kv_norm, wuq_p, *, bm, bk):
    m = proj.shape[0]
    r = bk // bm
    f32, bf = jnp.float32, jnp.bfloat16
    row = lambda w: pl.BlockSpec((bm, w), lambda i: (i, 0))
    full = lambda a: pl.BlockSpec(a.shape, lambda i: (0,) * a.ndim)
    vt = lambda heads: pl.BlockSpec((1, heads * VR, bm), lambda i: (i // r, 0, i % r))
    rowo = lambda w, dt: (row(w), jax.ShapeDtypeStruct((m, w), dt))
    vto = lambda heads: (vt(heads), jax.ShapeDtypeStruct((m // bk, heads * VR, bk), bf))
    outs = [rowo(FOX_W, bf), rowo(FOX_W, f32), rowo(FOX_W, f32), rowo(FOX_W, bf), rowo(FOX_W, bf), vto(FOX_HEADS),
            rowo(LANE, f32),
            rowo(DSA_W, bf), rowo(DSA_KV_W, f32), rowo(DSA_KV_W, f32), rowo(DSA_KV_W, bf), rowo(DSA_KV_W, bf),
            vto(DSA_KV_HEADS),
            rowo(IDX_W, bf), rowo(IDX_W, bf), rowo(LANE, f32), rowo(LANE, bf), rowo(LANE, f32),
            (pl.BlockSpec((LANE, bm), lambda i: (0, i)), jax.ShapeDtypeStruct((LANE, m), f32)),
            rowo(MLA_QW, bf), rowo(MLA_KV_LORA, f32), rowo(LANE, f32)]
    return pl.pallas_call(
        _post_kernel,
        grid=(m // bm,),
        in_specs=[row(PROJ_W), row(9 * LANE), full(fox_bias_p), full(q_norm), full(kv_norm), full(wuq_p)],
        out_specs=[s for s, _ in outs],
        out_shape=[o for _, o in outs],
        compiler_params=_cparams(("parallel",)),
        name="post_proj",
    )(proj, tab, fox_bias_p, q_norm, kv_norm, wuq_p)


def _kv_up_kernel(ckv_ref, kr_ref, w_ref, k_o, v_o, vt_o):
    bf = jnp.bfloat16
    kv = _dot(ckv_ref[...].astype(bf), w_ref[...])
    kr = kr_ref[...].astype(bf)
    for h in range(MLA_HEADS):
        k_o[:, h * MLA_QP:h * MLA_QP + LANE] = kv[:, h * LANE:(h + 1) * LANE].astype(bf)
        k_o[:, h * MLA_QP + LANE:(h + 1) * MLA_QP] = kr
        _store_value_t(vt_o, h, kv[:, (MLA_HEADS + h) * LANE:(MLA_HEADS + h + 1) * LANE])
    v_o[...] = kv[:, MLA_HEADS * LANE:].astype(bf)


def kv_up(ckv, kr, w, *, bm):
    r = ckv.shape[0]
    row = lambda w_: pl.BlockSpec((bm, w_), lambda i: (i, 0))
    return pl.pallas_call(
        _kv_up_kernel,
        grid=(r // bm,),
        in_specs=[row(MLA_KV_LORA), row(LANE), pl.BlockSpec(w.shape, lambda i: (0, 0))],
        out_specs=[row(MLA_QW), row(MLA_VW), pl.BlockSpec((1, MLA_HEADS * VR, bm), lambda i: (i, 0, 0))],
        out_shape=[jax.ShapeDtypeStruct((r, MLA_QW), jnp.bfloat16),
                   jax.ShapeDtypeStruct((r, MLA_VW), jnp.bfloat16),
                   jax.ShapeDtypeStruct((r // bm, MLA_HEADS * VR, bm), jnp.bfloat16)],
        compiler_params=_cparams(("parallel",)),
        name="mla_kv_up",
    )(ckv, kr, w)


def _cumsum_kernel(x_ref, c_o, ct_o, carry_ref, carry_t_ref):
    t = x_ref.shape[1]

    @pl.when(pl.program_id(1) == 0)
    def _():
        carry_ref[...] = jnp.zeros_like(carry_ref)
        carry_t_ref[...] = jnp.zeros_like(carry_t_ref)

    x = x_ref[0]
    r = lax.broadcasted_iota(jnp.int32, (t, t), 0)
    c = lax.broadcasted_iota(jnp.int32, (t, t), 1)
    lower = jnp.where(c <= r, 1.0, 0.0).astype(jnp.float32)
    upper = jnp.where(r <= c, 1.0, 0.0).astype(jnp.float32)
    cs = jnp.dot(lower, x, preferred_element_type=jnp.float32, precision=lax.Precision.HIGHEST)
    cs = cs + carry_ref[...]
    c_o[0] = cs * LOG2E
    carry_ref[...] = cs[t - 1:t, :]
    cst = jnp.dot(x.T, upper, preferred_element_type=jnp.float32, precision=lax.Precision.HIGHEST)
    cst = cst + carry_t_ref[...]
    ct_o[0] = cst[0:8, :] * LOG2E
    carry_t_ref[...] = cst[:, t - 1:t]


def cumsum_rows(x, *, bt):
    b, s, _ = x.shape
    return pl.pallas_call(
        _cumsum_kernel,
        grid=(b, s // bt),
        in_specs=[pl.BlockSpec((1, bt, LANE), lambda i, j: (i, j, 0))],
        out_specs=[pl.BlockSpec((1, bt, LANE), lambda i, j: (i, j, 0)),
                   pl.BlockSpec((1, 8, bt), lambda i, j: (i, 0, j))],
        out_shape=[jax.ShapeDtypeStruct((b, s, LANE), jnp.float32),
                   jax.ShapeDtypeStruct((b, 8, s), jnp.float32)],
        scratch_shapes=[pltpu.VMEM((1, LANE), jnp.float32), pltpu.VMEM((LANE, 1), jnp.float32)],
        compiler_params=_cparams(("parallel", "arbitrary")),
        name="logf_cumsum",
    )(x)


def _num_key_tiles(i, bq, bk, q_off, nk_max):
    last = q_off + (i + 1) * bq
    return jnp.minimum((last + bk - 1) // bk, nk_max)


def _softmax_steps(ss, vts, m_ref, l_ref, acc_ref):
    ps, alphas = [], []
    for t, s in enumerate(ss):
        m_prev = m_ref[t]
        m_new = jnp.maximum(m_prev, jnp.max(s, axis=1, keepdims=True))
        alpha = jnp.exp2(m_prev - m_new)
        p = jnp.exp2(s - m_new)
        l_ref[t] = alpha * l_ref[t] + jnp.sum(p, axis=1, keepdims=True)
        m_ref[t] = m_new
        ps.append(p.astype(jnp.bfloat16))
        alphas.append(alpha)
    for t, (p, alpha) in enumerate(zip(ps, alphas)):
        acc_ref[t] = alpha * acc_ref[t] + _dot(p, vts[t])


def _flash_kernel(*refs, bq, bk, q_off, nk_max, fox, hg, dqk, dv):
    if fox:
        q_ref, k_ref, v_ref, cq_ref, ck_ref, o_ref, m_ref, l_ref, acc_ref = refs
    else:
        q_ref, k_ref, v_ref, o_ref, m_ref, l_ref, acc_ref = refs
    gi = pl.program_id(1)
    i = pl.program_id(2)
    first = q_off + i * bq
    qpos = first + lax.broadcasted_iota(jnp.int32, (bq, 1), 0)
    if fox:
        lane = lax.broadcasted_iota(jnp.int32, (bq, LANE), 1)
        cq_all = cq_ref[0]
        cqs = [jnp.sum(jnp.where(lane == gi * hg + t, cq_all, 0.0), axis=1, keepdims=True) for t in range(hg)]
        c0 = [c[0:1, :] for c in cqs]
        bias_q = [c - z for c, z in zip(cqs, c0)]
    nk = _num_key_tiles(i, bq, bk, q_off, nk_max)
    n_full = jnp.minimum((first + (1 if fox else CHUNK)) // bk, nk)

    m_ref[...] = jnp.full(m_ref.shape, NEG_INF, jnp.float32)
    l_ref[...] = jnp.zeros(l_ref.shape, jnp.float32)
    acc_ref[...] = jnp.zeros(acc_ref.shape, jnp.float32)

    def make_body(masked):
        def body(j, _):
            start = pl.multiple_of(j * bk, bk)
            if masked:
                kpos = j * bk + lax.broadcasted_iota(jnp.int32, (1, bk), 1)
                mask = (kpos <= qpos) if fox else ((kpos >> 6) <= (qpos >> 6))
            ss = []
            for t in range(hg):
                kt = k_ref[0, pl.ds(start, bk), t * dqk:(t + 1) * dqk]
                s = _dot_nt(q_ref[0, :, t * dqk:(t + 1) * dqk], kt)
                if fox:
                    s = s + (bias_q[t] - (ck_ref[0, t, j] - c0[t]))
                ss.append(jnp.where(mask, s, NEG_INF) if masked else s)
            vts = [v_ref[0, pl.ds(start, bk), t * dv:(t + 1) * dv] for t in range(hg)]
            _softmax_steps(ss, vts, m_ref, l_ref, acc_ref)
            return 0
        return body

    lax.fori_loop(0, n_full, make_body(False), 0)
    lax.fori_loop(n_full, nk, make_body(True), 0)
    for t in range(hg):
        o_ref[0, :, t * dv:(t + 1) * dv] = (acc_ref[t] / l_ref[t]).astype(o_ref.dtype)


def flash_attention(q, k, v, cq=None, ck=None, *, lq, sp, heads, hg, dqk, dv, bq, bk, q_off):
    b = q.shape[0]
    nk_max = sp // bk
    fox = cq is not None
    in_specs = [pl.BlockSpec((1, bq, hg * dqk), lambda b_, g, i: (b_, i, g)),
                pl.BlockSpec((1, sp, hg * dqk), lambda b_, g, i: (b_, 0, g)),
                pl.BlockSpec((1, sp, hg * dv), lambda b_, g, i: (b_, 0, g))]
    args = [q, k, v]
    if fox:
        in_specs += [pl.BlockSpec((1, bq, LANE), lambda b_, g, i: (b_, i, 0)),
                     pl.BlockSpec((1, hg, nk_max, 1, bk), lambda b_, g, i: (b_, g, 0, 0, 0))]
        args += [cq, ck]
    return pl.pallas_call(
        functools.partial(_flash_kernel, bq=bq, bk=bk, q_off=q_off, nk_max=nk_max, fox=fox, hg=hg, dqk=dqk, dv=dv),
        grid=(b, heads // hg, lq // bq),
        in_specs=in_specs,
        out_specs=pl.BlockSpec((1, bq, hg * dv), lambda b_, g, i: (b_, i, g)),
        out_shape=jax.ShapeDtypeStruct((b, lq, heads * dv), jnp.bfloat16),
        scratch_shapes=[pltpu.VMEM((hg, bq, 1), jnp.float32), pltpu.VMEM((hg, bq, 1), jnp.float32),
                        pltpu.VMEM((hg, bq, dv), jnp.float32)],
        compiler_params=_cparams(("parallel", "parallel", "arbitrary")),
        name="fox_attention" if fox else "mla_attention",
    )(*args)


_KEY_NEG_INF = -2139095041
_F32_LOWEST = -3.4028234663852886e38


_FOLD = 64
_MAX_PROBES = 8 * 34


def _key_to_f32(key):
    bits = jnp.where(key >= 0, key, key ^ jnp.int32(0x7FFFFFFF))
    return lax.bitcast_convert_type(bits, jnp.float32)


def _f32_to_key(x):
    bits = lax.bitcast_convert_type(x, jnp.int32)
    return jnp.where(bits >= 0, bits, bits ^ jnp.int32(0x7FFFFFFF))


def _dsa_kernel(iq_ref, iqs_ref, iw_ref, dq_ref, ik_ref, dk_ref, dv_ref, o_ref,
                sc_ref, qg_ref, m_ref, l_ref, acc_ref, *, bq, bk, q_off, nk_max, topk):
    i = pl.program_id(1)
    nk = _num_key_tiles(i, bq, bk, q_off, nk_max)
    qpos = q_off + i * bq + lax.broadcasted_iota(jnp.int32, (bq, 1), 0)
    klim = ((qpos >> 6) + 1) << 6
    iw = iw_ref[0]

    def score_tile(j, _):
        kt = ik_ref[0, pl.ds(pl.multiple_of(j * bk, bk), bk), :]
        acc = jnp.zeros((bq, bk), jnp.float32)
        for g in range(IDX_HEADS // 2):
            se = _dot_nt(iq_ref[0, :, g * LANE:(g + 1) * LANE], kt)
            so = _dot_nt(iqs_ref[0, :, g * LANE:(g + 1) * LANE], kt)
            acc = acc + iw[:, 2 * g:2 * g + 1] * jnp.maximum(se, 0.0)
            acc = acc + iw[:, 2 * g + 1:2 * g + 2] * jnp.maximum(so, 0.0)
        kpos = j * bk + lax.broadcasted_iota(jnp.int32, (1, bk), 1)
        sc_ref[j] = jnp.where(kpos < klim, acc, -jnp.inf)
        return 0

    lax.fori_loop(0, nk, score_tile, 0)

    def count(hit):
        def body(j, acc):
            h = hit(sc_ref[j], j)
            for c in range(bk // LANE):
                acc = acc + h[:, c * LANE:(c + 1) * LANE]
            return acc
        acc = lax.fori_loop(0, nk, body, jnp.zeros((bq, LANE), jnp.float32))
        return jnp.sum(acc, axis=1, keepdims=True)

    count_ge = lambda pivot: count(lambda t, j: jnp.where(t >= pivot, 1.0, 0.0))
    kf = jnp.float32(topk)
    nonneg = count_ge(jnp.zeros((bq, 1), jnp.float32)) >= kf
    key0 = jnp.where(nonneg, jnp.int32(0), jnp.int32(-2 ** 31))

    def bit_step(it, key):
        cand = key | (jnp.int32(1) << (30 - it))
        ok = (count_ge(_key_to_f32(cand)) >= kf) | (cand <= _KEY_NEG_INF)
        return jnp.where(ok, cand, key)

    key = lax.fori_loop(0, 31, bit_step, key0)
    thr = jnp.maximum(_key_to_f32(key), _F32_LOWEST)

    tie = count_ge(thr) > kf
    total = nk_max * bk

    @pl.when(jnp.sum(jnp.where(tie, 1, 0)) > 0)
    def _():
        need = kf - count(lambda t, j: jnp.where(t > thr, 1.0, 0.0))
        cols = lax.broadcasted_iota(jnp.int32, (bq, bk), 1)

        def count_equal_upto(j_max):
            return count(lambda t, j: jnp.where(t == thr, jnp.where(cols <= j_max - j * bk, 1.0, 0.0), 0.0))

        def index_step(_, bracket):
            j_lo, j_hi = bracket
            mid = (j_lo + j_hi) >> 1
            ok = count_equal_upto(mid) >= need
            return jnp.where(ok, j_lo, mid), jnp.where(ok, mid, j_hi)

        bracket = (jnp.full((bq, 1), -1, jnp.int32), jnp.full((bq, 1), total - 1, jnp.int32))
        _, j_cut = lax.fori_loop(0, total.bit_length() + 1, index_step, bracket)
        j_cut = jnp.where(tie, j_cut, total)

        def drop_late_ties(j, _):
            t = sc_ref[j]
            sc_ref[j] = jnp.where(t == thr, jnp.where(cols > j_cut - j * bk, -jnp.inf, t), t)
            return 0

        lax.fori_loop(0, nk, drop_late_ties, 0)

    for h in range(DSA_HEADS):
        g, r = divmod(h, DSA_GROUP)
        qg_ref[g, r * bq:(r + 1) * bq, :] = dq_ref[0, :, h * LANE:(h + 1) * LANE]
    m_ref[...] = jnp.full(m_ref.shape, NEG_INF, jnp.float32)
    l_ref[...] = jnp.zeros(l_ref.shape, jnp.float32)
    acc_ref[...] = jnp.zeros(acc_ref.shape, jnp.float32)

    def att_tile(j, _):
        start = pl.multiple_of(j * bk, bk)
        drop = jnp.where(sc_ref[j] >= thr, 0.0, NEG_INF)
        drop = jnp.concatenate([drop] * DSA_GROUP, axis=0)
        ss = [_dot_nt(qg_ref[g], dk_ref[0, pl.ds(start, bk), g * LANE:(g + 1) * LANE]) + drop
              for g in range(DSA_KV_HEADS)]
        vts = [dv_ref[0, pl.ds(start, bk), g * LANE:(g + 1) * LANE] for g in range(DSA_KV_HEADS)]
        _softmax_steps(ss, vts, m_ref, l_ref, acc_ref)
        return 0

    lax.fori_loop(0, nk, att_tile, 0)
    for h in range(DSA_HEADS):
        g, r = divmod(h, DSA_GROUP)
        rows = slice(r * bq, (r + 1) * bq)
        o_ref[0, :, h * LANE:(h + 1) * LANE] = (acc_ref[g, rows, :] / l_ref[g, rows, :]).astype(o_ref.dtype)


def dsa_attention(iq, iqs, iw, dq, ik, dk, dv, *, lq, sp, bq, bk, q_off, topk):
    b = dq.shape[0]
    nk_max = sp // bk
    qrow = lambda w: pl.BlockSpec((1, bq, w), lambda b_, i: (b_, i, 0))
    krow = lambda w: pl.BlockSpec((1, sp, w), lambda b_, i: (b_, 0, 0))
    gr = DSA_GROUP * bq
    return pl.pallas_call(
        functools.partial(_dsa_kernel, bq=bq, bk=bk, q_off=q_off, nk_max=nk_max, topk=topk),
        grid=(b, lq // bq),
        in_specs=[qrow(IDX_W), qrow(IDX_W), qrow(LANE), qrow(DSA_W), krow(LANE), krow(DSA_KV_W), krow(DSA_KV_W)],
        out_specs=qrow(DSA_W),
        out_shape=jax.ShapeDtypeStruct((b, lq, DSA_W), jnp.bfloat16),
        scratch_shapes=[pltpu.VMEM((nk_max, bq, bk), jnp.float32),
                        pltpu.VMEM((DSA_KV_HEADS, gr, LANE), jnp.bfloat16),
                        pltpu.VMEM((DSA_KV_HEADS, gr, 1), jnp.float32),
                        pltpu.VMEM((DSA_KV_HEADS, gr, 1), jnp.float32),
                        pltpu.VMEM((DSA_KV_HEADS, gr, LANE), jnp.float32)],
        compiler_params=_cparams(("parallel", "arbitrary")),
        name="dsa_attention",
    )(iq, iqs, iw, dq, ik, dk, dv)


def _split3(x):
    hi = x.astype(jnp.bfloat16).astype(jnp.float32)
    r = x - hi
    mid = r.astype(jnp.bfloat16).astype(jnp.float32)
    lo = (r - mid).astype(jnp.bfloat16).astype(jnp.float32)
    return hi, mid, lo


def _fox_pack_kernel(q_ref, k_ref, c_ref, qa_o, ka_o):
    rows = q_ref.shape[0]
    lane = lax.broadcasted_iota(jnp.int32, (rows, LANE), 1)
    c = c_ref[...]
    zero = jnp.zeros((rows, LANE), jnp.float32)
    for h in range(FOX_HEADS):
        ch = jnp.sum(jnp.where(lane == h, c, 0.0), axis=1, keepdims=True)
        hi, mid, lo = _split3(ch)
        terms = jnp.where(lane == 0, hi, jnp.where(lane == 1, mid, jnp.where(lane == 2, lo, zero)))
        ones_hi = jnp.where((lane >= 3) & (lane < 6), 1.0, 0.0)
        qa_o[:, h * FOX_QP:h * FOX_QP + LANE] = q_ref[:, h * LANE:(h + 1) * LANE]
        qa_o[:, h * FOX_QP + LANE:(h + 1) * FOX_QP] = (terms + ones_hi).astype(jnp.bfloat16)
        ones_lo = jnp.where(lane < 3, 1.0, 0.0)
        ka_o[:, h * FOX_QP:h * FOX_QP + LANE] = k_ref[:, h * LANE:(h + 1) * LANE]
        ka_o[:, h * FOX_QP + LANE:(h + 1) * FOX_QP] = (ones_lo - pltpu.roll(terms, 3, 1)).astype(jnp.bfloat16)


def fox_pack(q, k, c, *, rows, bm):
    row = lambda w: pl.BlockSpec((bm, w), lambda i: (i, 0))
    return pl.pallas_call(
        _fox_pack_kernel,
        grid=(rows // bm,),
        in_specs=[row(FOX_W), row(FOX_W), row(LANE)],
        out_specs=[row(FOX_HEADS * FOX_QP), row(FOX_HEADS * FOX_QP)],
        out_shape=[jax.ShapeDtypeStruct((rows, FOX_HEADS * FOX_QP), jnp.bfloat16)] * 2,
        compiler_params=_cparams(("parallel",)),
        name="fox_pack",
    )(q, k, c)


def _logits_stage(slot, sts, s_ref, cm_ref):
    for t, st in enumerate(sts):
        s_ref[slot, t] = st
        cm_ref[slot, t] = jnp.max(st, axis=0, keepdims=True)


def _values_stage(slot, vtts, s_ref, cm_ref, m_ref, acc_ref):
    ps, alphas = [], []
    for t in range(len(vtts)):
        m_prev = m_ref[t]
        m_new = jnp.maximum(m_prev, cm_ref[slot, t])
        alphas.append(jnp.exp2(m_prev - m_new))
        ps.append(jnp.exp2(s_ref[slot, t] - m_new).astype(jnp.bfloat16))
        m_ref[t] = m_new
    for t, (p, alpha) in enumerate(zip(ps, alphas)):
        acc_ref[t] = alpha * acc_ref[t] + _dot(vtts[t], p)


def _pipelined_tiles(nk, nk_max, logits, values):
    last = nk_max - 1
    logits(0, 0)

    def run(first, tiles):
        for u in range(tiles):
            logits(jnp.minimum(first + u + 1, last), (u + 1) % 2)
            values(first + u, u % 2)

    def quad(jj, _):
        run(4 * jj, 4)
        return 0

    def pair(jj, _):
        run(2 * jj, 2)
        return 0

    n_quads = lax.shift_right_logical(nk, 2)
    lax.fori_loop(0, n_quads, quad, 0)
    lax.fori_loop(2 * n_quads, lax.shift_right_logical(nk, 1), pair, 0)

    @pl.when((nk & 1) == 1)
    def _():
        values(nk - 1, 0)


def _finish_t(acc):
    return (acc[0:HEAD_DIM, :] / acc[HEAD_DIM:HEAD_DIM + 1, :]).T


def _flash_t_kernel(q_ref, k_ref, vt_ref, o_ref, m_ref, acc_ref, s_ref, cm_ref, d_ref,
                    *, bq, bk, nk_max, frame_causal, hg, dqk):
    i = pl.program_id(2)
    first = i * bq
    nk = _num_key_tiles(i, bq, bk, 0, nk_max)
    m_ref[...] = jnp.full(m_ref.shape, NEG_INF, jnp.float32)
    acc_ref[...] = jnp.zeros(acc_ref.shape, jnp.float32)
    r = lax.broadcasted_iota(jnp.int32, (bk, bq), 0)
    c = lax.broadcasted_iota(jnp.int32, (bk, bq), 1)
    d_ref[...] = (r - c) if frame_causal else ((r >> 6) - (c >> 6))

    def logits(j, slot):
        start = pl.multiple_of(j * bk, bk)
        gap = first - j * bk
        visible = gap if frame_causal else (gap >> 6)
        mask = d_ref[...] <= visible
        sts = []
        for t in range(hg):
            kt = k_ref[0, pl.ds(start, bk), t * dqk:(t + 1) * dqk]
            st = _dot_nt(kt, q_ref[0, :, t * dqk:(t + 1) * dqk])
            sts.append(jnp.where(mask, st, NEG_INF))
        _logits_stage(slot, sts, s_ref, cm_ref)

    def values(j, slot):
        _values_stage(slot, [vt_ref[0, j, t * VR:(t + 1) * VR, :] for t in range(hg)], s_ref, cm_ref, m_ref, acc_ref)

    _pipelined_tiles(nk, nk_max, logits, values)
    for t in range(hg):
        o_ref[0, :, t * HEAD_DIM:(t + 1) * HEAD_DIM] = _finish_t(acc_ref[t]).astype(o_ref.dtype)


def flash_attention_t(q, k, vt, *, lq, sp, heads, hg, dqk, bq, bk, frame_causal, name):
    nk_max = sp // bk
    return pl.pallas_call(
        functools.partial(_flash_t_kernel, bq=bq, bk=bk, nk_max=nk_max, frame_causal=frame_causal, hg=hg, dqk=dqk),
        grid=(1, heads // hg, lq // bq),
        in_specs=[pl.BlockSpec((1, bq, hg * dqk), lambda b_, g, i: (b_, i, g)),
                  _resident((1, sp, hg * dqk), lambda b_, g, i: (b_, 0, g)),
                  _resident((1, nk_max, hg * VR, bk), lambda b_, g, i: (b_, 0, g, 0))],
        out_specs=pl.BlockSpec((1, bq, hg * HEAD_DIM), lambda b_, g, i: (b_, i, g)),
        out_shape=jax.ShapeDtypeStruct((1, lq, heads * HEAD_DIM), jnp.bfloat16),
        scratch_shapes=[pltpu.VMEM((hg, 1, bq), jnp.float32), pltpu.VMEM((hg, VR, bq), jnp.float32),
                        pltpu.VMEM((2, hg, bk, bq), jnp.float32), pltpu.VMEM((2, hg, 1, bq), jnp.float32),
                        pltpu.VMEM((bk, bq), jnp.int32)],
        compiler_params=_cparams(("parallel", "parallel", "arbitrary")),
        name=name,
    )(q, k, vt)


def _dsa_t_kernel(iq_ref, iqs_ref, iwt_ref, dq_ref, ik_ref, dk_ref, dvt_ref, o_ref,
                  sc_ref, m_ref, acc_ref, s_ref, cm_ref, *, bq, bk, nk_max, topk):
    i = pl.program_id(1)
    nk = _num_key_tiles(i, bq, bk, 0, nk_max)
    qpos = i * bq + lax.broadcasted_iota(jnp.int32, (1, bq), 1)
    klim = ((qpos >> 6) + 1) << 6

    fold = lambda a, op: op(a.reshape(bk // _FOLD, _FOLD, bq), axis=0)

    def score_tile(j, carry):
        hi, lo = carry
        kt = ik_ref[0, pl.ds(pl.multiple_of(j * bk, bk), bk), :]
        acc = jnp.zeros((bk, bq), jnp.float32)
        for g in range(IDX_HEADS // 2):
            se = _dot_nt(kt, iq_ref[0, :, g * LANE:(g + 1) * LANE])
            so = _dot_nt(kt, iqs_ref[0, :, g * LANE:(g + 1) * LANE])
            acc = acc + iwt_ref[2 * g:2 * g + 1, :] * jnp.maximum(se, 0.0)
            acc = acc + iwt_ref[2 * g + 1:2 * g + 2, :] * jnp.maximum(so, 0.0)
        adm = (j * bk + lax.broadcasted_iota(jnp.int32, (bk, bq), 0)) < klim
        sc = jnp.where(adm, acc, -jnp.inf)
        sc_ref[j] = sc
        return (jnp.maximum(hi, fold(sc, jnp.max)), jnp.minimum(lo, fold(jnp.where(adm, acc, jnp.inf), jnp.min)))

    n_pairs = lax.shift_right_logical(nk, 1)
    carry = lax.fori_loop(0, n_pairs, lambda jj, c: score_tile(2 * jj + 1, score_tile(2 * jj, c)),
                          (jnp.full((_FOLD, bq), -jnp.inf, jnp.float32), jnp.full((_FOLD, bq), jnp.inf, jnp.float32)))
    hi, lo = lax.fori_loop(2 * n_pairs, nk, score_tile, carry)
    hi = jnp.max(hi, axis=0, keepdims=True)
    lo = jnp.min(lo, axis=0, keepdims=True)

    def count_ge(pivot):
        def body(j, acc):
            for r in range(bk // _FOLD):
                acc = acc + jnp.where(sc_ref[j, r * _FOLD:(r + 1) * _FOLD, :] >= pivot, 1.0, 0.0)
            return acc
        acc = lax.fori_loop(0, nk, body, jnp.zeros((_FOLD, bq), jnp.float32))
        return jnp.sum(acc, axis=0, keepdims=True)

    kf = jnp.float32(topk)
    hi = _key_to_f32(_f32_to_key(hi) + 1)
    c_lo = klim.astype(jnp.float32)
    c_hi = jnp.zeros((1, bq), jnp.float32)

    def settled(lo, hi, c_lo):
        width = lax.bitcast_convert_type(_f32_to_key(hi) - _f32_to_key(lo), jnp.uint32)
        return (c_lo <= kf) | (width <= 1)

    def search_cond(state):
        it, n_open = state[0], state[1]
        return (n_open > 0) & (it < _MAX_PROBES)

    def search_step(state):
        it, _, lo, hi, c_lo, c_hi = state
        done = settled(lo, hi, c_lo)
        k_lo, k_hi = _f32_to_key(lo), _f32_to_key(hi)
        width = lax.bitcast_convert_type(k_hi - k_lo, jnp.uint32)
        k_mid = k_lo + lax.bitcast_convert_type(width >> 1, jnp.int32)
        k_val = jnp.clip(_f32_to_key(lo + (hi - lo) * 0.5), k_lo + 1, k_hi - 1)
        probe = _key_to_f32(jnp.where((it & 7) == 7, k_mid, k_val))
        probe = jnp.where(done, lo, probe)
        c = count_ge(probe)
        up = (c >= kf) & ~done
        down = (c < kf) & ~done
        lo, c_lo = jnp.where(up, probe, lo), jnp.where(up, c, c_lo)
        hi, c_hi = jnp.where(down, probe, hi), jnp.where(down, c, c_hi)
        n_open = jnp.sum(jnp.where(settled(lo, hi, c_lo), 0, 1))
        return it + 1, n_open, lo, hi, c_lo, c_hi

    state = (jnp.int32(0), jnp.sum(jnp.where(settled(lo, hi, c_lo), 0, 1)), lo, hi, c_lo, c_hi)
    _, _, thr, _, c_lo, c_hi = lax.while_loop(search_cond, search_step, state)

    tie = c_lo > kf
    total = nk_max * bk

    @pl.when(jnp.sum(jnp.where(tie, 1, 0)) > 0)
    def _():
        need = kf - c_hi
        rows = lax.broadcasted_iota(jnp.int32, (bk, bq), 0)

        def count_equal_upto(j_max):
            def body(j, acc):
                hit = jnp.where(sc_ref[j] == thr, jnp.where(rows <= j_max - j * bk, 1.0, 0.0), 0.0)
                return acc + fold(hit, jnp.sum)
            acc = lax.fori_loop(0, nk, body, jnp.zeros((_FOLD, bq), jnp.float32))
            return jnp.sum(acc, axis=0, keepdims=True)

        def index_step(_, bracket):
            j_lo, j_hi = bracket
            mid = (j_lo + j_hi) >> 1
            ok = count_equal_upto(mid) >= need
            return jnp.where(ok, j_lo, mid), jnp.where(ok, mid, j_hi)

        bracket = (jnp.full((1, bq), -1, jnp.int32), jnp.full((1, bq), total - 1, jnp.int32))
        _, j_cut = lax.fori_loop(0, total.bit_length() + 1, index_step, bracket)
        j_cut = jnp.where(tie, j_cut, total)

        def drop_late_ties(j, _):
            t = sc_ref[j]
            sc_ref[j] = jnp.where(t == thr, jnp.where(rows > j_cut - j * bk, -jnp.inf, t), t)
            return 0

        lax.fori_loop(0, nk, drop_late_ties, 0)

    m_ref[...] = jnp.full(m_ref.shape, NEG_INF, jnp.float32)
    acc_ref[...] = jnp.zeros(acc_ref.shape, jnp.float32)

    def logits(j, slot):
        start = pl.multiple_of(j * bk, bk)
        drop = jnp.where(sc_ref[j] >= thr, 0.0, NEG_INF)
        sts = []
        for h in range(DSA_HEADS):
            g = h // DSA_GROUP
            kt = dk_ref[0, pl.ds(start, bk), g * LANE:(g + 1) * LANE]
            sts.append(_dot_nt(kt, dq_ref[0, :, h * LANE:(h + 1) * LANE]) + drop)
        _logits_stage(slot, sts, s_ref, cm_ref)

    def values(j, slot):
        vtts = [dvt_ref[0, j, (h // DSA_GROUP) * VR:(h // DSA_GROUP + 1) * VR, :] for h in range(DSA_HEADS)]
        _values_stage(slot, vtts, s_ref, cm_ref, m_ref, acc_ref)

    _pipelined_tiles(nk, nk_max, logits, values)
    for h in range(DSA_HEADS):
        o_ref[0, :, h * LANE:(h + 1) * LANE] = _finish_t(acc_ref[h]).astype(o_ref.dtype)


def dsa_attention_t(iq, iqs, iwt, dq, ik, dk, dvt, *, lq, sp, bq, bk, topk):
    nk_max = sp // bk
    qrow = lambda w: pl.BlockSpec((1, bq, w), lambda b_, i: (b_, i, 0))
    krow = lambda w: _resident((1, sp, w), lambda b_, i: (b_, 0, 0))
    return pl.pallas_call(
        functools.partial(_dsa_t_kernel, bq=bq, bk=bk, nk_max=nk_max, topk=topk),
        grid=(1, lq // bq),
        in_specs=[qrow(IDX_W), qrow(IDX_W), pl.BlockSpec((LANE, bq), lambda b_, i: (0, i)), qrow(DSA_W),
                  krow(LANE), krow(DSA_KV_W),
                  _resident((1, nk_max, DSA_KV_HEADS * VR, bk), lambda b_, i: (b_, 0, 0, 0))],
        out_specs=qrow(DSA_W),
        out_shape=jax.ShapeDtypeStruct((1, lq, DSA_W), jnp.bfloat16),
        scratch_shapes=[pltpu.VMEM((nk_max, bk, bq), jnp.float32),
                        pltpu.VMEM((DSA_HEADS, 1, bq), jnp.float32),
                        pltpu.VMEM((DSA_HEADS, VR, bq), jnp.float32),
                        pltpu.VMEM((2, DSA_HEADS, bk, bq), jnp.float32),
                        pltpu.VMEM((2, DSA_HEADS, 1, bq), jnp.float32)],
        compiler_params=_cparams(("parallel", "arbitrary")),
        name="dsa_attention_t",
    )(iq, iqs, iwt, dq, ik, dk, dvt)


def _out_proj_kernel(x_ref, a_ref, b_ref, c_ref, wa_ref, wb_ref, wc_ref, o_ref):
    o_ref[...] = (x_ref[...] + _dot(a_ref[...], wa_ref[...]) + _dot(b_ref[...], wb_ref[...])
                  + _dot(c_ref[...], wc_ref[...]))


def out_proj(x, a, b, c, w, layer, *, bm):
    m, d = x.shape
    wa, wb, wc = a.shape[1], b.shape[1], c.shape[1]
    assert wa == wb and (wa + wb) % wc == 0
    row = lambda w_: pl.BlockSpec((bm, w_), lambda i: (i, 0))
    band = lambda rows, blk: _resident((None, rows, d), lambda i: (layer, blk, 0))
    return pl.pallas_call(
        _out_proj_kernel,
        grid=(m // bm,),
        in_specs=[row(d), row(wa), row(wb), row(wc), band(wa, 0), band(wb, 1), band(wc, (wa + wb) // wc)],
        out_specs=row(d),
        out_shape=jax.ShapeDtypeStruct((m, d), jnp.float32),
        compiler_params=_cparams(("parallel",)),
        name="out_proj",
    )(x, a, b, c, w, w, w)


def _ffn_kernel(x_ref, g_ref, wg_ref, wu_ref, wd_ref, o_ref, xn_ref, acc_ref):
    j = pl.program_id(1)

    @pl.when(j == 0)
    def _():
        xn_ref[...] = _rms(x_ref[...], g_ref[...]).astype(jnp.bfloat16)
        acc_ref[...] = jnp.zeros_like(acc_ref)

    xn = xn_ref[...]
    gate = _dot(xn, wg_ref[...])
    up = _dot(xn, wu_ref[...])
    hidden = (gate * (1.0 / (1.0 + jnp.exp(-gate))) * up).astype(jnp.bfloat16)
    acc_ref[...] += _dot(hidden, wd_ref[...])

    @pl.when(j == pl.num_programs(1) - 1)
    def _():
        o_ref[...] = x_ref[...] + acc_ref[...]


def ffn(x, g, wg, wu, wd, layer, *, bm, bf):
    m, d = x.shape
    f = wg.shape[2]
    return pl.pallas_call(
        _ffn_kernel,
        grid=(m // bm, f // bf),
        in_specs=[pl.BlockSpec((bm, d), lambda i, j: (i, 0)),
                  pl.BlockSpec((1, d), lambda i, j: (0, 0)),
                  pl.BlockSpec((None, d, bf), lambda i, j: (layer, 0, j)),
                  pl.BlockSpec((None, d, bf), lambda i, j: (layer, 0, j)),
                  pl.BlockSpec((None, bf, d), lambda i, j: (layer, j, 0))],
        out_specs=pl.BlockSpec((bm, d), lambda i, j: (i, 0)),
        out_shape=jax.ShapeDtypeStruct((m, d), jnp.float32),
        scratch_shapes=[pltpu.VMEM((bm, d), jnp.bfloat16), pltpu.VMEM((bm, d), jnp.float32)],
        compiler_params=_cparams(("parallel", "arbitrary")),
        name="swiglu",
    )(x, g.reshape(1, d), wg, wu, wd)


def _final_norm_kernel(x_ref, g_ref, o_ref):
    o_ref[...] = _rms(x_ref[...], g_ref[...])


def final_norm(x, g, *, bm):
    m, d = x.shape
    return pl.pallas_call(
        _final_norm_kernel,
        grid=(m // bm,),
        in_specs=[pl.BlockSpec((bm, d), lambda i: (i, 0)), pl.BlockSpec((1, d), lambda i: (0, 0))],
        out_specs=pl.BlockSpec((bm, d), lambda i: (i, 0)),
        out_shape=jax.ShapeDtypeStruct((m, d), jnp.float32),
        compiler_params=_cparams(("parallel",)),
        name="final_norm",
    )(x, g.reshape(1, d))


def _pad_cols(a, width):
    return jnp.pad(a, [(0, 0)] * (a.ndim - 1) + [(0, width - a.shape[-1])])


def _layout_w_in(w_in):
    sizes = (FOX_W, FOX_W, FOX_W, FOX_HEADS, DSA_W, DSA_KV_W, DSA_KV_W, IDX_W, IDX_DIM, IDX_HEADS,
             MLA_Q_LORA, MLA_KV_LORA, MLA_ROPE)
    splits = np.cumsum(sizes)[:-1]
    q_a, k_a, v_a, f_a, q_b, k_b, v_b, q_i, k_i, w_i, c_q, c_kv, k_r = jnp.split(w_in, splits, axis=-1)
    parts = {"fq": q_a, "fk": k_a, "fv": v_a, "dq": q_b, "dk": k_b, "dv": v_b, "iq": q_i, "cq": c_q,
             "ckv": c_kv, "fa": f_a, "ik": k_i, "iw": w_i, "kr": k_r}
    cols = [_pad_cols(parts[n], w) for n, w in _SEGS]
    return _pad_cols(jnp.concatenate(cols, axis=-1), PROJ_W).astype(jnp.bfloat16)


def _layout_w_uq(w_uq):
    dp = w_uq.shape[0]
    w = w_uq.reshape(dp, MLA_Q_LORA, MLA_HEADS, MLA_NOPE + MLA_ROPE)
    return _pad_cols(w, MLA_QP).reshape(dp, MLA_Q_LORA, MLA_QW).astype(jnp.bfloat16)


def _layout_w_ukv(w_ukv):
    dp = w_ukv.shape[0]
    w = w_ukv.reshape(dp, MLA_KV_LORA, MLA_HEADS, MLA_NOPE + MLA_V)
    kn = w[..., :MLA_NOPE].reshape(dp, MLA_KV_LORA, MLA_HEADS * MLA_NOPE)
    vv = w[..., MLA_NOPE:].reshape(dp, MLA_KV_LORA, MLA_VW)
    return jnp.concatenate([kn, vv], axis=-1).astype(jnp.bfloat16)


def _rope_table(pos, rot, period):
    half = rot // 2
    inv_freq = ROPE_THETA ** (-jnp.arange(half, dtype=jnp.float32) / half)
    ang = pos.astype(jnp.float32)[:, None] * inv_freq[None, :]
    cos, sin = jnp.cos(ang), jnp.sin(ang)
    r = pos.shape[0]
    ones = jnp.ones((r, period - rot), jnp.float32)
    zeros = jnp.zeros((r, period - rot), jnp.float32)
    zh = jnp.zeros((r, half), jnp.float32)
    c = jnp.concatenate([cos, cos, ones], axis=1)
    s1 = jnp.concatenate([zh, sin, zeros], axis=1)
    s2 = jnp.concatenate([-sin, zh, zeros], axis=1)
    rep = LANE // period
    return jnp.concatenate([jnp.tile(c, (1, rep)), jnp.tile(s1, (1, rep)), jnp.tile(s2, (1, rep))], axis=1)


def _pick(n, candidates):
    for c in candidates:
        if n % c == 0:
            return c
    raise ValueError(f"no block size among {candidates} divides {n}")


def _pad_seq(a, sp):
    return jnp.pad(a, [(0, 0), (0, sp - a.shape[1])] + [(0, 0)] * (a.ndim - 2))


def kernel(x_prompt, x_sample, cache_fox_k, cache_fox_v, cache_fox_logf, cache_dsa_k, cache_dsa_v, cache_idx_k, cache_mla_ckv, cache_mla_krope, w_in, fox_bias, mla_q_norm, mla_w_uq, mla_kv_norm, mla_w_ukv, w_o, attn_norm, ffn_norm, w_gate, w_up, w_down, final_norm_w):
    bf = jnp.bfloat16
    depth = w_in.shape[0]
    nb, seq, d = x_prompt.shape
    db, dec, _ = x_sample.shape
    past = cache_fox_k.shape[2]
    assert nb == 1 and seq % CHUNK == 0 and dec % CHUNK == 0 and past % CHUNK == 0
    mp, ms = nb * seq, db * dec
    m = mp + ms
    s_len = past + dec

    bm = _pick(m, (512, 256, 128, 64))
    assert mp % bm == 0
    bq_p = _pick(seq, (256, 128))
    bk_p = bm
    bq_s = _pick(dec, (64,))
    bk_s = 256
    sp = ((s_len + bk_s - 1) // bk_s) * bk_s
    topk_p = min(IDX_TOPK_MAX, seq // 4)
    topk_s = min(IDX_TOPK_MAX, s_len // 4)
    fox_hg, mla_hg = 3, 2
    d_ff = w_gate.shape[2]
    bff = _pick(d_ff, (512, 256, 128))

    w_in_p = _layout_w_in(w_in)
    wuq_p = _layout_w_uq(mla_w_uq)
    wukv_p = _layout_w_ukv(mla_w_ukv)
    w_o_b = w_o.astype(bf)
    wg_b, wu_b, wd_b = w_gate.astype(bf), w_up.astype(bf), w_down.astype(bf)
    fox_bias_p = _pad_cols(fox_bias, LANE).reshape(depth, 1, LANE)

    pos = jnp.concatenate([jnp.tile(jnp.arange(seq, dtype=jnp.int32), nb),
                           jnp.tile(past + jnp.arange(dec, dtype=jnp.int32), db)])
    tab = jnp.concatenate([_rope_table(pos, PARTIAL_ROT, HEAD_DIM), _rope_table(pos, IDX_ROT, IDX_DIM),
                           _rope_table(pos, MLA_ROPE, LANE)], axis=1)

    x = jnp.concatenate([x_prompt.reshape(mp, d), x_sample.reshape(ms, d)], axis=0)
    new_rows = []
    for l in range(depth):
        proj = norm_matmul(x, attn_norm[l], w_in_p, l, bm=bm, bn=_pick(PROJ_W, (1536, 1024, 512)))
        (fq, fk, fv, fkb, fvb, fvt, lf, dq, dk, dv, dkb, dvb, dvt, iq, iqs, ik, ikb, iw, iwt, mq, ckv, kr) = post_proj(
            proj, tab, fox_bias_p[l], mla_q_norm[l].reshape(1, -1), mla_kv_norm[l].reshape(1, -1), wuq_p[l],
            bm=bm // 2, bk=bm)

        pr = lambda a: a[:mp].reshape(nb, seq, -1)
        sr = lambda a: a[mp:].reshape(db, dec, -1)
        new_rows.append((fk, fv, lf, dk, dv, ik, ckv, kr))

        cat = lambda c, new: _pad_seq(jnp.concatenate([c, new], axis=1), sp)
        s_fkb = cat(cache_fox_k[l].reshape(db, past, FOX_W).astype(bf), sr(fkb))
        s_fvb = cat(cache_fox_v[l].reshape(db, past, FOX_W).astype(bf), sr(fvb))
        s_lf = cat(_pad_cols(cache_fox_logf[l], LANE), sr(lf))
        s_dkb = cat(cache_dsa_k[l].reshape(db, past, DSA_KV_W).astype(bf), sr(dkb))
        s_dvb = cat(cache_dsa_v[l].reshape(db, past, DSA_KV_W).astype(bf), sr(dvb))
        s_ikb = cat(_pad_cols(cache_idx_k[l], LANE).astype(bf), sr(ikb))
        s_ckv = cat(cache_mla_ckv[l], sr(ckv))
        s_kr = cat(_pad_cols(cache_mla_krope[l], LANE), sr(kr))

        ckv_all = jnp.concatenate([ckv[:mp], s_ckv.reshape(db * sp, -1)], axis=0)
        kr_all = jnp.concatenate([kr[:mp], s_kr.reshape(db * sp, -1)], axis=0)
        kmla, vmla, vtmla = kv_up(ckv_all, kr_all, wukv_p[l], bm=bm)
        s_kmla, s_vmla = kmla[mp:].reshape(db, sp, -1), vmla[mp:].reshape(db, sp, -1)

        p_c, _ = cumsum_rows(pr(lf), bt=_pick(seq, (256, 128)))
        s_c, s_ct = cumsum_rows(s_lf, bt=_pick(sp, (256, 128)))
        ck_form = lambda ct, bk: ct[:, :FOX_HEADS].reshape(ct.shape[0], FOX_HEADS, ct.shape[2] // bk, 1, bk)

        al = lambda a: a.reshape(1, *a.shape)
        fqa, fka = fox_pack(fq, fkb, p_c[0], rows=mp, bm=bm)
        out_a = flash_attention_t(al(fqa), al(fka), al(fvt), lq=seq, sp=seq, heads=FOX_HEADS, hg=fox_hg, dqk=FOX_QP,
                                  bq=bq_p, bk=bk_p, frame_causal=True, name="fox_attention_t")
        out_b = dsa_attention_t(al(iq), al(iqs), iwt, al(dq), al(ikb), al(dkb), al(dvt), lq=seq, sp=seq,
                                bq=bq_p, bk=bk_p, topk=topk_p)
        out_c = flash_attention_t(al(mq), al(kmla), al(vtmla), lq=seq, sp=seq, heads=MLA_HEADS, hg=mla_hg, dqk=MLA_QP,
                                  bq=bq_p, bk=bk_p, frame_causal=False, name="mla_attention_t")
        s_out_a = flash_attention(sr(fq), s_fkb, s_fvb, s_c[:, past:past + dec], ck_form(s_ct, bk_s), lq=dec, sp=sp,
                                  heads=FOX_HEADS, hg=fox_hg, dqk=HEAD_DIM, dv=HEAD_DIM, bq=bq_s, bk=bk_s, q_off=past)
        s_out_b = dsa_attention(sr(iq), sr(iqs), sr(iw), sr(dq), s_ikb, s_dkb, s_dvb, lq=dec, sp=sp,
                                bq=bq_s, bk=bk_s, q_off=past, topk=topk_s)
        s_out_c = flash_attention(sr(mq), s_kmla, s_vmla, lq=dec, sp=sp, heads=MLA_HEADS, hg=mla_hg,
                                  dqk=MLA_QP, dv=MLA_V, bq=bq_s, bk=bk_s, q_off=past)

        rows = lambda p, s: jnp.concatenate([p.reshape(mp, -1), s.reshape(ms, -1)], axis=0)
        x = out_proj(x, rows(out_a, s_out_a), rows(out_b, s_out_b), rows(out_c, s_out_c), w_o_b, l, bm=bm)
        x = ffn(x, ffn_norm[l], wg_b, wu_b, wd_b, l, bm=bm, bf=bff)

    y = final_norm(x, final_norm_w, bm=bm)
    y_prompt = y[:mp].reshape(nb, seq, d)
    y_sample = y[mp:].reshape(db, dec, d)
    tails = ((FOX_HEADS, HEAD_DIM), (FOX_HEADS, HEAD_DIM), (FOX_HEADS,), (DSA_KV_HEADS, HEAD_DIM),
             (DSA_KV_HEADS, HEAD_DIM), (IDX_DIM,), (MLA_KV_LORA,), (MLA_ROPE,))
    p_out, s_out = [], []
    for arrays, tail in zip(zip(*new_rows), tails):
        w = math.prod(tail)
        p_out.append(jnp.stack([a[:mp, :w] for a in arrays]).reshape((depth, nb, seq) + tail))
        s_out.append(jnp.stack([a[mp:, :w] for a in arrays]).reshape((depth, db, dec) + tail))
    return (y_prompt, y_sample) + tuple(p_out) + tuple(s_out)
```

```python
import functools
import math

import jax
import jax.numpy as jnp
import numpy as np
from jax import lax
from jax.experimental import pallas as pl
from jax.experimental.pallas import tpu as pltpu

CHUNK = 64
HEAD_DIM = 128
FOX_HEADS = 6
DSA_HEADS = 6
DSA_KV_HEADS = 2
DSA_GROUP = DSA_HEADS // DSA_KV_HEADS
IDX_HEADS = 16
IDX_DIM = 64
IDX_TOPK_MAX = 256
MLA_HEADS = 4
MLA_Q_LORA = 512
MLA_KV_LORA = 256
MLA_NOPE = 128
MLA_ROPE = 64
MLA_V = 128
PARTIAL_ROT = HEAD_DIM // 4
IDX_ROT = IDX_DIM // 4
ROPE_THETA = 500000.0
EPS = 1e-6
NEG_INF = -1e30
FOX_SCALE = HEAD_DIM ** -0.5
DSA_SCALE = HEAD_DIM ** -0.5
MLA_SCALE = (MLA_NOPE + MLA_ROPE) ** -0.5
IDX_W_SCALE = (IDX_HEADS * IDX_DIM) ** -0.5
LOG2E = math.log2(math.e)

LANE = 128
VMEM_LIMIT = 56 * 1024 * 1024

FOX_W = FOX_HEADS * HEAD_DIM
DSA_W = DSA_HEADS * HEAD_DIM
DSA_KV_W = DSA_KV_HEADS * HEAD_DIM
IDX_W = IDX_HEADS * IDX_DIM
MLA_QP = 2 * LANE
MLA_QW = MLA_HEADS * MLA_QP
MLA_VW = MLA_HEADS * MLA_V
VR = HEAD_DIM + 16
FOX_QP = 2 * LANE

_SEGS = (("fq", FOX_W), ("fk", FOX_W), ("fv", FOX_W), ("dq", DSA_W), ("dk", DSA_KV_W), ("dv", DSA_KV_W),
         ("iq", IDX_W), ("cq", MLA_Q_LORA), ("ckv", MLA_KV_LORA),
         ("fa", LANE), ("ik", LANE), ("iw", LANE), ("kr", LANE))
_OFF = {}
_o = 0
for _n, _w in _SEGS:
    _OFF[_n] = _o
    _o += _w
PROJ_W = ((_o + 511) // 512) * 512

_NT = (((1,), (1,)), ((), ()))


def _dot(a, b):
    return jnp.dot(a, b, preferred_element_type=jnp.float32)


def _dot_nt(a, b):
    return lax.dot_general(a, b, _NT, preferred_element_type=jnp.float32)


def _cparams(sem):
    return pltpu.CompilerParams(dimension_semantics=sem, vmem_limit_bytes=VMEM_LIMIT)


def _resident(block_shape, index_map):
    return pl.BlockSpec(block_shape, index_map, pipeline_mode=pl.Buffered(1))


def _rms(x, g):
    return x * lax.rsqrt(jnp.mean(x * x, axis=-1, keepdims=True) + EPS) * g


def _norm_matmul_kernel(x_ref, g_ref, w_ref, o_ref, xn_ref):
    @pl.when(pl.program_id(1) == 0)
    def _():
        xn_ref[...] = _rms(x_ref[...], g_ref[...]).astype(jnp.bfloat16)

    o_ref[...] = _dot(xn_ref[...], w_ref[...])


def norm_matmul(x, g, w, layer, *, bm, bn):
    m, d = x.shape
    n = w.shape[2]
    return pl.pallas_call(
        _norm_matmul_kernel,
        grid=(m // bm, n // bn),
        in_specs=[pl.BlockSpec((bm, d), lambda i, j: (i, 0)),
                  pl.BlockSpec((1, d), lambda i, j: (0, 0)),
                  pl.BlockSpec((None, d, bn), lambda i, j: (layer, 0, j))],
        out_specs=pl.BlockSpec((bm, bn), lambda i, j: (i, j)),
        out_shape=jax.ShapeDtypeStruct((m, n), jnp.float32),
        scratch_shapes=[pltpu.VMEM((bm, d), jnp.bfloat16)],
        compiler_params=_cparams(("parallel", "arbitrary")),
        name="norm_matmul",
    )(x, g.reshape(1, d), w)


def _rope(x, tab, half):
    c, s1, s2 = tab[:, 0:LANE], tab[:, LANE:2 * LANE], tab[:, 2 * LANE:3 * LANE]
    return x * c + pltpu.roll(x, half, 1) * s1 + pltpu.roll(x, LANE - half, 1) * s2


def _value_t_tail(width):
    r = lax.broadcasted_iota(jnp.int32, (16, width), 0)
    return jnp.where(r == 0, 1.0, 0.0).astype(jnp.bfloat16)


def _store_value_t(o_ref, h, v):
    o_ref[0, h * VR:h * VR + HEAD_DIM, :] = v.T.astype(jnp.bfloat16)
    o_ref[0, h * VR + HEAD_DIM:(h + 1) * VR, :] = _value_t_tail(v.shape[0])


def _post_kernel(p_ref, tab_ref, fb_ref, qn_ref, kvn_ref, wuq_ref,
                 fq_o, fk_o, fv_o, fkb_o, fvb_o, fvt_o, lf_o,
                 dq_o, dk_o, dv_o, dkb_o, dvb_o, dvt_o,
                 iq_o, iqs_o, ik_o, ikb_o, iw_o, iwt_o,
                 mq_o, ckv_o, kr_o):
    bf = jnp.bfloat16
    seg = lambda name, w: p_ref[:, _OFF[name]:_OFF[name] + w]
    tab_d = tab_ref[:, 0:3 * LANE]
    tab_i = tab_ref[:, 3 * LANE:6 * LANE]
    tab_m = tab_ref[:, 6 * LANE:9 * LANE]

    fq_o[...] = (seg("fq", FOX_W) * (FOX_SCALE * LOG2E)).astype(bf)
    fk = seg("fk", FOX_W)
    fk_o[...] = fk
    fkb_o[...] = fk.astype(bf)
    fv = seg("fv", FOX_W)
    fv_o[...] = fv
    fvb_o[...] = fv.astype(bf)
    for h in range(FOX_HEADS):
        _store_value_t(fvt_o, h, p_ref[:, _OFF["fv"] + h * LANE:_OFF["fv"] + (h + 1) * LANE])
    z = seg("fa", LANE) + fb_ref[...]
    lf_o[...] = jnp.minimum(z, 0.0) - jnp.log1p(jnp.exp(-jnp.abs(z)))

    for h in range(DSA_HEADS):
        x = p_ref[:, _OFF["dq"] + h * LANE:_OFF["dq"] + (h + 1) * LANE]
        dq_o[:, h * LANE:(h + 1) * LANE] = (_rope(x, tab_d, PARTIAL_ROT // 2) * (DSA_SCALE * LOG2E)).astype(bf)
    for h in range(DSA_KV_HEADS):
        x = p_ref[:, _OFF["dk"] + h * LANE:_OFF["dk"] + (h + 1) * LANE]
        y = _rope(x, tab_d, PARTIAL_ROT // 2)
        dk_o[:, h * LANE:(h + 1) * LANE] = y
        dkb_o[:, h * LANE:(h + 1) * LANE] = y.astype(bf)
        _store_value_t(dvt_o, h, p_ref[:, _OFF["dv"] + h * LANE:_OFF["dv"] + (h + 1) * LANE])
    dv = seg("dv", DSA_KV_W)
    dv_o[...] = dv
    dvb_o[...] = dv.astype(bf)
    for j in range(IDX_W // LANE):
        x = p_ref[:, _OFF["iq"] + j * LANE:_OFF["iq"] + (j + 1) * LANE]
        y = _rope(x, tab_i, IDX_ROT // 2)
        iq_o[:, j * LANE:(j + 1) * LANE] = y.astype(bf)
        iqs_o[:, j * LANE:(j + 1) * LANE] = pltpu.roll(y, IDX_DIM, 1).astype(bf)
    y = _rope(seg("ik", LANE), tab_i, IDX_ROT // 2)
    ik_o[...] = y
    ikb_o[...] = y.astype(bf)
    iw = seg("iw", LANE) * IDX_W_SCALE
    iw_o[...] = iw
    iwt_o[...] = iw.T

    cqn = _rms(seg("cq", MLA_Q_LORA), qn_ref[...]).astype(bf)
    mq = _dot(cqn, wuq_ref[...])
    for h in range(MLA_HEADS):
        a = h * MLA_QP
        mq_o[:, a:a + LANE] = (mq[:, a:a + LANE] * (MLA_SCALE * LOG2E)).astype(bf)
        r = _rope(mq[:, a + LANE:a + 2 * LANE], tab_m, MLA_ROPE // 2)
        mq_o[:, a + LANE:a + 2 * LANE] = (r * (MLA_SCALE * LOG2E)).astype(bf)
    ckv_o[...] = _rms(seg("ckv", MLA_KV_LORA), kvn_ref[...])
    kr_o[...] = _rope(seg("kr", LANE), tab_m, MLA_ROPE // 2)


def post_proj(proj, tab, fox_bias_p, q_norm, kv_norm, wuq_p, *, bm, bk):
    m = proj.shape[0]
    r = bk // bm
    f32, bf = jnp.float32, jnp.bfloat16
    row = lambda w: pl.BlockSpec((bm, w), lambda i: (i, 0))
    full = lambda a: pl.BlockSpec(a.shape, lambda i: (0,) * a.ndim)
    vt = lambda heads: pl.BlockSpec((1, heads * VR, bm), lambda i: (i // r, 0, i % r))
    rowo = lambda w, dt: (row(w), jax.ShapeDtypeStruct((m, w), dt))
    vto = lambda heads: (vt(heads), jax.ShapeDtypeStruct((m // bk, heads * VR, bk), bf))
    outs = [rowo(FOX_W, bf), rowo(FOX_W, f32), rowo(FOX_W, f32), rowo(FOX_W, bf), rowo(FOX_W, bf), vto(FOX_HEADS),
            rowo(LANE, f32),
            rowo(DSA_W, bf), rowo(DSA_KV_W, f32), rowo(DSA_KV_W, f32), rowo(DSA_KV_W, bf), rowo(DSA_KV_W, bf),
            vto(DSA_KV_HEADS),
            rowo(IDX_W, bf), rowo(IDX_W, bf), rowo(LANE, f32), rowo(LANE, bf), rowo(LANE, f32),
            (pl.BlockSpec((LANE, bm), lambda i: (0, i)), jax.ShapeDtypeStruct((LANE, m), f32)),
            rowo(MLA_QW, bf), rowo(MLA_KV_LORA, f32), rowo(LANE, f32)]
    return pl.pallas_call(
        _post_kernel,
        grid=(m // bm,),
        in_specs=[row(PROJ_W), row(9 * LANE), full(fox_bias_p), full(q_norm), full(kv_norm), full(wuq_p)],
        out_specs=[s for s, _ in outs],
        out_shape=[o for _, o in outs],
        compiler_params=_cparams(("parallel",)),
        name="post_proj",
    )(proj, tab, fox_bias_p, q_norm, kv_norm, wuq_p)


def _kv_up_kernel(ckv_ref, kr_ref, w_ref, k_o, v_o, vt_o):
    bf = jnp.bfloat16
    kv = _dot(ckv_ref[...].astype(bf), w_ref[...])
    kr = kr_ref[...].astype(bf)
    for h in range(MLA_HEADS):
        k_o[:, h * MLA_QP:h * MLA_QP + LANE] = kv[:, h * LANE:(h + 1) * LANE].astype(bf)
        k_o[:, h * MLA_QP + LANE:(h + 1) * MLA_QP] = kr
        _store_value_t(vt_o, h, kv[:, (MLA_HEADS + h) * LANE:(MLA_HEADS + h + 1) * LANE])
    v_o[...] = kv[:, MLA_HEADS * LANE:].astype(bf)


def kv_up(ckv, kr, w, *, bm):
    r = ckv.shape[0]
    row = lambda w_: pl.BlockSpec((bm, w_), lambda i: (i, 0))
    return pl.pallas_call(
        _kv_up_kernel,
        grid=(r // bm,),
        in_specs=[row(MLA_KV_LORA), row(LANE), pl.BlockSpec(w.shape, lambda i: (0, 0))],
        out_specs=[row(MLA_QW), row(MLA_VW), pl.BlockSpec((1, MLA_HEADS * VR, bm), lambda i: (i, 0, 0))],
        out_shape=[jax.ShapeDtypeStruct((r, MLA_QW), jnp.bfloat16),
                   jax.ShapeDtypeStruct((r, MLA_VW), jnp.bfloat16),
                   jax.ShapeDtypeStruct((r // bm, MLA_HEADS * VR, bm), jnp.bfloat16)],
        compiler_params=_cparams(("parallel",)),
        name="mla_kv_up",
    )(ckv, kr, w)


def _cumsum_kernel(x_ref, c_o, ct_o, carry_ref, carry_t_ref):
    t = x_ref.shape[1]

    @pl.when(pl.program_id(1) == 0)
    def _():
        carry_ref[...] = jnp.zeros_like(carry_ref)
        carry_t_ref[...] = jnp.zeros_like(carry_t_ref)

    x = x_ref[0]
    r = lax.broadcasted_iota(jnp.int32, (t, t), 0)
    c = lax.broadcasted_iota(jnp.int32, (t, t), 1)
    lower = jnp.where(c <= r, 1.0, 0.0).astype(jnp.float32)
    upper = jnp.where(r <= c, 1.0, 0.0).astype(jnp.float32)
    cs = jnp.dot(lower, x, preferred_element_type=jnp.float32, precision=lax.Precision.HIGHEST)
    cs = cs + carry_ref[...]
    c_o[0] = cs * LOG2E
    carry_ref[...] = cs[t - 1:t, :]
    cst = jnp.dot(x.T, upper, preferred_element_type=jnp.float32, precision=lax.Precision.HIGHEST)
    cst = cst + carry_t_ref[...]
    ct_o[0] = cst[0:8, :] * LOG2E
    carry_t_ref[...] = cst[:, t - 1:t]


def cumsum_rows(x, *, bt):
    b, s, _ = x.shape
    return pl.pallas_call(
        _cumsum_kernel,
        grid=(b, s // bt),
        in_specs=[pl.BlockSpec((1, bt, LANE), lambda i, j: (i, j, 0))],
        out_specs=[pl.BlockSpec((1, bt, LANE), lambda i, j: (i, j, 0)),
                   pl.BlockSpec((1, 8, bt), lambda i, j: (i, 0, j))],
        out_shape=[jax.ShapeDtypeStruct((b, s, LANE), jnp.float32),
                   jax.ShapeDtypeStruct((b, 8, s), jnp.float32)],
        scratch_shapes=[pltpu.VMEM((1, LANE), jnp.float32), pltpu.VMEM((LANE, 1), jnp.float32)],
        compiler_params=_cparams(("parallel", "arbitrary")),
        name="logf_cumsum",
    )(x)


def _num_key_tiles(i, bq, bk, q_off, nk_max):
    last = q_off + (i + 1) * bq
    return jnp.minimum((last + bk - 1) // bk, nk_max)


def _softmax_steps(ss, vts, m_ref, l_ref, acc_ref):
    ps, alphas = [], []
    for t, s in enumerate(ss):
        m_prev = m_ref[t]
        m_new = jnp.maximum(m_prev, jnp.max(s, axis=1, keepdims=True))
        alpha = jnp.exp2(m_prev - m_new)
        p = jnp.exp2(s - m_new)
        l_ref[t] = alpha * l_ref[t] + jnp.sum(p, axis=1, keepdims=True)
        m_ref[t] = m_new
        ps.append(p.astype(jnp.bfloat16))
        alphas.append(alpha)
    for t, (p, alpha) in enumerate(zip(ps, alphas)):
        acc_ref[t] = alpha * acc_ref[t] + _dot(p, vts[t])


def _flash_kernel(*refs, bq, bk, q_off, nk_max, fox, hg, dqk, dv):
    if fox:
        q_ref, k_ref, v_ref, cq_ref, ck_ref, o_ref, m_ref, l_ref, acc_ref = refs
    else:
        q_ref, k_ref, v_ref, o_ref, m_ref, l_ref, acc_ref = refs
    gi = pl.program_id(1)
    i = pl.program_id(2)
    first = q_off + i * bq
    qpos = first + lax.broadcasted_iota(jnp.int32, (bq, 1), 0)
    if fox:
        lane = lax.broadcasted_iota(jnp.int32, (bq, LANE), 1)
        cq_all = cq_ref[0]
        cqs = [jnp.sum(jnp.where(lane == gi * hg + t, cq_all, 0.0), axis=1, keepdims=True) for t in range(hg)]
        c0 = [c[0:1, :] for c in cqs]
        bias_q = [c - z for c, z in zip(cqs, c0)]
    nk = _num_key_tiles(i, bq, bk, q_off, nk_max)
    n_full = jnp.minimum((first + (1 if fox else CHUNK)) // bk, nk)

    m_ref[...] = jnp.full(m_ref.shape, NEG_INF, jnp.float32)
    l_ref[...] = jnp.zeros(l_ref.shape, jnp.float32)
    acc_ref[...] = jnp.zeros(acc_ref.shape, jnp.float32)

    def make_body(masked):
        def body(j, _):
            start = pl.multiple_of(j * bk, bk)
            if masked:
                kpos = j * bk + lax.broadcasted_iota(jnp.int32, (1, bk), 1)
                mask = (kpos <= qpos) if fox else ((kpos >> 6) <= (qpos >> 6))
            ss = []
            for t in range(hg):
                kt = k_ref[0, pl.ds(start, bk), t * dqk:(t + 1) * dqk]
                s = _dot_nt(q_ref[0, :, t * dqk:(t + 1) * dqk], kt)
                if fox:
                    s = s + (bias_q[t] - (ck_ref[0, t, j] - c0[t]))
                ss.append(jnp.where(mask, s, NEG_INF) if masked else s)
            vts = [v_ref[0, pl.ds(start, bk), t * dv:(t + 1) * dv] for t in range(hg)]
            _softmax_steps(ss, vts, m_ref, l_ref, acc_ref)
            return 0
        return body

    lax.fori_loop(0, n_full, make_body(False), 0)
    lax.fori_loop(n_full, nk, make_body(True), 0)
    for t in range(hg):
        o_ref[0, :, t * dv:(t + 1) * dv] = (acc_ref[t] / l_ref[t]).astype(o_ref.dtype)


def flash_attention(q, k, v, cq=None, ck=None, *, lq, sp, heads, hg, dqk, dv, bq, bk, q_off):
    b = q.shape[0]
    nk_max = sp // bk
    fox = cq is not None
    in_specs = [pl.BlockSpec((1, bq, hg * dqk), lambda b_, g, i: (b_, i, g)),
                pl.BlockSpec((1, sp, hg * dqk), lambda b_, g, i: (b_, 0, g)),
                pl.BlockSpec((1, sp, hg * dv), lambda b_, g, i: (b_, 0, g))]
    args = [q, k, v]
    if fox:
        in_specs += [pl.BlockSpec((1, bq, LANE), lambda b_, g, i: (b_, i, 0)),
                     pl.BlockSpec((1, hg, nk_max, 1, bk), lambda b_, g, i: (b_, g, 0, 0, 0))]
        args += [cq, ck]
    return pl.pallas_call(
        functools.partial(_flash_kernel, bq=bq, bk=bk, q_off=q_off, nk_max=nk_max, fox=fox, hg=hg, dqk=dqk, dv=dv),
        grid=(b, heads // hg, lq // bq),
        in_specs=in_specs,
        out_specs=pl.BlockSpec((1, bq, hg * dv), lambda b_, g, i: (b_, i, g)),
        out_shape=jax.ShapeDtypeStruct((b, lq, heads * dv), jnp.bfloat16),
        scratch_shapes=[pltpu.VMEM((hg, bq, 1), jnp.float32), pltpu.VMEM((hg, bq, 1), jnp.float32),
                        pltpu.VMEM((hg, bq, dv), jnp.float32)],
        compiler_params=_cparams(("parallel", "parallel", "arbitrary")),
        name="fox_attention" if fox else "mla_attention",
    )(*args)


_KEY_NEG_INF = -2139095041
_F32_LOWEST = -3.4028234663852886e38


_FOLD = 64
_MAX_PROBES = 8 * 34


def _key_to_f32(key):
    bits = jnp.where(key >= 0, key, key ^ jnp.int32(0x7FFFFFFF))
    return lax.bitcast_convert_type(bits, jnp.float32)


def _f32_to_key(x):
    bits = lax.bitcast_convert_type(x, jnp.int32)
    return jnp.where(bits >= 0, bits, bits ^ jnp.int32(0x7FFFFFFF))


def _dsa_kernel(iq_ref, iqs_ref, iw_ref, dq_ref, ik_ref, dk_ref, dv_ref, o_ref,
                sc_ref, qg_ref, m_ref, l_ref, acc_ref, *, bq, bk, q_off, nk_max, topk):
    i = pl.program_id(1)
    nk = _num_key_tiles(i, bq, bk, q_off, nk_max)
    qpos = q_off + i * bq + lax.broadcasted_iota(jnp.int32, (bq, 1), 0)
    klim = ((qpos >> 6) + 1) << 6
    iw = iw_ref[0]

    def score_tile(j, _):
        kt = ik_ref[0, pl.ds(pl.multiple_of(j * bk, bk), bk), :]
        acc = jnp.zeros((bq, bk), jnp.float32)
        for g in range(IDX_HEADS // 2):
            se = _dot_nt(iq_ref[0, :, g * LANE:(g + 1) * LANE], kt)
            so = _dot_nt(iqs_ref[0, :, g * LANE:(g + 1) * LANE], kt)
            acc = acc + iw[:, 2 * g:2 * g + 1] * jnp.maximum(se, 0.0)
            acc = acc + iw[:, 2 * g + 1:2 * g + 2] * jnp.maximum(so, 0.0)
        kpos = j * bk + lax.broadcasted_iota(jnp.int32, (1, bk), 1)
        sc_ref[j] = jnp.where(kpos < klim, acc, -jnp.inf)
        return 0

    lax.fori_loop(0, nk, score_tile, 0)

    def count(hit):
        def body(j, acc):
            h = hit(sc_ref[j], j)
            for c in range(bk // LANE):
                acc = acc + h[:, c * LANE:(c + 1) * LANE]
            return acc
        acc = lax.fori_loop(0, nk, body, jnp.zeros((bq, LANE), jnp.float32))
        return jnp.sum(acc, axis=1, keepdims=True)

    count_ge = lambda pivot: count(lambda t, j: jnp.where(t >= pivot, 1.0, 0.0))
    kf = jnp.float32(topk)
    nonneg = count_ge(jnp.zeros((bq, 1), jnp.float32)) >= kf
    key0 = jnp.where(nonneg, jnp.int32(0), jnp.int32(-2 ** 31))

    def bit_step(it, key):
        cand = key | (jnp.int32(1) << (30 - it))
        ok = (count_ge(_key_to_f32(cand)) >= kf) | (cand <= _KEY_NEG_INF)
        return jnp.where(ok, cand, key)

    key = lax.fori_loop(0, 31, bit_step, key0)
    thr = jnp.maximum(_key_to_f32(key), _F32_LOWEST)

    tie = count_ge(thr) > kf
    total = nk_max * bk

    @pl.when(jnp.sum(jnp.where(tie, 1, 0)) > 0)
    def _():
        need = kf - count(lambda t, j: jnp.where(t > thr, 1.0, 0.0))
        cols = lax.broadcasted_iota(jnp.int32, (bq, bk), 1)

        def count_equal_upto(j_max):
            return count(lambda t, j: jnp.where(t == thr, jnp.where(cols <= j_max - j * bk, 1.0, 0.0), 0.0))

        def index_step(_, bracket):
            j_lo, j_hi = bracket
            mid = (j_lo + j_hi) >> 1
            ok = count_equal_upto(mid) >= need
            return jnp.where(ok, j_lo, mid), jnp.where(ok, mid, j_hi)

        bracket = (jnp.full((bq, 1), -1, jnp.int32), jnp.full((bq, 1), total - 1, jnp.int32))
        _, j_cut = lax.fori_loop(0, total.bit_length() + 1, index_step, bracket)
        j_cut = jnp.where(tie, j_cut, total)

        def drop_late_ties(j, _):
            t = sc_ref[j]
            sc_ref[j] = jnp.where(t == thr, jnp.where(cols > j_cut - j * bk, -jnp.inf, t), t)
            return 0

        lax.fori_loop(0, nk, drop_late_ties, 0)

    for h in range(DSA_HEADS):
        g, r = divmod(h, DSA_GROUP)
        qg_ref[g, r * bq:(r + 1) * bq, :] = dq_ref[0, :, h * LANE:(h + 1) * LANE]
    m_ref[...] = jnp.full(m_ref.shape, NEG_INF, jnp.float32)
    l_ref[...] = jnp.zeros(l_ref.shape, jnp.float32)
    acc_ref[...] = jnp.zeros(acc_ref.shape, jnp.float32)

    def att_tile(j, _):
        start = pl.multiple_of(j * bk, bk)
        drop = jnp.where(sc_ref[j] >= thr, 0.0, NEG_INF)
        drop = jnp.concatenate([drop] * DSA_GROUP, axis=0)
        ss = [_dot_nt(qg_ref[g], dk_ref[0, pl.ds(start, bk), g * LANE:(g + 1) * LANE]) + drop
              for g in range(DSA_KV_HEADS)]
        vts = [dv_ref[0, pl.ds(start, bk), g * LANE:(g + 1) * LANE] for g in range(DSA_KV_HEADS)]
        _softmax_steps(ss, vts, m_ref, l_ref, acc_ref)
        return 0

    lax.fori_loop(0, nk, att_tile, 0)
    for h in range(DSA_HEADS):
        g, r = divmod(h, DSA_GROUP)
        rows = slice(r * bq, (r + 1) * bq)
        o_ref[0, :, h * LANE:(h + 1) * LANE] = (acc_ref[g, rows, :] / l_ref[g, rows, :]).astype(o_ref.dtype)


def dsa_attention(iq, iqs, iw, dq, ik, dk, dv, *, lq, sp, bq, bk, q_off, topk):
    b = dq.shape[0]
    nk_max = sp // bk
    qrow = lambda w: pl.BlockSpec((1, bq, w), lambda b_, i: (b_, i, 0))
    krow = lambda w: pl.BlockSpec((1, sp, w), lambda b_, i: (b_, 0, 0))
    gr = DSA_GROUP * bq
    return pl.pallas_call(
        functools.partial(_dsa_kernel, bq=bq, bk=bk, q_off=q_off, nk_max=nk_max, topk=topk),
        grid=(b, lq // bq),
        in_specs=[qrow(IDX_W), qrow(IDX_W), qrow(LANE), qrow(DSA_W), krow(LANE), krow(DSA_KV_W), krow(DSA_KV_W)],
        out_specs=qrow(DSA_W),
        out_shape=jax.ShapeDtypeStruct((b, lq, DSA_W), jnp.bfloat16),
        scratch_shapes=[pltpu.VMEM((nk_max, bq, bk), jnp.float32),
                        pltpu.VMEM((DSA_KV_HEADS, gr, LANE), jnp.bfloat16),
                        pltpu.VMEM((DSA_KV_HEADS, gr, 1), jnp.float32),
                        pltpu.VMEM((DSA_KV_HEADS, gr, 1), jnp.float32),
                        pltpu.VMEM((DSA_KV_HEADS, gr, LANE), jnp.float32)],
        compiler_params=_cparams(("parallel", "arbitrary")),
        name="dsa_attention",
    )(iq, iqs, iw, dq, ik, dk, dv)


def _split3(x):
    hi = x.astype(jnp.bfloat16).astype(jnp.float32)
    r = x - hi
    mid = r.astype(jnp.bfloat16).astype(jnp.float32)
    lo = (r - mid).astype(jnp.bfloat16).astype(jnp.float32)
    return hi, mid, lo


def _fox_pack_kernel(q_ref, k_ref, c_ref, qa_o, ka_o, qn_o, kn_o):
    rows = q_ref.shape[0]
    lane = lax.broadcasted_iota(jnp.int32, (rows, LANE), 1)
    c = c_ref[...]
    zero = jnp.zeros((rows, LANE), jnp.float32)
    qn, kn = zero, zero
    for h in range(FOX_HEADS):
        qh = q_ref[:, h * LANE:(h + 1) * LANE].astype(jnp.float32)
        kh = k_ref[:, h * LANE:(h + 1) * LANE].astype(jnp.float32)
        qn = jnp.where(lane == h, jnp.sum(qh * qh, axis=1, keepdims=True), qn)
        kn = jnp.where(lane == h, jnp.sum(kh * kh, axis=1, keepdims=True), kn)
        ch = jnp.sum(jnp.where(lane == h, c, 0.0), axis=1, keepdims=True)
        hi, mid, lo = _split3(ch)
        terms = jnp.where(lane == 0, hi, jnp.where(lane == 1, mid, jnp.where(lane == 2, lo, zero)))
        ones_hi = jnp.where((lane >= 3) & (lane < 6), 1.0, 0.0)
        qa_o[:, h * FOX_QP:h * FOX_QP + LANE] = q_ref[:, h * LANE:(h + 1) * LANE]
        qa_o[:, h * FOX_QP + LANE:(h + 1) * FOX_QP] = (terms + ones_hi).astype(jnp.bfloat16)
        ones_lo = jnp.where(lane < 3, 1.0, 0.0)
        ka_o[:, h * FOX_QP:h * FOX_QP + LANE] = k_ref[:, h * LANE:(h + 1) * LANE]
        ka_o[:, h * FOX_QP + LANE:(h + 1) * FOX_QP] = (ones_lo - pltpu.roll(terms, 3, 1)).astype(jnp.bfloat16)
    qn_o[...] = qn
    kn_o[...] = kn


def fox_pack(q, k, c, *, rows, bm):
    row = lambda w: pl.BlockSpec((bm, w), lambda i: (i, 0))
    return pl.pallas_call(
        _fox_pack_kernel,
        grid=(rows // bm,),
        in_specs=[row(FOX_W), row(FOX_W), row(LANE)],
        out_specs=[row(FOX_HEADS * FOX_QP), row(FOX_HEADS * FOX_QP), row(LANE), row(LANE)],
        out_shape=[jax.ShapeDtypeStruct((rows, FOX_HEADS * FOX_QP), jnp.bfloat16)] * 2
        + [jax.ShapeDtypeStruct((rows, LANE), jnp.float32)] * 2,
        compiler_params=_cparams(("parallel",)),
        name="fox_pack",
    )(q, k, c)


def _logits_stage(slot, sts, s_ref, cm_ref):
    for t, st in enumerate(sts):
        s_ref[slot, t] = st
        cm_ref[slot, t] = jnp.max(st, axis=0, keepdims=True)


def _values_stage(slot, vtts, s_ref, cm_ref, m_ref, acc_ref):
    ps, alphas = [], []
    for t in range(len(vtts)):
        m_prev = m_ref[t]
        m_new = jnp.maximum(m_prev, cm_ref[slot, t])
        alphas.append(jnp.exp2(m_prev - m_new))
        ps.append(jnp.exp2(s_ref[slot, t] - m_new).astype(jnp.bfloat16))
        m_ref[t] = m_new
    for t, (p, alpha) in enumerate(zip(ps, alphas)):
        acc_ref[t] = alpha * acc_ref[t] + _dot(vtts[t], p)


def _pipelined_tiles(nk, nk_max, logits, values, start=0):
    last = nk_max - 1
    n = nk - start
    logits(start, 0)

    def run(first, tiles):
        for u in range(tiles):
            logits(jnp.minimum(start + first + u + 1, last), (u + 1) % 2)
            values(start + first + u, u % 2)

    def quad(jj, _):
        run(4 * jj, 4)
        return 0

    def pair(jj, _):
        run(2 * jj, 2)
        return 0

    n_quads = lax.shift_right_logical(n, 2)
    lax.fori_loop(0, n_quads, quad, 0)
    lax.fori_loop(2 * n_quads, lax.shift_right_logical(n, 1), pair, 0)

    @pl.when((n & 1) == 1)
    def _():
        values(nk - 1, 0)


def _finish_t(acc):
    return (acc[0:HEAD_DIM, :] / acc[HEAD_DIM:HEAD_DIM + 1, :]).T


_UNDERFLOW_EXP = 160.0


def _first_live_tile(q_ref, k_ref, qn_ref, c_first_ref, c_end_ref, kmax_ref, *, i, first, bq, bk, hg, dqk):
    g = pl.program_id(1)
    slack = []
    for t in range(hg):
        h = g * hg + t
        lane = lax.broadcasted_iota(jnp.int32, (bq, LANE), 1)
        q_norm = jnp.sqrt(jnp.max(jnp.where(lane == h, qn_ref[0], 0.0)))
        qt = q_ref[0, :, t * dqk:(t + 1) * dqk].astype(jnp.float32)
        kt = k_ref[0, pl.ds(pl.multiple_of(first, bq), bq), t * dqk:(t + 1) * dqk].astype(jnp.float32)
        diag_min = jnp.min(jnp.sum(qt * kt, axis=1, keepdims=True))
        slack.append(q_norm * kmax_ref[0, h] * 1.01 + c_first_ref[i, h] - diag_min + _UNDERFLOW_EXP)
    g0 = g * hg

    def dead(j):
        ok = c_end_ref[j, g0] > slack[0]
        for t in range(1, hg):
            ok = ok & (c_end_ref[j, g0 + t] > slack[t])
        return ok

    n_old = first // bk
    return lax.while_loop(lambda j: (j < n_old) & dead(j), lambda j: j + 1, jnp.int32(0))


def _flash_t_kernel(*refs, bq, bk, nk_max, frame_causal, hg, dqk, decay):
    if decay:
        q_ref, k_ref, vt_ref, qn_ref, c_first_ref, c_end_ref, kmax_ref, o_ref, m_ref, acc_ref, s_ref, cm_ref, d_ref = refs
    else:
        q_ref, k_ref, vt_ref, o_ref, m_ref, acc_ref, s_ref, cm_ref, d_ref = refs
    i = pl.program_id(2)
    first = i * bq
    nk = _num_key_tiles(i, bq, bk, 0, nk_max)
    m_ref[...] = jnp.full(m_ref.shape, NEG_INF, jnp.float32)
    acc_ref[...] = jnp.zeros(acc_ref.shape, jnp.float32)
    r = lax.broadcasted_iota(jnp.int32, (bk, bq), 0)
    c = lax.broadcasted_iota(jnp.int32, (bk, bq), 1)
    d_ref[...] = (r - c) if frame_causal else ((r >> 6) - (c >> 6))

    def logits(j, slot):
        start = pl.multiple_of(j * bk, bk)
        gap = first - j * bk
        visible = gap if frame_causal else (gap >> 6)
        mask = d_ref[...] <= visible
        sts = []
        for t in range(hg):
            kt = k_ref[0, pl.ds(start, bk), t * dqk:(t + 1) * dqk]
            st = _dot_nt(kt, q_ref[0, :, t * dqk:(t + 1) * dqk])
            sts.append(jnp.where(mask, st, NEG_INF))
        _logits_stage(slot, sts, s_ref, cm_ref)

    def values(j, slot):
        _values_stage(slot, [vt_ref[0, j, t * VR:(t + 1) * VR, :] for t in range(hg)], s_ref, cm_ref, m_ref, acc_ref)

    start = 0
    if decay:
        start = _first_live_tile(q_ref, k_ref, qn_ref, c_first_ref, c_end_ref, kmax_ref,
                                 i=i, first=first, bq=bq, bk=bk, hg=hg, dqk=dqk)
    _pipelined_tiles(nk, nk_max, logits, values, start)
    for t in range(hg):
        o_ref[0, :, t * HEAD_DIM:(t + 1) * HEAD_DIM] = _finish_t(acc_ref[t]).astype(o_ref.dtype)


def flash_attention_t(q, k, vt, decay=None, *, lq, sp, heads, hg, dqk, bq, bk, frame_causal, name):
    nk_max = sp // bk
    smem = pl.BlockSpec(memory_space=pltpu.SMEM)
    extra_specs, extra = [], []
    if decay is not None:
        extra_specs = [pl.BlockSpec((1, bq, LANE), lambda b_, g, i: (b_, i, 0)), smem, smem, smem]
        extra = list(decay)
    return pl.pallas_call(
        functools.partial(_flash_t_kernel, bq=bq, bk=bk, nk_max=nk_max, frame_causal=frame_causal, hg=hg, dqk=dqk,
                          decay=decay is not None),
        grid=(1, heads // hg, lq // bq),
        in_specs=[pl.BlockSpec((1, bq, hg * dqk), lambda b_, g, i: (b_, i, g)),
                  _resident((1, sp, hg * dqk), lambda b_, g, i: (b_, 0, g)),
                  _resident((1, nk_max, hg * VR, bk), lambda b_, g, i: (b_, 0, g, 0))] + extra_specs,
        out_specs=pl.BlockSpec((1, bq, hg * HEAD_DIM), lambda b_, g, i: (b_, i, g)),
        out_shape=jax.ShapeDtypeStruct((1, lq, heads * HEAD_DIM), jnp.bfloat16),
        scratch_shapes=[pltpu.VMEM((hg, 1, bq), jnp.float32), pltpu.VMEM((hg, VR, bq), jnp.float32),
                        pltpu.VMEM((2, hg, bk, bq), jnp.float32), pltpu.VMEM((2, hg, 1, bq), jnp.float32),
                        pltpu.VMEM((bk, bq), jnp.int32)],
        compiler_params=_cparams(("parallel", "parallel", "arbitrary")),
        name=name,
    )(q, k, vt, *extra)


def _dsa_t_kernel(iq_ref, iqs_ref, iwt_ref, dq_ref, ik_ref, dk_ref, dvt_ref, o_ref,
                  sc_ref, m_ref, acc_ref, s_ref, cm_ref, *, bq, bk, nk_max, topk):
    i = pl.program_id(1)
    nk = _num_key_tiles(i, bq, bk, 0, nk_max)
    qpos = i * bq + lax.broadcasted_iota(jnp.int32, (1, bq), 1)
    klim = ((qpos >> 6) + 1) << 6

    fold = lambda a, op: op(a.reshape(bk // _FOLD, _FOLD, bq), axis=0)

    def score_tile(j, carry):
        hi, lo = carry
        kt = ik_ref[0, pl.ds(pl.multiple_of(j * bk, bk), bk), :]
        acc = jnp.zeros((bk, bq), jnp.float32)
        for g in range(IDX_HEADS // 2):
            se = _dot_nt(kt, iq_ref[0, :, g * LANE:(g + 1) * LANE])
            so = _dot_nt(kt, iqs_ref[0, :, g * LANE:(g + 1) * LANE])
            acc = acc + iwt_ref[2 * g:2 * g + 1, :] * jnp.maximum(se, 0.0)
            acc = acc + iwt_ref[2 * g + 1:2 * g + 2, :] * jnp.maximum(so, 0.0)
        adm = (j * bk + lax.broadcasted_iota(jnp.int32, (bk, bq), 0)) < klim
        sc = jnp.where(adm, acc, -jnp.inf)
        sc_ref[j] = sc
        return (jnp.maximum(hi, fold(sc, jnp.max)), jnp.minimum(lo, fold(jnp.where(adm, acc, jnp.inf), jnp.min)))

    n_pairs = lax.shift_right_logical(nk, 1)
    carry = lax.fori_loop(0, n_pairs, lambda jj, c: score_tile(2 * jj + 1, score_tile(2 * jj, c)),
                          (jnp.full((_FOLD, bq), -jnp.inf, jnp.float32), jnp.full((_FOLD, bq), jnp.inf, jnp.float32)))
    hi, lo = lax.fori_loop(2 * n_pairs, nk, score_tile, carry)
    hi = jnp.max(hi, axis=0, keepdims=True)
    lo = jnp.min(lo, axis=0, keepdims=True)

    def count_ge(pivot):
        def body(j, acc):
            for r in range(bk // _FOLD):
                acc = acc + jnp.where(sc_ref[j, r * _FOLD:(r + 1) * _FOLD, :] >= pivot, 1.0, 0.0)
            return acc
        acc = lax.fori_loop(0, nk, body, jnp.zeros((_FOLD, bq), jnp.float32))
        return jnp.sum(acc, axis=0, keepdims=True)

    kf = jnp.float32(topk)
    hi = _key_to_f32(_f32_to_key(hi) + 1)
    c_lo = klim.astype(jnp.float32)
    c_hi = jnp.zeros((1, bq), jnp.float32)

    def settled(lo, hi, c_lo):
        width = lax.bitcast_convert_type(_f32_to_key(hi) - _f32_to_key(lo), jnp.uint32)
        return (c_lo <= kf) | (width <= 1)

    def search_cond(state):
        it, n_open = state[0], state[1]
        return (n_open > 0) & (it < _MAX_PROBES)

    def search_step(state):
        it, _, lo, hi, c_lo, c_hi = state
        done = settled(lo, hi, c_lo)
        k_lo, k_hi = _f32_to_key(lo), _f32_to_key(hi)
        width = lax.bitcast_convert_type(k_hi - k_lo, jnp.uint32)
        k_mid = k_lo + lax.bitcast_convert_type(width >> 1, jnp.int32)
        k_val = jnp.clip(_f32_to_key(lo + (hi - lo) * 0.5), k_lo + 1, k_hi - 1)
        probe = _key_to_f32(jnp.where((it & 7) == 7, k_mid, k_val))
        probe = jnp.where(done, lo, probe)
        c = count_ge(probe)
        up = (c >= kf) & ~done
        down = (c < kf) & ~done
        lo, c_lo = jnp.where(up, probe, lo), jnp.where(up, c, c_lo)
        hi, c_hi = jnp.where(down, probe, hi), jnp.where(down, c, c_hi)
        n_open = jnp.sum(jnp.where(settled(lo, hi, c_lo), 0, 1))
        return it + 1, n_open, lo, hi, c_lo, c_hi

    state = (jnp.int32(0), jnp.sum(jnp.where(settled(lo, hi, c_lo), 0, 1)), lo, hi, c_lo, c_hi)
    _, _, thr, _, c_lo, c_hi = lax.while_loop(search_cond, search_step, state)

    tie = c_lo > kf
    total = nk_max * bk

    @pl.when(jnp.sum(jnp.where(tie, 1, 0)) > 0)
    def _():
        need = kf - c_hi
        rows = lax.broadcasted_iota(jnp.int32, (bk, bq), 0)

        def count_equal_upto(j_max):
            def body(j, acc):
                hit = jnp.where(sc_ref[j] == thr, jnp.where(rows <= j_max - j * bk, 1.0, 0.0), 0.0)
                return acc + fold(hit, jnp.sum)
            acc = lax.fori_loop(0, nk, body, jnp.zeros((_FOLD, bq), jnp.float32))
            return jnp.sum(acc, axis=0, keepdims=True)

        def index_step(_, bracket):
            j_lo, j_hi = bracket
            mid = (j_lo + j_hi) >> 1
            ok = count_equal_upto(mid) >= need
            return jnp.where(ok, j_lo, mid), jnp.where(ok, mid, j_hi)

        bracket = (jnp.full((1, bq), -1, jnp.int32), jnp.full((1, bq), total - 1, jnp.int32))
        _, j_cut = lax.fori_loop(0, total.bit_length() + 1, index_step, bracket)
        j_cut = jnp.where(tie, j_cut, total)

        def drop_late_ties(j, _):
            t = sc_ref[j]
            sc_ref[j] = jnp.where(t == thr, jnp.where(rows > j_cut - j * bk, -jnp.inf, t), t)
            return 0

        lax.fori_loop(0, nk, drop_late_ties, 0)

    m_ref[...] = jnp.full(m_ref.shape, NEG_INF, jnp.float32)
    acc_ref[...] = jnp.zeros(acc_ref.shape, jnp.float32)

    def logits(j, slot):
        start = pl.multiple_of(j * bk, bk)
        drop = jnp.where(sc_ref[j] >= thr, 0.0, NEG_INF)
        sts = []
        for h in range(DSA_HEADS):
            g = h // DSA_GROUP
            kt = dk_ref[0, pl.ds(start, bk), g * LANE:(g + 1) * LANE]
            sts.append(_dot_nt(kt, dq_ref[0, :, h * LANE:(h + 1) * LANE]) + drop)
        _logits_stage(slot, sts, s_ref, cm_ref)

    def values(j, slot):
        vtts = [dvt_ref[0, j, (h // DSA_GROUP) * VR:(h // DSA_GROUP + 1) * VR, :] for h in range(DSA_HEADS)]
        _values_stage(slot, vtts, s_ref, cm_ref, m_ref, acc_ref)

    _pipelined_tiles(nk, nk_max, logits, values)
    for h in range(DSA_HEADS):
        o_ref[0, :, h * LANE:(h + 1) * LANE] = _finish_t(acc_ref[h]).astype(o_ref.dtype)


def dsa_attention_t(iq, iqs, iwt, dq, ik, dk, dvt, *, lq, sp, bq, bk, topk):
    nk_max = sp // bk
    qrow = lambda w: pl.BlockSpec((1, bq, w), lambda b_, i: (b_, i, 0))
    krow = lambda w: _resident((1, sp, w), lambda b_, i: (b_, 0, 0))
    return pl.pallas_call(
        functools.partial(_dsa_t_kernel, bq=bq, bk=bk, nk_max=nk_max, topk=topk),
        grid=(1, lq // bq),
        in_specs=[qrow(IDX_W), qrow(IDX_W), pl.BlockSpec((LANE, bq), lambda b_, i: (0, i)), qrow(DSA_W),
                  krow(LANE), krow(DSA_KV_W),
                  _resident((1, nk_max, DSA_KV_HEADS * VR, bk), lambda b_, i: (b_, 0, 0, 0))],
        out_specs=qrow(DSA_W),
        out_shape=jax.ShapeDtypeStruct((1, lq, DSA_W), jnp.bfloat16),
        scratch_shapes=[pltpu.VMEM((nk_max, bk, bq), jnp.float32),
                        pltpu.VMEM((DSA_HEADS, 1, bq), jnp.float32),
                        pltpu.VMEM((DSA_HEADS, VR, bq), jnp.float32),
                        pltpu.VMEM((2, DSA_HEADS, bk, bq), jnp.float32),
                        pltpu.VMEM((2, DSA_HEADS, 1, bq), jnp.float32)],
        compiler_params=_cparams(("parallel", "arbitrary")),
        name="dsa_attention_t",
    )(iq, iqs, iwt, dq, ik, dk, dvt)


def _out_proj_kernel(x_ref, a_ref, b_ref, c_ref, wa_ref, wb_ref, wc_ref, o_ref):
    o_ref[...] = (x_ref[...] + _dot(a_ref[...], wa_ref[...]) + _dot(b_ref[...], wb_ref[...])
                  + _dot(c_ref[...], wc_ref[...]))


def out_proj(x, a, b, c, w, layer, *, bm):
    m, d = x.shape
    wa, wb, wc = a.shape[1], b.shape[1], c.shape[1]
    assert wa == wb and (wa + wb) % wc == 0
    row = lambda w_: pl.BlockSpec((bm, w_), lambda i: (i, 0))
    band = lambda rows, blk: _resident((None, rows, d), lambda i: (layer, blk, 0))
    return pl.pallas_call(
        _out_proj_kernel,
        grid=(m // bm,),
        in_specs=[row(d), row(wa), row(wb), row(wc), band(wa, 0), band(wb, 1), band(wc, (wa + wb) // wc)],
        out_specs=row(d),
        out_shape=jax.ShapeDtypeStruct((m, d), jnp.float32),
        compiler_params=_cparams(("parallel",)),
        name="out_proj",
    )(x, a, b, c, w, w, w)


def _ffn_kernel(x_ref, g_ref, wg_ref, wu_ref, wd_ref, o_ref, xn_ref, acc_ref):
    j = pl.program_id(1)

    @pl.when(j == 0)
    def _():
        xn_ref[...] = _rms(x_ref[...], g_ref[...]).astype(jnp.bfloat16)
        acc_ref[...] = jnp.zeros_like(acc_ref)

    xn = xn_ref[...]
    gate = _dot(xn, wg_ref[...])
    up = _dot(xn, wu_ref[...])
    hidden = (gate * (1.0 / (1.0 + jnp.exp(-gate))) * up).astype(jnp.bfloat16)
    acc_ref[...] += _dot(hidden, wd_ref[...])

    @pl.when(j == pl.num_programs(1) - 1)
    def _():
        o_ref[...] = x_ref[...] + acc_ref[...]


def ffn(x, g, wg, wu, wd, layer, *, bm, bf):
    m, d = x.shape
    f = wg.shape[2]
    return pl.pallas_call(
        _ffn_kernel,
        grid=(m // bm, f // bf),
        in_specs=[pl.BlockSpec((bm, d), lambda i, j: (i, 0)),
                  pl.BlockSpec((1, d), lambda i, j: (0, 0)),
                  pl.BlockSpec((None, d, bf), lambda i, j: (layer, 0, j)),
                  pl.BlockSpec((None, d, bf), lambda i, j: (layer, 0, j)),
                  pl.BlockSpec((None, bf, d), lambda i, j: (layer, j, 0))],
        out_specs=pl.BlockSpec((bm, d), lambda i, j: (i, 0)),
        out_shape=jax.ShapeDtypeStruct((m, d), jnp.float32),
        scratch_shapes=[pltpu.VMEM((bm, d), jnp.bfloat16), pltpu.VMEM((bm, d), jnp.float32)],
        compiler_params=_cparams(("parallel", "arbitrary")),
        name="swiglu",
    )(x, g.reshape(1, d), wg, wu, wd)


def _final_norm_kernel(x_ref, g_ref, o_ref):
    o_ref[...] = _rms(x_ref[...], g_ref[...])


def final_norm(x, g, *, bm):
    m, d = x.shape
    return pl.pallas_call(
        _final_norm_kernel,
        grid=(m // bm,),
        in_specs=[pl.BlockSpec((bm, d), lambda i: (i, 0)), pl.BlockSpec((1, d), lambda i: (0, 0))],
        out_specs=pl.BlockSpec((bm, d), lambda i: (i, 0)),
        out_shape=jax.ShapeDtypeStruct((m, d), jnp.float32),
        compiler_params=_cparams(("parallel",)),
        name="final_norm",
    )(x, g.reshape(1, d))


def _pad_cols(a, width):
    return jnp.pad(a, [(0, 0)] * (a.ndim - 1) + [(0, width - a.shape[-1])])


def _layout_w_in(w_in):
    sizes = (FOX_W, FOX_W, FOX_W, FOX_HEADS, DSA_W, DSA_KV_W, DSA_KV_W, IDX_W, IDX_DIM, IDX_HEADS,
             MLA_Q_LORA, MLA_KV_LORA, MLA_ROPE)
    splits = np.cumsum(sizes)[:-1]
    q_a, k_a, v_a, f_a, q_b, k_b, v_b, q_i, k_i, w_i, c_q, c_kv, k_r = jnp.split(w_in, splits, axis=-1)
    parts = {"fq": q_a, "fk": k_a, "fv": v_a, "dq": q_b, "dk": k_b, "dv": v_b, "iq": q_i, "cq": c_q,
             "ckv": c_kv, "fa": f_a, "ik": k_i, "iw": w_i, "kr": k_r}
    cols = [_pad_cols(parts[n], w) for n, w in _SEGS]
    return _pad_cols(jnp.concatenate(cols, axis=-1), PROJ_W).astype(jnp.bfloat16)


def _layout_w_uq(w_uq):
    dp = w_uq.shape[0]
    w = w_uq.reshape(dp, MLA_Q_LORA, MLA_HEADS, MLA_NOPE + MLA_ROPE)
    return _pad_cols(w, MLA_QP).reshape(dp, MLA_Q_LORA, MLA_QW).astype(jnp.bfloat16)


def _layout_w_ukv(w_ukv):
    dp = w_ukv.shape[0]
    w = w_ukv.reshape(dp, MLA_KV_LORA, MLA_HEADS, MLA_NOPE + MLA_V)
    kn = w[..., :MLA_NOPE].reshape(dp, MLA_KV_LORA, MLA_HEADS * MLA_NOPE)
    vv = w[..., MLA_NOPE:].reshape(dp, MLA_KV_LORA, MLA_VW)
    return jnp.concatenate([kn, vv], axis=-1).astype(jnp.bfloat16)


def _rope_table(pos, rot, period):
    half = rot // 2
    inv_freq = ROPE_THETA ** (-jnp.arange(half, dtype=jnp.float32) / half)
    ang = pos.astype(jnp.float32)[:, None] * inv_freq[None, :]
    cos, sin = jnp.cos(ang), jnp.sin(ang)
    r = pos.shape[0]
    ones = jnp.ones((r, period - rot), jnp.float32)
    zeros = jnp.zeros((r, period - rot), jnp.float32)
    zh = jnp.zeros((r, half), jnp.float32)
    c = jnp.concatenate([cos, cos, ones], axis=1)
    s1 = jnp.concatenate([zh, sin, zeros], axis=1)
    s2 = jnp.concatenate([-sin, zh, zeros], axis=1)
    rep = LANE // period
    return jnp.concatenate([jnp.tile(c, (1, rep)), jnp.tile(s1, (1, rep)), jnp.tile(s2, (1, rep))], axis=1)


def _pick(n, candidates):
    for c in candidates:
        if n % c == 0:
            return c
    raise ValueError(f"no block size among {candidates} divides {n}")


def _pad_seq(a, sp):
    return jnp.pad(a, [(0, 0), (0, sp - a.shape[1])] + [(0, 0)] * (a.ndim - 2))


def kernel(x_prompt, x_sample, cache_fox_k, cache_fox_v, cache_fox_logf, cache_dsa_k, cache_dsa_v, cache_idx_k, cache_mla_ckv, cache_mla_krope, w_in, fox_bias, mla_q_norm, mla_w_uq, mla_kv_norm, mla_w_ukv, w_o, attn_norm, ffn_norm, w_gate, w_up, w_down, final_norm_w):
    bf = jnp.bfloat16
    depth = w_in.shape[0]
    nb, seq, d = x_prompt.shape
    db, dec, _ = x_sample.shape
    past = cache_fox_k.shape[2]
    assert nb == 1 and seq % CHUNK == 0 and dec % CHUNK == 0 and past % CHUNK == 0
    mp, ms = nb * seq, db * dec
    m = mp + ms
    s_len = past + dec

    bm = _pick(m, (512, 256, 128, 64))
    assert mp % bm == 0
    bq_p = _pick(seq, (256, 128))
    bk_p = bm
    bq_s = _pick(dec, (64,))
    bk_s = 256
    sp = ((s_len + bk_s - 1) // bk_s) * bk_s
    topk_p = min(IDX_TOPK_MAX, seq // 4)
    topk_s = min(IDX_TOPK_MAX, s_len // 4)
    fox_hg, mla_hg = 3, 2
    d_ff = w_gate.shape[2]
    bff = _pick(d_ff, (512, 256, 128))

    w_in_p = _layout_w_in(w_in)
    wuq_p = _layout_w_uq(mla_w_uq)
    wukv_p = _layout_w_ukv(mla_w_ukv)
    w_o_b = w_o.astype(bf)
    wg_b, wu_b, wd_b = w_gate.astype(bf), w_up.astype(bf), w_down.astype(bf)
    fox_bias_p = _pad_cols(fox_bias, LANE).reshape(depth, 1, LANE)

    pos = jnp.concatenate([jnp.tile(jnp.arange(seq, dtype=jnp.int32), nb),
                           jnp.tile(past + jnp.arange(dec, dtype=jnp.int32), db)])
    tab = jnp.concatenate([_rope_table(pos, PARTIAL_ROT, HEAD_DIM), _rope_table(pos, IDX_ROT, IDX_DIM),
                           _rope_table(pos, MLA_ROPE, LANE)], axis=1)

    x = jnp.concatenate([x_prompt.reshape(mp, d), x_sample.reshape(ms, d)], axis=0)
    new_rows = []
    for l in range(depth):
        proj = norm_matmul(x, attn_norm[l], w_in_p, l, bm=bm, bn=_pick(PROJ_W, (1536, 1024, 512)))
        (fq, fk, fv, fkb, fvb, fvt, lf, dq, dk, dv, dkb, dvb, dvt, iq, iqs, ik, ikb, iw, iwt, mq, ckv, kr) = post_proj(
            proj, tab, fox_bias_p[l], mla_q_norm[l].reshape(1, -1), mla_kv_norm[l].reshape(1, -1), wuq_p[l],
            bm=bm // 2, bk=bm)

        pr = lambda a: a[:mp].reshape(nb, seq, -1)
        sr = lambda a: a[mp:].reshape(db, dec, -1)
        new_rows.append((fk, fv, lf, dk, dv, ik, ckv, kr))

        cat = lambda c, new: _pad_seq(jnp.concatenate([c, new], axis=1), sp)
        s_fkb = cat(cache_fox_k[l].reshape(db, past, FOX_W).astype(bf), sr(fkb))
        s_fvb = cat(cache_fox_v[l].reshape(db, past, FOX_W).astype(bf), sr(fvb))
        s_lf = cat(_pad_cols(cache_fox_logf[l], LANE), sr(lf))
        s_dkb = cat(cache_dsa_k[l].reshape(db, past, DSA_KV_W).astype(bf), sr(dkb))
        s_dvb = cat(cache_dsa_v[l].reshape(db, past, DSA_KV_W).astype(bf), sr(dvb))
        s_ikb = cat(_pad_cols(cache_idx_k[l], LANE).astype(bf), sr(ikb))
        s_ckv = cat(cache_mla_ckv[l], sr(ckv))
        s_kr = cat(_pad_cols(cache_mla_krope[l], LANE), sr(kr))

        ckv_all = jnp.concatenate([ckv[:mp], s_ckv.reshape(db * sp, -1)], axis=0)
        kr_all = jnp.concatenate([kr[:mp], s_kr.reshape(db * sp, -1)], axis=0)
        kmla, vmla, vtmla = kv_up(ckv_all, kr_all, wukv_p[l], bm=bm)
        s_kmla, s_vmla = kmla[mp:].reshape(db, sp, -1), vmla[mp:].reshape(db, sp, -1)

        p_c, _ = cumsum_rows(pr(lf), bt=_pick(seq, (256, 128)))
        s_c, s_ct = cumsum_rows(s_lf, bt=_pick(sp, (256, 128)))
        ck_form = lambda ct, bk: ct[:, :FOX_HEADS].reshape(ct.shape[0], FOX_HEADS, ct.shape[2] // bk, 1, bk)

        al = lambda a: a.reshape(1, *a.shape)
        fqa, fka, fqn, fkn = fox_pack(fq, fkb, p_c[0], rows=mp, bm=bm)
        decay = (al(fqn), p_c[0, ::bq_p], p_c[0, bk_p - 1::bk_p], jnp.sqrt(jnp.max(fkn, axis=0, keepdims=True)))
        out_a = flash_attention_t(al(fqa), al(fka), al(fvt), decay, lq=seq, sp=seq, heads=FOX_HEADS, hg=fox_hg, dqk=FOX_QP,
                                  bq=bq_p, bk=bk_p, frame_causal=True, name="fox_attention_t")
        out_b = dsa_attention_t(al(iq), al(iqs), iwt, al(dq), al(ikb), al(dkb), al(dvt), lq=seq, sp=seq,
                                bq=bq_p, bk=bk_p, topk=topk_p)
        out_c = flash_attention_t(al(mq), al(kmla), al(vtmla), lq=seq, sp=seq, heads=MLA_HEADS, hg=mla_hg, dqk=MLA_QP,
                                  bq=bq_p, bk=bk_p, frame_causal=False, name="mla_attention_t")
        s_out_a = flash_attention(sr(fq), s_fkb, s_fvb, s_c[:, past:past + dec], ck_form(s_ct, bk_s), lq=dec, sp=sp,
                                  heads=FOX_HEADS, hg=fox_hg, dqk=HEAD_DIM, dv=HEAD_DIM, bq=bq_s, bk=bk_s, q_off=past)
        s_out_b = dsa_attention(sr(iq), sr(iqs), sr(iw), sr(dq), s_ikb, s_dkb, s_dvb, lq=dec, sp=sp,
                                bq=bq_s, bk=bk_s, q_off=past, topk=topk_s)
        s_out_c = flash_attention(sr(mq), s_kmla, s_vmla, lq=dec, sp=sp, heads=MLA_HEADS, hg=mla_hg,
                                  dqk=MLA_QP, dv=MLA_V, bq=bq_s, bk=bk_s, q_off=past)

        rows = lambda p, s: jnp.concatenate([p.reshape(mp, -1), s.reshape(ms, -1)], axis=0)
        x = out_proj(x, rows(out_a, s_out_a), rows(out_b, s_out_b), rows(out_c, s_out_c), w_o_b, l, bm=bm)
        x = ffn(x, ffn_norm[l], wg_b, wu_b, wd_b, l, bm=bm, bf=bff)

    y = final_norm(x, final_norm_w, bm=bm)
    y_prompt = y[:mp].reshape(nb, seq, d)
    y_sample = y[mp:].reshape(db, dec, d)
    tails = ((FOX_HEADS, HEAD_DIM), (FOX_HEADS, HEAD_DIM), (FOX_HEADS,), (DSA_KV_HEADS, HEAD_DIM),
             (DSA_KV_HEADS, HEAD_DIM), (IDX_DIM,), (MLA_KV_LORA,), (MLA_ROPE,))
    p_out, s_out = [], []
    for arrays, tail in zip(zip(*new_rows), tails):
        w = math.prod(tail)
        p_out.append(jnp.stack([a[:mp, :w] for a in arrays]).reshape((depth, nb, seq) + tail))
        s_out.append(jnp.stack([a[mp:, :w] for a in arrays]).reshape((depth, db, dec) + tail))
    return (y_prompt, y_sample) + tuple(p_out) + tuple(s_out)
```

```python
import functools
import math

import jax
import jax.numpy as jnp
import numpy as np
from jax import lax
from jax.experimental import pallas as pl
from jax.experimental.pallas import tpu as pltpu

CHUNK = 64
HEAD_DIM = 128
FOX_HEADS = 6
DSA_HEADS = 6
DSA_KV_HEADS = 2
DSA_GROUP = DSA_HEADS // DSA_KV_HEADS
IDX_HEADS = 16
IDX_DIM = 64
IDX_TOPK_MAX = 256
MLA_HEADS = 4
MLA_Q_LORA = 512
MLA_KV_LORA = 256
MLA_NOPE = 128
MLA_ROPE = 64
MLA_V = 128
PARTIAL_ROT = HEAD_DIM // 4
IDX_ROT = IDX_DIM // 4
ROPE_THETA = 500000.0
EPS = 1e-6
NEG_INF = -1e30
FOX_SCALE = HEAD_DIM ** -0.5
DSA_SCALE = HEAD_DIM ** -0.5
MLA_SCALE = (MLA_NOPE + MLA_ROPE) ** -0.5
IDX_W_SCALE = (IDX_HEADS * IDX_DIM) ** -0.5
LOG2E = math.log2(math.e)

LANE = 128
VMEM_LIMIT = 56 * 1024 * 1024

FOX_W = FOX_HEADS * HEAD_DIM
DSA_W = DSA_HEADS * HEAD_DIM
DSA_KV_W = DSA_KV_HEADS * HEAD_DIM
IDX_W = IDX_HEADS * IDX_DIM
MLA_QP = 2 * LANE
MLA_QW = MLA_HEADS * MLA_QP
MLA_VW = MLA_HEADS * MLA_V
VR = HEAD_DIM + 16
FOX_QP = 2 * LANE

_SEGS = (("fq", FOX_W), ("fk", FOX_W), ("fv", FOX_W), ("dq", DSA_W), ("dk", DSA_KV_W), ("dv", DSA_KV_W),
         ("iq", IDX_W), ("cq", MLA_Q_LORA), ("ckv", MLA_KV_LORA),
         ("fa", LANE), ("ik", LANE), ("iw", LANE), ("kr", LANE))
_OFF = {}
_o = 0
for _n, _w in _SEGS:
    _OFF[_n] = _o
    _o += _w
PROJ_W = ((_o + 511) // 512) * 512

_NT = (((1,), (1,)), ((), ()))


def _dot(a, b):
    return jnp.dot(a, b, preferred_element_type=jnp.float32)


def _dot_nt(a, b):
    return lax.dot_general(a, b, _NT, preferred_element_type=jnp.float32)


def _cparams(sem):
    return pltpu.CompilerParams(dimension_semantics=sem, vmem_limit_bytes=VMEM_LIMIT)


def _resident(block_shape, index_map):
    return pl.BlockSpec(block_shape, index_map, pipeline_mode=pl.Buffered(1))


def _rms(x, g):
    return x * lax.rsqrt(jnp.mean(x * x, axis=-1, keepdims=True) + EPS) * g


def _norm_matmul_kernel(x_ref, g_ref, w_ref, o_ref, xn_ref):
    @pl.when(pl.program_id(1) == 0)
    def _():
        xn_ref[...] = _rms(x_ref[...], g_ref[...]).astype(jnp.bfloat16)

    o_ref[...] = _dot(xn_ref[...], w_ref[...])


def norm_matmul(x, g, w, layer, *, bm, bn):
    m, d = x.shape
    n = w.shape[2]
    return pl.pallas_call(
        _norm_matmul_kernel,
        grid=(m // bm, n // bn),
        in_specs=[pl.BlockSpec((bm, d), lambda i, j: (i, 0)),
                  pl.BlockSpec((1, d), lambda i, j: (0, 0)),
                  pl.BlockSpec((None, d, bn), lambda i, j: (layer, 0, j))],
        out_specs=pl.BlockSpec((bm, bn), lambda i, j: (i, j)),
        out_shape=jax.ShapeDtypeStruct((m, n), jnp.float32),
        scratch_shapes=[pltpu.VMEM((bm, d), jnp.bfloat16)],
        compiler_params=_cparams(("parallel", "arbitrary")),
        name="norm_matmul",
    )(x, g.reshape(1, d), w)


def _rope(x, tab, half):
    c, s1, s2 = tab[:, 0:LANE], tab[:, LANE:2 * LANE], tab[:, 2 * LANE:3 * LANE]
    return x * c + pltpu.roll(x, half, 1) * s1 + pltpu.roll(x, LANE - half, 1) * s2


def _value_t_tail(width):
    r = lax.broadcasted_iota(jnp.int32, (16, width), 0)
    return jnp.where(r == 0, 1.0, 0.0).astype(jnp.bfloat16)


def _store_value_t(o_ref, h, v):
    o_ref[0, h * VR:h * VR + HEAD_DIM, :] = v.T.astype(jnp.bfloat16)
    o_ref[0, h * VR + HEAD_DIM:(h + 1) * VR, :] = _value_t_tail(v.shape[0])


def _post_kernel(p_ref, tab_ref, fb_ref, qn_ref, kvn_ref, wuq_ref,
                 fq_o, fk_o, fv_o, fkb_o, fvb_o, fvt_o, lf_o,
                 dq_o, dk_o, dv_o, dkb_o, dvb_o, dvt_o,
                 iq_o, iqs_o, ik_o, ikb_o, iw_o, iwt_o,
                 mq_o, ckv_o, kr_o):
    bf = jnp.bfloat16
    seg = lambda name, w: p_ref[:, _OFF[name]:_OFF[name] + w]
    tab_d = tab_ref[:, 0:3 * LANE]
    tab_i = tab_ref[:, 3 * LANE:6 * LANE]
    tab_m = tab_ref[:, 6 * LANE:9 * LANE]

    fq_o[...] = (seg("fq", FOX_W) * (FOX_SCALE * LOG2E)).astype(bf)
    fk = seg("fk", FOX_W)
    fk_o[...] = fk
    fkb_o[...] = fk.astype(bf)
    fv = seg("fv", FOX_W)
    fv_o[...] = fv
    fvb_o[...] = fv.astype(bf)
    for h in range(FOX_HEADS):
        _store_value_t(fvt_o, h, p_ref[:, _OFF["fv"] + h * LANE:_OFF["fv"] + (h + 1) * LANE])
    z = seg("fa", LANE) + fb_ref[...]
    lf_o[...] = jnp.minimum(z, 0.0) - jnp.log1p(jnp.exp(-jnp.abs(z)))

    for h in range(DSA_HEADS):
        x = p_ref[:, _OFF["dq"] + h * LANE:_OFF["dq"] + (h + 1) * LANE]
        dq_o[:, h * LANE:(h + 1) * LANE] = (_rope(x, tab_d, PARTIAL_ROT // 2) * (DSA_SCALE * LOG2E)).astype(bf)
    for h in range(DSA_KV_HEADS):
        x = p_ref[:, _OFF["dk"] + h * LANE:_OFF["dk"] + (h + 1) * LANE]
        y = _rope(x, tab_d, PARTIAL_ROT // 2)
        dk_o[:, h * LANE:(h + 1) * LANE] = y
        dkb_o[:, h * LANE:(h + 1) * LANE] = y.astype(bf)
        _store_value_t(dvt_o, h, p_ref[:, _OFF["dv"] + h * LANE:_OFF["dv"] + (h + 1) * LANE])
    dv = seg("dv", DSA_KV_W)
    dv_o[...] = dv
    dvb_o[...] = dv.astype(bf)
    for j in range(IDX_W // LANE):
        x = p_ref[:, _OFF["iq"] + j * LANE:_OFF["iq"] + (j + 1) * LANE]
        y = _rope(x, tab_i, IDX_ROT // 2)
        iq_o[:, j * LANE:(j + 1) * LANE] = y.astype(bf)
        iqs_o[:, j * LANE:(j + 1) * LANE] = pltpu.roll(y, IDX_DIM, 1).astype(bf)
    y = _rope(seg("ik", LANE), tab_i, IDX_ROT // 2)
    ik_o[...] = y
    ikb_o[...] = y.astype(bf)
    iw = seg("iw", LANE) * IDX_W_SCALE
    iw_o[...] = iw
    iwt_o[...] = iw.T

    cqn = _rms(seg("cq", MLA_Q_LORA), qn_ref[...]).astype(bf)
    mq = _dot(cqn, wuq_ref[...])
    for h in range(MLA_HEADS):
        a = h * MLA_QP
        mq_o[:, a:a + LANE] = (mq[:, a:a + LANE] * (MLA_SCALE * LOG2E)).astype(bf)
        r = _rope(mq[:, a + LANE:a + 2 * LANE], tab_m, MLA_ROPE // 2)
        mq_o[:, a + LANE:a + 2 * LANE] = (r * (MLA_SCALE * LOG2E)).astype(bf)
    ckv_o[...] = _rms(seg("ckv", MLA_KV_LORA), kvn_ref[...])
    kr_o[...] = _rope(seg("kr", LANE), tab_m, MLA_ROPE // 2)


def post_proj(proj, tab, fox_bias_p, q_norm, kv_norm, wuq_p, *, bm, bk):
    m = proj.shape[0]
    r = bk // bm
    f32, bf = jnp.float32, jnp.bfloat16
    row = lambda w: pl.BlockSpec((bm, w), lambda i: (i, 0))
    full = lambda a: pl.BlockSpec(a.shape, lambda i: (0,) * a.ndim)
    vt = lambda heads: pl.BlockSpec((1, heads * VR, bm), lambda i: (i // r, 0, i % r))
    rowo = lambda w, dt: (row(w), jax.ShapeDtypeStruct((m, w), dt))
    vto = lambda heads: (vt(heads), jax.ShapeDtypeStruct((m // bk, heads * VR, bk), bf))
    outs = [rowo(FOX_W, bf), rowo(FOX_W, f32), rowo(FOX_W, f32), rowo(FOX_W, bf), rowo(FOX_W, bf), vto(FOX_HEADS),
            rowo(LANE, f32),
            rowo(DSA_W, bf), rowo(DSA_KV_W, f32), rowo(DSA_KV_W, f32), rowo(DSA_KV_W, bf), rowo(DSA_KV_W, bf),
            vto(DSA_KV_HEADS),
            rowo(IDX_W, bf), rowo(IDX_W, bf), rowo(LANE, f32), rowo(LANE, bf), rowo(LANE, f32),
            (pl.BlockSpec((LANE, bm), lambda i: (0, i)), jax.ShapeDtypeStruct((LANE, m), f32)),
            rowo(MLA_QW, bf), rowo(MLA_KV_LORA, f32), rowo(LANE, f32)]
    return pl.pallas_call(
        _post_kernel,
        grid=(m // bm,),
        in_specs=[row(PROJ_W), row(9 * LANE), full(fox_bias_p), full(q_norm), full(kv_norm), full(wuq_p)],
        out_specs=[s for s, _ in outs],
        out_shape=[o for _, o in outs],
        compiler_params=_cparams(("parallel",)),
        name="post_proj",
    )(proj, tab, fox_bias_p, q_norm, kv_norm, wuq_p)


def _kv_up_kernel(ckv_ref, kr_ref, w_ref, k_o, v_o, vt_o):
    bf = jnp.bfloat16
    kv = _dot(ckv_ref[...].astype(bf), w_ref[...])
    kr = kr_ref[...].astype(bf)
    for h in range(MLA_HEADS):
        k_o[:, h * MLA_QP:h * MLA_QP + LANE] = kv[:, h * LANE:(h + 1) * LANE].astype(bf)
        k_o[:, h * MLA_QP + LANE:(h + 1) * MLA_QP] = kr
        _store_value_t(vt_o, h, kv[:, (MLA_HEADS + h) * LANE:(MLA_HEADS + h + 1) * LANE])
    v_o[...] = kv[:, MLA_HEADS * LANE:].astype(bf)


def kv_up(ckv, kr, w, *, bm):
    r = ckv.shape[0]
    row = lambda w_: pl.BlockSpec((bm, w_), lambda i: (i, 0))
    return pl.pallas_call(
        _kv_up_kernel,
        grid=(r // bm,),
        in_specs=[row(MLA_KV_LORA), row(LANE), pl.BlockSpec(w.shape, lambda i: (0, 0))],
        out_specs=[row(MLA_QW), row(MLA_VW), pl.BlockSpec((1, MLA_HEADS * VR, bm), lambda i: (i, 0, 0))],
        out_shape=[jax.ShapeDtypeStruct((r, MLA_QW), jnp.bfloat16),
                   jax.ShapeDtypeStruct((r, MLA_VW), jnp.bfloat16),
                   jax.ShapeDtypeStruct((r // bm, MLA_HEADS * VR, bm), jnp.bfloat16)],
        compiler_params=_cparams(("parallel",)),
        name="mla_kv_up",
    )(ckv, kr, w)


def _cumsum_kernel(x_ref, c_o, ct_o, carry_ref, carry_t_ref):
    t = x_ref.shape[1]

    @pl.when(pl.program_id(1) == 0)
    def _():
        carry_ref[...] = jnp.zeros_like(carry_ref)
        carry_t_ref[...] = jnp.zeros_like(carry_t_ref)

    x = x_ref[0]
    r = lax.broadcasted_iota(jnp.int32, (t, t), 0)
    c = lax.broadcasted_iota(jnp.int32, (t, t), 1)
    lower = jnp.where(c <= r, 1.0, 0.0).astype(jnp.float32)
    upper = jnp.where(r <= c, 1.0, 0.0).astype(jnp.float32)
    cs = jnp.dot(lower, x, preferred_element_type=jnp.float32, precision=lax.Precision.HIGHEST)
    cs = cs + carry_ref[...]
    c_o[0] = cs * LOG2E
    carry_ref[...] = cs[t - 1:t, :]
    cst = jnp.dot(x.T, upper, preferred_element_type=jnp.float32, precision=lax.Precision.HIGHEST)
    cst = cst + carry_t_ref[...]
    ct_o[0] = cst[0:8, :] * LOG2E
    carry_t_ref[...] = cst[:, t - 1:t]


def cumsum_rows(x, *, bt):
    b, s, _ = x.shape
    return pl.pallas_call(
        _cumsum_kernel,
        grid=(b, s // bt),
        in_specs=[pl.BlockSpec((1, bt, LANE), lambda i, j: (i, j, 0))],
        out_specs=[pl.BlockSpec((1, bt, LANE), lambda i, j: (i, j, 0)),
                   pl.BlockSpec((1, 8, bt), lambda i, j: (i, 0, j))],
        out_shape=[jax.ShapeDtypeStruct((b, s, LANE), jnp.float32),
                   jax.ShapeDtypeStruct((b, 8, s), jnp.float32)],
        scratch_shapes=[pltpu.VMEM((1, LANE), jnp.float32), pltpu.VMEM((LANE, 1), jnp.float32)],
        compiler_params=_cparams(("parallel", "arbitrary")),
        name="logf_cumsum",
    )(x)


def _num_key_tiles(i, bq, bk, q_off, nk_max):
    last = q_off + (i + 1) * bq
    return jnp.minimum((last + bk - 1) // bk, nk_max)


def _softmax_steps(ss, vts, m_ref, l_ref, acc_ref):
    ps, alphas = [], []
    for t, s in enumerate(ss):
        m_prev = m_ref[t]
        m_new = jnp.maximum(m_prev, jnp.max(s, axis=1, keepdims=True))
        alpha = jnp.exp2(m_prev - m_new)
        p = jnp.exp2(s - m_new)
        l_ref[t] = alpha * l_ref[t] + jnp.sum(p, axis=1, keepdims=True)
        m_ref[t] = m_new
        ps.append(p.astype(jnp.bfloat16))
        alphas.append(alpha)
    for t, (p, alpha) in enumerate(zip(ps, alphas)):
        acc_ref[t] = alpha * acc_ref[t] + _dot(p, vts[t])


def _flash_kernel(*refs, bq, bk, q_off, nk_max, fox, hg, dqk, dv):
    if fox:
        q_ref, k_ref, v_ref, cq_ref, ck_ref, o_ref, m_ref, l_ref, acc_ref = refs
    else:
        q_ref, k_ref, v_ref, o_ref, m_ref, l_ref, acc_ref = refs
    gi = pl.program_id(1)
    i = pl.program_id(2)
    first = q_off + i * bq
    qpos = first + lax.broadcasted_iota(jnp.int32, (bq, 1), 0)
    if fox:
        lane = lax.broadcasted_iota(jnp.int32, (bq, LANE), 1)
        cq_all = cq_ref[0]
        cqs = [jnp.sum(jnp.where(lane == gi * hg + t, cq_all, 0.0), axis=1, keepdims=True) for t in range(hg)]
        c0 = [c[0:1, :] for c in cqs]
        bias_q = [c - z for c, z in zip(cqs, c0)]
    nk = _num_key_tiles(i, bq, bk, q_off, nk_max)
    n_full = jnp.minimum((first + (1 if fox else CHUNK)) // bk, nk)

    m_ref[...] = jnp.full(m_ref.shape, NEG_INF, jnp.float32)
    l_ref[...] = jnp.zeros(l_ref.shape, jnp.float32)
    acc_ref[...] = jnp.zeros(acc_ref.shape, jnp.float32)

    def make_body(masked):
        def body(j, _):
            start = pl.multiple_of(j * bk, bk)
            if masked:
                kpos = j * bk + lax.broadcasted_iota(jnp.int32, (1, bk), 1)
                mask = (kpos <= qpos) if fox else ((kpos >> 6) <= (qpos >> 6))
            ss = []
            for t in range(hg):
                kt = k_ref[0, pl.ds(start, bk), t * dqk:(t + 1) * dqk]
                s = _dot_nt(q_ref[0, :, t * dqk:(t + 1) * dqk], kt)
                if fox:
                    s = s + (bias_q[t] - (ck_ref[0, t, j] - c0[t]))
                ss.append(jnp.where(mask, s, NEG_INF) if masked else s)
            vts = [v_ref[0, pl.ds(start, bk), t * dv:(t + 1) * dv] for t in range(hg)]
            _softmax_steps(ss, vts, m_ref, l_ref, acc_ref)
            return 0
        return body

    lax.fori_loop(0, n_full, make_body(False), 0)
    lax.fori_loop(n_full, nk, make_body(True), 0)
    for t in range(hg):
        o_ref[0, :, t * dv:(t + 1) * dv] = (acc_ref[t] / l_ref[t]).astype(o_ref.dtype)


def flash_attention(q, k, v, cq=None, ck=None, *, lq, sp, heads, hg, dqk, dv, bq, bk, q_off):
    b = q.shape[0]
    nk_max = sp // bk
    fox = cq is not None
    in_specs = [pl.BlockSpec((1, bq, hg * dqk), lambda b_, g, i: (b_, i, g)),
                pl.BlockSpec((1, sp, hg * dqk), lambda b_, g, i: (b_, 0, g)),
                pl.BlockSpec((1, sp, hg * dv), lambda b_, g, i: (b_, 0, g))]
    args = [q, k, v]
    if fox:
        in_specs += [pl.BlockSpec((1, bq, LANE), lambda b_, g, i: (b_, i, 0)),
                     pl.BlockSpec((1, hg, nk_max, 1, bk), lambda b_, g, i: (b_, g, 0, 0, 0))]
        args += [cq, ck]
    return pl.pallas_call(
        functools.partial(_flash_kernel, bq=bq, bk=bk, q_off=q_off, nk_max=nk_max, fox=fox, hg=hg, dqk=dqk, dv=dv),
        grid=(b, heads // hg, lq // bq),
        in_specs=in_specs,
        out_specs=pl.BlockSpec((1, bq, hg * dv), lambda b_, g, i: (b_, i, g)),
        out_shape=jax.ShapeDtypeStruct((b, lq, heads * dv), jnp.bfloat16),
        scratch_shapes=[pltpu.VMEM((hg, bq, 1), jnp.float32), pltpu.VMEM((hg, bq, 1), jnp.float32),
                        pltpu.VMEM((hg, bq, dv), jnp.float32)],
        compiler_params=_cparams(("parallel", "parallel", "arbitrary")),
        name="fox_attention" if fox else "mla_attention",
    )(*args)


_KEY_NEG_INF = -2139095041
_F32_LOWEST = -3.4028234663852886e38


_FOLD = 64
_MAX_PROBES = 8 * 34


def _key_to_f32(key):
    bits = jnp.where(key >= 0, key, key ^ jnp.int32(0x7FFFFFFF))
    return lax.bitcast_convert_type(bits, jnp.float32)


def _f32_to_key(x):
    bits = lax.bitcast_convert_type(x, jnp.int32)
    return jnp.where(bits >= 0, bits, bits ^ jnp.int32(0x7FFFFFFF))


def _dsa_kernel(iq_ref, iqs_ref, iw_ref, dq_ref, ik_ref, dk_ref, dv_ref, o_ref,
                sc_ref, qg_ref, m_ref, l_ref, acc_ref, *, bq, bk, q_off, nk_max, topk):
    i = pl.program_id(1)
    nk = _num_key_tiles(i, bq, bk, q_off, nk_max)
    qpos = q_off + i * bq + lax.broadcasted_iota(jnp.int32, (bq, 1), 0)
    klim = ((qpos >> 6) + 1) << 6
    iw = iw_ref[0]

    def score_tile(j, _):
        kt = ik_ref[0, pl.ds(pl.multiple_of(j * bk, bk), bk), :]
        acc = jnp.zeros((bq, bk), jnp.float32)
        for g in range(IDX_HEADS // 2):
            se = _dot_nt(iq_ref[0, :, g * LANE:(g + 1) * LANE], kt)
            so = _dot_nt(iqs_ref[0, :, g * LANE:(g + 1) * LANE], kt)
            acc = acc + iw[:, 2 * g:2 * g + 1] * jnp.maximum(se, 0.0)
            acc = acc + iw[:, 2 * g + 1:2 * g + 2] * jnp.maximum(so, 0.0)
        kpos = j * bk + lax.broadcasted_iota(jnp.int32, (1, bk), 1)
        sc_ref[j] = jnp.where(kpos < klim, acc, -jnp.inf)
        return 0

    lax.fori_loop(0, nk, score_tile, 0)

    def count(hit):
        def body(j, acc):
            h = hit(sc_ref[j], j)
            for c in range(bk // LANE):
                acc = acc + h[:, c * LANE:(c + 1) * LANE]
            return acc
        acc = lax.fori_loop(0, nk, body, jnp.zeros((bq, LANE), jnp.float32))
        return jnp.sum(acc, axis=1, keepdims=True)

    count_ge = lambda pivot: count(lambda t, j: jnp.where(t >= pivot, 1.0, 0.0))
    kf = jnp.float32(topk)
    nonneg = count_ge(jnp.zeros((bq, 1), jnp.float32)) >= kf
    key0 = jnp.where(nonneg, jnp.int32(0), jnp.int32(-2 ** 31))

    def bit_step(it, key):
        cand = key | (jnp.int32(1) << (30 - it))
        ok = (count_ge(_key_to_f32(cand)) >= kf) | (cand <= _KEY_NEG_INF)
        return jnp.where(ok, cand, key)

    key = lax.fori_loop(0, 31, bit_step, key0)
    thr = jnp.maximum(_key_to_f32(key), _F32_LOWEST)

    tie = count_ge(thr) > kf
    total = nk_max * bk

    @pl.when(jnp.sum(jnp.where(tie, 1, 0)) > 0)
    def _():
        need = kf - count(lambda t, j: jnp.where(t > thr, 1.0, 0.0))
        cols = lax.broadcasted_iota(jnp.int32, (bq, bk), 1)

        def count_equal_upto(j_max):
            return count(lambda t, j: jnp.where(t == thr, jnp.where(cols <= j_max - j * bk, 1.0, 0.0), 0.0))

        def index_step(_, bracket):
            j_lo, j_hi = bracket
            mid = (j_lo + j_hi) >> 1
            ok = count_equal_upto(mid) >= need
            return jnp.where(ok, j_lo, mid), jnp.where(ok, mid, j_hi)

        bracket = (jnp.full((bq, 1), -1, jnp.int32), jnp.full((bq, 1), total - 1, jnp.int32))
        _, j_cut = lax.fori_loop(0, total.bit_length() + 1, index_step, bracket)
        j_cut = jnp.where(tie, j_cut, total)

        def drop_late_ties(j, _):
            t = sc_ref[j]
            sc_ref[j] = jnp.where(t == thr, jnp.where(cols > j_cut - j * bk, -jnp.inf, t), t)
            return 0

        lax.fori_loop(0, nk, drop_late_ties, 0)

    for h in range(DSA_HEADS):
        g, r = divmod(h, DSA_GROUP)
        qg_ref[g, r * bq:(r + 1) * bq, :] = dq_ref[0, :, h * LANE:(h + 1) * LANE]
    m_ref[...] = jnp.full(m_ref.shape, NEG_INF, jnp.float32)
    l_ref[...] = jnp.zeros(l_ref.shape, jnp.float32)
    acc_ref[...] = jnp.zeros(acc_ref.shape, jnp.float32)

    def att_tile(j, _):
        start = pl.multiple_of(j * bk, bk)
        drop = jnp.where(sc_ref[j] >= thr, 0.0, NEG_INF)
        drop = jnp.concatenate([drop] * DSA_GROUP, axis=0)
        ss = [_dot_nt(qg_ref[g], dk_ref[0, pl.ds(start, bk), g * LANE:(g + 1) * LANE]) + drop
              for g in range(DSA_KV_HEADS)]
        vts = [dv_ref[0, pl.ds(start, bk), g * LANE:(g + 1) * LANE] for g in range(DSA_KV_HEADS)]
        _softmax_steps(ss, vts, m_ref, l_ref, acc_ref)
        return 0

    lax.fori_loop(0, nk, att_tile, 0)
    for h in range(DSA_HEADS):
        g, r = divmod(h, DSA_GROUP)
        rows = slice(r * bq, (r + 1) * bq)
        o_ref[0, :, h * LANE:(h + 1) * LANE] = (acc_ref[g, rows, :] / l_ref[g, rows, :]).astype(o_ref.dtype)


def dsa_attention(iq, iqs, iw, dq, ik, dk, dv, *, lq, sp, bq, bk, q_off, topk):
    b = dq.shape[0]
    nk_max = sp // bk
    qrow = lambda w: pl.BlockSpec((1, bq, w), lambda b_, i: (b_, i, 0))
    krow = lambda w: pl.BlockSpec((1, sp, w), lambda b_, i: (b_, 0, 0))
    gr = DSA_GROUP * bq
    return pl.pallas_call(
        functools.partial(_dsa_kernel, bq=bq, bk=bk, q_off=q_off, nk_max=nk_max, topk=topk),
        grid=(b, lq // bq),
        in_specs=[qrow(IDX_W), qrow(IDX_W), qrow(LANE), qrow(DSA_W), krow(LANE), krow(DSA_KV_W), krow(DSA_KV_W)],
        out_specs=qrow(DSA_W),
        out_shape=jax.ShapeDtypeStruct((b, lq, DSA_W), jnp.bfloat16),
        scratch_shapes=[pltpu.VMEM((nk_max, bq, bk), jnp.float32),
                        pltpu.VMEM((DSA_KV_HEADS, gr, LANE), jnp.bfloat16),
                        pltpu.VMEM((DSA_KV_HEADS, gr, 1), jnp.float32),
                        pltpu.VMEM((DSA_KV_HEADS, gr, 1), jnp.float32),
                        pltpu.VMEM((DSA_KV_HEADS, gr, LANE), jnp.float32)],
        compiler_params=_cparams(("parallel", "arbitrary")),
        name="dsa_attention",
    )(iq, iqs, iw, dq, ik, dk, dv)


def _split3(x):
    hi = x.astype(jnp.bfloat16).astype(jnp.float32)
    r = x - hi
    mid = r.astype(jnp.bfloat16).astype(jnp.float32)
    lo = (r - mid).astype(jnp.bfloat16).astype(jnp.float32)
    return hi, mid, lo


def _fox_pack_kernel(q_ref, k_ref, c_ref, qa_o, ka_o, qn_o, kn_o):
    rows = q_ref.shape[0]
    lane = lax.broadcasted_iota(jnp.int32, (rows, LANE), 1)
    c = c_ref[...]
    zero = jnp.zeros((rows, LANE), jnp.float32)
    qn, kn = zero, zero
    for h in range(FOX_HEADS):
        qh = q_ref[:, h * LANE:(h + 1) * LANE].astype(jnp.float32)
        kh = k_ref[:, h * LANE:(h + 1) * LANE].astype(jnp.float32)
        qn = jnp.where(lane == h, jnp.sum(qh * qh, axis=1, keepdims=True), qn)
        kn = jnp.where(lane == h, jnp.sum(kh * kh, axis=1, keepdims=True), kn)
        ch = jnp.sum(jnp.where(lane == h, c, 0.0), axis=1, keepdims=True)
        hi, mid, lo = _split3(ch)
        terms = jnp.where(lane == 0, hi, jnp.where(lane == 1, mid, jnp.where(lane == 2, lo, zero)))
        ones_hi = jnp.where((lane >= 3) & (lane < 6), 1.0, 0.0)
        qa_o[:, h * FOX_QP:h * FOX_QP + LANE] = q_ref[:, h * LANE:(h + 1) * LANE]
        qa_o[:, h * FOX_QP + LANE:(h + 1) * FOX_QP] = (terms + ones_hi).astype(jnp.bfloat16)
        ones_lo = jnp.where(lane < 3, 1.0, 0.0)
        ka_o[:, h * FOX_QP:h * FOX_QP + LANE] = k_ref[:, h * LANE:(h + 1) * LANE]
        ka_o[:, h * FOX_QP + LANE:(h + 1) * FOX_QP] = (ones_lo - pltpu.roll(terms, 3, 1)).astype(jnp.bfloat16)
    qn_o[...] = qn
    kn_o[...] = kn


def fox_pack(q, k, c, *, rows, bm):
    row = lambda w: pl.BlockSpec((bm, w), lambda i: (i, 0))
    return pl.pallas_call(
        _fox_pack_kernel,
        grid=(rows // bm,),
        in_specs=[row(FOX_W), row(FOX_W), row(LANE)],
        out_specs=[row(FOX_HEADS * FOX_QP), row(FOX_HEADS * FOX_QP), row(LANE), row(LANE)],
        out_shape=[jax.ShapeDtypeStruct((rows, FOX_HEADS * FOX_QP), jnp.bfloat16)] * 2
        + [jax.ShapeDtypeStruct((rows, LANE), jnp.float32)] * 2,
        compiler_params=_cparams(("parallel",)),
        name="fox_pack",
    )(q, k, c)


def _logits_stage(slot, sts, s_ref, cm_ref):
    for t, st in enumerate(sts):
        s_ref[slot, t] = st
        cm_ref[slot, t] = jnp.max(st, axis=0, keepdims=True)


def _values_stage(slot, vtts, s_ref, cm_ref, m_ref, acc_ref):
    ps, alphas = [], []
    for t in range(len(vtts)):
        m_prev = m_ref[t]
        m_new = jnp.maximum(m_prev, cm_ref[slot, t])
        alphas.append(jnp.exp2(m_prev - m_new))
        ps.append(jnp.exp2(s_ref[slot, t] - m_new).astype(jnp.bfloat16))
        m_ref[t] = m_new
    for t, (p, alpha) in enumerate(zip(ps, alphas)):
        acc_ref[t] = alpha * acc_ref[t] + _dot(vtts[t], p)


def _pipelined_tiles(nk, nk_max, logits, values, start=0):
    last = nk_max - 1
    n = nk - start
    logits(start, 0)

    def run(first, tiles):
        for u in range(tiles):
            logits(jnp.minimum(start + first + u + 1, last), (u + 1) % 2)
            values(start + first + u, u % 2)

    def quad(jj, _):
        run(4 * jj, 4)
        return 0

    def pair(jj, _):
        run(2 * jj, 2)
        return 0

    n_quads = lax.shift_right_logical(n, 2)
    lax.fori_loop(0, n_quads, quad, 0)
    lax.fori_loop(2 * n_quads, lax.shift_right_logical(n, 1), pair, 0)

    @pl.when((n & 1) == 1)
    def _():
        values(nk - 1, 0)


def _finish_t(acc):
    return (acc[0:HEAD_DIM, :] / acc[HEAD_DIM:HEAD_DIM + 1, :]).T


_UNDERFLOW_EXP = 160.0


def _first_live_tile(q_ref, k_ref, qn_ref, c_first_ref, c_end_ref, kmax_ref, *, i, first, bq, bk, hg, dqk):
    g = pl.program_id(1)
    slack = []
    for t in range(hg):
        h = g * hg + t
        lane = lax.broadcasted_iota(jnp.int32, (bq, LANE), 1)
        q_norm = jnp.sqrt(jnp.max(jnp.where(lane == h, qn_ref[0], 0.0)))
        qt = q_ref[0, :, t * dqk:(t + 1) * dqk].astype(jnp.float32)
        kt = k_ref[0, pl.ds(pl.multiple_of(first, bq), bq), t * dqk:(t + 1) * dqk].astype(jnp.float32)
        diag_min = jnp.min(jnp.sum(qt * kt, axis=1, keepdims=True))
        slack.append(q_norm * kmax_ref[0, h] * 1.01 + c_first_ref[i, h] - diag_min + _UNDERFLOW_EXP)
    g0 = g * hg

    def dead(j):
        ok = c_end_ref[j, g0] > slack[0]
        for t in range(1, hg):
            ok = ok & (c_end_ref[j, g0 + t] > slack[t])
        return ok

    n_old = first // bk
    return lax.while_loop(lambda j: (j < n_old) & dead(j), lambda j: j + 1, jnp.int32(0))


def _flash_t_kernel(*refs, bq, bk, nk_max, frame_causal, hg, dqk, decay):
    if decay:
        q_ref, k_ref, vt_ref, qn_ref, c_first_ref, c_end_ref, kmax_ref, o_ref, m_ref, acc_ref, s_ref, cm_ref, d_ref = refs
    else:
        q_ref, k_ref, vt_ref, o_ref, m_ref, acc_ref, s_ref, cm_ref, d_ref = refs
    i = pl.program_id(2)
    first = i * bq
    nk = _num_key_tiles(i, bq, bk, 0, nk_max)
    m_ref[...] = jnp.full(m_ref.shape, NEG_INF, jnp.float32)
    acc_ref[...] = jnp.zeros(acc_ref.shape, jnp.float32)
    r = lax.broadcasted_iota(jnp.int32, (bk, bq), 0)
    c = lax.broadcasted_iota(jnp.int32, (bk, bq), 1)
    d_ref[...] = (r - c) if frame_causal else ((r >> 6) - (c >> 6))

    def logits(j, slot):
        start = pl.multiple_of(j * bk, bk)
        gap = first - j * bk
        visible = gap if frame_causal else (gap >> 6)
        mask = d_ref[...] <= visible
        sts = []
        for t in range(hg):
            kt = k_ref[0, pl.ds(start, bk), t * dqk:(t + 1) * dqk]
            st = _dot_nt(kt, q_ref[0, :, t * dqk:(t + 1) * dqk])
            sts.append(jnp.where(mask, st, NEG_INF))
        _logits_stage(slot, sts, s_ref, cm_ref)

    def values(j, slot):
        _values_stage(slot, [vt_ref[0, j, t * VR:(t + 1) * VR, :] for t in range(hg)], s_ref, cm_ref, m_ref, acc_ref)

    start = 0
    if decay:
        start = _first_live_tile(q_ref, k_ref, qn_ref, c_first_ref, c_end_ref, kmax_ref,
                                 i=i, first=first, bq=bq, bk=bk, hg=hg, dqk=dqk)
    _pipelined_tiles(nk, nk_max, logits, values, start)
    for t in range(hg):
        o_ref[0, :, t * HEAD_DIM:(t + 1) * HEAD_DIM] = _finish_t(acc_ref[t]).astype(o_ref.dtype)


def flash_attention_t(q, k, vt, decay=None, *, lq, sp, heads, hg, dqk, bq, bk, frame_causal, name):
    nk_max = sp // bk
    smem = pl.BlockSpec(memory_space=pltpu.SMEM)
    extra_specs, extra = [], []
    if decay is not None:
        extra_specs = [pl.BlockSpec((1, bq, LANE), lambda b_, g, i: (b_, i, 0)), smem, smem, smem]
        extra = list(decay)
    return pl.pallas_call(
        functools.partial(_flash_t_kernel, bq=bq, bk=bk, nk_max=nk_max, frame_causal=frame_causal, hg=hg, dqk=dqk,
                          decay=decay is not None),
        grid=(1, heads // hg, lq // bq),
        in_specs=[pl.BlockSpec((1, bq, hg * dqk), lambda b_, g, i: (b_, i, g)),
                  _resident((1, sp, hg * dqk), lambda b_, g, i: (b_, 0, g)),
                  _resident((1, nk_max, hg * VR, bk), lambda b_, g, i: (b_, 0, g, 0))] + extra_specs,
        out_specs=pl.BlockSpec((1, bq, hg * HEAD_DIM), lambda b_, g, i: (b_, i, g)),
        out_shape=jax.ShapeDtypeStruct((1, lq, heads * HEAD_DIM), jnp.bfloat16),
        scratch_shapes=[pltpu.VMEM((hg, 1, bq), jnp.float32), pltpu.VMEM((hg, VR, bq), jnp.float32),
                        pltpu.VMEM((2, hg, bk, bq), jnp.float32), pltpu.VMEM((2, hg, 1, bq), jnp.float32),
                        pltpu.VMEM((bk, bq), jnp.int32)],
        compiler_params=_cparams(("parallel", "parallel", "arbitrary")),
        name=name,
    )(q, k, vt, *extra)


def _dsa_t_kernel(iq_ref, iqs_ref, iwt_ref, dq_ref, ik_ref, dk_ref, dvt_ref, o_ref,
                  sc_ref, m_ref, acc_ref, s_ref, cm_ref, *, bq, bk, nk_max, topk):
    i = pl.program_id(1)
    nk = _num_key_tiles(i, bq, bk, 0, nk_max)
    qpos = i * bq + lax.broadcasted_iota(jnp.int32, (1, bq), 1)
    klim = ((qpos >> 6) + 1) << 6

    fold = lambda a, op: op(a.reshape(bk // _FOLD, _FOLD, bq), axis=0)

    def score_tile(j, carry):
        hi, lo = carry
        kt = ik_ref[0, pl.ds(pl.multiple_of(j * bk, bk), bk), :]
        acc = jnp.zeros((bk, bq), jnp.float32)
        for g in range(IDX_HEADS // 2):
            se = _dot_nt(kt, iq_ref[0, :, g * LANE:(g + 1) * LANE])
            so = _dot_nt(kt, iqs_ref[0, :, g * LANE:(g + 1) * LANE])
            acc = acc + iwt_ref[2 * g:2 * g + 1, :] * jnp.maximum(se, 0.0)
            acc = acc + iwt_ref[2 * g + 1:2 * g + 2, :] * jnp.maximum(so, 0.0)
        adm = (j * bk + lax.broadcasted_iota(jnp.int32, (bk, bq), 0)) < klim
        sc = jnp.where(adm, acc, -jnp.inf)
        sc_ref[j] = sc
        return (jnp.maximum(hi, fold(sc, jnp.max)), jnp.minimum(lo, fold(jnp.where(adm, acc, jnp.inf), jnp.min)))

    n_pairs = lax.shift_right_logical(nk, 1)
    carry = lax.fori_loop(0, n_pairs, lambda jj, c: score_tile(2 * jj + 1, score_tile(2 * jj, c)),
                          (jnp.full((_FOLD, bq), -jnp.inf, jnp.float32), jnp.full((_FOLD, bq), jnp.inf, jnp.float32)))
    hi, lo = lax.fori_loop(2 * n_pairs, nk, score_tile, carry)
    hi = jnp.max(hi, axis=0, keepdims=True)
    lo = jnp.min(lo, axis=0, keepdims=True)

    def count_ge(pivot):
        def body(j, acc):
            for r in range(bk // _FOLD):
                acc = acc + jnp.where(sc_ref[j, r * _FOLD:(r + 1) * _FOLD, :] >= pivot, 1.0, 0.0)
            return acc
        acc = lax.fori_loop(0, nk, body, jnp.zeros((_FOLD, bq), jnp.float32))
        return jnp.sum(acc, axis=0, keepdims=True)

    kf = jnp.float32(topk)
    hi = _key_to_f32(_f32_to_key(hi) + 1)
    c_lo = klim.astype(jnp.float32)
    c_hi = jnp.zeros((1, bq), jnp.float32)

    def settled(lo, hi, c_lo):
        width = lax.bitcast_convert_type(_f32_to_key(hi) - _f32_to_key(lo), jnp.uint32)
        return (c_lo <= kf) | (width <= 1)

    def search_cond(state):
        it, n_open = state[0], state[1]
        return (n_open > 0) & (it < _MAX_PROBES)

    def probe_once(it, lo, hi, c_lo, c_hi):
        done = settled(lo, hi, c_lo)
        k_lo, k_hi = _f32_to_key(lo), _f32_to_key(hi)
        width = lax.bitcast_convert_type(k_hi - k_lo, jnp.uint32)
        k_mid = k_lo + lax.bitcast_convert_type(width >> 1, jnp.int32)
        k_val = jnp.clip(_f32_to_key(lo + (hi - lo) * 0.5), k_lo + 1, k_hi - 1)
        probe = _key_to_f32(jnp.where((it & 7) == 7, k_mid, k_val))
        probe = jnp.where(done, lo, probe)
        c = count_ge(probe)
        up = (c >= kf) & ~done
        down = (c < kf) & ~done
        return (jnp.where(up, probe, lo), jnp.where(down, probe, hi), jnp.where(up, c, c_lo), jnp.where(down, c, c_hi))

    def search_step(state):
        it, _, lo, hi, c_lo, c_hi = state
        lo, hi, c_lo, c_hi = probe_once(it, lo, hi, c_lo, c_hi)
        lo, hi, c_lo, c_hi = probe_once(it + 1, lo, hi, c_lo, c_hi)
        n_open = jnp.sum(jnp.where(settled(lo, hi, c_lo), 0, 1))
        return it + 2, n_open, lo, hi, c_lo, c_hi

    state = (jnp.int32(0), jnp.sum(jnp.where(settled(lo, hi, c_lo), 0, 1)), lo, hi, c_lo, c_hi)
    _, _, thr, _, c_lo, c_hi = lax.while_loop(search_cond, search_step, state)

    tie = c_lo > kf
    total = nk_max * bk

    @pl.when(jnp.sum(jnp.where(tie, 1, 0)) > 0)
    def _():
        need = kf - c_hi
        rows = lax.broadcasted_iota(jnp.int32, (bk, bq), 0)

        def count_equal_upto(j_max):
            def body(j, acc):
                hit = jnp.where(sc_ref[j] == thr, jnp.where(rows <= j_max - j * bk, 1.0, 0.0), 0.0)
                return acc + fold(hit, jnp.sum)
            acc = lax.fori_loop(0, nk, body, jnp.zeros((_FOLD, bq), jnp.float32))
            return jnp.sum(acc, axis=0, keepdims=True)

        def index_step(_, bracket):
            j_lo, j_hi = bracket
            mid = (j_lo + j_hi) >> 1
            ok = count_equal_upto(mid) >= need
            return jnp.where(ok, j_lo, mid), jnp.where(ok, mid, j_hi)

        bracket = (jnp.full((1, bq), -1, jnp.int32), jnp.full((1, bq), total - 1, jnp.int32))
        _, j_cut = lax.fori_loop(0, total.bit_length() + 1, index_step, bracket)
        j_cut = jnp.where(tie, j_cut, total)

        def drop_late_ties(j, _):
            t = sc_ref[j]
            sc_ref[j] = jnp.where(t == thr, jnp.where(rows > j_cut - j * bk, -jnp.inf, t), t)
            return 0

        lax.fori_loop(0, nk, drop_late_ties, 0)

    m_ref[...] = jnp.full(m_ref.shape, NEG_INF, jnp.float32)
    acc_ref[...] = jnp.zeros(acc_ref.shape, jnp.float32)

    def logits(j, slot):
        start = pl.multiple_of(j * bk, bk)
        drop = jnp.where(sc_ref[j] >= thr, 0.0, NEG_INF)
        sts = []
        for h in range(DSA_HEADS):
            g = h // DSA_GROUP
            kt = dk_ref[0, pl.ds(start, bk), g * LANE:(g + 1) * LANE]
            sts.append(_dot_nt(kt, dq_ref[0, :, h * LANE:(h + 1) * LANE]) + drop)
        _logits_stage(slot, sts, s_ref, cm_ref)

    def values(j, slot):
        vtts = [dvt_ref[0, j, (h // DSA_GROUP) * VR:(h // DSA_GROUP + 1) * VR, :] for h in range(DSA_HEADS)]
        _values_stage(slot, vtts, s_ref, cm_ref, m_ref, acc_ref)

    _pipelined_tiles(nk, nk_max, logits, values)
    for h in range(DSA_HEADS):
        o_ref[0, :, h * LANE:(h + 1) * LANE] = _finish_t(acc_ref[h]).astype(o_ref.dtype)


def dsa_attention_t(iq, iqs, iwt, dq, ik, dk, dvt, *, lq, sp, bq, bk, topk):
    nk_max = sp // bk
    qrow = lambda w: pl.BlockSpec((1, bq, w), lambda b_, i: (b_, i, 0))
    krow = lambda w: _resident((1, sp, w), lambda b_, i: (b_, 0, 0))
    return pl.pallas_call(
        functools.partial(_dsa_t_kernel, bq=bq, bk=bk, nk_max=nk_max, topk=topk),
        grid=(1, lq // bq),
        in_specs=[qrow(IDX_W), qrow(IDX_W), pl.BlockSpec((LANE, bq), lambda b_, i: (0, i)), qrow(DSA_W),
                  krow(LANE), krow(DSA_KV_W),
                  _resident((1, nk_max, DSA_KV_HEADS * VR, bk), lambda b_, i: (b_, 0, 0, 0))],
        out_specs=qrow(DSA_W),
        out_shape=jax.ShapeDtypeStruct((1, lq, DSA_W), jnp.bfloat16),
        scratch_shapes=[pltpu.VMEM((nk_max, bk, bq), jnp.float32),
                        pltpu.VMEM((DSA_HEADS, 1, bq), jnp.float32),
                        pltpu.VMEM((DSA_HEADS, VR, bq), jnp.float32),
                        pltpu.VMEM((2, DSA_HEADS, bk, bq), jnp.float32),
                        pltpu.VMEM((2, DSA_HEADS, 1, bq), jnp.float32)],
        compiler_params=_cparams(("parallel", "arbitrary")),
        name="dsa_attention_t",
    )(iq, iqs, iwt, dq, ik, dk, dvt)


def _out_proj_kernel(x_ref, a_ref, b_ref, c_ref, wa_ref, wb_ref, wc_ref, o_ref):
    o_ref[...] = (x_ref[...] + _dot(a_ref[...], wa_ref[...]) + _dot(b_ref[...], wb_ref[...])
                  + _dot(c_ref[...], wc_ref[...]))


def out_proj(x, a, b, c, w, layer, *, bm):
    m, d = x.shape
    wa, wb, wc = a.shape[1], b.shape[1], c.shape[1]
    assert wa == wb and (wa + wb) % wc == 0
    row = lambda w_: pl.BlockSpec((bm, w_), lambda i: (i, 0))
    band = lambda rows, blk: _resident((None, rows, d), lambda i: (layer, blk, 0))
    return pl.pallas_call(
        _out_proj_kernel,
        grid=(m // bm,),
        in_specs=[row(d), row(wa), row(wb), row(wc), band(wa, 0), band(wb, 1), band(wc, (wa + wb) // wc)],
        out_specs=row(d),
        out_shape=jax.ShapeDtypeStruct((m, d), jnp.float32),
        compiler_params=_cparams(("parallel",)),
        name="out_proj",
    )(x, a, b, c, w, w, w)


def _ffn_kernel(x_ref, g_ref, wg_ref, wu_ref, wd_ref, o_ref, xn_ref, acc_ref):
    j = pl.program_id(1)

    @pl.when(j == 0)
    def _():
        xn_ref[...] = _rms(x_ref[...], g_ref[...]).astype(jnp.bfloat16)
        acc_ref[...] = jnp.zeros_like(acc_ref)

    xn = xn_ref[...]
    gate = _dot(xn, wg_ref[...])
    up = _dot(xn, wu_ref[...])
    hidden = (gate * (1.0 / (1.0 + jnp.exp(-gate))) * up).astype(jnp.bfloat16)
    acc_ref[...] += _dot(hidden, wd_ref[...])

    @pl.when(j == pl.num_programs(1) - 1)
    def _():
        o_ref[...] = x_ref[...] + acc_ref[...]


def ffn(x, g, wg, wu, wd, layer, *, bm, bf):
    m, d = x.shape
    f = wg.shape[2]
    return pl.pallas_call(
        _ffn_kernel,
        grid=(m // bm, f // bf),
        in_specs=[pl.BlockSpec((bm, d), lambda i, j: (i, 0)),
                  pl.BlockSpec((1, d), lambda i, j: (0, 0)),
                  pl.BlockSpec((None, d, bf), lambda i, j: (layer, 0, j)),
                  pl.BlockSpec((None, d, bf), lambda i, j: (layer, 0, j)),
                  pl.BlockSpec((None, bf, d), lambda i, j: (layer, j, 0))],
        out_specs=pl.BlockSpec((bm, d), lambda i, j: (i, 0)),
        out_shape=jax.ShapeDtypeStruct((m, d), jnp.float32),
        scratch_shapes=[pltpu.VMEM((bm, d), jnp.bfloat16), pltpu.VMEM((bm, d), jnp.float32)],
        compiler_params=_cparams(("parallel", "arbitrary")),
        name="swiglu",
    )(x, g.reshape(1, d), wg, wu, wd)


def _final_norm_kernel(x_ref, g_ref, o_ref):
    o_ref[...] = _rms(x_ref[...], g_ref[...])


def final_norm(x, g, *, bm):
    m, d = x.shape
    return pl.pallas_call(
        _final_norm_kernel,
        grid=(m // bm,),
        in_specs=[pl.BlockSpec((bm, d), lambda i: (i, 0)), pl.BlockSpec((1, d), lambda i: (0, 0))],
        out_specs=pl.BlockSpec((bm, d), lambda i: (i, 0)),
        out_shape=jax.ShapeDtypeStruct((m, d), jnp.float32),
        compiler_params=_cparams(("parallel",)),
        name="final_norm",
    )(x, g.reshape(1, d))


def _pad_cols(a, width):
    return jnp.pad(a, [(0, 0)] * (a.ndim - 1) + [(0, width - a.shape[-1])])


def _layout_w_in(w_in):
    sizes = (FOX_W, FOX_W, FOX_W, FOX_HEADS, DSA_W, DSA_KV_W, DSA_KV_W, IDX_W, IDX_DIM, IDX_HEADS,
             MLA_Q_LORA, MLA_KV_LORA, MLA_ROPE)
    splits = np.cumsum(sizes)[:-1]
    q_a, k_a, v_a, f_a, q_b, k_b, v_b, q_i, k_i, w_i, c_q, c_kv, k_r = jnp.split(w_in, splits, axis=-1)
    parts = {"fq": q_a, "fk": k_a, "fv": v_a, "dq": q_b, "dk": k_b, "dv": v_b, "iq": q_i, "cq": c_q,
             "ckv": c_kv, "fa": f_a, "ik": k_i, "iw": w_i, "kr": k_r}
    cols = [_pad_cols(parts[n], w) for n, w in _SEGS]
    return _pad_cols(jnp.concatenate(cols, axis=-1), PROJ_W).astype(jnp.bfloat16)


def _layout_w_uq(w_uq):
    dp = w_uq.shape[0]
    w = w_uq.reshape(dp, MLA_Q_LORA, MLA_HEADS, MLA_NOPE + MLA_ROPE)
    return _pad_cols(w, MLA_QP).reshape(dp, MLA_Q_LORA, MLA_QW).astype(jnp.bfloat16)


def _layout_w_ukv(w_ukv):
    dp = w_ukv.shape[0]
    w = w_ukv.reshape(dp, MLA_KV_LORA, MLA_HEADS, MLA_NOPE + MLA_V)
    kn = w[..., :MLA_NOPE].reshape(dp, MLA_KV_LORA, MLA_HEADS * MLA_NOPE)
    vv = w[..., MLA_NOPE:].reshape(dp, MLA_KV_LORA, MLA_VW)
    return jnp.concatenate([kn, vv], axis=-1).astype(jnp.bfloat16)


def _rope_table(pos, rot, period):
    half = rot // 2
    inv_freq = ROPE_THETA ** (-jnp.arange(half, dtype=jnp.float32) / half)
    ang = pos.astype(jnp.float32)[:, None] * inv_freq[None, :]
    cos, sin = jnp.cos(ang), jnp.sin(ang)
    r = pos.shape[0]
    ones = jnp.ones((r, period - rot), jnp.float32)
    zeros = jnp.zeros((r, period - rot), jnp.float32)
    zh = jnp.zeros((r, half), jnp.float32)
    c = jnp.concatenate([cos, cos, ones], axis=1)
    s1 = jnp.concatenate([zh, sin, zeros], axis=1)
    s2 = jnp.concatenate([-sin, zh, zeros], axis=1)
    rep = LANE // period
    return jnp.concatenate([jnp.tile(c, (1, rep)), jnp.tile(s1, (1, rep)), jnp.tile(s2, (1, rep))], axis=1)


def _pick(n, candidates):
    for c in candidates:
        if n % c == 0:
            return c
    raise ValueError(f"no block size among {candidates} divides {n}")


def _pad_seq(a, sp):
    return jnp.pad(a, [(0, 0), (0, sp - a.shape[1])] + [(0, 0)] * (a.ndim - 2))


def kernel(x_prompt, x_sample, cache_fox_k, cache_fox_v, cache_fox_logf, cache_dsa_k, cache_dsa_v, cache_idx_k, cache_mla_ckv, cache_mla_krope, w_in, fox_bias, mla_q_norm, mla_w_uq, mla_kv_norm, mla_w_ukv, w_o, attn_norm, ffn_norm, w_gate, w_up, w_down, final_norm_w):
    bf = jnp.bfloat16
    depth = w_in.shape[0]
    nb, seq, d = x_prompt.shape
    db, dec, _ = x_sample.shape
    past = cache_fox_k.shape[2]
    assert nb == 1 and seq % CHUNK == 0 and dec % CHUNK == 0 and past % CHUNK == 0
    mp, ms = nb * seq, db * dec
    m = mp + ms
    s_len = past + dec

    bm = _pick(m, (512, 256, 128, 64))
    assert mp % bm == 0
    bq_p = _pick(seq, (256, 128))
    bk_p = bm
    bq_s = _pick(dec, (64,))
    bk_s = 256
    sp = ((s_len + bk_s - 1) // bk_s) * bk_s
    topk_p = min(IDX_TOPK_MAX, seq // 4)
    topk_s = min(IDX_TOPK_MAX, s_len // 4)
    fox_hg, mla_hg = 3, 2
    d_ff = w_gate.shape[2]
    bff = _pick(d_ff, (512, 256, 128))

    w_in_p = _layout_w_in(w_in)
    wuq_p = _layout_w_uq(mla_w_uq)
    wukv_p = _layout_w_ukv(mla_w_ukv)
    w_o_b = w_o.astype(bf)
    wg_b, wu_b, wd_b = w_gate.astype(bf), w_up.astype(bf), w_down.astype(bf)
    fox_bias_p = _pad_cols(fox_bias, LANE).reshape(depth, 1, LANE)

    pos = jnp.concatenate([jnp.tile(jnp.arange(seq, dtype=jnp.int32), nb),
                           jnp.tile(past + jnp.arange(dec, dtype=jnp.int32), db)])
    tab = jnp.concatenate([_rope_table(pos, PARTIAL_ROT, HEAD_DIM), _rope_table(pos, IDX_ROT, IDX_DIM),
                           _rope_table(pos, MLA_ROPE, LANE)], axis=1)

    x = jnp.concatenate([x_prompt.reshape(mp, d), x_sample.reshape(ms, d)], axis=0)
    new_rows = []
    for l in range(depth):
        proj = norm_matmul(x, attn_norm[l], w_in_p, l, bm=bm, bn=_pick(PROJ_W, (1536, 1024, 512)))
        (fq, fk, fv, fkb, fvb, fvt, lf, dq, dk, dv, dkb, dvb, dvt, iq, iqs, ik, ikb, iw, iwt, mq, ckv, kr) = post_proj(
            proj, tab, fox_bias_p[l], mla_q_norm[l].reshape(1, -1), mla_kv_norm[l].reshape(1, -1), wuq_p[l],
            bm=bm // 2, bk=bm)

        pr = lambda a: a[:mp].reshape(nb, seq, -1)
        sr = lambda a: a[mp:].reshape(db, dec, -1)
        new_rows.append((fk, fv, lf, dk, dv, ik, ckv, kr))

        cat = lambda c, new: _pad_seq(jnp.concatenate([c, new], axis=1), sp)
        s_fkb = cat(cache_fox_k[l].reshape(db, past, FOX_W).astype(bf), sr(fkb))
        s_fvb = cat(cache_fox_v[l].reshape(db, past, FOX_W).astype(bf), sr(fvb))
        s_lf = cat(_pad_cols(cache_fox_logf[l], LANE), sr(lf))
        s_dkb = cat(cache_dsa_k[l].reshape(db, past, DSA_KV_W).astype(bf), sr(dkb))
        s_dvb = cat(cache_dsa_v[l].reshape(db, past, DSA_KV_W).astype(bf), sr(dvb))
        s_ikb = cat(_pad_cols(cache_idx_k[l], LANE).astype(bf), sr(ikb))
        s_ckv = cat(cache_mla_ckv[l], sr(ckv))
        s_kr = cat(_pad_cols(cache_mla_krope[l], LANE), sr(kr))

        ckv_all = jnp.concatenate([ckv[:mp], s_ckv.reshape(db * sp, -1)], axis=0)
        kr_all = jnp.concatenate([kr[:mp], s_kr.reshape(db * sp, -1)], axis=0)
        kmla, vmla, vtmla = kv_up(ckv_all, kr_all, wukv_p[l], bm=bm)
        s_kmla, s_vmla = kmla[mp:].reshape(db, sp, -1), vmla[mp:].reshape(db, sp, -1)

        p_c, _ = cumsum_rows(pr(lf), bt=_pick(seq, (256, 128)))
        s_c, s_ct = cumsum_rows(s_lf, bt=_pick(sp, (256, 128)))
        ck_form = lambda ct, bk: ct[:, :FOX_HEADS].reshape(ct.shape[0], FOX_HEADS, ct.shape[2] // bk, 1, bk)

        al = lambda a: a.reshape(1, *a.shape)
        fqa, fka, fqn, fkn = fox_pack(fq, fkb, p_c[0], rows=mp, bm=bm)
        decay = (al(fqn), p_c[0, ::bq_p], p_c[0, bk_p - 1::bk_p], jnp.sqrt(jnp.max(fkn, axis=0, keepdims=True)))
        out_a = flash_attention_t(al(fqa), al(fka), al(fvt), decay, lq=seq, sp=seq, heads=FOX_HEADS, hg=fox_hg, dqk=FOX_QP,
                                  bq=bq_p, bk=bk_p, frame_causal=True, name="fox_attention_t")
        out_b = dsa_attention_t(al(iq), al(iqs), iwt, al(dq), al(ikb), al(dkb), al(dvt), lq=seq, sp=seq,
                                bq=bq_p, bk=bk_p, topk=topk_p)
        out_c = flash_attention_t(al(mq), al(kmla), al(vtmla), lq=seq, sp=seq, heads=MLA_HEADS, hg=mla_hg, dqk=MLA_QP,
                                  bq=bq_p, bk=bk_p, frame_causal=False, name="mla_attention_t")
        s_out_a = flash_attention(sr(fq), s_fkb, s_fvb, s_c[:, past:past + dec], ck_form(s_ct, bk_s), lq=dec, sp=sp,
                                  heads=FOX_HEADS, hg=fox_hg, dqk=HEAD_DIM, dv=HEAD_DIM, bq=bq_s, bk=bk_s, q_off=past)
        s_out_b = dsa_attention(sr(iq), sr(iqs), sr(iw), sr(dq), s_ikb, s_dkb, s_dvb, lq=dec, sp=sp,
                                bq=bq_s, bk=bk_s, q_off=past, topk=topk_s)
        s_out_c = flash_attention(sr(mq), s_kmla, s_vmla, lq=dec, sp=sp, heads=MLA_HEADS, hg=mla_hg,
                                  dqk=MLA_QP, dv=MLA_V, bq=bq_s, bk=bk_s, q_off=past)

        rows = lambda p, s: jnp.concatenate([p.reshape(mp, -1), s.reshape(ms, -1)], axis=0)
        x = out_proj(x, rows(out_a, s_out_a), rows(out_b, s_out_b), rows(out_c, s_out_c), w_o_b, l, bm=bm)
        x = ffn(x, ffn_norm[l], wg_b, wu_b, wd_b, l, bm=bm, bf=bff)

    y = final_norm(x, final_norm_w, bm=bm)
    y_prompt = y[:mp].reshape(nb, seq, d)
    y_sample = y[mp:].reshape(db, dec, d)
    tails = ((FOX_HEADS, HEAD_DIM), (FOX_HEADS, HEAD_DIM), (FOX_HEADS,), (DSA_KV_HEADS, HEAD_DIM),
             (DSA_KV_HEADS, HEAD_DIM), (IDX_DIM,), (MLA_KV_LORA,), (MLA_ROPE,))
    p_out, s_out = [], []
    for arrays, tail in zip(zip(*new_rows), tails):
        w = math.prod(tail)
        p_out.append(jnp.stack([a[:mp, :w] for a in arrays]).reshape((depth, nb, seq) + tail))
        s_out.append(jnp.stack([a[mp:, :w] for a in arrays]).reshape((depth, db, dec) + tail))
    return (y_prompt, y_sample) + tuple(p_out) + tuple(s_out)
```

```python
import functools
import math

import jax
import jax.numpy as jnp
import numpy as np
from jax import lax
from jax.experimental import pallas as pl
from jax.experimental.pallas import tpu as pltpu

CHUNK = 64
HEAD_DIM = 128
FOX_HEADS = 6
DSA_HEADS = 6
DSA_KV_HEADS = 2
DSA_GROUP = DSA_HEADS // DSA_KV_HEADS
IDX_HEADS = 16
IDX_DIM = 64
IDX_TOPK_MAX = 256
MLA_HEADS = 4
MLA_Q_LORA = 512
MLA_KV_LORA = 256
MLA_NOPE = 128
MLA_ROPE = 64
MLA_V = 128
PARTIAL_ROT = HEAD_DIM // 4
IDX_ROT = IDX_DIM // 4
ROPE_THETA = 500000.0
EPS = 1e-6
NEG_INF = -1e30
FOX_SCALE = HEAD_DIM ** -0.5
DSA_SCALE = HEAD_DIM ** -0.5
MLA_SCALE = (MLA_NOPE + MLA_ROPE) ** -0.5
IDX_W_SCALE = (IDX_HEADS * IDX_DIM) ** -0.5
LOG2E = math.log2(math.e)

LANE = 128
VMEM_LIMIT = 56 * 1024 * 1024

FOX_W = FOX_HEADS * HEAD_DIM
DSA_W = DSA_HEADS * HEAD_DIM
DSA_KV_W = DSA_KV_HEADS * HEAD_DIM
IDX_W = IDX_HEADS * IDX_DIM
MLA_QP = 2 * LANE
MLA_QW = MLA_HEADS * MLA_QP
MLA_VW = MLA_HEADS * MLA_V
VR = HEAD_DIM + 16
FOX_QP = 2 * LANE

_SEGS = (("fq", FOX_W), ("fk", FOX_W), ("fv", FOX_W), ("dq", DSA_W), ("dk", DSA_KV_W), ("dv", DSA_KV_W),
         ("iq", IDX_W), ("cq", MLA_Q_LORA), ("ckv", MLA_KV_LORA),
         ("fa", LANE), ("ik", LANE), ("iw", LANE), ("kr", LANE))
_OFF = {}
_o = 0
for _n, _w in _SEGS:
    _OFF[_n] = _o
    _o += _w
PROJ_W = ((_o + 511) // 512) * 512

_NT = (((1,), (1,)), ((), ()))


def _dot(a, b):
    return jnp.dot(a, b, preferred_element_type=jnp.float32)


def _dot_nt(a, b):
    return lax.dot_general(a, b, _NT, preferred_element_type=jnp.float32)


def _cparams(sem):
    return pltpu.CompilerParams(dimension_semantics=sem, vmem_limit_bytes=VMEM_LIMIT)


def _resident(block_shape, index_map):
    return pl.BlockSpec(block_shape, index_map, pipeline_mode=pl.Buffered(1))


def _rms(x, g):
    return x * lax.rsqrt(jnp.mean(x * x, axis=-1, keepdims=True) + EPS) * g


def _norm_matmul_kernel(x_ref, g_ref, w_ref, o_ref, xn_ref):
    @pl.when(pl.program_id(1) == 0)
    def _():
        xn_ref[...] = _rms(x_ref[...], g_ref[...]).astype(jnp.bfloat16)

    o_ref[...] = _dot(xn_ref[...], w_ref[...])


def norm_matmul(x, g, w, layer, *, bm, bn):
    m, d = x.shape
    n = w.shape[2]
    return pl.pallas_call(
        _norm_matmul_kernel,
        grid=(m // bm, n // bn),
        in_specs=[pl.BlockSpec((bm, d), lambda i, j: (i, 0)),
                  pl.BlockSpec((1, d), lambda i, j: (0, 0)),
                  pl.BlockSpec((None, d, bn), lambda i, j: (layer, 0, j))],
        out_specs=pl.BlockSpec((bm, bn), lambda i, j: (i, j)),
        out_shape=jax.ShapeDtypeStruct((m, n), jnp.float32),
        scratch_shapes=[pltpu.VMEM((bm, d), jnp.bfloat16)],
        compiler_params=_cparams(("parallel", "arbitrary")),
        name="norm_matmul",
    )(x, g.reshape(1, d), w)


def _rope(x, tab, half):
    c, s1, s2 = tab[:, 0:LANE], tab[:, LANE:2 * LANE], tab[:, 2 * LANE:3 * LANE]
    return x * c + pltpu.roll(x, half, 1) * s1 + pltpu.roll(x, LANE - half, 1) * s2


def _value_t_tail(width):
    r = lax.broadcasted_iota(jnp.int32, (16, width), 0)
    return jnp.where(r == 0, 1.0, 0.0).astype(jnp.bfloat16)


def _store_value_t(o_ref, h, v):
    o_ref[0, h * VR:h * VR + HEAD_DIM, :] = v.T.astype(jnp.bfloat16)
    o_ref[0, h * VR + HEAD_DIM:(h + 1) * VR, :] = _value_t_tail(v.shape[0])


def _post_kernel(p_ref, tab_ref, fb_ref, qn_ref, kvn_ref, wuq_ref,
                 fq_o, fk_o, fv_o, fkb_o, fvb_o, fvt_o, lf_o,
                 dq_o, dk_o, dv_o, dkb_o, dvb_o, dvt_o,
                 iq_o, iqs_o, ik_o, ikb_o, iw_o, iwt_o,
                 mq_o, ckv_o, kr_o):
    bf = jnp.bfloat16
    seg = lambda name, w: p_ref[:, _OFF[name]:_OFF[name] + w]
    tab_d = tab_ref[:, 0:3 * LANE]
    tab_i = tab_ref[:, 3 * LANE:6 * LANE]
    tab_m = tab_ref[:, 6 * LANE:9 * LANE]

    fq_o[...] = (seg("fq", FOX_W) * (FOX_SCALE * LOG2E)).astype(bf)
    fk = seg("fk", FOX_W)
    fk_o[...] = fk
    fkb_o[...] = fk.astype(bf)
    fv = seg("fv", FOX_W)
    fv_o[...] = fv
    fvb_o[...] = fv.astype(bf)
    for h in range(FOX_HEADS):
        _store_value_t(fvt_o, h, p_ref[:, _OFF["fv"] + h * LANE:_OFF["fv"] + (h + 1) * LANE])
    z = seg("fa", LANE) + fb_ref[...]
    lf_o[...] = jnp.minimum(z, 0.0) - jnp.log1p(jnp.exp(-jnp.abs(z)))

    for h in range(DSA_HEADS):
        x = p_ref[:, _OFF["dq"] + h * LANE:_OFF["dq"] + (h + 1) * LANE]
        dq_o[:, h * LANE:(h + 1) * LANE] = (_rope(x, tab_d, PARTIAL_ROT // 2) * (DSA_SCALE * LOG2E)).astype(bf)
    for h in range(DSA_KV_HEADS):
        x = p_ref[:, _OFF["dk"] + h * LANE:_OFF["dk"] + (h + 1) * LANE]
        y = _rope(x, tab_d, PARTIAL_ROT // 2)
        dk_o[:, h * LANE:(h + 1) * LANE] = y
        dkb_o[:, h * LANE:(h + 1) * LANE] = y.astype(bf)
        _store_value_t(dvt_o, h, p_ref[:, _OFF["dv"] + h * LANE:_OFF["dv"] + (h + 1) * LANE])
    dv = seg("dv", DSA_KV_W)
    dv_o[...] = dv
    dvb_o[...] = dv.astype(bf)
    for j in range(IDX_W // LANE):
        x = p_ref[:, _OFF["iq"] + j * LANE:_OFF["iq"] + (j + 1) * LANE]
        y = _rope(x, tab_i, IDX_ROT // 2)
        iq_o[:, j * LANE:(j + 1) * LANE] = y.astype(bf)
        iqs_o[:, j * LANE:(j + 1) * LANE] = pltpu.roll(y, IDX_DIM, 1).astype(bf)
    y = _rope(seg("ik", LANE), tab_i, IDX_ROT // 2)
    ik_o[...] = y
    ikb_o[...] = y.astype(bf)
    iw = seg("iw", LANE) * IDX_W_SCALE
    iw_o[...] = iw
    iwt_o[...] = iw.T

    cqn = _rms(seg("cq", MLA_Q_LORA), qn_ref[...]).astype(bf)
    mq = _dot(cqn, wuq_ref[...])
    for h in range(MLA_HEADS):
        a = h * MLA_QP
        mq_o[:, a:a + LANE] = (mq[:, a:a + LANE] * (MLA_SCALE * LOG2E)).astype(bf)
        r = _rope(mq[:, a + LANE:a + 2 * LANE], tab_m, MLA_ROPE // 2)
        mq_o[:, a + LANE:a + 2 * LANE] = (r * (MLA_SCALE * LOG2E)).astype(bf)
    ckv_o[...] = _rms(seg("ckv", MLA_KV_LORA), kvn_ref[...])
    kr_o[...] = _rope(seg("kr", LANE), tab_m, MLA_ROPE // 2)


def post_proj(proj, tab, fox_bias_p, q_norm, kv_norm, wuq_p, *, bm, bk):
    m = proj.shape[0]
    r = bk // bm
    f32, bf = jnp.float32, jnp.bfloat16
    row = lambda w: pl.BlockSpec((bm, w), lambda i: (i, 0))
    full = lambda a: pl.BlockSpec(a.shape, lambda i: (0,) * a.ndim)
    vt = lambda heads: pl.BlockSpec((1, heads * VR, bm), lambda i: (i // r, 0, i % r))
    rowo = lambda w, dt: (row(w), jax.ShapeDtypeStruct((m, w), dt))
    vto = lambda heads: (vt(heads), jax.ShapeDtypeStruct((m // bk, heads * VR, bk), bf))
    outs = [rowo(FOX_W, bf), rowo(FOX_W, f32), rowo(FOX_W, f32), rowo(FOX_W, bf), rowo(FOX_W, bf), vto(FOX_HEADS),
            rowo(LANE, f32),
            rowo(DSA_W, bf), rowo(DSA_KV_W, f32), rowo(DSA_KV_W, f32), rowo(DSA_KV_W, bf), rowo(DSA_KV_W, bf),
            vto(DSA_KV_HEADS),
            rowo(IDX_W, bf), rowo(IDX_W, bf), rowo(LANE, f32), rowo(LANE, bf), rowo(LANE, f32),
            (pl.BlockSpec((LANE, bm), lambda i: (0, i)), jax.ShapeDtypeStruct((LANE, m), f32)),
            rowo(MLA_QW, bf), rowo(MLA_KV_LORA, f32), rowo(LANE, f32)]
    return pl.pallas_call(
        _post_kernel,
        grid=(m // bm,),
        in_specs=[row(PROJ_W), row(9 * LANE), full(fox_bias_p), full(q_norm), full(kv_norm), full(wuq_p)],
        out_specs=[s for s, _ in outs],
        out_shape=[o for _, o in outs],
        compiler_params=_cparams(("parallel",)),
        name="post_proj",
    )(proj, tab, fox_bias_p, q_norm, kv_norm, wuq_p)


def _kv_up_kernel(ckv_ref, kr_ref, w_ref, k_o, v_o, vt_o):
    bf = jnp.bfloat16
    kv = _dot(ckv_ref[...].astype(bf), w_ref[...])
    kr = kr_ref[...].astype(bf)
    for h in range(MLA_HEADS):
        k_o[:, h * MLA_QP:h * MLA_QP + LANE] = kv[:, h * LANE:(h + 1) * LANE].astype(bf)
        k_o[:, h * MLA_QP + LANE:(h + 1) * MLA_QP] = kr
        _store_value_t(vt_o, h, kv[:, (MLA_HEADS + h) * LANE:(MLA_HEADS + h + 1) * LANE])
    v_o[...] = kv[:, MLA_HEADS * LANE:].astype(bf)


def kv_up(ckv, kr, w, *, bm):
    r = ckv.shape[0]
    row = lambda w_: pl.BlockSpec((bm, w_), lambda i: (i, 0))
    return pl.pallas_call(
        _kv_up_kernel,
        grid=(r // bm,),
        in_specs=[row(MLA_KV_LORA), row(LANE), pl.BlockSpec(w.shape, lambda i: (0, 0))],
        out_specs=[row(MLA_QW), row(MLA_VW), pl.BlockSpec((1, MLA_HEADS * VR, bm), lambda i: (i, 0, 0))],
        out_shape=[jax.ShapeDtypeStruct((r, MLA_QW), jnp.bfloat16),
                   jax.ShapeDtypeStruct((r, MLA_VW), jnp.bfloat16),
                   jax.ShapeDtypeStruct((r // bm, MLA_HEADS * VR, bm), jnp.bfloat16)],
        compiler_params=_cparams(("parallel",)),
        name="mla_kv_up",
    )(ckv, kr, w)


def _cumsum_kernel(x_ref, c_o, ct_o, carry_ref, carry_t_ref):
    t = x_ref.shape[1]

    @pl.when(pl.program_id(1) == 0)
    def _():
        carry_ref[...] = jnp.zeros_like(carry_ref)
        carry_t_ref[...] = jnp.zeros_like(carry_t_ref)

    x = x_ref[0]
    r = lax.broadcasted_iota(jnp.int32, (t, t), 0)
    c = lax.broadcasted_iota(jnp.int32, (t, t), 1)
    lower = jnp.where(c <= r, 1.0, 0.0).astype(jnp.float32)
    upper = jnp.where(r <= c, 1.0, 0.0).astype(jnp.float32)
    cs = jnp.dot(lower, x, preferred_element_type=jnp.float32, precision=lax.Precision.HIGHEST)
    cs = cs + carry_ref[...]
    c_o[0] = cs * LOG2E
    carry_ref[...] = cs[t - 1:t, :]
    cst = jnp.dot(x.T, upper, preferred_element_type=jnp.float32, precision=lax.Precision.HIGHEST)
    cst = cst + carry_t_ref[...]
    ct_o[0] = cst[0:8, :] * LOG2E
    carry_t_ref[...] = cst[:, t - 1:t]


def cumsum_rows(x, *, bt):
    b, s, _ = x.shape
    return pl.pallas_call(
        _cumsum_kernel,
        grid=(b, s // bt),
        in_specs=[pl.BlockSpec((1, bt, LANE), lambda i, j: (i, j, 0))],
        out_specs=[pl.BlockSpec((1, bt, LANE), lambda i, j: (i, j, 0)),
                   pl.BlockSpec((1, 8, bt), lambda i, j: (i, 0, j))],
        out_shape=[jax.ShapeDtypeStruct((b, s, LANE), jnp.float32),
                   jax.ShapeDtypeStruct((b, 8, s), jnp.float32)],
        scratch_shapes=[pltpu.VMEM((1, LANE), jnp.float32), pltpu.VMEM((LANE, 1), jnp.float32)],
        compiler_params=_cparams(("parallel", "arbitrary")),
        name="logf_cumsum",
    )(x)


def _num_key_tiles(i, bq, bk, q_off, nk_max):
    last = q_off + (i + 1) * bq
    return jnp.minimum((last + bk - 1) // bk, nk_max)


def _softmax_steps(ss, vts, m_ref, l_ref, acc_ref):
    ps, alphas = [], []
    for t, s in enumerate(ss):
        m_prev = m_ref[t]
        m_new = jnp.maximum(m_prev, jnp.max(s, axis=1, keepdims=True))
        alpha = jnp.exp2(m_prev - m_new)
        p = jnp.exp2(s - m_new)
        l_ref[t] = alpha * l_ref[t] + jnp.sum(p, axis=1, keepdims=True)
        m_ref[t] = m_new
        ps.append(p.astype(jnp.bfloat16))
        alphas.append(alpha)
    for t, (p, alpha) in enumerate(zip(ps, alphas)):
        acc_ref[t] = alpha * acc_ref[t] + _dot(p, vts[t])


def _flash_kernel(*refs, bq, bk, q_off, nk_max, fox, hg, dqk, dv):
    if fox:
        q_ref, k_ref, v_ref, cq_ref, ck_ref, o_ref, m_ref, l_ref, acc_ref = refs
    else:
        q_ref, k_ref, v_ref, o_ref, m_ref, l_ref, acc_ref = refs
    gi = pl.program_id(1)
    i = pl.program_id(2)
    first = q_off + i * bq
    qpos = first + lax.broadcasted_iota(jnp.int32, (bq, 1), 0)
    if fox:
        lane = lax.broadcasted_iota(jnp.int32, (bq, LANE), 1)
        cq_all = cq_ref[0]
        cqs = [jnp.sum(jnp.where(lane == gi * hg + t, cq_all, 0.0), axis=1, keepdims=True) for t in range(hg)]
        c0 = [c[0:1, :] for c in cqs]
        bias_q = [c - z for c, z in zip(cqs, c0)]
    nk = _num_key_tiles(i, bq, bk, q_off, nk_max)
    n_full = jnp.minimum((first + (1 if fox else CHUNK)) // bk, nk)

    m_ref[...] = jnp.full(m_ref.shape, NEG_INF, jnp.float32)
    l_ref[...] = jnp.zeros(l_ref.shape, jnp.float32)
    acc_ref[...] = jnp.zeros(acc_ref.shape, jnp.float32)

    def make_body(masked):
        def body(j, _):
            start = pl.multiple_of(j * bk, bk)
            if masked:
                kpos = j * bk + lax.broadcasted_iota(jnp.int32, (1, bk), 1)
                mask = (kpos <= qpos) if fox else ((kpos >> 6) <= (qpos >> 6))
            ss = []
            for t in range(hg):
                kt = k_ref[0, pl.ds(start, bk), t * dqk:(t + 1) * dqk]
                s = _dot_nt(q_ref[0, :, t * dqk:(t + 1) * dqk], kt)
                if fox:
                    s = s + (bias_q[t] - (ck_ref[0, t, j] - c0[t]))
                ss.append(jnp.where(mask, s, NEG_INF) if masked else s)
            vts = [v_ref[0, pl.ds(start, bk), t * dv:(t + 1) * dv] for t in range(hg)]
            _softmax_steps(ss, vts, m_ref, l_ref, acc_ref)
            return 0
        return body

    lax.fori_loop(0, n_full, make_body(False), 0)
    lax.fori_loop(n_full, nk, make_body(True), 0)
    for t in range(hg):
        o_ref[0, :, t * dv:(t + 1) * dv] = (acc_ref[t] / l_ref[t]).astype(o_ref.dtype)


def flash_attention(q, k, v, cq=None, ck=None, *, lq, sp, heads, hg, dqk, dv, bq, bk, q_off):
    b = q.shape[0]
    nk_max = sp // bk
    fox = cq is not None
    in_specs = [pl.BlockSpec((1, bq, hg * dqk), lambda b_, g, i: (b_, i, g)),
                pl.BlockSpec((1, sp, hg * dqk), lambda b_, g, i: (b_, 0, g)),
                pl.BlockSpec((1, sp, hg * dv), lambda b_, g, i: (b_, 0, g))]
    args = [q, k, v]
    if fox:
        in_specs += [pl.BlockSpec((1, bq, LANE), lambda b_, g, i: (b_, i, 0)),
                     pl.BlockSpec((1, hg, nk_max, 1, bk), lambda b_, g, i: (b_, g, 0, 0, 0))]
        args += [cq, ck]
    return pl.pallas_call(
        functools.partial(_flash_kernel, bq=bq, bk=bk, q_off=q_off, nk_max=nk_max, fox=fox, hg=hg, dqk=dqk, dv=dv),
        grid=(b, heads // hg, lq // bq),
        in_specs=in_specs,
        out_specs=pl.BlockSpec((1, bq, hg * dv), lambda b_, g, i: (b_, i, g)),
        out_shape=jax.ShapeDtypeStruct((b, lq, heads * dv), jnp.bfloat16),
        scratch_shapes=[pltpu.VMEM((hg, bq, 1), jnp.float32), pltpu.VMEM((hg, bq, 1), jnp.float32),
                        pltpu.VMEM((hg, bq, dv), jnp.float32)],
        compiler_params=_cparams(("parallel", "parallel", "arbitrary")),
        name="fox_attention" if fox else "mla_attention",
    )(*args)


_KEY_NEG_INF = -2139095041
_F32_LOWEST = -3.4028234663852886e38


_FOLD = 64
_MAX_PROBES = 8 * 34


def _key_to_f32(key):
    bits = jnp.where(key >= 0, key, key ^ jnp.int32(0x7FFFFFFF))
    return lax.bitcast_convert_type(bits, jnp.float32)


def _f32_to_key(x):
    bits = lax.bitcast_convert_type(x, jnp.int32)
    return jnp.where(bits >= 0, bits, bits ^ jnp.int32(0x7FFFFFFF))


def _dsa_kernel(iq_ref, iqs_ref, iw_ref, dq_ref, ik_ref, dk_ref, dv_ref, o_ref,
                sc_ref, qg_ref, m_ref, l_ref, acc_ref, *, bq, bk, q_off, nk_max, topk):
    i = pl.program_id(1)
    nk = _num_key_tiles(i, bq, bk, q_off, nk_max)
    qpos = q_off + i * bq + lax.broadcasted_iota(jnp.int32, (bq, 1), 0)
    klim = ((qpos >> 6) + 1) << 6
    iw = iw_ref[0]

    def score_tile(j, _):
        kt = ik_ref[0, pl.ds(pl.multiple_of(j * bk, bk), bk), :]
        acc = jnp.zeros((bq, bk), jnp.float32)
        for g in range(IDX_HEADS // 2):
            se = _dot_nt(iq_ref[0, :, g * LANE:(g + 1) * LANE], kt)
            so = _dot_nt(iqs_ref[0, :, g * LANE:(g + 1) * LANE], kt)
            acc = acc + iw[:, 2 * g:2 * g + 1] * jnp.maximum(se, 0.0)
            acc = acc + iw[:, 2 * g + 1:2 * g + 2] * jnp.maximum(so, 0.0)
        kpos = j * bk + lax.broadcasted_iota(jnp.int32, (1, bk), 1)
        sc_ref[j] = jnp.where(kpos < klim, acc, -jnp.inf)
        return 0

    lax.fori_loop(0, nk, score_tile, 0)

    def count(hit):
        def body(j, acc):
            h = hit(sc_ref[j], j)
            for c in range(bk // LANE):
                acc = acc + h[:, c * LANE:(c + 1) * LANE]
            return acc
        acc = lax.fori_loop(0, nk, body, jnp.zeros((bq, LANE), jnp.float32))
        return jnp.sum(acc, axis=1, keepdims=True)

    count_ge = lambda pivot: count(lambda t, j: jnp.where(t >= pivot, 1.0, 0.0))
    kf = jnp.float32(topk)
    nonneg = count_ge(jnp.zeros((bq, 1), jnp.float32)) >= kf
    key0 = jnp.where(nonneg, jnp.int32(0), jnp.int32(-2 ** 31))

    def bit_step(it, key):
        cand = key | (jnp.int32(1) << (30 - it))
        ok = (count_ge(_key_to_f32(cand)) >= kf) | (cand <= _KEY_NEG_INF)
        return jnp.where(ok, cand, key)

    key = lax.fori_loop(0, 31, bit_step, key0)
    thr = jnp.maximum(_key_to_f32(key), _F32_LOWEST)

    tie = count_ge(thr) > kf
    total = nk_max * bk

    @pl.when(jnp.sum(jnp.where(tie, 1, 0)) > 0)
    def _():
        need = kf - count(lambda t, j: jnp.where(t > thr, 1.0, 0.0))
        cols = lax.broadcasted_iota(jnp.int32, (bq, bk), 1)

        def count_equal_upto(j_max):
            return count(lambda t, j: jnp.where(t == thr, jnp.where(cols <= j_max - j * bk, 1.0, 0.0), 0.0))

        def index_step(_, bracket):
            j_lo, j_hi = bracket
            mid = (j_lo + j_hi) >> 1
            ok = count_equal_upto(mid) >= need
            return jnp.where(ok, j_lo, mid), jnp.where(ok, mid, j_hi)

        bracket = (jnp.full((bq, 1), -1, jnp.int32), jnp.full((bq, 1), total - 1, jnp.int32))
        _, j_cut = lax.fori_loop(0, total.bit_length() + 1, index_step, bracket)
        j_cut = jnp.where(tie, j_cut, total)

        def drop_late_ties(j, _):
            t = sc_ref[j]
            sc_ref[j] = jnp.where(t == thr, jnp.where(cols > j_cut - j * bk, -jnp.inf, t), t)
            return 0

        lax.fori_loop(0, nk, drop_late_ties, 0)

    for h in range(DSA_HEADS):
        g, r = divmod(h, DSA_GROUP)
        qg_ref[g, r * bq:(r + 1) * bq, :] = dq_ref[0, :, h * LANE:(h + 1) * LANE]
    m_ref[...] = jnp.full(m_ref.shape, NEG_INF, jnp.float32)
    l_ref[...] = jnp.zeros(l_ref.shape, jnp.float32)
    acc_ref[...] = jnp.zeros(acc_ref.shape, jnp.float32)

    def att_tile(j, _):
        start = pl.multiple_of(j * bk, bk)
        drop = jnp.where(sc_ref[j] >= thr, 0.0, NEG_INF)
        drop = jnp.concatenate([drop] * DSA_GROUP, axis=0)
        ss = [_dot_nt(qg_ref[g], dk_ref[0, pl.ds(start, bk), g * LANE:(g + 1) * LANE]) + drop
              for g in range(DSA_KV_HEADS)]
        vts = [dv_ref[0, pl.ds(start, bk), g * LANE:(g + 1) * LANE] for g in range(DSA_KV_HEADS)]
        _softmax_steps(ss, vts, m_ref, l_ref, acc_ref)
        return 0

    lax.fori_loop(0, nk, att_tile, 0)
    for h in range(DSA_HEADS):
        g, r = divmod(h, DSA_GROUP)
        rows = slice(r * bq, (r + 1) * bq)
        o_ref[0, :, h * LANE:(h + 1) * LANE] = (acc_ref[g, rows, :] / l_ref[g, rows, :]).astype(o_ref.dtype)


def dsa_attention(iq, iqs, iw, dq, ik, dk, dv, *, lq, sp, bq, bk, q_off, topk):
    b = dq.shape[0]
    nk_max = sp // bk
    qrow = lambda w: pl.BlockSpec((1, bq, w), lambda b_, i: (b_, i, 0))
    krow = lambda w: pl.BlockSpec((1, sp, w), lambda b_, i: (b_, 0, 0))
    gr = DSA_GROUP * bq
    return pl.pallas_call(
        functools.partial(_dsa_kernel, bq=bq, bk=bk, q_off=q_off, nk_max=nk_max, topk=topk),
        grid=(b, lq // bq),
        in_specs=[qrow(IDX_W), qrow(IDX_W), qrow(LANE), qrow(DSA_W), krow(LANE), krow(DSA_KV_W), krow(DSA_KV_W)],
        out_specs=qrow(DSA_W),
        out_shape=jax.ShapeDtypeStruct((b, lq, DSA_W), jnp.bfloat16),
        scratch_shapes=[pltpu.VMEM((nk_max, bq, bk), jnp.float32),
                        pltpu.VMEM((DSA_KV_HEADS, gr, LANE), jnp.bfloat16),
                        pltpu.VMEM((DSA_KV_HEADS, gr, 1), jnp.float32),
                        pltpu.VMEM((DSA_KV_HEADS, gr, 1), jnp.float32),
                        pltpu.VMEM((DSA_KV_HEADS, gr, LANE), jnp.float32)],
        compiler_params=_cparams(("parallel", "arbitrary")),
        name="dsa_attention",
    )(iq, iqs, iw, dq, ik, dk, dv)


def _split3(x):
    hi = x.astype(jnp.bfloat16).astype(jnp.float32)
    r = x - hi
    mid = r.astype(jnp.bfloat16).astype(jnp.float32)
    lo = (r - mid).astype(jnp.bfloat16).astype(jnp.float32)
    return hi, mid, lo


def _fox_pack_kernel(q_ref, k_ref, c_ref, qa_o, ka_o, qn_o, kn_o):
    rows = q_ref.shape[0]
    lane = lax.broadcasted_iota(jnp.int32, (rows, LANE), 1)
    c = c_ref[...]
    zero = jnp.zeros((rows, LANE), jnp.float32)
    qn, kn = zero, zero
    for h in range(FOX_HEADS):
        qh = q_ref[:, h * LANE:(h + 1) * LANE].astype(jnp.float32)
        kh = k_ref[:, h * LANE:(h + 1) * LANE].astype(jnp.float32)
        qn = jnp.where(lane == h, jnp.sum(qh * qh, axis=1, keepdims=True), qn)
        kn = jnp.where(lane == h, jnp.sum(kh * kh, axis=1, keepdims=True), kn)
        ch = jnp.sum(jnp.where(lane == h, c, 0.0), axis=1, keepdims=True)
        hi, mid, lo = _split3(ch)
        terms = jnp.where(lane == 0, hi, jnp.where(lane == 1, mid, jnp.where(lane == 2, lo, zero)))
        ones_hi = jnp.where((lane >= 3) & (lane < 6), 1.0, 0.0)
        qa_o[:, h * FOX_QP:h * FOX_QP + LANE] = q_ref[:, h * LANE:(h + 1) * LANE]
        qa_o[:, h * FOX_QP + LANE:(h + 1) * FOX_QP] = (terms + ones_hi).astype(jnp.bfloat16)
        ones_lo = jnp.where(lane < 3, 1.0, 0.0)
        ka_o[:, h * FOX_QP:h * FOX_QP + LANE] = k_ref[:, h * LANE:(h + 1) * LANE]
        ka_o[:, h * FOX_QP + LANE:(h + 1) * FOX_QP] = (ones_lo - pltpu.roll(terms, 3, 1)).astype(jnp.bfloat16)
    qn_o[...] = qn
    kn_o[...] = kn


def fox_pack(q, k, c, *, rows, bm):
    row = lambda w: pl.BlockSpec((bm, w), lambda i: (i, 0))
    return pl.pallas_call(
        _fox_pack_kernel,
        grid=(rows // bm,),
        in_specs=[row(FOX_W), row(FOX_W), row(LANE)],
        out_specs=[row(FOX_HEADS * FOX_QP), row(FOX_HEADS * FOX_QP), row(LANE), row(LANE)],
        out_shape=[jax.ShapeDtypeStruct((rows, FOX_HEADS * FOX_QP), jnp.bfloat16)] * 2
        + [jax.ShapeDtypeStruct((rows, LANE), jnp.float32)] * 2,
        compiler_params=_cparams(("parallel",)),
        name="fox_pack",
    )(q, k, c)


def _logits_stage(slot, sts, s_ref, cm_ref):
    for t, st in enumerate(sts):
        s_ref[slot, t] = st
        cm_ref[slot, t] = jnp.max(st, axis=0, keepdims=True)


def _values_stage(slot, vtts, s_ref, cm_ref, m_ref, acc_ref):
    ps, alphas = [], []
    for t in range(len(vtts)):
        m_prev = m_ref[t]
        m_new = jnp.maximum(m_prev, cm_ref[slot, t])
        alphas.append(jnp.exp2(m_prev - m_new))
        ps.append(jnp.exp2(s_ref[slot, t] - m_new).astype(jnp.bfloat16))
        m_ref[t] = m_new
    for t, (p, alpha) in enumerate(zip(ps, alphas)):
        acc_ref[t] = alpha * acc_ref[t] + _dot(vtts[t], p)


def _pipelined_tiles(nk, nk_max, logits, values, start=0):
    last = nk_max - 1
    n = nk - start
    logits(start, 0)

    def run(first, tiles):
        for u in range(tiles):
            logits(jnp.minimum(start + first + u + 1, last), (u + 1) % 2)
            values(start + first + u, u % 2)

    def quad(jj, _):
        run(4 * jj, 4)
        return 0

    def pair(jj, _):
        run(2 * jj, 2)
        return 0

    n_quads = lax.shift_right_logical(n, 2)
    lax.fori_loop(0, n_quads, quad, 0)
    lax.fori_loop(2 * n_quads, lax.shift_right_logical(n, 1), pair, 0)

    @pl.when((n & 1) == 1)
    def _():
        values(nk - 1, 0)


def _finish_t(acc):
    return (acc[0:HEAD_DIM, :] / acc[HEAD_DIM:HEAD_DIM + 1, :]).T


_UNDERFLOW_EXP = 160.0


def _first_live_tile(q_ref, k_ref, qn_ref, c_first_ref, c_end_ref, kmax_ref, *, i, first, bq, bk, hg, dqk):
    g = pl.program_id(1)
    slack = []
    for t in range(hg):
        h = g * hg + t
        lane = lax.broadcasted_iota(jnp.int32, (bq, LANE), 1)
        q_norm = jnp.sqrt(jnp.max(jnp.where(lane == h, qn_ref[0], 0.0)))
        qt = q_ref[0, :, t * dqk:(t + 1) * dqk].astype(jnp.float32)
        kt = k_ref[0, pl.ds(pl.multiple_of(first, bq), bq), t * dqk:(t + 1) * dqk].astype(jnp.float32)
        diag_min = jnp.min(jnp.sum(qt * kt, axis=1, keepdims=True))
        slack.append(q_norm * kmax_ref[0, h] * 1.01 + c_first_ref[i, h] - diag_min + _UNDERFLOW_EXP)
    g0 = g * hg

    def dead(j):
        ok = c_end_ref[j, g0] > slack[0]
        for t in range(1, hg):
            ok = ok & (c_end_ref[j, g0 + t] > slack[t])
        return ok

    n_old = first // bk
    return lax.while_loop(lambda j: (j < n_old) & dead(j), lambda j: j + 1, jnp.int32(0))


def _flash_t_kernel(*refs, bq, bk, nk_max, frame_causal, hg, dqk, decay):
    if decay:
        q_ref, k_ref, vt_ref, qn_ref, c_first_ref, c_end_ref, kmax_ref, o_ref, m_ref, acc_ref, s_ref, cm_ref, d_ref = refs
    else:
        q_ref, k_ref, vt_ref, o_ref, m_ref, acc_ref, s_ref, cm_ref, d_ref = refs
    i = pl.program_id(2)
    first = i * bq
    nk = _num_key_tiles(i, bq, bk, 0, nk_max)
    m_ref[...] = jnp.full(m_ref.shape, NEG_INF, jnp.float32)
    acc_ref[...] = jnp.zeros(acc_ref.shape, jnp.float32)
    r = lax.broadcasted_iota(jnp.int32, (bk, bq), 0)
    c = lax.broadcasted_iota(jnp.int32, (bk, bq), 1)
    d_ref[...] = (r - c) if frame_causal else ((r >> 6) - (c >> 6))

    def logits(j, slot):
        start = pl.multiple_of(j * bk, bk)
        gap = first - j * bk
        visible = gap if frame_causal else (gap >> 6)
        mask = d_ref[...] <= visible
        sts = []
        for t in range(hg):
            kt = k_ref[0, pl.ds(start, bk), t * dqk:(t + 1) * dqk]
            st = _dot_nt(kt, q_ref[0, :, t * dqk:(t + 1) * dqk])
            sts.append(jnp.where(mask, st, NEG_INF))
        _logits_stage(slot, sts, s_ref, cm_ref)

    def values(j, slot):
        _values_stage(slot, [vt_ref[0, j, t * VR:(t + 1) * VR, :] for t in range(hg)], s_ref, cm_ref, m_ref, acc_ref)

    start = 0
    if decay:
        start = _first_live_tile(q_ref, k_ref, qn_ref, c_first_ref, c_end_ref, kmax_ref,
                                 i=i, first=first, bq=bq, bk=bk, hg=hg, dqk=dqk)
    _pipelined_tiles(nk, nk_max, logits, values, start)
    for t in range(hg):
        o_ref[0, :, t * HEAD_DIM:(t + 1) * HEAD_DIM] = _finish_t(acc_ref[t]).astype(o_ref.dtype)


def flash_attention_t(q, k, vt, decay=None, *, lq, sp, heads, hg, dqk, bq, bk, frame_causal, name):
    nk_max = sp // bk
    smem = pl.BlockSpec(memory_space=pltpu.SMEM)
    extra_specs, extra = [], []
    if decay is not None:
        extra_specs = [pl.BlockSpec((1, bq, LANE), lambda b_, g, i: (b_, i, 0)), smem, smem, smem]
        extra = list(decay)
    return pl.pallas_call(
        functools.partial(_flash_t_kernel, bq=bq, bk=bk, nk_max=nk_max, frame_causal=frame_causal, hg=hg, dqk=dqk,
                          decay=decay is not None),
        grid=(1, heads // hg, lq // bq),
        in_specs=[pl.BlockSpec((1, bq, hg * dqk), lambda b_, g, i: (b_, i, g)),
                  _resident((1, sp, hg * dqk), lambda b_, g, i: (b_, 0, g)),
                  _resident((1, nk_max, hg * VR, bk), lambda b_, g, i: (b_, 0, g, 0))] + extra_specs,
        out_specs=pl.BlockSpec((1, bq, hg * HEAD_DIM), lambda b_, g, i: (b_, i, g)),
        out_shape=jax.ShapeDtypeStruct((1, lq, heads * HEAD_DIM), jnp.bfloat16),
        scratch_shapes=[pltpu.VMEM((hg, 1, bq), jnp.float32), pltpu.VMEM((hg, VR, bq), jnp.float32),
                        pltpu.VMEM((2, hg, bk, bq), jnp.float32), pltpu.VMEM((2, hg, 1, bq), jnp.float32),
                        pltpu.VMEM((bk, bq), jnp.int32)],
        compiler_params=_cparams(("parallel", "parallel", "arbitrary")),
        name=name,
    )(q, k, vt, *extra)


def _dsa_t_kernel(iq_ref, iqs_ref, iwt_ref, dq_ref, ik_ref, dk_ref, dvt_ref, o_ref,
                  sc_ref, m_ref, acc_ref, s_ref, cm_ref, *, bq, bk, nk_max, topk):
    i = pl.program_id(1)
    nk = _num_key_tiles(i, bq, bk, 0, nk_max)
    qpos = i * bq + lax.broadcasted_iota(jnp.int32, (1, bq), 1)
    klim = ((qpos >> 6) + 1) << 6

    fold = lambda a, op: op(a.reshape(bk // _FOLD, _FOLD, bq), axis=0)

    def score_tile(j, carry):
        hi, lo = carry
        kt = ik_ref[0, pl.ds(pl.multiple_of(j * bk, bk), bk), :]
        acc = jnp.zeros((bk, bq), jnp.float32)
        for g in range(IDX_HEADS // 2):
            se = _dot_nt(kt, iq_ref[0, :, g * LANE:(g + 1) * LANE])
            so = _dot_nt(kt, iqs_ref[0, :, g * LANE:(g + 1) * LANE])
            acc = acc + iwt_ref[2 * g:2 * g + 1, :] * jnp.maximum(se, 0.0)
            acc = acc + iwt_ref[2 * g + 1:2 * g + 2, :] * jnp.maximum(so, 0.0)
        adm = (j * bk + lax.broadcasted_iota(jnp.int32, (bk, bq), 0)) < klim
        sc = jnp.where(adm, acc, -jnp.inf)
        sc_ref[j] = sc
        return (jnp.maximum(hi, fold(sc, jnp.max)), jnp.minimum(lo, fold(jnp.where(adm, acc, jnp.inf), jnp.min)))

    def score_quad(jj, c):
        for u in range(4):
            c = score_tile(4 * jj + u, c)
        return c

    n_quads = lax.shift_right_logical(nk, 2)
    carry = lax.fori_loop(0, n_quads, score_quad,
                          (jnp.full((_FOLD, bq), -jnp.inf, jnp.float32), jnp.full((_FOLD, bq), jnp.inf, jnp.float32)))
    hi, lo = lax.fori_loop(4 * n_quads, nk, score_tile, carry)
    hi = jnp.max(hi, axis=0, keepdims=True)
    lo = jnp.min(lo, axis=0, keepdims=True)

    def count_ge(pivot):
        def body(j, acc):
            for r in range(bk // _FOLD):
                acc = acc + jnp.where(sc_ref[j, r * _FOLD:(r + 1) * _FOLD, :] >= pivot, 1.0, 0.0)
            return acc
        n_pairs = lax.shift_right_logical(nk, 1)
        acc = lax.fori_loop(0, n_pairs, lambda jj, a: body(2 * jj + 1, body(2 * jj, a)),
                            jnp.zeros((_FOLD, bq), jnp.float32))
        acc = lax.fori_loop(2 * n_pairs, nk, body, acc)
        return jnp.sum(acc, axis=0, keepdims=True)

    kf = jnp.float32(topk)
    hi = _key_to_f32(_f32_to_key(hi) + 1)
    c_lo = klim.astype(jnp.float32)
    c_hi = jnp.zeros((1, bq), jnp.float32)

    def settled(lo, hi, c_lo):
        width = lax.bitcast_convert_type(_f32_to_key(hi) - _f32_to_key(lo), jnp.uint32)
        return (c_lo <= kf) | (width <= 1)

    def search_cond(state):
        it, n_open = state[0], state[1]
        return (n_open > 0) & (it < _MAX_PROBES)

    def probe_once(it, lo, hi, c_lo, c_hi):
        done = settled(lo, hi, c_lo)
        k_lo, k_hi = _f32_to_key(lo), _f32_to_key(hi)
        width = lax.bitcast_convert_type(k_hi - k_lo, jnp.uint32)
        k_mid = k_lo + lax.bitcast_convert_type(width >> 1, jnp.int32)
        k_val = jnp.clip(_f32_to_key(lo + (hi - lo) * 0.5), k_lo + 1, k_hi - 1)
        probe = _key_to_f32(jnp.where((it & 7) == 7, k_mid, k_val))
        probe = jnp.where(done, lo, probe)
        c = count_ge(probe)
        up = (c >= kf) & ~done
        down = (c < kf) & ~done
        return (jnp.where(up, probe, lo), jnp.where(down, probe, hi), jnp.where(up, c, c_lo), jnp.where(down, c, c_hi))

    def search_step(state):
        it, _, lo, hi, c_lo, c_hi = state
        lo, hi, c_lo, c_hi = probe_once(it, lo, hi, c_lo, c_hi)
        lo, hi, c_lo, c_hi = probe_once(it + 1, lo, hi, c_lo, c_hi)
        n_open = jnp.sum(jnp.where(settled(lo, hi, c_lo), 0, 1))
        return it + 2, n_open, lo, hi, c_lo, c_hi

    state = (jnp.int32(0), jnp.sum(jnp.where(settled(lo, hi, c_lo), 0, 1)), lo, hi, c_lo, c_hi)
    _, _, thr, _, c_lo, c_hi = lax.while_loop(search_cond, search_step, state)

    tie = c_lo > kf
    total = nk_max * bk

    @pl.when(jnp.sum(jnp.where(tie, 1, 0)) > 0)
    def _():
        need = kf - c_hi
        rows = lax.broadcasted_iota(jnp.int32, (bk, bq), 0)

        def count_equal_upto(j_max):
            def body(j, acc):
                hit = jnp.where(sc_ref[j] == thr, jnp.where(rows <= j_max - j * bk, 1.0, 0.0), 0.0)
                return acc + fold(hit, jnp.sum)
            acc = lax.fori_loop(0, nk, body, jnp.zeros((_FOLD, bq), jnp.float32))
            return jnp.sum(acc, axis=0, keepdims=True)

        def index_step(_, bracket):
            j_lo, j_hi = bracket
            mid = (j_lo + j_hi) >> 1
            ok = count_equal_upto(mid) >= need
            return jnp.where(ok, j_lo, mid), jnp.where(ok, mid, j_hi)

        bracket = (jnp.full((1, bq), -1, jnp.int32), jnp.full((1, bq), total - 1, jnp.int32))
        _, j_cut = lax.fori_loop(0, total.bit_length() + 1, index_step, bracket)
        j_cut = jnp.where(tie, j_cut, total)

        def drop_late_ties(j, _):
            t = sc_ref[j]
            sc_ref[j] = jnp.where(t == thr, jnp.where(rows > j_cut - j * bk, -jnp.inf, t), t)
            return 0

        lax.fori_loop(0, nk, drop_late_ties, 0)

    m_ref[...] = jnp.full(m_ref.shape, NEG_INF, jnp.float32)
    acc_ref[...] = jnp.zeros(acc_ref.shape, jnp.float32)

    def logits(j, slot):
        start = pl.multiple_of(j * bk, bk)
        drop = jnp.where(sc_ref[j] >= thr, 0.0, NEG_INF)
        sts = []
        for h in range(DSA_HEADS):
            g = h // DSA_GROUP
            kt = dk_ref[0, pl.ds(start, bk), g * LANE:(g + 1) * LANE]
            sts.append(_dot_nt(kt, dq_ref[0, :, h * LANE:(h + 1) * LANE]) + drop)
        _logits_stage(slot, sts, s_ref, cm_ref)

    def values(j, slot):
        vtts = [dvt_ref[0, j, (h // DSA_GROUP) * VR:(h // DSA_GROUP + 1) * VR, :] for h in range(DSA_HEADS)]
        _values_stage(slot, vtts, s_ref, cm_ref, m_ref, acc_ref)

    _pipelined_tiles(nk, nk_max, logits, values)
    for h in range(DSA_HEADS):
        o_ref[0, :, h * LANE:(h + 1) * LANE] = _finish_t(acc_ref[h]).astype(o_ref.dtype)


def dsa_attention_t(iq, iqs, iwt, dq, ik, dk, dvt, *, lq, sp, bq, bk, topk):
    nk_max = sp // bk
    qrow = lambda w: pl.BlockSpec((1, bq, w), lambda b_, i: (b_, i, 0))
    krow = lambda w: _resident((1, sp, w), lambda b_, i: (b_, 0, 0))
    return pl.pallas_call(
        functools.partial(_dsa_t_kernel, bq=bq, bk=bk, nk_max=nk_max, topk=topk),
        grid=(1, lq // bq),
        in_specs=[qrow(IDX_W), qrow(IDX_W), pl.BlockSpec((LANE, bq), lambda b_, i: (0, i)), qrow(DSA_W),
                  krow(LANE), krow(DSA_KV_W),
                  _resident((1, nk_max, DSA_KV_HEADS * VR, bk), lambda b_, i: (b_, 0, 0, 0))],
        out_specs=qrow(DSA_W),
        out_shape=jax.ShapeDtypeStruct((1, lq, DSA_W), jnp.bfloat16),
        scratch_shapes=[pltpu.VMEM((nk_max, bk, bq), jnp.float32),
                        pltpu.VMEM((DSA_HEADS, 1, bq), jnp.float32),
                        pltpu.VMEM((DSA_HEADS, VR, bq), jnp.float32),
                        pltpu.VMEM((2, DSA_HEADS, bk, bq), jnp.float32),
                        pltpu.VMEM((2, DSA_HEADS, 1, bq), jnp.float32)],
        compiler_params=_cparams(("parallel", "arbitrary")),
        name="dsa_attention_t",
    )(iq, iqs, iwt, dq, ik, dk, dvt)


def _out_proj_kernel(x_ref, a_ref, b_ref, c_ref, wa_ref, wb_ref, wc_ref, o_ref):
    o_ref[...] = (x_ref[...] + _dot(a_ref[...], wa_ref[...]) + _dot(b_ref[...], wb_ref[...])
                  + _dot(c_ref[...], wc_ref[...]))


def out_proj(x, a, b, c, w, layer, *, bm):
    m, d = x.shape
    wa, wb, wc = a.shape[1], b.shape[1], c.shape[1]
    assert wa == wb and (wa + wb) % wc == 0
    row = lambda w_: pl.BlockSpec((bm, w_), lambda i: (i, 0))
    band = lambda rows, blk: _resident((None, rows, d), lambda i: (layer, blk, 0))
    return pl.pallas_call(
        _out_proj_kernel,
        grid=(m // bm,),
        in_specs=[row(d), row(wa), row(wb), row(wc), band(wa, 0), band(wb, 1), band(wc, (wa + wb) // wc)],
        out_specs=row(d),
        out_shape=jax.ShapeDtypeStruct((m, d), jnp.float32),
        compiler_params=_cparams(("parallel",)),
        name="out_proj",
    )(x, a, b, c, w, w, w)


def _ffn_kernel(x_ref, g_ref, wg_ref, wu_ref, wd_ref, o_ref, xn_ref, acc_ref):
    j = pl.program_id(1)

    @pl.when(j == 0)
    def _():
        xn_ref[...] = _rms(x_ref[...], g_ref[...]).astype(jnp.bfloat16)
        acc_ref[...] = jnp.zeros_like(acc_ref)

    xn = xn_ref[...]
    gate = _dot(xn, wg_ref[...])
    up = _dot(xn, wu_ref[...])
    hidden = (gate * (1.0 / (1.0 + jnp.exp(-gate))) * up).astype(jnp.bfloat16)
    acc_ref[...] += _dot(hidden, wd_ref[...])

    @pl.when(j == pl.num_programs(1) - 1)
    def _():
        o_ref[...] = x_ref[...] + acc_ref[...]


def ffn(x, g, wg, wu, wd, layer, *, bm, bf):
    m, d = x.shape
    f = wg.shape[2]
    return pl.pallas_call(
        _ffn_kernel,
        grid=(m // bm, f // bf),
        in_specs=[pl.BlockSpec((bm, d), lambda i, j: (i, 0)),
                  pl.BlockSpec((1, d), lambda i, j: (0, 0)),
                  pl.BlockSpec((None, d, bf), lambda i, j: (layer, 0, j)),
                  pl.BlockSpec((None, d, bf), lambda i, j: (layer, 0, j)),
                  pl.BlockSpec((None, bf, d), lambda i, j: (layer, j, 0))],
        out_specs=pl.BlockSpec((bm, d), lambda i, j: (i, 0)),
        out_shape=jax.ShapeDtypeStruct((m, d), jnp.float32),
        scratch_shapes=[pltpu.VMEM((bm, d), jnp.bfloat16), pltpu.VMEM((bm, d), jnp.float32)],
        compiler_params=_cparams(("parallel", "arbitrary")),
        name="swiglu",
    )(x, g.reshape(1, d), wg, wu, wd)


def _final_norm_kernel(x_ref, g_ref, o_ref):
    o_ref[...] = _rms(x_ref[...], g_ref[...])


def final_norm(x, g, *, bm):
    m, d = x.shape
    return pl.pallas_call(
        _final_norm_kernel,
        grid=(m // bm,),
        in_specs=[pl.BlockSpec((bm, d), lambda i: (i, 0)), pl.BlockSpec((1, d), lambda i: (0, 0))],
        out_specs=pl.BlockSpec((bm, d), lambda i: (i, 0)),
        out_shape=jax.ShapeDtypeStruct((m, d), jnp.float32),
        compiler_params=_cparams(("parallel",)),
        name="final_norm",
    )(x, g.reshape(1, d))


def _pad_cols(a, width):
    return jnp.pad(a, [(0, 0)] * (a.ndim - 1) + [(0, width - a.shape[-1])])


def _layout_w_in(w_in):
    sizes = (FOX_W, FOX_W, FOX_W, FOX_HEADS, DSA_W, DSA_KV_W, DSA_KV_W, IDX_W, IDX_DIM, IDX_HEADS,
             MLA_Q_LORA, MLA_KV_LORA, MLA_ROPE)
    splits = np.cumsum(sizes)[:-1]
    q_a, k_a, v_a, f_a, q_b, k_b, v_b, q_i, k_i, w_i, c_q, c_kv, k_r = jnp.split(w_in, splits, axis=-1)
    parts = {"fq": q_a, "fk": k_a, "fv": v_a, "dq": q_b, "dk": k_b, "dv": v_b, "iq": q_i, "cq": c_q,
             "ckv": c_kv, "fa": f_a, "ik": k_i, "iw": w_i, "kr": k_r}
    cols = [_pad_cols(parts[n], w) for n, w in _SEGS]
    return _pad_cols(jnp.concatenate(cols, axis=-1), PROJ_W).astype(jnp.bfloat16)


def _layout_w_uq(w_uq):
    dp = w_uq.shape[0]
    w = w_uq.reshape(dp, MLA_Q_LORA, MLA_HEADS, MLA_NOPE + MLA_ROPE)
    return _pad_cols(w, MLA_QP).reshape(dp, MLA_Q_LORA, MLA_QW).astype(jnp.bfloat16)


def _layout_w_ukv(w_ukv):
    dp = w_ukv.shape[0]
    w = w_ukv.reshape(dp, MLA_KV_LORA, MLA_HEADS, MLA_NOPE + MLA_V)
    kn = w[..., :MLA_NOPE].reshape(dp, MLA_KV_LORA, MLA_HEADS * MLA_NOPE)
    vv = w[..., MLA_NOPE:].reshape(dp, MLA_KV_LORA, MLA_VW)
    return jnp.concatenate([kn, vv], axis=-1).astype(jnp.bfloat16)


def _rope_table(pos, rot, period):
    half = rot // 2
    inv_freq = ROPE_THETA ** (-jnp.arange(half, dtype=jnp.float32) / half)
    ang = pos.astype(jnp.float32)[:, None] * inv_freq[None, :]
    cos, sin = jnp.cos(ang), jnp.sin(ang)
    r = pos.shape[0]
    ones = jnp.ones((r, period - rot), jnp.float32)
    zeros = jnp.zeros((r, period - rot), jnp.float32)
    zh = jnp.zeros((r, half), jnp.float32)
    c = jnp.concatenate([cos, cos, ones], axis=1)
    s1 = jnp.concatenate([zh, sin, zeros], axis=1)
    s2 = jnp.concatenate([-sin, zh, zeros], axis=1)
    rep = LANE // period
    return jnp.concatenate([jnp.tile(c, (1, rep)), jnp.tile(s1, (1, rep)), jnp.tile(s2, (1, rep))], axis=1)


def _pick(n, candidates):
    for c in candidates:
        if n % c == 0:
            return c
    raise ValueError(f"no block size among {candidates} divides {n}")


def _pad_seq(a, sp):
    return jnp.pad(a, [(0, 0), (0, sp - a.shape[1])] + [(0, 0)] * (a.ndim - 2))


def kernel(x_prompt, x_sample, cache_fox_k, cache_fox_v, cache_fox_logf, cache_dsa_k, cache_dsa_v, cache_idx_k, cache_mla_ckv, cache_mla_krope, w_in, fox_bias, mla_q_norm, mla_w_uq, mla_kv_norm, mla_w_ukv, w_o, attn_norm, ffn_norm, w_gate, w_up, w_down, final_norm_w):
    bf = jnp.bfloat16
    depth = w_in.shape[0]
    nb, seq, d = x_prompt.shape
    db, dec, _ = x_sample.shape
    past = cache_fox_k.shape[2]
    assert nb == 1 and seq % CHUNK == 0 and dec % CHUNK == 0 and past % CHUNK == 0
    mp, ms = nb * seq, db * dec
    m = mp + ms
    s_len = past + dec

    bm = _pick(m, (512, 256, 128, 64))
    assert mp % bm == 0
    bq_p = _pick(seq, (256, 128))
    bk_p = bm
    bq_s = _pick(dec, (64,))
    bk_s = 256
    sp = ((s_len + bk_s - 1) // bk_s) * bk_s
    topk_p = min(IDX_TOPK_MAX, seq // 4)
    topk_s = min(IDX_TOPK_MAX, s_len // 4)
    fox_hg, mla_hg = 3, 2
    d_ff = w_gate.shape[2]
    bff = _pick(d_ff, (512, 256, 128))

    w_in_p = _layout_w_in(w_in)
    wuq_p = _layout_w_uq(mla_w_uq)
    wukv_p = _layout_w_ukv(mla_w_ukv)
    w_o_b = w_o.astype(bf)
    wg_b, wu_b, wd_b = w_gate.astype(bf), w_up.astype(bf), w_down.astype(bf)
    fox_bias_p = _pad_cols(fox_bias, LANE).reshape(depth, 1, LANE)

    pos = jnp.concatenate([jnp.tile(jnp.arange(seq, dtype=jnp.int32), nb),
                           jnp.tile(past + jnp.arange(dec, dtype=jnp.int32), db)])
    tab = jnp.concatenate([_rope_table(pos, PARTIAL_ROT, HEAD_DIM), _rope_table(pos, IDX_ROT, IDX_DIM),
                           _rope_table(pos, MLA_ROPE, LANE)], axis=1)

    x = jnp.concatenate([x_prompt.reshape(mp, d), x_sample.reshape(ms, d)], axis=0)
    new_rows = []
    for l in range(depth):
        proj = norm_matmul(x, attn_norm[l], w_in_p, l, bm=bm, bn=_pick(PROJ_W, (1536, 1024, 512)))
        (fq, fk, fv, fkb, fvb, fvt, lf, dq, dk, dv, dkb, dvb, dvt, iq, iqs, ik, ikb, iw, iwt, mq, ckv, kr) = post_proj(
            proj, tab, fox_bias_p[l], mla_q_norm[l].reshape(1, -1), mla_kv_norm[l].reshape(1, -1), wuq_p[l],
            bm=bm // 2, bk=bm)

        pr = lambda a: a[:mp].reshape(nb, seq, -1)
        sr = lambda a: a[mp:].reshape(db, dec, -1)
        new_rows.append((fk, fv, lf, dk, dv, ik, ckv, kr))

        cat = lambda c, new: _pad_seq(jnp.concatenate([c, new], axis=1), sp)
        s_fkb = cat(cache_fox_k[l].reshape(db, past, FOX_W).astype(bf), sr(fkb))
        s_fvb = cat(cache_fox_v[l].reshape(db, past, FOX_W).astype(bf), sr(fvb))
        s_lf = cat(_pad_cols(cache_fox_logf[l], LANE), sr(lf))
        s_dkb = cat(cache_dsa_k[l].reshape(db, past, DSA_KV_W).astype(bf), sr(dkb))
        s_dvb = cat(cache_dsa_v[l].reshape(db, past, DSA_KV_W).astype(bf), sr(dvb))
        s_ikb = cat(_pad_cols(cache_idx_k[l], LANE).astype(bf), sr(ikb))
        s_ckv = cat(cache_mla_ckv[l], sr(ckv))
        s_kr = cat(_pad_cols(cache_mla_krope[l], LANE), sr(kr))

        ckv_all = jnp.concatenate([ckv[:mp], s_ckv.reshape(db * sp, -1)], axis=0)
        kr_all = jnp.concatenate([kr[:mp], s_kr.reshape(db * sp, -1)], axis=0)
        kmla, vmla, vtmla = kv_up(ckv_all, kr_all, wukv_p[l], bm=bm)
        s_kmla, s_vmla = kmla[mp:].reshape(db, sp, -1), vmla[mp:].reshape(db, sp, -1)

        p_c, _ = cumsum_rows(pr(lf), bt=_pick(seq, (256, 128)))
        s_c, s_ct = cumsum_rows(s_lf, bt=_pick(sp, (256, 128)))
        ck_form = lambda ct, bk: ct[:, :FOX_HEADS].reshape(ct.shape[0], FOX_HEADS, ct.shape[2] // bk, 1, bk)

        al = lambda a: a.reshape(1, *a.shape)
        fqa, fka, fqn, fkn = fox_pack(fq, fkb, p_c[0], rows=mp, bm=bm)
        decay = (al(fqn), p_c[0, ::bq_p], p_c[0, bk_p - 1::bk_p], jnp.sqrt(jnp.max(fkn, axis=0, keepdims=True)))
        out_a = flash_attention_t(al(fqa), al(fka), al(fvt), decay, lq=seq, sp=seq, heads=FOX_HEADS, hg=fox_hg, dqk=FOX_QP,
                                  bq=bq_p, bk=bk_p, frame_causal=True, name="fox_attention_t")
        out_b = dsa_attention_t(al(iq), al(iqs), iwt, al(dq), al(ikb), al(dkb), al(dvt), lq=seq, sp=seq,
                                bq=bq_p, bk=bk_p, topk=topk_p)
        out_c = flash_attention_t(al(mq), al(kmla), al(vtmla), lq=seq, sp=seq, heads=MLA_HEADS, hg=mla_hg, dqk=MLA_QP,
                                  bq=bq_p, bk=bk_p, frame_causal=False, name="mla_attention_t")
        s_out_a = flash_attention(sr(fq), s_fkb, s_fvb, s_c[:, past:past + dec], ck_form(s_ct, bk_s), lq=dec, sp=sp,
                                  heads=FOX_HEADS, hg=fox_hg, dqk=HEAD_DIM, dv=HEAD_DIM, bq=bq_s, bk=bk_s, q_off=past)
        s_out_b = dsa_attention(sr(iq), sr(iqs), sr(iw), sr(dq), s_ikb, s_dkb, s_dvb, lq=dec, sp=sp,
                                bq=bq_s, bk=bk_s, q_off=past, topk=topk_s)
        s_out_c = flash_attention(sr(mq), s_kmla, s_vmla, lq=dec, sp=sp, heads=MLA_HEADS, hg=mla_hg,
                                  dqk=MLA_QP, dv=MLA_V, bq=bq_s, bk=bk_s, q_off=past)

        rows = lambda p, s: jnp.concatenate([p.reshape(mp, -1), s.reshape(ms, -1)], axis=0)
        x = out_proj(x, rows(out_a, s_out_a), rows(out_b, s_out_b), rows(out_c, s_out_c), w_o_b, l, bm=bm)
        x = ffn(x, ffn_norm[l], wg_b, wu_b, wd_b, l, bm=bm, bf=bff)

    y = final_norm(x, final_norm_w, bm=bm)
    y_prompt = y[:mp].reshape(nb, seq, d)
    y_sample = y[mp:].reshape(db, dec, d)
    tails = ((FOX_HEADS, HEAD_DIM), (FOX_HEADS, HEAD_DIM), (FOX_HEADS,), (DSA_KV_HEADS, HEAD_DIM),
             (DSA_KV_HEADS, HEAD_DIM), (IDX_DIM,), (MLA_KV_LORA,), (MLA_ROPE,))
    p_out, s_out = [], []
    for arrays, tail in zip(zip(*new_rows), tails):
        w = math.prod(tail)
        p_out.append(jnp.stack([a[:mp, :w] for a in arrays]).reshape((depth, nb, seq) + tail))
        s_out.append(jnp.stack([a[mp:, :w] for a in arrays]).reshape((depth, db, dec) + tail))
    return (y_prompt, y_sample) + tuple(p_out) + tuple(s_out)
```

```python
import functools
import math

import jax
import jax.numpy as jnp
import numpy as np
from jax import lax
from jax.experimental import pallas as pl
from jax.experimental.pallas import tpu as pltpu

CHUNK = 64
HEAD_DIM = 128
FOX_HEADS = 6
DSA_HEADS = 6
DSA_KV_HEADS = 2
DSA_GROUP = DSA_HEADS // DSA_KV_HEADS
IDX_HEADS = 16
IDX_DIM = 64
IDX_TOPK_MAX = 256
MLA_HEADS = 4
MLA_Q_LORA = 512
MLA_KV_LORA = 256
MLA_NOPE = 128
MLA_ROPE = 64
MLA_V = 128
PARTIAL_ROT = HEAD_DIM // 4
IDX_ROT = IDX_DIM // 4
ROPE_THETA = 500000.0
EPS = 1e-6
NEG_INF = -1e30
FOX_SCALE = HEAD_DIM ** -0.5
DSA_SCALE = HEAD_DIM ** -0.5
MLA_SCALE = (MLA_NOPE + MLA_ROPE) ** -0.5
IDX_W_SCALE = (IDX_HEADS * IDX_DIM) ** -0.5
LOG2E = math.log2(math.e)

LANE = 128
VMEM_LIMIT = 56 * 1024 * 1024

FOX_W = FOX_HEADS * HEAD_DIM
DSA_W = DSA_HEADS * HEAD_DIM
DSA_KV_W = DSA_KV_HEADS * HEAD_DIM
IDX_W = IDX_HEADS * IDX_DIM
MLA_QP = 2 * LANE
MLA_QW = MLA_HEADS * MLA_QP
MLA_VW = MLA_HEADS * MLA_V
VR = HEAD_DIM + 16
FOX_QP = 2 * LANE

_SEGS = (("fq", FOX_W), ("fk", FOX_W), ("fv", FOX_W), ("dq", DSA_W), ("dk", DSA_KV_W), ("dv", DSA_KV_W),
         ("iq", IDX_W), ("cq", MLA_Q_LORA), ("ckv", MLA_KV_LORA),
         ("fa", LANE), ("ik", LANE), ("iw", LANE), ("kr", LANE))
_OFF = {}
_o = 0
for _n, _w in _SEGS:
    _OFF[_n] = _o
    _o += _w
PROJ_W = ((_o + 511) // 512) * 512

_NT = (((1,), (1,)), ((), ()))


def _dot(a, b):
    return jnp.dot(a, b, preferred_element_type=jnp.float32)


def _dot_nt(a, b):
    return lax.dot_general(a, b, _NT, preferred_element_type=jnp.float32)


def _cparams(sem):
    return pltpu.CompilerParams(dimension_semantics=sem, vmem_limit_bytes=VMEM_LIMIT)


def _resident(block_shape, index_map):
    return pl.BlockSpec(block_shape, index_map, pipeline_mode=pl.Buffered(1))


def _rms(x, g):
    return x * lax.rsqrt(jnp.mean(x * x, axis=-1, keepdims=True) + EPS) * g


def _norm_matmul_kernel(x_ref, g_ref, w_ref, o_ref, xn_ref):
    @pl.when(pl.program_id(1) == 0)
    def _():
        xn_ref[...] = _rms(x_ref[...], g_ref[...]).astype(jnp.bfloat16)

    o_ref[...] = _dot(xn_ref[...], w_ref[...])


def norm_matmul(x, g, w, layer, *, bm, bn):
    m, d = x.shape
    n = w.shape[2]
    return pl.pallas_call(
        _norm_matmul_kernel,
        grid=(m // bm, n // bn),
        in_specs=[pl.BlockSpec((bm, d), lambda i, j: (i, 0)),
                  pl.BlockSpec((1, d), lambda i, j: (0, 0)),
                  pl.BlockSpec((None, d, bn), lambda i, j: (layer, 0, j))],
        out_specs=pl.BlockSpec((bm, bn), lambda i, j: (i, j)),
        out_shape=jax.ShapeDtypeStruct((m, n), jnp.float32),
        scratch_shapes=[pltpu.VMEM((bm, d), jnp.bfloat16)],
        compiler_params=_cparams(("parallel", "arbitrary")),
        name="norm_matmul",
    )(x, g.reshape(1, d), w)


def _rope(x, tab, half):
    c, s1, s2 = tab[:, 0:LANE], tab[:, LANE:2 * LANE], tab[:, 2 * LANE:3 * LANE]
    return x * c + pltpu.roll(x, half, 1) * s1 + pltpu.roll(x, LANE - half, 1) * s2


def _value_t_tail(width):
    r = lax.broadcasted_iota(jnp.int32, (16, width), 0)
    return jnp.where(r == 0, 1.0, 0.0).astype(jnp.bfloat16)


def _store_value_t(o_ref, h, v):
    o_ref[0, h * VR:h * VR + HEAD_DIM, :] = v.T.astype(jnp.bfloat16)
    o_ref[0, h * VR + HEAD_DIM:(h + 1) * VR, :] = _value_t_tail(v.shape[0])


def _post_kernel(p_ref, tab_ref, fb_ref, qn_ref, kvn_ref, wuq_ref,
                 fq_o, fk_o, fv_o, fkb_o, fvb_o, fvt_o, lf_o,
                 dq_o, dk_o, dv_o, dkb_o, dvb_o, dvt_o,
                 iq_o, iqs_o, ik_o, ikb_o, iw_o, iwt_o,
                 mq_o, ckv_o, kr_o):
    bf = jnp.bfloat16
    seg = lambda name, w: p_ref[:, _OFF[name]:_OFF[name] + w]
    tab_d = tab_ref[:, 0:3 * LANE]
    tab_i = tab_ref[:, 3 * LANE:6 * LANE]
    tab_m = tab_ref[:, 6 * LANE:9 * LANE]

    fq_o[...] = (seg("fq", FOX_W) * (FOX_SCALE * LOG2E)).astype(bf)
    fk = seg("fk", FOX_W)
    fk_o[...] = fk
    fkb_o[...] = fk.astype(bf)
    fv = seg("fv", FOX_W)
    fv_o[...] = fv
    fvb_o[...] = fv.astype(bf)
    for h in range(FOX_HEADS):
        _store_value_t(fvt_o, h, p_ref[:, _OFF["fv"] + h * LANE:_OFF["fv"] + (h + 1) * LANE])
    z = seg("fa", LANE) + fb_ref[...]
    lf_o[...] = jnp.minimum(z, 0.0) - jnp.log1p(jnp.exp(-jnp.abs(z)))

    for h in range(DSA_HEADS):
        x = p_ref[:, _OFF["dq"] + h * LANE:_OFF["dq"] + (h + 1) * LANE]
        dq_o[:, h * LANE:(h + 1) * LANE] = (_rope(x, tab_d, PARTIAL_ROT // 2) * (DSA_SCALE * LOG2E)).astype(bf)
    for h in range(DSA_KV_HEADS):
        x = p_ref[:, _OFF["dk"] + h * LANE:_OFF["dk"] + (h + 1) * LANE]
        y = _rope(x, tab_d, PARTIAL_ROT // 2)
        dk_o[:, h * LANE:(h + 1) * LANE] = y
        dkb_o[:, h * LANE:(h + 1) * LANE] = y.astype(bf)
        _store_value_t(dvt_o, h, p_ref[:, _OFF["dv"] + h * LANE:_OFF["dv"] + (h + 1) * LANE])
    dv = seg("dv", DSA_KV_W)
    dv_o[...] = dv
    dvb_o[...] = dv.astype(bf)
    for j in range(IDX_W // LANE):
        x = p_ref[:, _OFF["iq"] + j * LANE:_OFF["iq"] + (j + 1) * LANE]
        y = _rope(x, tab_i, IDX_ROT // 2)
        iq_o[:, j * LANE:(j + 1) * LANE] = y.astype(bf)
        iqs_o[:, j * LANE:(j + 1) * LANE] = pltpu.roll(y, IDX_DIM, 1).astype(bf)
    y = _rope(seg("ik", LANE), tab_i, IDX_ROT // 2)
    ik_o[...] = y
    ikb_o[...] = y.astype(bf)
    iw = seg("iw", LANE) * IDX_W_SCALE
    iw_o[...] = iw
    iwt_o[...] = iw.T

    cqn = _rms(seg("cq", MLA_Q_LORA), qn_ref[...]).astype(bf)
    mq = _dot(cqn, wuq_ref[...])
    for h in range(MLA_HEADS):
        a = h * MLA_QP
        mq_o[:, a:a + LANE] = (mq[:, a:a + LANE] * (MLA_SCALE * LOG2E)).astype(bf)
        r = _rope(mq[:, a + LANE:a + 2 * LANE], tab_m, MLA_ROPE // 2)
        mq_o[:, a + LANE:a + 2 * LANE] = (r * (MLA_SCALE * LOG2E)).astype(bf)
    ckv_o[...] = _rms(seg("ckv", MLA_KV_LORA), kvn_ref[...])
    kr_o[...] = _rope(seg("kr", LANE), tab_m, MLA_ROPE // 2)


def post_proj(proj, tab, fox_bias_p, q_norm, kv_norm, wuq_p, *, bm, bk):
    m = proj.shape[0]
    r = bk // bm
    f32, bf = jnp.float32, jnp.bfloat16
    row = lambda w: pl.BlockSpec((bm, w), lambda i: (i, 0))
    full = lambda a: pl.BlockSpec(a.shape, lambda i: (0,) * a.ndim)
    vt = lambda heads: pl.BlockSpec((1, heads * VR, bm), lambda i: (i // r, 0, i % r))
    rowo = lambda w, dt: (row(w), jax.ShapeDtypeStruct((m, w), dt))
    vto = lambda heads: (vt(heads), jax.ShapeDtypeStruct((m // bk, heads * VR, bk), bf))
    outs = [rowo(FOX_W, bf), rowo(FOX_W, f32), rowo(FOX_W, f32), rowo(FOX_W, bf), rowo(FOX_W, bf), vto(FOX_HEADS),
            rowo(LANE, f32),
            rowo(DSA_W, bf), rowo(DSA_KV_W, f32), rowo(DSA_KV_W, f32), rowo(DSA_KV_W, bf), rowo(DSA_KV_W, bf),
            vto(DSA_KV_HEADS),
            rowo(IDX_W, bf), rowo(IDX_W, bf), rowo(LANE, f32), rowo(LANE, bf), rowo(LANE, f32),
            (pl.BlockSpec((LANE, bm), lambda i: (0, i)), jax.ShapeDtypeStruct((LANE, m), f32)),
            rowo(MLA_QW, bf), rowo(MLA_KV_LORA, f32), rowo(LANE, f32)]
    return pl.pallas_call(
        _post_kernel,
        grid=(m // bm,),
        in_specs=[row(PROJ_W), row(9 * LANE), full(fox_bias_p), full(q_norm), full(kv_norm), full(wuq_p)],
        out_specs=[s for s, _ in outs],
        out_shape=[o for _, o in outs],
        compiler_params=_cparams(("parallel",)),
        name="post_proj",
    )(proj, tab, fox_bias_p, q_norm, kv_norm, wuq_p)


def _kv_up_kernel(ckv_ref, kr_ref, w_ref, k_o, v_o, vt_o):
    bf = jnp.bfloat16
    kv = _dot(ckv_ref[...].astype(bf), w_ref[...])
    kr = kr_ref[...].astype(bf)
    for h in range(MLA_HEADS):
        k_o[:, h * MLA_QP:h * MLA_QP + LANE] = kv[:, h * LANE:(h + 1) * LANE].astype(bf)
        k_o[:, h * MLA_QP + LANE:(h + 1) * MLA_QP] = kr
        _store_value_t(vt_o, h, kv[:, (MLA_HEADS + h) * LANE:(MLA_HEADS + h + 1) * LANE])
    v_o[...] = kv[:, MLA_HEADS * LANE:].astype(bf)


def kv_up(ckv, kr, w, *, bm):
    r = ckv.shape[0]
    row = lambda w_: pl.BlockSpec((bm, w_), lambda i: (i, 0))
    return pl.pallas_call(
        _kv_up_kernel,
        grid=(r // bm,),
        in_specs=[row(MLA_KV_LORA), row(LANE), pl.BlockSpec(w.shape, lambda i: (0, 0))],
        out_specs=[row(MLA_QW), row(MLA_VW), pl.BlockSpec((1, MLA_HEADS * VR, bm), lambda i: (i, 0, 0))],
        out_shape=[jax.ShapeDtypeStruct((r, MLA_QW), jnp.bfloat16),
                   jax.ShapeDtypeStruct((r, MLA_VW), jnp.bfloat16),
                   jax.ShapeDtypeStruct((r // bm, MLA_HEADS * VR, bm), jnp.bfloat16)],
        compiler_params=_cparams(("parallel",)),
        name="mla_kv_up",
    )(ckv, kr, w)


def _cumsum_kernel(x_ref, c_o, ct_o, carry_ref, carry_t_ref):
    t = x_ref.shape[1]

    @pl.when(pl.program_id(1) == 0)
    def _():
        carry_ref[...] = jnp.zeros_like(carry_ref)
        carry_t_ref[...] = jnp.zeros_like(carry_t_ref)

    x = x_ref[0]
    r = lax.broadcasted_iota(jnp.int32, (t, t), 0)
    c = lax.broadcasted_iota(jnp.int32, (t, t), 1)
    lower = jnp.where(c <= r, 1.0, 0.0).astype(jnp.float32)
    upper = jnp.where(r <= c, 1.0, 0.0).astype(jnp.float32)
    cs = jnp.dot(lower, x, preferred_element_type=jnp.float32, precision=lax.Precision.HIGHEST)
    cs = cs + carry_ref[...]
    c_o[0] = cs * LOG2E
    carry_ref[...] = cs[t - 1:t, :]
    cst = jnp.dot(x.T, upper, preferred_element_type=jnp.float32, precision=lax.Precision.HIGHEST)
    cst = cst + carry_t_ref[...]
    ct_o[0] = cst[0:8, :] * LOG2E
    carry_t_ref[...] = cst[:, t - 1:t]


def cumsum_rows(x, *, bt):
    b, s, _ = x.shape
    return pl.pallas_call(
        _cumsum_kernel,
        grid=(b, s // bt),
        in_specs=[pl.BlockSpec((1, bt, LANE), lambda i, j: (i, j, 0))],
        out_specs=[pl.BlockSpec((1, bt, LANE), lambda i, j: (i, j, 0)),
                   pl.BlockSpec((1, 8, bt), lambda i, j: (i, 0, j))],
        out_shape=[jax.ShapeDtypeStruct((b, s, LANE), jnp.float32),
                   jax.ShapeDtypeStruct((b, 8, s), jnp.float32)],
        scratch_shapes=[pltpu.VMEM((1, LANE), jnp.float32), pltpu.VMEM((LANE, 1), jnp.float32)],
        compiler_params=_cparams(("parallel", "arbitrary")),
        name="logf_cumsum",
    )(x)


def _num_key_tiles(i, bq, bk, q_off, nk_max):
    last = q_off + (i + 1) * bq
    return jnp.minimum((last + bk - 1) // bk, nk_max)


def _softmax_steps(ss, vts, m_ref, l_ref, acc_ref):
    ps, alphas = [], []
    for t, s in enumerate(ss):
        m_prev = m_ref[t]
        m_new = jnp.maximum(m_prev, jnp.max(s, axis=1, keepdims=True))
        alpha = jnp.exp2(m_prev - m_new)
        p = jnp.exp2(s - m_new)
        l_ref[t] = alpha * l_ref[t] + jnp.sum(p, axis=1, keepdims=True)
        m_ref[t] = m_new
        ps.append(p.astype(jnp.bfloat16))
        alphas.append(alpha)
    for t, (p, alpha) in enumerate(zip(ps, alphas)):
        acc_ref[t] = alpha * acc_ref[t] + _dot(p, vts[t])


def _flash_kernel(*refs, bq, bk, q_off, nk_max, fox, hg, dqk, dv):
    if fox:
        q_ref, k_ref, v_ref, cq_ref, ck_ref, o_ref, m_ref, l_ref, acc_ref = refs
    else:
        q_ref, k_ref, v_ref, o_ref, m_ref, l_ref, acc_ref = refs
    gi = pl.program_id(1)
    i = pl.program_id(2)
    first = q_off + i * bq
    qpos = first + lax.broadcasted_iota(jnp.int32, (bq, 1), 0)
    if fox:
        lane = lax.broadcasted_iota(jnp.int32, (bq, LANE), 1)
        cq_all = cq_ref[0]
        cqs = [jnp.sum(jnp.where(lane == gi * hg + t, cq_all, 0.0), axis=1, keepdims=True) for t in range(hg)]
        c0 = [c[0:1, :] for c in cqs]
        bias_q = [c - z for c, z in zip(cqs, c0)]
    nk = _num_key_tiles(i, bq, bk, q_off, nk_max)
    n_full = jnp.minimum((first + (1 if fox else CHUNK)) // bk, nk)

    m_ref[...] = jnp.full(m_ref.shape, NEG_INF, jnp.float32)
    l_ref[...] = jnp.zeros(l_ref.shape, jnp.float32)
    acc_ref[...] = jnp.zeros(acc_ref.shape, jnp.float32)

    def make_body(masked):
        def body(j, _):
            start = pl.multiple_of(j * bk, bk)
            if masked:
                kpos = j * bk + lax.broadcasted_iota(jnp.int32, (1, bk), 1)
                mask = (kpos <= qpos) if fox else ((kpos >> 6) <= (qpos >> 6))
            ss = []
            for t in range(hg):
                kt = k_ref[0, pl.ds(start, bk), t * dqk:(t + 1) * dqk]
                s = _dot_nt(q_ref[0, :, t * dqk:(t + 1) * dqk], kt)
                if fox:
                    s = s + (bias_q[t] - (ck_ref[0, t, j] - c0[t]))
                ss.append(jnp.where(mask, s, NEG_INF) if masked else s)
            vts = [v_ref[0, pl.ds(start, bk), t * dv:(t + 1) * dv] for t in range(hg)]
            _softmax_steps(ss, vts, m_ref, l_ref, acc_ref)
            return 0
        return body

    lax.fori_loop(0, n_full, make_body(False), 0)
    lax.fori_loop(n_full, nk, make_body(True), 0)
    for t in range(hg):
        o_ref[0, :, t * dv:(t + 1) * dv] = (acc_ref[t] / l_ref[t]).astype(o_ref.dtype)


def flash_attention(q, k, v, cq=None, ck=None, *, lq, sp, heads, hg, dqk, dv, bq, bk, q_off):
    b = q.shape[0]
    nk_max = sp // bk
    fox = cq is not None
    in_specs = [pl.BlockSpec((1, bq, hg * dqk), lambda b_, g, i: (b_, i, g)),
                pl.BlockSpec((1, sp, hg * dqk), lambda b_, g, i: (b_, 0, g)),
                pl.BlockSpec((1, sp, hg * dv), lambda b_, g, i: (b_, 0, g))]
    args = [q, k, v]
    if fox:
        in_specs += [pl.BlockSpec((1, bq, LANE), lambda b_, g, i: (b_, i, 0)),
                     pl.BlockSpec((1, hg, nk_max, 1, bk), lambda b_, g, i: (b_, g, 0, 0, 0))]
        args += [cq, ck]
    return pl.pallas_call(
        functools.partial(_flash_kernel, bq=bq, bk=bk, q_off=q_off, nk_max=nk_max, fox=fox, hg=hg, dqk=dqk, dv=dv),
        grid=(b, heads // hg, lq // bq),
        in_specs=in_specs,
        out_specs=pl.BlockSpec((1, bq, hg * dv), lambda b_, g, i: (b_, i, g)),
        out_shape=jax.ShapeDtypeStruct((b, lq, heads * dv), jnp.bfloat16),
        scratch_shapes=[pltpu.VMEM((hg, bq, 1), jnp.float32), pltpu.VMEM((hg, bq, 1), jnp.float32),
                        pltpu.VMEM((hg, bq, dv), jnp.float32)],
        compiler_params=_cparams(("parallel", "parallel", "arbitrary")),
        name="fox_attention" if fox else "mla_attention",
    )(*args)


_KEY_NEG_INF = -2139095041
_F32_LOWEST = -3.4028234663852886e38


_FOLD = 64
_MAX_PROBES = 8 * 34


def _key_to_f32(key):
    bits = jnp.where(key >= 0, key, key ^ jnp.int32(0x7FFFFFFF))
    return lax.bitcast_convert_type(bits, jnp.float32)


def _f32_to_key(x):
    bits = lax.bitcast_convert_type(x, jnp.int32)
    return jnp.where(bits >= 0, bits, bits ^ jnp.int32(0x7FFFFFFF))


def _dsa_kernel(iq_ref, iqs_ref, iw_ref, dq_ref, ik_ref, dk_ref, dv_ref, o_ref,
                sc_ref, qg_ref, m_ref, l_ref, acc_ref, *, bq, bk, q_off, nk_max, topk):
    i = pl.program_id(1)
    nk = _num_key_tiles(i, bq, bk, q_off, nk_max)
    qpos = q_off + i * bq + lax.broadcasted_iota(jnp.int32, (bq, 1), 0)
    klim = ((qpos >> 6) + 1) << 6
    iw = iw_ref[0]

    def score_tile(j, _):
        kt = ik_ref[0, pl.ds(pl.multiple_of(j * bk, bk), bk), :]
        acc = jnp.zeros((bq, bk), jnp.float32)
        for g in range(IDX_HEADS // 2):
            se = _dot_nt(iq_ref[0, :, g * LANE:(g + 1) * LANE], kt)
            so = _dot_nt(iqs_ref[0, :, g * LANE:(g + 1) * LANE], kt)
            acc = acc + iw[:, 2 * g:2 * g + 1] * jnp.maximum(se, 0.0)
            acc = acc + iw[:, 2 * g + 1:2 * g + 2] * jnp.maximum(so, 0.0)
        kpos = j * bk + lax.broadcasted_iota(jnp.int32, (1, bk), 1)
        sc_ref[j] = jnp.where(kpos < klim, acc, -jnp.inf)
        return 0

    lax.fori_loop(0, nk, score_tile, 0)

    def count(hit):
        def body(j, acc):
            h = hit(sc_ref[j], j)
            for c in range(bk // LANE):
                acc = acc + h[:, c * LANE:(c + 1) * LANE]
            return acc
        acc = lax.fori_loop(0, nk, body, jnp.zeros((bq, LANE), jnp.float32))
        return jnp.sum(acc, axis=1, keepdims=True)

    count_ge = lambda pivot: count(lambda t, j: jnp.where(t >= pivot, 1.0, 0.0))
    kf = jnp.float32(topk)
    nonneg = count_ge(jnp.zeros((bq, 1), jnp.float32)) >= kf
    key0 = jnp.where(nonneg, jnp.int32(0), jnp.int32(-2 ** 31))

    def bit_step(it, key):
        cand = key | (jnp.int32(1) << (30 - it))
        ok = (count_ge(_key_to_f32(cand)) >= kf) | (cand <= _KEY_NEG_INF)
        return jnp.where(ok, cand, key)

    key = lax.fori_loop(0, 31, bit_step, key0)
    thr = jnp.maximum(_key_to_f32(key), _F32_LOWEST)

    tie = count_ge(thr) > kf
    total = nk_max * bk

    @pl.when(jnp.sum(jnp.where(tie, 1, 0)) > 0)
    def _():
        need = kf - count(lambda t, j: jnp.where(t > thr, 1.0, 0.0))
        cols = lax.broadcasted_iota(jnp.int32, (bq, bk), 1)

        def count_equal_upto(j_max):
            return count(lambda t, j: jnp.where(t == thr, jnp.where(cols <= j_max - j * bk, 1.0, 0.0), 0.0))

        def index_step(_, bracket):
            j_lo, j_hi = bracket
            mid = (j_lo + j_hi) >> 1
            ok = count_equal_upto(mid) >= need
            return jnp.where(ok, j_lo, mid), jnp.where(ok, mid, j_hi)

        bracket = (jnp.full((bq, 1), -1, jnp.int32), jnp.full((bq, 1), total - 1, jnp.int32))
        _, j_cut = lax.fori_loop(0, total.bit_length() + 1, index_step, bracket)
        j_cut = jnp.where(tie, j_cut, total)

        def drop_late_ties(j, _):
            t = sc_ref[j]
            sc_ref[j] = jnp.where(t == thr, jnp.where(cols > j_cut - j * bk, -jnp.inf, t), t)
            return 0

        lax.fori_loop(0, nk, drop_late_ties, 0)

    for h in range(DSA_HEADS):
        g, r = divmod(h, DSA_GROUP)
        qg_ref[g, r * bq:(r + 1) * bq, :] = dq_ref[0, :, h * LANE:(h + 1) * LANE]
    m_ref[...] = jnp.full(m_ref.shape, NEG_INF, jnp.float32)
    l_ref[...] = jnp.zeros(l_ref.shape, jnp.float32)
    acc_ref[...] = jnp.zeros(acc_ref.shape, jnp.float32)

    def att_tile(j, _):
        start = pl.multiple_of(j * bk, bk)
        drop = jnp.where(sc_ref[j] >= thr, 0.0, NEG_INF)
        drop = jnp.concatenate([drop] * DSA_GROUP, axis=0)
        ss = [_dot_nt(qg_ref[g], dk_ref[0, pl.ds(start, bk), g * LANE:(g + 1) * LANE]) + drop
              for g in range(DSA_KV_HEADS)]
        vts = [dv_ref[0, pl.ds(start, bk), g * LANE:(g + 1) * LANE] for g in range(DSA_KV_HEADS)]
        _softmax_steps(ss, vts, m_ref, l_ref, acc_ref)
        return 0

    lax.fori_loop(0, nk, att_tile, 0)
    for h in range(DSA_HEADS):
        g, r = divmod(h, DSA_GROUP)
        rows = slice(r * bq, (r + 1) * bq)
        o_ref[0, :, h * LANE:(h + 1) * LANE] = (acc_ref[g, rows, :] / l_ref[g, rows, :]).astype(o_ref.dtype)


def dsa_attention(iq, iqs, iw, dq, ik, dk, dv, *, lq, sp, bq, bk, q_off, topk):
    b = dq.shape[0]
    nk_max = sp // bk
    qrow = lambda w: pl.BlockSpec((1, bq, w), lambda b_, i: (b_, i, 0))
    krow = lambda w: pl.BlockSpec((1, sp, w), lambda b_, i: (b_, 0, 0))
    gr = DSA_GROUP * bq
    return pl.pallas_call(
        functools.partial(_dsa_kernel, bq=bq, bk=bk, q_off=q_off, nk_max=nk_max, topk=topk),
        grid=(b, lq // bq),
        in_specs=[qrow(IDX_W), qrow(IDX_W), qrow(LANE), qrow(DSA_W), krow(LANE), krow(DSA_KV_W), krow(DSA_KV_W)],
        out_specs=qrow(DSA_W),
        out_shape=jax.ShapeDtypeStruct((b, lq, DSA_W), jnp.bfloat16),
        scratch_shapes=[pltpu.VMEM((nk_max, bq, bk), jnp.float32),
                        pltpu.VMEM((DSA_KV_HEADS, gr, LANE), jnp.bfloat16),
                        pltpu.VMEM((DSA_KV_HEADS, gr, 1), jnp.float32),
                        pltpu.VMEM((DSA_KV_HEADS, gr, 1), jnp.float32),
                        pltpu.VMEM((DSA_KV_HEADS, gr, LANE), jnp.float32)],
        compiler_params=_cparams(("parallel", "arbitrary")),
        name="dsa_attention",
    )(iq, iqs, iw, dq, ik, dk, dv)


def _split3(x):
    hi = x.astype(jnp.bfloat16).astype(jnp.float32)
    r = x - hi
    mid = r.astype(jnp.bfloat16).astype(jnp.float32)
    lo = (r - mid).astype(jnp.bfloat16).astype(jnp.float32)
    return hi, mid, lo


def _fox_pack_kernel(q_ref, k_ref, c_ref, qa_o, ka_o, qn_o, kn_o):
    rows = q_ref.shape[0]
    lane = lax.broadcasted_iota(jnp.int32, (rows, LANE), 1)
    c = c_ref[...]
    zero = jnp.zeros((rows, LANE), jnp.float32)
    qn, kn = zero, zero
    for h in range(FOX_HEADS):
        qh = q_ref[:, h * LANE:(h + 1) * LANE].astype(jnp.float32)
        kh = k_ref[:, h * LANE:(h + 1) * LANE].astype(jnp.float32)
        qn = jnp.where(lane == h, jnp.sum(qh * qh, axis=1, keepdims=True), qn)
        kn = jnp.where(lane == h, jnp.sum(kh * kh, axis=1, keepdims=True), kn)
        ch = jnp.sum(jnp.where(lane == h, c, 0.0), axis=1, keepdims=True)
        hi, mid, lo = _split3(ch)
        terms = jnp.where(lane == 0, hi, jnp.where(lane == 1, mid, jnp.where(lane == 2, lo, zero)))
        ones_hi = jnp.where((lane >= 3) & (lane < 6), 1.0, 0.0)
        qa_o[:, h * FOX_QP:h * FOX_QP + LANE] = q_ref[:, h * LANE:(h + 1) * LANE]
        qa_o[:, h * FOX_QP + LANE:(h + 1) * FOX_QP] = (terms + ones_hi).astype(jnp.bfloat16)
        ones_lo = jnp.where(lane < 3, 1.0, 0.0)
        ka_o[:, h * FOX_QP:h * FOX_QP + LANE] = k_ref[:, h * LANE:(h + 1) * LANE]
        ka_o[:, h * FOX_QP + LANE:(h + 1) * FOX_QP] = (ones_lo - pltpu.roll(terms, 3, 1)).astype(jnp.bfloat16)
    qn_o[...] = qn
    kn_o[...] = kn


def fox_pack(q, k, c, *, rows, bm):
    row = lambda w: pl.BlockSpec((bm, w), lambda i: (i, 0))
    return pl.pallas_call(
        _fox_pack_kernel,
        grid=(rows // bm,),
        in_specs=[row(FOX_W), row(FOX_W), row(LANE)],
        out_specs=[row(FOX_HEADS * FOX_QP), row(FOX_HEADS * FOX_QP), row(LANE), row(LANE)],
        out_shape=[jax.ShapeDtypeStruct((rows, FOX_HEADS * FOX_QP), jnp.bfloat16)] * 2
        + [jax.ShapeDtypeStruct((rows, LANE), jnp.float32)] * 2,
        compiler_params=_cparams(("parallel",)),
        name="fox_pack",
    )(q, k, c)


def _logits_stage(slot, sts, s_ref, cm_ref):
    for t, st in enumerate(sts):
        s_ref[slot, t] = st
        cm_ref[slot, t] = jnp.max(st, axis=0, keepdims=True)


def _values_stage(slot, vtts, s_ref, cm_ref, m_ref, acc_ref):
    ps, alphas = [], []
    for t in range(len(vtts)):
        m_prev = m_ref[t]
        m_new = jnp.maximum(m_prev, cm_ref[slot, t])
        alphas.append(jnp.exp2(m_prev - m_new))
        ps.append(jnp.exp2(s_ref[slot, t] - m_new).astype(jnp.bfloat16))
        m_ref[t] = m_new
    for t, (p, alpha) in enumerate(zip(ps, alphas)):
        acc_ref[t] = alpha * acc_ref[t] + _dot(vtts[t], p)


def _pipelined_tiles(nk, nk_max, logits, values, start=0):
    last = nk_max - 1
    n = nk - start
    logits(start, 0)

    def run(first, tiles):
        for u in range(tiles):
            logits(jnp.minimum(start + first + u + 1, last), (u + 1) % 2)
            values(start + first + u, u % 2)

    def quad(jj, _):
        run(4 * jj, 4)
        return 0

    def pair(jj, _):
        run(2 * jj, 2)
        return 0

    n_quads = lax.shift_right_logical(n, 2)
    lax.fori_loop(0, n_quads, quad, 0)
    lax.fori_loop(2 * n_quads, lax.shift_right_logical(n, 1), pair, 0)

    @pl.when((n & 1) == 1)
    def _():
        values(nk - 1, 0)


def _finish_t(acc):
    return (acc[0:HEAD_DIM, :] / acc[HEAD_DIM:HEAD_DIM + 1, :]).T


_UNDERFLOW_EXP = 160.0


def _first_live_tile(q_ref, k_ref, qn_ref, c_first_ref, c_end_ref, kmax_ref, *, i, first, bq, bk, hg, dqk):
    g = pl.program_id(1)
    slack = []
    for t in range(hg):
        h = g * hg + t
        lane = lax.broadcasted_iota(jnp.int32, (bq, LANE), 1)
        q_norm = jnp.sqrt(jnp.max(jnp.where(lane == h, qn_ref[0], 0.0)))
        qt = q_ref[0, :, t * dqk:(t + 1) * dqk].astype(jnp.float32)
        kt = k_ref[0, pl.ds(pl.multiple_of(first, bq), bq), t * dqk:(t + 1) * dqk].astype(jnp.float32)
        diag_min = jnp.min(jnp.sum(qt * kt, axis=1, keepdims=True))
        slack.append(q_norm * kmax_ref[0, h] * 1.01 + c_first_ref[i, h] - diag_min + _UNDERFLOW_EXP)
    g0 = g * hg

    def dead(j):
        ok = c_end_ref[j, g0] > slack[0]
        for t in range(1, hg):
            ok = ok & (c_end_ref[j, g0 + t] > slack[t])
        return ok

    n_old = first // bk
    return lax.while_loop(lambda j: (j < n_old) & dead(j), lambda j: j + 1, jnp.int32(0))


def _flash_t_kernel(*refs, bq, bk, nk_max, frame_causal, hg, dqk, decay):
    if decay:
        q_ref, k_ref, vt_ref, qn_ref, c_first_ref, c_end_ref, kmax_ref, o_ref, m_ref, acc_ref, s_ref, cm_ref, d_ref = refs
    else:
        q_ref, k_ref, vt_ref, o_ref, m_ref, acc_ref, s_ref, cm_ref, d_ref = refs
    i = pl.program_id(2)
    first = i * bq
    nk = _num_key_tiles(i, bq, bk, 0, nk_max)
    m_ref[...] = jnp.full(m_ref.shape, NEG_INF, jnp.float32)
    acc_ref[...] = jnp.zeros(acc_ref.shape, jnp.float32)
    r = lax.broadcasted_iota(jnp.int32, (bk, bq), 0)
    c = lax.broadcasted_iota(jnp.int32, (bk, bq), 1)
    d_ref[...] = (r - c) if frame_causal else ((r >> 6) - (c >> 6))

    def logits(j, slot):
        start = pl.multiple_of(j * bk, bk)
        gap = first - j * bk
        visible = gap if frame_causal else (gap >> 6)
        mask = d_ref[...] <= visible
        sts = []
        for t in range(hg):
            kt = k_ref[0, pl.ds(start, bk), t * dqk:(t + 1) * dqk]
            st = _dot_nt(kt, q_ref[0, :, t * dqk:(t + 1) * dqk])
            sts.append(jnp.where(mask, st, NEG_INF))
        _logits_stage(slot, sts, s_ref, cm_ref)

    def values(j, slot):
        _values_stage(slot, [vt_ref[0, j, t * VR:(t + 1) * VR, :] for t in range(hg)], s_ref, cm_ref, m_ref, acc_ref)

    start = 0
    if decay:
        start = _first_live_tile(q_ref, k_ref, qn_ref, c_first_ref, c_end_ref, kmax_ref,
                                 i=i, first=first, bq=bq, bk=bk, hg=hg, dqk=dqk)
    _pipelined_tiles(nk, nk_max, logits, values, start)
    for t in range(hg):
        o_ref[0, :, t * HEAD_DIM:(t + 1) * HEAD_DIM] = _finish_t(acc_ref[t]).astype(o_ref.dtype)


def flash_attention_t(q, k, vt, decay=None, *, lq, sp, heads, hg, dqk, bq, bk, frame_causal, name):
    nk_max = sp // bk
    smem = pl.BlockSpec(memory_space=pltpu.SMEM)
    extra_specs, extra = [], []
    if decay is not None:
        extra_specs = [pl.BlockSpec((1, bq, LANE), lambda b_, g, i: (b_, i, 0)), smem, smem, smem]
        extra = list(decay)
    return pl.pallas_call(
        functools.partial(_flash_t_kernel, bq=bq, bk=bk, nk_max=nk_max, frame_causal=frame_causal, hg=hg, dqk=dqk,
                          decay=decay is not None),
        grid=(1, heads // hg, lq // bq),
        in_specs=[pl.BlockSpec((1, bq, hg * dqk), lambda b_, g, i: (b_, i, g)),
                  _resident((1, sp, hg * dqk), lambda b_, g, i: (b_, 0, g)),
                  _resident((1, nk_max, hg * VR, bk), lambda b_, g, i: (b_, 0, g, 0))] + extra_specs,
        out_specs=pl.BlockSpec((1, bq, hg * HEAD_DIM), lambda b_, g, i: (b_, i, g)),
        out_shape=jax.ShapeDtypeStruct((1, lq, heads * HEAD_DIM), jnp.bfloat16),
        scratch_shapes=[pltpu.VMEM((hg, 1, bq), jnp.float32), pltpu.VMEM((hg, VR, bq), jnp.float32),
                        pltpu.VMEM((2, hg, bk, bq), jnp.float32), pltpu.VMEM((2, hg, 1, bq), jnp.float32),
                        pltpu.VMEM((bk, bq), jnp.int32)],
        compiler_params=_cparams(("parallel", "parallel", "arbitrary")),
        name=name,
    )(q, k, vt, *extra)


def _dsa_t_kernel(iq_ref, iqs_ref, iwt_ref, dq_ref, ik_ref, dk_ref, dvt_ref, o_ref,
                  sc_ref, m_ref, acc_ref, s_ref, cm_ref, *, bq, bk, nk_max, topk):
    i = pl.program_id(1)
    nk = _num_key_tiles(i, bq, bk, 0, nk_max)
    qpos = i * bq + lax.broadcasted_iota(jnp.int32, (1, bq), 1)
    klim = ((qpos >> 6) + 1) << 6

    fold = lambda a, op: op(a.reshape(bk // _FOLD, _FOLD, bq), axis=0)

    def score_tile(j, carry):
        hi, lo = carry
        kt = ik_ref[0, pl.ds(pl.multiple_of(j * bk, bk), bk), :]
        acc = jnp.zeros((bk, bq), jnp.float32)
        for g in range(IDX_HEADS // 2):
            se = _dot_nt(kt, iq_ref[0, :, g * LANE:(g + 1) * LANE])
            so = _dot_nt(kt, iqs_ref[0, :, g * LANE:(g + 1) * LANE])
            acc = acc + iwt_ref[2 * g:2 * g + 1, :] * jnp.maximum(se, 0.0)
            acc = acc + iwt_ref[2 * g + 1:2 * g + 2, :] * jnp.maximum(so, 0.0)
        adm = (j * bk + lax.broadcasted_iota(jnp.int32, (bk, bq), 0)) < klim
        sc = jnp.where(adm, acc, -jnp.inf)
        sc_ref[j] = sc
        return (jnp.maximum(hi, fold(sc, jnp.max)), jnp.minimum(lo, fold(jnp.where(adm, acc, jnp.inf), jnp.min)))

    def score_quad(jj, c):
        for u in range(4):
            c = score_tile(4 * jj + u, c)
        return c

    n_quads = lax.shift_right_logical(nk, 2)
    carry = lax.fori_loop(0, n_quads, score_quad,
                          (jnp.full((_FOLD, bq), -jnp.inf, jnp.float32), jnp.full((_FOLD, bq), jnp.inf, jnp.float32)))
    hi, lo = lax.fori_loop(4 * n_quads, nk, score_tile, carry)
    hi = jnp.max(hi, axis=0, keepdims=True)
    lo = jnp.min(lo, axis=0, keepdims=True)

    def count_ge(pivot):
        def body(j, acc):
            for r in range(bk // _FOLD):
                acc = acc + jnp.where(sc_ref[j, r * _FOLD:(r + 1) * _FOLD, :] >= pivot, 1.0, 0.0)
            return acc
        n_pairs = lax.shift_right_logical(nk, 1)
        acc = lax.fori_loop(0, n_pairs, lambda jj, a: body(2 * jj + 1, body(2 * jj, a)),
                            jnp.zeros((_FOLD, bq), jnp.float32))
        acc = lax.fori_loop(2 * n_pairs, nk, body, acc)
        return jnp.sum(acc, axis=0, keepdims=True)

    kf = jnp.float32(topk)
    hi = _key_to_f32(_f32_to_key(hi) + 1)
    c_lo = klim.astype(jnp.float32)
    c_hi = jnp.zeros((1, bq), jnp.float32)

    def settled(lo, hi, c_lo):
        width = lax.bitcast_convert_type(_f32_to_key(hi) - _f32_to_key(lo), jnp.uint32)
        return (c_lo <= kf) | (width <= 1)

    def search_cond(state):
        it, n_open = state[0], state[1]
        return (n_open > 0) & (it < _MAX_PROBES)

    def probe_once(it, lo, hi, c_lo, c_hi):
        done = settled(lo, hi, c_lo)
        k_lo, k_hi = _f32_to_key(lo), _f32_to_key(hi)
        width = lax.bitcast_convert_type(k_hi - k_lo, jnp.uint32)
        k_mid = k_lo + lax.bitcast_convert_type(width >> 1, jnp.int32)
        k_val = jnp.clip(_f32_to_key(lo + (hi - lo) * 0.5), k_lo + 1, k_hi - 1)
        probe = _key_to_f32(jnp.where((it & 7) == 7, k_mid, k_val))
        probe = jnp.where(done, lo, probe)
        c = count_ge(probe)
        up = (c >= kf) & ~done
        down = (c < kf) & ~done
        return (jnp.where(up, probe, lo), jnp.where(down, probe, hi), jnp.where(up, c, c_lo), jnp.where(down, c, c_hi))

    def search_step(state):
        it, _, lo, hi, c_lo, c_hi = state
        lo, hi, c_lo, c_hi = probe_once(it, lo, hi, c_lo, c_hi)
        lo, hi, c_lo, c_hi = probe_once(it + 1, lo, hi, c_lo, c_hi)
        n_open = jnp.sum(jnp.where(settled(lo, hi, c_lo), 0, 1))
        return it + 2, n_open, lo, hi, c_lo, c_hi

    state = (jnp.int32(0), jnp.sum(jnp.where(settled(lo, hi, c_lo), 0, 1)), lo, hi, c_lo, c_hi)
    _, _, thr, _, c_lo, c_hi = lax.while_loop(search_cond, search_step, state)

    tie = c_lo > kf
    total = nk_max * bk

    @pl.when(jnp.sum(jnp.where(tie, 1, 0)) > 0)
    def _():
        need = kf - c_hi
        rows = lax.broadcasted_iota(jnp.int32, (bk, bq), 0)

        def count_equal_upto(j_max):
            def body(j, acc):
                hit = jnp.where(sc_ref[j] == thr, jnp.where(rows <= j_max - j * bk, 1.0, 0.0), 0.0)
                return acc + fold(hit, jnp.sum)
            acc = lax.fori_loop(0, nk, body, jnp.zeros((_FOLD, bq), jnp.float32))
            return jnp.sum(acc, axis=0, keepdims=True)

        def index_step(_, bracket):
            j_lo, j_hi = bracket
            mid = (j_lo + j_hi) >> 1
            ok = count_equal_upto(mid) >= need
            return jnp.where(ok, j_lo, mid), jnp.where(ok, mid, j_hi)

        bracket = (jnp.full((1, bq), -1, jnp.int32), jnp.full((1, bq), total - 1, jnp.int32))
        _, j_cut = lax.fori_loop(0, total.bit_length() + 1, index_step, bracket)
        j_cut = jnp.where(tie, j_cut, total)

        def drop_late_ties(j, _):
            t = sc_ref[j]
            sc_ref[j] = jnp.where(t == thr, jnp.where(rows > j_cut - j * bk, -jnp.inf, t), t)
            return 0

        lax.fori_loop(0, nk, drop_late_ties, 0)

    m_ref[...] = jnp.full(m_ref.shape, NEG_INF, jnp.float32)
    acc_ref[...] = jnp.zeros(acc_ref.shape, jnp.float32)

    def logits(j, slot):
        start = pl.multiple_of(j * bk, bk)
        drop = jnp.where(sc_ref[j] >= thr, 0.0, NEG_INF)
        sts = []
        for h in range(DSA_HEADS):
            g = h // DSA_GROUP
            kt = dk_ref[0, pl.ds(start, bk), g * LANE:(g + 1) * LANE]
            sts.append(_dot_nt(kt, dq_ref[0, :, h * LANE:(h + 1) * LANE]) + drop)
        _logits_stage(slot, sts, s_ref, cm_ref)

    def values(j, slot):
        vtts = [dvt_ref[0, j, (h // DSA_GROUP) * VR:(h // DSA_GROUP + 1) * VR, :] for h in range(DSA_HEADS)]
        _values_stage(slot, vtts, s_ref, cm_ref, m_ref, acc_ref)

    _pipelined_tiles(nk, nk_max, logits, values)
    for h in range(DSA_HEADS):
        o_ref[0, :, h * LANE:(h + 1) * LANE] = _finish_t(acc_ref[h]).astype(o_ref.dtype)


def dsa_attention_t(iq, iqs, iwt, dq, ik, dk, dvt, *, lq, sp, bq, bk, topk):
    nk_max = sp // bk
    qrow = lambda w: pl.BlockSpec((1, bq, w), lambda b_, i: (b_, i, 0))
    krow = lambda w: _resident((1, sp, w), lambda b_, i: (b_, 0, 0))
    return pl.pallas_call(
        functools.partial(_dsa_t_kernel, bq=bq, bk=bk, nk_max=nk_max, topk=topk),
        grid=(1, lq // bq),
        in_specs=[qrow(IDX_W), qrow(IDX_W), pl.BlockSpec((LANE, bq), lambda b_, i: (0, i)), qrow(DSA_W),
                  krow(LANE), krow(DSA_KV_W),
                  _resident((1, nk_max, DSA_KV_HEADS * VR, bk), lambda b_, i: (b_, 0, 0, 0))],
        out_specs=qrow(DSA_W),
        out_shape=jax.ShapeDtypeStruct((1, lq, DSA_W), jnp.bfloat16),
        scratch_shapes=[pltpu.VMEM((nk_max, bk, bq), jnp.float32),
                        pltpu.VMEM((DSA_HEADS, 1, bq), jnp.float32),
                        pltpu.VMEM((DSA_HEADS, VR, bq), jnp.float32),
                        pltpu.VMEM((2, DSA_HEADS, bk, bq), jnp.float32),
                        pltpu.VMEM((2, DSA_HEADS, 1, bq), jnp.float32)],
        compiler_params=_cparams(("parallel", "arbitrary")),
        name="dsa_attention_t",
    )(iq, iqs, iwt, dq, ik, dk, dvt)


def _out_proj_kernel(x_ref, a_ref, b_ref, c_ref, wa_ref, wb_ref, wc_ref, o_ref):
    o_ref[...] = (x_ref[...] + _dot(a_ref[...], wa_ref[...]) + _dot(b_ref[...], wb_ref[...])
                  + _dot(c_ref[...], wc_ref[...]))


def out_proj(x, a, b, c, w, layer, *, bm):
    m, d = x.shape
    wa, wb, wc = a.shape[1], b.shape[1], c.shape[1]
    assert wa == wb and (wa + wb) % wc == 0
    row = lambda w_: pl.BlockSpec((bm, w_), lambda i: (i, 0))
    band = lambda rows, blk: _resident((None, rows, d), lambda i: (layer, blk, 0))
    return pl.pallas_call(
        _out_proj_kernel,
        grid=(m // bm,),
        in_specs=[row(d), row(wa), row(wb), row(wc), band(wa, 0), band(wb, 1), band(wc, (wa + wb) // wc)],
        out_specs=row(d),
        out_shape=jax.ShapeDtypeStruct((m, d), jnp.float32),
        compiler_params=_cparams(("parallel",)),
        name="out_proj",
    )(x, a, b, c, w, w, w)


def _ffn_kernel(x_ref, g_ref, wg_ref, wu_ref, wd_ref, o_ref, xn_ref, acc_ref):
    j = pl.program_id(1)

    @pl.when(j == 0)
    def _():
        xn_ref[...] = _rms(x_ref[...], g_ref[...]).astype(jnp.bfloat16)
        acc_ref[...] = jnp.zeros_like(acc_ref)

    xn = xn_ref[...]
    gate = _dot(xn, wg_ref[...])
    up = _dot(xn, wu_ref[...])
    hidden = (gate * (1.0 / (1.0 + jnp.exp(-gate))) * up).astype(jnp.bfloat16)
    acc_ref[...] += _dot(hidden, wd_ref[...])

    @pl.when(j == pl.num_programs(1) - 1)
    def _():
        o_ref[...] = x_ref[...] + acc_ref[...]


def ffn(x, g, wg, wu, wd, layer, *, bm, bf):
    m, d = x.shape
    f = wg.shape[2]
    return pl.pallas_call(
        _ffn_kernel,
        grid=(m // bm, f // bf),
        in_specs=[pl.BlockSpec((bm, d), lambda i, j: (i, 0)),
                  pl.BlockSpec((1, d), lambda i, j: (0, 0)),
                  pl.BlockSpec((None, d, bf), lambda i, j: (layer, 0, j)),
                  pl.BlockSpec((None, d, bf), lambda i, j: (layer, 0, j)),
                  pl.BlockSpec((None, bf, d), lambda i, j: (layer, j, 0))],
        out_specs=pl.BlockSpec((bm, d), lambda i, j: (i, 0)),
        out_shape=jax.ShapeDtypeStruct((m, d), jnp.float32),
        scratch_shapes=[pltpu.VMEM((bm, d), jnp.bfloat16), pltpu.VMEM((bm, d), jnp.float32)],
        compiler_params=_cparams(("parallel", "arbitrary")),
        name="swiglu",
    )(x, g.reshape(1, d), wg, wu, wd)


def _mix_ffn_kernel(x_ref, a_ref, b_ref, c_ref, wa_ref, wb_ref, wc_ref, g_ref, wg_ref, wu_ref, wd_ref,
                    o_ref, xn_ref, acc_ref):
    j = pl.program_id(1)

    @pl.when(j == 0)
    def _():
        x1 = (x_ref[...] + _dot(a_ref[...], wa_ref[...]) + _dot(b_ref[...], wb_ref[...])
              + _dot(c_ref[...], wc_ref[...]))
        o_ref[...] = x1
        xn_ref[...] = _rms(x1, g_ref[...]).astype(jnp.bfloat16)
        acc_ref[...] = jnp.zeros_like(acc_ref)

    xn = xn_ref[...]
    gate = _dot(xn, wg_ref[...])
    up = _dot(xn, wu_ref[...])
    hidden = (gate * (1.0 / (1.0 + jnp.exp(-gate))) * up).astype(jnp.bfloat16)
    acc_ref[...] += _dot(hidden, wd_ref[...])

    @pl.when(j == pl.num_programs(1) - 1)
    def _():
        o_ref[...] = o_ref[...] + acc_ref[...]


def mix_ffn(x, a, b, c, w_o, g, wg, wu, wd, layer, *, bm, bf):
    m, d = x.shape
    f = wg.shape[2]
    wa, wb, wc = a.shape[1], b.shape[1], c.shape[1]
    assert wa == wb and (wa + wb) % wc == 0
    row = lambda w_: pl.BlockSpec((bm, w_), lambda i, j: (i, 0))
    band = lambda rows, blk: _resident((None, rows, d), lambda i, j: (layer, blk, 0))
    return pl.pallas_call(
        _mix_ffn_kernel,
        grid=(m // bm, f // bf),
        in_specs=[row(d), row(wa), row(wb), row(wc), band(wa, 0), band(wb, 1), band(wc, (wa + wb) // wc),
                  pl.BlockSpec((1, d), lambda i, j: (0, 0)),
                  pl.BlockSpec((None, d, bf), lambda i, j: (layer, 0, j)),
                  pl.BlockSpec((None, d, bf), lambda i, j: (layer, 0, j)),
                  pl.BlockSpec((None, bf, d), lambda i, j: (layer, j, 0))],
        out_specs=row(d),
        out_shape=jax.ShapeDtypeStruct((m, d), jnp.float32),
        scratch_shapes=[pltpu.VMEM((bm, d), jnp.bfloat16), pltpu.VMEM((bm, d), jnp.float32)],
        compiler_params=_cparams(("parallel", "arbitrary")),
        name="mix_swiglu",
    )(x, a, b, c, w_o, w_o, w_o, g.reshape(1, d), wg, wu, wd)


def _final_norm_kernel(x_ref, g_ref, o_ref):
    o_ref[...] = _rms(x_ref[...], g_ref[...])


def final_norm(x, g, *, bm):
    m, d = x.shape
    return pl.pallas_call(
        _final_norm_kernel,
        grid=(m // bm,),
        in_specs=[pl.BlockSpec((bm, d), lambda i: (i, 0)), pl.BlockSpec((1, d), lambda i: (0, 0))],
        out_specs=pl.BlockSpec((bm, d), lambda i: (i, 0)),
        out_shape=jax.ShapeDtypeStruct((m, d), jnp.float32),
        compiler_params=_cparams(("parallel",)),
        name="final_norm",
    )(x, g.reshape(1, d))


def _pad_cols(a, width):
    return jnp.pad(a, [(0, 0)] * (a.ndim - 1) + [(0, width - a.shape[-1])])


def _layout_w_in(w_in):
    sizes = (FOX_W, FOX_W, FOX_W, FOX_HEADS, DSA_W, DSA_KV_W, DSA_KV_W, IDX_W, IDX_DIM, IDX_HEADS,
             MLA_Q_LORA, MLA_KV_LORA, MLA_ROPE)
    splits = np.cumsum(sizes)[:-1]
    q_a, k_a, v_a, f_a, q_b, k_b, v_b, q_i, k_i, w_i, c_q, c_kv, k_r = jnp.split(w_in, splits, axis=-1)
    parts = {"fq": q_a, "fk": k_a, "fv": v_a, "dq": q_b, "dk": k_b, "dv": v_b, "iq": q_i, "cq": c_q,
             "ckv": c_kv, "fa": f_a, "ik": k_i, "iw": w_i, "kr": k_r}
    cols = [_pad_cols(parts[n], w) for n, w in _SEGS]
    return _pad_cols(jnp.concatenate(cols, axis=-1), PROJ_W).astype(jnp.bfloat16)


def _layout_w_uq(w_uq):
    dp = w_uq.shape[0]
    w = w_uq.reshape(dp, MLA_Q_LORA, MLA_HEADS, MLA_NOPE + MLA_ROPE)
    return _pad_cols(w, MLA_QP).reshape(dp, MLA_Q_LORA, MLA_QW).astype(jnp.bfloat16)


def _layout_w_ukv(w_ukv):
    dp = w_ukv.shape[0]
    w = w_ukv.reshape(dp, MLA_KV_LORA, MLA_HEADS, MLA_NOPE + MLA_V)
    kn = w[..., :MLA_NOPE].reshape(dp, MLA_KV_LORA, MLA_HEADS * MLA_NOPE)
    vv = w[..., MLA_NOPE:].reshape(dp, MLA_KV_LORA, MLA_VW)
    return jnp.concatenate([kn, vv], axis=-1).astype(jnp.bfloat16)


def _rope_table(pos, rot, period):
    half = rot // 2
    inv_freq = ROPE_THETA ** (-jnp.arange(half, dtype=jnp.float32) / half)
    ang = pos.astype(jnp.float32)[:, None] * inv_freq[None, :]
    cos, sin = jnp.cos(ang), jnp.sin(ang)
    r = pos.shape[0]
    ones = jnp.ones((r, period - rot), jnp.float32)
    zeros = jnp.zeros((r, period - rot), jnp.float32)
    zh = jnp.zeros((r, half), jnp.float32)
    c = jnp.concatenate([cos, cos, ones], axis=1)
    s1 = jnp.concatenate([zh, sin, zeros], axis=1)
    s2 = jnp.concatenate([-sin, zh, zeros], axis=1)
    rep = LANE // period
    return jnp.concatenate([jnp.tile(c, (1, rep)), jnp.tile(s1, (1, rep)), jnp.tile(s2, (1, rep))], axis=1)


def _pick(n, candidates):
    for c in candidates:
        if n % c == 0:
            return c
    raise ValueError(f"no block size among {candidates} divides {n}")


def _pad_seq(a, sp):
    return jnp.pad(a, [(0, 0), (0, sp - a.shape[1])] + [(0, 0)] * (a.ndim - 2))


def kernel(x_prompt, x_sample, cache_fox_k, cache_fox_v, cache_fox_logf, cache_dsa_k, cache_dsa_v, cache_idx_k, cache_mla_ckv, cache_mla_krope, w_in, fox_bias, mla_q_norm, mla_w_uq, mla_kv_norm, mla_w_ukv, w_o, attn_norm, ffn_norm, w_gate, w_up, w_down, final_norm_w):
    bf = jnp.bfloat16
    depth = w_in.shape[0]
    nb, seq, d = x_prompt.shape
    db, dec, _ = x_sample.shape
    past = cache_fox_k.shape[2]
    assert nb == 1 and seq % CHUNK == 0 and dec % CHUNK == 0 and past % CHUNK == 0
    mp, ms = nb * seq, db * dec
    m = mp + ms
    s_len = past + dec

    bm = _pick(m, (512, 256, 128, 64))
    assert mp % bm == 0
    bq_p = _pick(seq, (256, 128))
    bk_p = bm
    bq_s = _pick(dec, (64,))
    bk_s = 256
    sp = ((s_len + bk_s - 1) // bk_s) * bk_s
    topk_p = min(IDX_TOPK_MAX, seq // 4)
    topk_s = min(IDX_TOPK_MAX, s_len // 4)
    fox_hg, mla_hg = 3, 2
    d_ff = w_gate.shape[2]
    bff = _pick(d_ff, (512, 256, 128))

    w_in_p = _layout_w_in(w_in)
    wuq_p = _layout_w_uq(mla_w_uq)
    wukv_p = _layout_w_ukv(mla_w_ukv)
    w_o_b = w_o.astype(bf)
    wg_b, wu_b, wd_b = w_gate.astype(bf), w_up.astype(bf), w_down.astype(bf)
    fox_bias_p = _pad_cols(fox_bias, LANE).reshape(depth, 1, LANE)

    pos = jnp.concatenate([jnp.tile(jnp.arange(seq, dtype=jnp.int32), nb),
                           jnp.tile(past + jnp.arange(dec, dtype=jnp.int32), db)])
    tab = jnp.concatenate([_rope_table(pos, PARTIAL_ROT, HEAD_DIM), _rope_table(pos, IDX_ROT, IDX_DIM),
                           _rope_table(pos, MLA_ROPE, LANE)], axis=1)

    x = jnp.concatenate([x_prompt.reshape(mp, d), x_sample.reshape(ms, d)], axis=0)
    new_rows = []
    for l in range(depth):
        proj = norm_matmul(x, attn_norm[l], w_in_p, l, bm=bm, bn=_pick(PROJ_W, (1536, 1024, 512)))
        (fq, fk, fv, fkb, fvb, fvt, lf, dq, dk, dv, dkb, dvb, dvt, iq, iqs, ik, ikb, iw, iwt, mq, ckv, kr) = post_proj(
            proj, tab, fox_bias_p[l], mla_q_norm[l].reshape(1, -1), mla_kv_norm[l].reshape(1, -1), wuq_p[l],
            bm=bm // 2, bk=bm)

        pr = lambda a: a[:mp].reshape(nb, seq, -1)
        sr = lambda a: a[mp:].reshape(db, dec, -1)
        new_rows.append((fk, fv, lf, dk, dv, ik, ckv, kr))

        cat = lambda c, new: _pad_seq(jnp.concatenate([c, new], axis=1), sp)
        s_fkb = cat(cache_fox_k[l].reshape(db, past, FOX_W).astype(bf), sr(fkb))
        s_fvb = cat(cache_fox_v[l].reshape(db, past, FOX_W).astype(bf), sr(fvb))
        s_lf = cat(_pad_cols(cache_fox_logf[l], LANE), sr(lf))
        s_dkb = cat(cache_dsa_k[l].reshape(db, past, DSA_KV_W).astype(bf), sr(dkb))
        s_dvb = cat(cache_dsa_v[l].reshape(db, past, DSA_KV_W).astype(bf), sr(dvb))
        s_ikb = cat(_pad_cols(cache_idx_k[l], LANE).astype(bf), sr(ikb))
        s_ckv = cat(cache_mla_ckv[l], sr(ckv))
        s_kr = cat(_pad_cols(cache_mla_krope[l], LANE), sr(kr))

        ckv_all = jnp.concatenate([ckv[:mp], s_ckv.reshape(db * sp, -1)], axis=0)
        kr_all = jnp.concatenate([kr[:mp], s_kr.reshape(db * sp, -1)], axis=0)
        kmla, vmla, vtmla = kv_up(ckv_all, kr_all, wukv_p[l], bm=bm)
        s_kmla, s_vmla = kmla[mp:].reshape(db, sp, -1), vmla[mp:].reshape(db, sp, -1)

        p_c, _ = cumsum_rows(pr(lf), bt=_pick(seq, (256, 128)))
        s_c, s_ct = cumsum_rows(s_lf, bt=_pick(sp, (256, 128)))
        ck_form = lambda ct, bk: ct[:, :FOX_HEADS].reshape(ct.shape[0], FOX_HEADS, ct.shape[2] // bk, 1, bk)

        al = lambda a: a.reshape(1, *a.shape)
        fqa, fka, fqn, fkn = fox_pack(fq, fkb, p_c[0], rows=mp, bm=bm)
        decay = (al(fqn), p_c[0, ::bq_p], p_c[0, bk_p - 1::bk_p], jnp.sqrt(jnp.max(fkn, axis=0, keepdims=True)))
        out_a = flash_attention_t(al(fqa), al(fka), al(fvt), decay, lq=seq, sp=seq, heads=FOX_HEADS, hg=fox_hg, dqk=FOX_QP,
                                  bq=bq_p, bk=bk_p, frame_causal=True, name="fox_attention_t")
        out_b = dsa_attention_t(al(iq), al(iqs), iwt, al(dq), al(ikb), al(dkb), al(dvt), lq=seq, sp=seq,
                                bq=bq_p, bk=bk_p, topk=topk_p)
        out_c = flash_attention_t(al(mq), al(kmla), al(vtmla), lq=seq, sp=seq, heads=MLA_HEADS, hg=mla_hg, dqk=MLA_QP,
                                  bq=bq_p, bk=bk_p, frame_causal=False, name="mla_attention_t")
        s_out_a = flash_attention(sr(fq), s_fkb, s_fvb, s_c[:, past:past + dec], ck_form(s_ct, bk_s), lq=dec, sp=sp,
                                  heads=FOX_HEADS, hg=fox_hg, dqk=HEAD_DIM, dv=HEAD_DIM, bq=bq_s, bk=bk_s, q_off=past)
        s_out_b = dsa_attention(sr(iq), sr(iqs), sr(iw), sr(dq), s_ikb, s_dkb, s_dvb, lq=dec, sp=sp,
                                bq=bq_s, bk=bk_s, q_off=past, topk=topk_s)
        s_out_c = flash_attention(sr(mq), s_kmla, s_vmla, lq=dec, sp=sp, heads=MLA_HEADS, hg=mla_hg,
                                  dqk=MLA_QP, dv=MLA_V, bq=bq_s, bk=bk_s, q_off=past)

        rows = lambda p, s: jnp.concatenate([p.reshape(mp, -1), s.reshape(ms, -1)], axis=0)
        x = mix_ffn(x, rows(out_a, s_out_a), rows(out_b, s_out_b), rows(out_c, s_out_c), w_o_b, ffn_norm[l],
                    wg_b, wu_b, wd_b, l, bm=bm, bf=bff)

    y = final_norm(x, final_norm_w, bm=bm)
    y_prompt = y[:mp].reshape(nb, seq, d)
    y_sample = y[mp:].reshape(db, dec, d)
    tails = ((FOX_HEADS, HEAD_DIM), (FOX_HEADS, HEAD_DIM), (FOX_HEADS,), (DSA_KV_HEADS, HEAD_DIM),
             (DSA_KV_HEADS, HEAD_DIM), (IDX_DIM,), (MLA_KV_LORA,), (MLA_ROPE,))
    p_out, s_out = [], []
    for arrays, tail in zip(zip(*new_rows), tails):
        w = math.prod(tail)
        p_out.append(jnp.stack([a[:mp, :w] for a in arrays]).reshape((depth, nb, seq) + tail))
        s_out.append(jnp.stack([a[mp:, :w] for a in arrays]).reshape((depth, db, dec) + tail))
    return (y_prompt, y_sample) + tuple(p_out) + tuple(s_out)
```
